```python
import math
import jax, jax.numpy as jnp
from jax import lax
import numpy as np

D_MODEL = 2048
BATCH = 16
SEQ = 2048
DEPTH = 1
DEC_BATCH = 16
DEC_SEQ = 16
PAST_LEN = 1024

CHUNK = 64
Q_BLOCK = 128
EPS = 1e-6
D_CONV = D_MODEL // 2
CONV_WIDTH = 3
D_ATTN = D_MODEL - D_CONV
N_HEADS = 8
HEAD_DIM = D_ATTN // (2 * N_HEADS)
V_DIM = 2 * HEAD_DIM
D_PROJ = 3 * D_CONV + 3 * D_ATTN
SPLITS = (D_CONV, 2 * D_CONV, 3 * D_CONV, 3 * D_CONV + D_ATTN, 3 * D_CONV + 2 * D_ATTN)
N_GROUPS = 4
EXPERTS_PER_GROUP = 8
N_EXPERTS = N_GROUPS * EXPERTS_PER_GROUP
TOP_K = 2
D_EXPERT = 512

kernel_name = "hymba_conv_diffattn_hiermoe_stream_step"


def rms_norm(x, g):
    xf = x.astype(jnp.float32)
    y = xf * lax.rsqrt(jnp.mean(xf * xf, axis=-1, keepdims=True) + EPS)
    return (y * g.astype(jnp.float32)).astype(x.dtype)


def diff_lambda(lq1, lk1, lq2, lk2, lam_init):
    f = jnp.float32
    return (jnp.exp(jnp.sum(lq1.astype(f) * lk1.astype(f)))
            - jnp.exp(jnp.sum(lq2.astype(f) * lk2.astype(f))) + lam_init)


def diff_attend(q, k, v, mask, lam):
    s = jnp.einsum('bqhmd,bkhmd->bmhqk', q, k).astype(jnp.float32) * (HEAD_DIM ** -0.5)
    if mask is not None:
        s = jnp.where(mask, s, -jnp.inf)
    p = jax.nn.softmax(s, axis=-1)
    a = p[:, 0] - lam * p[:, 1]
    return jnp.einsum('bhqk,bkhe->bqhe', a.astype(v.dtype), v)


def prompt_diff_attention(q, k, v, lam):
    B, T = q.shape[:2]
    nb = T // Q_BLOCK
    qb = q.reshape(B, nb, Q_BLOCK, N_HEADS, 2, HEAD_DIM).transpose(1, 0, 2, 3, 4, 5)
    k_chunk = jnp.arange(T) // CHUNK

    def one(args):
        i, qi = args
        q_chunk = (i * Q_BLOCK + jnp.arange(Q_BLOCK)) // CHUNK
        mask = k_chunk[None, :] <= q_chunk[:, None]
        return diff_attend(qi, k, v, mask, lam)

    o = lax.map(one, (jnp.arange(nb), qb))
    return o.transpose(1, 0, 2, 3, 4).reshape(B, T, N_HEADS, V_DIM)


def short_conv(b_gate, c_gate, x_in, prev, conv_w):
    u = c_gate * x_in
    u_ext = jnp.concatenate([prev, u], axis=1)
    T = u.shape[1]
    y = sum(conv_w[j] * u_ext[:, j:j + T] for j in range(CONV_WIDTH))
    return b_gate * y, u_ext[:, -(CONV_WIDTH - 1):]


def mixer_sublayer(x, conv_prev, k_past, v_past, norm_g, w_in, conv_w, lam, lam_init, head_norm, w_out):
    B, T = x.shape[:2]
    z = rms_norm(x, norm_g) @ w_in
    b_gate, c_gate, x_in, q, k, v = jnp.split(z, SPLITS, axis=-1)
    q = q.reshape(B, T, N_HEADS, 2, HEAD_DIM)
    k = k.reshape(B, T, N_HEADS, 2, HEAD_DIM)
    v = v.reshape(B, T, N_HEADS, V_DIM)
    a, conv_state = short_conv(b_gate, c_gate, x_in, conv_prev, conv_w)
    if k_past is None:
        o = prompt_diff_attention(q, k, v, lam)
    else:
        o = diff_attend(q, jnp.concatenate([k_past, k], axis=1),
                        jnp.concatenate([v_past, v], axis=1), None, lam)
    o = (rms_norm(o, head_norm) * (1.0 - lam_init)).reshape(B, T, D_ATTN)
    y = jnp.concatenate([a, o], axis=-1) @ w_out
    return x + y, k, v, conv_state


def hier_moe(h, w_rg, b_rg, w_re, b_re, w_ei, w_eo):
    B, T, D = h.shape
    n = B * T
    f32 = jnp.float32
    hf = h.reshape(n, D)
    lg = (hf @ w_rg).astype(f32) + b_rg.astype(f32)
    pg = jax.nn.softmax(lg, axis=-1)
    g = jnp.argmax(lg, axis=-1)
    gate_g = jnp.take_along_axis(pg, g[:, None], axis=-1)
    le = ((hf @ w_re).astype(f32) + b_re.astype(f32)).reshape(n, N_GROUPS, EXPERTS_PER_GROUP)
    le_g = jnp.take_along_axis(le, g[:, None, None], axis=1)[:, 0]
    top_v, top_i = lax.top_k(le_g, TOP_K)
    w_tok = gate_g * jax.nn.softmax(top_v, axis=-1)
    eid = (g[:, None] * EXPERTS_PER_GROUP + top_i).reshape(-1)
    tok = jnp.repeat(jnp.arange(n), TOP_K)
    order = jnp.argsort(eid)
    tok_s = tok[order]
    w_s = w_tok.reshape(-1)[order]
    sizes = jnp.bincount(eid, length=N_EXPERTS).astype(jnp.int32)
    xs = hf[tok_s]
    gu = lax.ragged_dot(xs, w_ei, sizes)
    gate, up = jnp.split(gu, 2, axis=-1)
    ys = lax.ragged_dot(jax.nn.silu(gate) * up, w_eo, sizes)
    y = jnp.zeros_like(hf).at[tok_s].add(ys * w_s[:, None].astype(ys.dtype))
    return y.reshape(B, T, D)


def setup_inputs(seed: int = 0) -> dict:
    key = jax.random.key(seed)
    ks = jax.random.split(key, 24)
    nrm = lambda k, s, sc: jax.random.normal(k, s, jnp.float32) * sc
    L = DEPTH
    return {
        "x_prompt": nrm(ks[0], (BATCH, SEQ, D_MODEL), 1.0),
        "x_sample": nrm(ks[1], (DEC_BATCH, DEC_SEQ, D_MODEL), 1.0),
        "cache_k": nrm(ks[2], (L, DEC_BATCH, PAST_LEN, N_HEADS, 2, HEAD_DIM), 1.0),
        "cache_v": nrm(ks[3], (L, DEC_BATCH, PAST_LEN, N_HEADS, V_DIM), 1.0),
        "state_conv": nrm(ks[4], (L, DEC_BATCH, CONV_WIDTH - 1, D_CONV), 1.0),
        "norm_mix": 1.0 + nrm(ks[5], (L, D_MODEL), 0.02),
        "w_in": nrm(ks[6], (L, D_MODEL, D_PROJ), D_MODEL ** -0.5),
        "conv_w": nrm(ks[7], (L, CONV_WIDTH, D_CONV), CONV_WIDTH ** -0.5),
        "lambda_q1": nrm(ks[8], (L, HEAD_DIM), 0.1),
        "lambda_k1": nrm(ks[9], (L, HEAD_DIM), 0.1),
        "lambda_q2": nrm(ks[10], (L, HEAD_DIM), 0.1),
        "lambda_k2": nrm(ks[11], (L, HEAD_DIM), 0.1),
        "head_norm": 1.0 + nrm(ks[12], (L, V_DIM), 0.02),
        "w_out": nrm(ks[13], (L, D_MODEL, D_MODEL), D_MODEL ** -0.5),
        "norm_ffn": 1.0 + nrm(ks[14], (L, D_MODEL), 0.02),
        "w_router_group": nrm(ks[15], (L, D_MODEL, N_GROUPS), D_MODEL ** -0.5),
        "b_router_group": nrm(ks[16], (L, N_GROUPS), 0.01),
        "w_router_expert": nrm(ks[17], (L, D_MODEL, N_EXPERTS), D_MODEL ** -0.5),
        "b_router_expert": nrm(ks[18], (L, N_EXPERTS), 0.01),
        "w_expert_in": nrm(ks[19], (L, N_EXPERTS, D_MODEL, 2 * D_EXPERT), D_MODEL ** -0.5),
        "w_expert_out": nrm(ks[20], (L, N_EXPERTS, D_EXPERT, D_MODEL), D_EXPERT ** -0.5),
        "norm_final": 1.0 + nrm(ks[21], (D_MODEL,), 0.02),
    }


def reference(x_prompt, x_sample, cache_k, cache_v, state_conv, norm_mix, w_in, conv_w,
              lambda_q1, lambda_k1, lambda_q2, lambda_k2, head_norm, w_out, norm_ffn,
              w_router_group, b_router_group, w_router_expert, b_router_expert,
              w_expert_in, w_expert_out, norm_final):
    hp, hs = x_prompt, x_sample
    kp_l, vp_l, cp_l, ks_l, vs_l, cs_l = [], [], [], [], [], []
    for l in range(DEPTH):
        lam_init = 0.8 - 0.6 * math.exp(-0.3 * l)
        lam = diff_lambda(lambda_q1[l], lambda_k1[l], lambda_q2[l], lambda_k2[l], lam_init)
        zero_conv = jnp.zeros((hp.shape[0], CONV_WIDTH - 1, D_CONV), hp.dtype)
        hp, kp, vp, cp = mixer_sublayer(hp, zero_conv, None, None, norm_mix[l], w_in[l], conv_w[l],
                                        lam, lam_init, head_norm[l], w_out[l])
        hp = hp + hier_moe(rms_norm(hp, norm_ffn[l]), w_router_group[l], b_router_group[l],
                           w_router_expert[l], b_router_expert[l], w_expert_in[l], w_expert_out[l])
        hs, ksn, vsn, csn = mixer_sublayer(hs, state_conv[l], cache_k[l], cache_v[l], norm_mix[l], w_in[l],
                                           conv_w[l], lam, lam_init, head_norm[l], w_out[l])
        hs = hs + hier_moe(rms_norm(hs, norm_ffn[l]), w_router_group[l], b_router_group[l],
                           w_router_expert[l], b_router_expert[l], w_expert_in[l], w_expert_out[l])
        kp_l.append(kp); vp_l.append(vp); cp_l.append(cp)
        ks_l.append(ksn); vs_l.append(vsn); cs_l.append(csn)
    y_prompt = rms_norm(hp, norm_final)
    y_sample = rms_norm(hs, norm_final)
    k_prompt = jnp.stack(kp_l)
    v_prompt = jnp.stack(vp_l)
    conv_prompt = jnp.stack(cp_l)
    k_sample = jnp.stack(ks_l)
    v_sample = jnp.stack(vs_l)
    conv_sample = jnp.stack(cs_l)
    return (y_prompt, y_sample, k_prompt, v_prompt, conv_prompt, k_sample, v_sample, conv_sample)
```

```python
import functools
import math

import jax
import jax.numpy as jnp
from jax import lax
from jax.experimental import pallas as pl
from jax.experimental.pallas import tpu as pltpu

EPS = 1e-6
N_HEADS = 8
CHUNK = 64
CONV_WIDTH = 3
N_GROUPS = 4
EXPERTS_PER_GROUP = 8
N_EXPERTS = N_GROUPS * EXPERTS_PER_GROUP
LANES = 128
SUBLANES = 8
VMEM_LIMIT = 56 * 1024 * 1024

F32 = jnp.float32
BF16 = jnp.bfloat16
U32 = jnp.uint32
NEG_INF = float("-inf")


def _params(n_axes, vmem=VMEM_LIMIT, **kw):
    return pltpu.CompilerParams(dimension_semantics=("arbitrary",) * n_axes,
                                vmem_limit_bytes=vmem, **kw)


def _rms(x, g):
    return x * lax.rsqrt(jnp.mean(x * x, axis=-1, keepdims=True) + EPS) * g


def _lane_pick(x, lane, idx):
    return jnp.sum(jnp.where(lane == idx, x, 0.0), axis=1, keepdims=True)


def _in_proj_kernel(x_ref, g_ref, wa_ref, wb_ref, wc_ref, cw_ref, prev_ref,
                    a_ref, q_ref, k_ref, v_ref, u_ref, xn_ref, carry_ref,
                    *, seq_len, tiles_per_seq, rows_span_sequences):
    i = pl.program_id(0)
    part = pl.program_id(1)
    j = pl.program_id(2)
    tm = x_ref.shape[0]

    @pl.when((part == 0) & (j == 0))
    def _():
        xn_ref[...] = _rms(x_ref[...], g_ref[...]).astype(BF16)

    xn = xn_ref[...]
    za = jnp.dot(xn, wa_ref[...], preferred_element_type=F32)
    zb = jnp.dot(xn, wb_ref[...], preferred_element_type=F32)
    zc = jnp.dot(xn, wc_ref[...], preferred_element_type=F32)

    @pl.when(part == 0)
    def _():
        u = zb * zc
        row = lax.broadcasted_iota(jnp.int32, u.shape, 0)
        if rows_span_sequences:
            rmod = row % seq_len
            pm2, pm1 = prev_ref[0], prev_ref[1]
            first, second = rmod == 0, rmod == 1
            u_ref[...] = u
        else:
            seq_start = (i % tiles_per_seq) == 0
            tail = carry_ref[j]
            pm2 = jnp.where(seq_start, prev_ref[0, 0:1, :], tail[SUBLANES - 2:SUBLANES - 1, :])
            pm1 = jnp.where(seq_start, prev_ref[0, 1:2, :], tail[SUBLANES - 1:SUBLANES, :])
            first, second = row == 0, row == 1
            new_tail = u[tm - SUBLANES:, :]
            carry_ref[j] = new_tail
            u_ref[0] = new_tail
        u1 = jnp.where(first, pm1, pltpu.roll(u, 1, 0))
        u2 = jnp.where(first, pm2, jnp.where(second, pm1, pltpu.roll(u, 2, 0)))
        cw = cw_ref[...]
        y = cw[0:1, :] * u2 + cw[1:2, :] * u1 + cw[2:3, :] * u
        a_ref[...] = (za * y).astype(BF16)

    @pl.when(part == 1)
    def _():
        q_ref[...] = za.astype(BF16)
        k_ref[...] = zb
        v_ref[...] = zc


def _in_proj(x2d, norm_g, w_in_bf, conv_w, prev, *, seq_len, tm, tn):
    n, d = x2d.shape
    d_conv = conv_w.shape[1]
    assert w_in_bf.shape == (d, 6 * d_conv), "conv and attention widths must match"
    assert n % tm == 0 and d_conv % tn == 0
    nj = d_conv // tn
    n_i = n // tm
    rows_span_sequences = tm > seq_len
    if rows_span_sequences:
        assert tm % seq_len == 0 and prev.shape == (2, n, d_conv)
        tiles_per_seq = 1
        prev_spec = pl.BlockSpec((2, tm, tn), lambda i, p, j: (0, i, j))
        u_shape = jax.ShapeDtypeStruct((n, d_conv), F32)
        u_spec = pl.BlockSpec((tm, tn), lambda i, p, j: (i, jnp.where(p == 0, j, nj - 1)))
    else:
        assert seq_len % tm == 0 and prev.shape == (n // seq_len, CONV_WIDTH - 1, d_conv)
        tiles_per_seq = seq_len // tm
        prev_spec = pl.BlockSpec((1, CONV_WIDTH - 1, tn), lambda i, p, j: (i // tiles_per_seq, 0, j))
        u_shape = jax.ShapeDtypeStruct((n_i, SUBLANES, d_conv), F32)
        u_spec = pl.BlockSpec((1, SUBLANES, tn), lambda i, p, j: (i, 0, jnp.where(p == 0, j, nj - 1)))

    def wspec(k):
        return pl.BlockSpec((d, tn), lambda i, p, j: (0, p * 3 * nj + k * nj + j))

    conv_out = lambda i, p, j: (i, jnp.where(p == 0, j, nj - 1))
    attn_out = lambda i, p, j: (i, jnp.where(p == 1, j, 0))
    kern = functools.partial(_in_proj_kernel, seq_len=seq_len, tiles_per_seq=tiles_per_seq,
                             rows_span_sequences=rows_span_sequences)
    return pl.pallas_call(
        kern,
        grid=(n_i, 2, nj),
        in_specs=[
            pl.BlockSpec((tm, d), lambda i, p, j: (i, 0)),
            pl.BlockSpec((1, d), lambda i, p, j: (0, 0)),
            wspec(0), wspec(1), wspec(2),
            pl.BlockSpec((CONV_WIDTH, tn), lambda i, p, j: (0, j)),
            prev_spec,
        ],
        out_specs=[
            pl.BlockSpec((tm, tn), conv_out),
            pl.BlockSpec((tm, tn), attn_out),
            pl.BlockSpec((tm, tn), attn_out),
            pl.BlockSpec((tm, tn), attn_out),
            u_spec,
        ],
        out_shape=[
            jax.ShapeDtypeStruct((n, d_conv), BF16),
            jax.ShapeDtypeStruct((n, d_conv), BF16),
            jax.ShapeDtypeStruct((n, d_conv), F32),
            jax.ShapeDtypeStruct((n, d_conv), F32),
            u_shape,
        ],
        scratch_shapes=[pltpu.VMEM((tm, d), BF16), pltpu.VMEM((nj, SUBLANES, tn), F32)],
        compiler_params=_params(3),
        name="in_proj",
    )(x2d, norm_g, w_in_bf, w_in_bf, w_in_bf, conv_w, prev)


def _diff_lambda(lq1_ref, lk1_ref, lq2_ref, lk2_ref, lam_init):
    s1 = jnp.sum(lq1_ref[...] * lk1_ref[...], axis=1, keepdims=True)
    s2 = jnp.sum(lq2_ref[...] * lk2_ref[...], axis=1, keepdims=True)
    return jnp.exp(s1) - jnp.exp(s2) + lam_init


def _stacked_queries(q, head_dim):
    lane = lax.broadcasted_iota(jnp.int32, q.shape, 1)
    qs = q * jnp.asarray(head_dim ** -0.5, q.dtype)
    zero = jnp.zeros_like(qs)
    return jnp.concatenate([jnp.where(lane < head_dim, qs, zero),
                            jnp.where(lane >= head_dim, qs, zero)], axis=0)


def _scores(qq, kblk):
    return lax.dot_general(qq, kblk, (((1,), (1,)), ((), ())), preferred_element_type=F32)


def _finish_heads(acc, l, lam, hn, lam_init, tq):
    o = acc[:tq] / l[:tq] - lam * (acc[tq:] / l[tq:])
    return (_rms(o, hn) * (1.0 - lam_init)).astype(BF16)


def _attn_prompt_kernel(lq1_ref, lk1_ref, lq2_ref, lk2_ref, hn_ref, q_ref, k_ref, v_ref,
                        o_ref, kb_ref, vb_ref, *, lam_init, head_dim):
    qi = pl.program_id(2)
    tq = q_ref.shape[0]

    @pl.when(qi == 0)
    def _():
        kb_ref[...] = k_ref[...].astype(BF16)
        vb_ref[...] = v_ref[...].astype(BF16)

    lam = _diff_lambda(lq1_ref, lk1_ref, lq2_ref, lk2_ref, lam_init)
    qq = _stacked_queries(q_ref[...], head_dim)

    def step(kj, carry, masked):
        m, l, acc = carry
        start = pl.multiple_of(kj * tq, tq)
        s = _scores(qq, kb_ref[pl.ds(start, tq), :])
        if masked:
            row = lax.broadcasted_iota(jnp.int32, s.shape, 0)
            col = lax.broadcasted_iota(jnp.int32, s.shape, 1)
            s = jnp.where((col // CHUNK) <= ((row % tq) // CHUNK), s, NEG_INF)
        m_new = jnp.maximum(m, jnp.max(s, axis=1, keepdims=True))
        alpha = jnp.exp(m - m_new)
        p = jnp.exp(s - m_new)
        l = alpha * l + jnp.sum(p, axis=1, keepdims=True)
        acc = alpha * acc + jnp.dot(p.astype(BF16), vb_ref[pl.ds(start, tq), :],
                                    preferred_element_type=F32)
        return m_new, l, acc

    init = (jnp.full((2 * tq, 1), NEG_INF, F32), jnp.zeros((2 * tq, 1), F32),
            jnp.zeros((2 * tq, v_ref.shape[1]), F32))
    carry = lax.fori_loop(0, qi, lambda kj, c: step(kj, c, False), init)
    _, l, acc = step(qi, carry, True)
    o_ref[...] = _finish_heads(acc, l, lam, hn_ref[...], lam_init, tq)


def _attn_prompt(q, k, v, lams, head_norm, *, batch, seq_len, lam_init, tq):
    n, d_attn = q.shape
    hw = d_attn // N_HEADS
    nq = seq_len // tq
    assert seq_len % tq == 0 and tq % CHUNK == 0
    small = pl.BlockSpec((1, hw // 2), lambda b, h, qi: (0, 0))
    kern = functools.partial(_attn_prompt_kernel, lam_init=lam_init, head_dim=hw // 2)
    return pl.pallas_call(
        kern,
        grid=(batch, N_HEADS, nq),
        in_specs=[small, small, small, small,
                  pl.BlockSpec((1, hw), lambda b, h, qi: (0, 0)),
                  pl.BlockSpec((tq, hw), lambda b, h, qi: (b * nq + qi, h)),
                  pl.BlockSpec((seq_len, hw), lambda b, h, qi: (b, h)),
                  pl.BlockSpec((seq_len, hw), lambda b, h, qi: (b, h))],
        out_specs=pl.BlockSpec((tq, hw), lambda b, h, qi: (b * nq + qi, h)),
        out_shape=jax.ShapeDtypeStruct((n, d_attn), BF16),
        scratch_shapes=[pltpu.VMEM((seq_len, hw), BF16), pltpu.VMEM((seq_len, hw), BF16)],
        compiler_params=_params(3),
        name="attn_prompt",
    )(*lams, head_norm, q, k, v)


def _attn_sample_kernel(lq1_ref, lk1_ref, lq2_ref, lk2_ref, hn_ref, q_ref, kc_ref, vc_ref,
                        kn_ref, vn_ref, o_ref, *, lam_init, head_dim):
    tq = q_ref.shape[0]
    lam = _diff_lambda(lq1_ref, lk1_ref, lq2_ref, lk2_ref, lam_init)
    qq = _stacked_queries(q_ref[...], head_dim)
    s_past = _scores(qq, kc_ref[...].astype(BF16))
    s_new = _scores(qq, kn_ref[...].astype(BF16))
    m = jnp.maximum(jnp.max(s_past, axis=1, keepdims=True), jnp.max(s_new, axis=1, keepdims=True))
    p_past = jnp.exp(s_past - m)
    p_new = jnp.exp(s_new - m)
    l = jnp.sum(p_past, axis=1, keepdims=True) + jnp.sum(p_new, axis=1, keepdims=True)
    acc = (jnp.dot(p_past.astype(BF16), vc_ref[...].astype(BF16), preferred_element_type=F32)
           + jnp.dot(p_new.astype(BF16), vn_ref[...].astype(BF16), preferred_element_type=F32))
    o_ref[...] = _finish_heads(acc, l, lam, hn_ref[...], lam_init, tq)


def _attn_sample(q, k_new, v_new, k_past, v_past, lams, head_norm, *, batch, seq_len, past_len, lam_init):
    n, d_attn = q.shape
    hw = d_attn // N_HEADS
    small = pl.BlockSpec((1, hw // 2), lambda b, h: (0, 0))
    new_spec = pl.BlockSpec((seq_len, hw), lambda b, h: (b, h))
    past_spec = pl.BlockSpec((past_len, hw), lambda b, h: (b, h))
    kern = functools.partial(_attn_sample_kernel, lam_init=lam_init, head_dim=hw // 2)
    return pl.pallas_call(
        kern,
        grid=(batch, N_HEADS),
        in_specs=[small, small, small, small, pl.BlockSpec((1, hw), lambda b, h: (0, 0)),
                  new_spec, past_spec, past_spec, new_spec, new_spec],
        out_specs=new_spec,
        out_shape=jax.ShapeDtypeStruct((n, d_attn), BF16),
        compiler_params=_params(2),
        name="attn_sample",
    )(*lams, head_norm, q, k_past, v_past, k_new, v_new)


R_E1, R_E2, R_W1, R_W2, R_RANK1, R_RANK2 = 0, 1, 2, 3, 4, 5


def _pack_bf16_pairs(x):
    c = x.shape[1] // 2
    lo = lax.bitcast_convert_type(x[:, :c].astype(BF16).astype(F32), U32)
    hi = lax.bitcast_convert_type(x[:, c:].astype(BF16).astype(F32), U32)
    return (hi & jnp.uint32(0xFFFF0000)) | (lo >> 16)


def _unpack_bf16_pairs(w):
    lo = lax.bitcast_convert_type(w << 16, F32).astype(BF16)
    hi = lax.bitcast_convert_type(w & jnp.uint32(0xFFFF0000), F32).astype(BF16)
    return lo, hi


def _out_proj_kernel(ap_ref, op_ref, xp_ref, as_ref, os_ref, xs_ref, wt_ref, wb_ref, g_ref,
                     wr_ref, br_ref, h_ref, hnp_ref, rinfo_ref, cnt_ref, carry_ref, *, n_prompt_tiles):
    i = pl.program_id(0)
    is_prompt = i < n_prompt_tiles
    a = jnp.where(is_prompt, ap_ref[...], as_ref[...])
    o = jnp.where(is_prompt, op_ref[...], os_ref[...])
    x = jnp.where(is_prompt, xp_ref[...], xs_ref[...])
    tm = x.shape[0]

    @pl.when(i == 0)
    def _():
        carry_ref[...] = jnp.zeros_like(carry_ref)

    y = (jnp.dot(a, wt_ref[...], preferred_element_type=F32)
         + jnp.dot(o, wb_ref[...], preferred_element_type=F32))
    h = x + y
    h_ref[...] = h
    hn = _rms(h, g_ref[...])
    hnp_ref[...] = _pack_bf16_pairs(hn)

    logits = jnp.dot(hn.astype(BF16), wr_ref[...], preferred_element_type=F32) + br_ref[...]
    lane = lax.broadcasted_iota(jnp.int32, logits.shape, 1)
    lane_f = lane.astype(F32)
    lg = jnp.where(lane < N_GROUPS, logits, NEG_INF)
    mg = jnp.max(lg, axis=1, keepdims=True)
    grp = jnp.min(jnp.where(lg == mg, lane_f, float(LANES)), axis=1, keepdims=True)
    gate_g = 1.0 / jnp.sum(jnp.exp(lg - mg), axis=1, keepdims=True)
    ex = lane - N_GROUPS
    in_grp = (ex >= 0) & (ex < N_EXPERTS) & ((ex >> 3).astype(F32) == grp)
    le = jnp.where(in_grp, logits, NEG_INF)
    v1 = jnp.max(le, axis=1, keepdims=True)
    i1 = jnp.min(jnp.where(le == v1, lane_f, float(LANES)), axis=1, keepdims=True)
    le2 = jnp.where(lane_f == i1, NEG_INF, le)
    v2 = jnp.max(le2, axis=1, keepdims=True)
    i2 = jnp.min(jnp.where(le2 == v2, lane_f, float(LANES)), axis=1, keepdims=True)
    e21 = jnp.exp(v2 - v1)
    w1 = gate_g / (1.0 + e21)
    w2 = gate_g * e21 / (1.0 + e21)
    e1 = i1 - float(N_GROUPS)
    e2 = i2 - float(N_GROUPS)

    onehot = jnp.where((lane_f == e1) | (lane_f == e2), 1.0, 0.0)
    r = lax.broadcasted_iota(jnp.int32, (tm, tm), 0)
    c = lax.broadcasted_iota(jnp.int32, (tm, tm), 1)
    earlier = jnp.where(c < r, 1.0, 0.0).astype(BF16)
    before = jnp.dot(earlier, onehot.astype(BF16), preferred_element_type=F32) + carry_ref[0:1, :]
    rank1 = _lane_pick(before, lane_f, e1)
    rank2 = _lane_pick(before, lane_f, e2)
    new_carry = carry_ref[0:1, :] + jnp.sum(onehot, axis=0, keepdims=True)
    carry_ref[...] = jnp.broadcast_to(new_carry, carry_ref.shape)
    cnt_ref[...] = jnp.broadcast_to(new_carry, cnt_ref.shape)

    rec = jnp.zeros_like(logits)
    for idx, val in ((R_E1, e1), (R_E2, e2), (R_W1, w1), (R_W2, w2), (R_RANK1, rank1), (R_RANK2, rank2)):
        rec = jnp.where(lane == idx, val, rec)
    rinfo_ref[...] = rec


def _out_proj(a_p, o_p, x_p, a_s, o_s, x_s, w_out_bf, norm_g, w_router, b_router, *, tm):
    n_p, d = x_p.shape
    n_s = x_s.shape[0]
    dc = a_p.shape[1]
    assert n_p % tm == 0 and n_s == tm, "sample rows must fill exactly one tile"
    npt = n_p // tm
    n_all = n_p + n_s
    p_idx = lambda i: (jnp.minimum(i, npt - 1), 0)
    zero = lambda i: (0, 0)
    kern = functools.partial(_out_proj_kernel, n_prompt_tiles=npt)
    return pl.pallas_call(
        kern,
        grid=(npt + 1,),
        in_specs=[
            pl.BlockSpec((tm, dc), p_idx), pl.BlockSpec((tm, dc), p_idx), pl.BlockSpec((tm, d), p_idx),
            pl.BlockSpec((tm, dc), zero), pl.BlockSpec((tm, dc), zero), pl.BlockSpec((tm, d), zero),
            pl.BlockSpec((dc, d), zero), pl.BlockSpec((dc, d), lambda i: (1, 0)),
            pl.BlockSpec((1, d), zero),
            pl.BlockSpec((d, LANES), zero), pl.BlockSpec((1, LANES), zero),
        ],
        out_specs=[
            pl.BlockSpec((tm, d), lambda i: (i, 0)),
            pl.BlockSpec((tm, d // 2), lambda i: (i, 0)),
            pl.BlockSpec((tm, LANES), lambda i: (i, 0)),
            pl.BlockSpec((SUBLANES, LANES), zero),
        ],
        out_shape=[
            jax.ShapeDtypeStruct((n_all, d), F32),
            jax.ShapeDtypeStruct((n_all, d // 2), U32),
            jax.ShapeDtypeStruct((n_all, LANES), F32),
            jax.ShapeDtypeStruct((SUBLANES, LANES), F32),
        ],
        scratch_shapes=[pltpu.VMEM((SUBLANES, LANES), F32)],
        compiler_params=_params(1),
        name="out_proj",
    )(a_p, o_p, x_p, a_s, o_s, x_s, w_out_bf, w_out_bf, norm_g, w_router, b_router)


def _dispatch_kernel(pos_hbm, hnp_ref, xs_hbm, idx_ref, idx_sem, row_sem):
    i = pl.program_id(0)
    tm = hnp_ref.shape[0]
    idx_copy = pltpu.make_async_copy(pos_hbm.at[pl.ds(i, 1), :], idx_ref, idx_sem)
    idx_copy.start()
    idx_copy.wait()

    def row_copy(r, slot):
        dst = idx_ref[0, slot * tm + r]
        return pltpu.make_async_copy(hnp_ref.at[pl.ds(r, 1), :], xs_hbm.at[pl.ds(dst, 1), :], row_sem)

    def start(r, _):
        row_copy(r, 0).start()
        row_copy(r, 1).start()
        return 0

    def wait(r, _):
        row_copy(r, 0).wait()
        row_copy(r, 1).wait()
        return 0

    lax.fori_loop(0, tm, start, 0, unroll=8)
    lax.fori_loop(0, tm, wait, 0, unroll=8)


def _dispatch(pos_tiles, hnp, *, tm):
    n_all, dw = hnp.shape
    nt = n_all // tm
    assert n_all % tm == 0 and pos_tiles.shape == (nt, 2 * tm)
    return pl.pallas_call(
        _dispatch_kernel,
        grid=(nt,),
        in_specs=[pl.BlockSpec(memory_space=pl.ANY), pl.BlockSpec((tm, dw), lambda i: (i, 0))],
        out_specs=pl.BlockSpec(memory_space=pl.ANY),
        out_shape=jax.ShapeDtypeStruct((2 * n_all, dw), U32),
        scratch_shapes=[pltpu.SMEM((1, 2 * tm), jnp.int32), pltpu.SemaphoreType.DMA(()),
                        pltpu.SemaphoreType.DMA(())],
        compiler_params=_params(1),
        name="dispatch",
    )(pos_tiles, hnp)


def _experts_kernel(item_tile_ref, item_expert_ref, n_items_ref, starts_ref,
                    xs_ref, wi_ref, wo_ref, ys_ref):
    w = pl.program_id(0)
    tm = xs_ref.shape[0]
    t = item_tile_ref[w]
    e = item_expert_ref[w]

    @pl.when(w < n_items_ref[0])
    def _():
        lo, hi = _unpack_bf16_pairs(xs_ref[...])
        half = lo.shape[1]
        gu = (jnp.dot(lo, wi_ref[0, :half, :], preferred_element_type=F32)
              + jnp.dot(hi, wi_ref[0, half:, :], preferred_element_type=F32))
        f = gu.shape[1] // 2
        gate, up = gu[:, :f], gu[:, f:]
        act = (gate * jax.nn.sigmoid(gate) * up).astype(BF16)
        y = jnp.dot(act, wo_ref[0], preferred_element_type=F32)
        row = t * tm + lax.broadcasted_iota(jnp.int32, (tm, 1), 0)
        mine = (row >= starts_ref[e]) & (row < starts_ref[e + 1])
        first = (w == 0) | (item_tile_ref[jnp.maximum(w - 1, 0)] != t)

        @pl.when(first)
        def _():
            ys_ref[...] = jnp.where(mine, y, 0.0)

        @pl.when(jnp.logical_not(first))
        def _():
            ys_ref[...] = jnp.where(mine, y, ys_ref[...])


def _experts(item_tile, item_expert, n_items, starts, xs, w_ei_bf, w_eo_bf, *, tm):
    n2, dw = xs.shape
    n_e, d, f2 = w_ei_bf.shape
    n_work = item_tile.shape[0]
    grid_spec = pltpu.PrefetchScalarGridSpec(
        num_scalar_prefetch=4,
        grid=(n_work,),
        in_specs=[
            pl.BlockSpec((tm, dw), lambda w, it, ie, ni, st: (it[w], 0)),
            pl.BlockSpec((1, d, f2), lambda w, it, ie, ni, st: (ie[w], 0, 0)),
            pl.BlockSpec((1, f2 // 2, d), lambda w, it, ie, ni, st: (ie[w], 0, 0)),
        ],
        out_specs=pl.BlockSpec((tm, d), lambda w, it, ie, ni, st: (it[w], 0)),
    )
    return pl.pallas_call(
        _experts_kernel,
        grid_spec=grid_spec,
        out_shape=jax.ShapeDtypeStruct((n2, d), F32),
        compiler_params=_params(1),
        name="experts",
    )(item_tile, item_expert, n_items, starts, xs, w_ei_bf, w_eo_bf)


def _expert_work_items(counts, n_rows, tm):
    n_tiles = n_rows // tm
    n_work = n_tiles + N_EXPERTS - 1
    ends = jnp.cumsum(counts)
    starts = ends - counts
    first_tile = starts // tm
    last_tile = jnp.maximum(ends - 1, 0) // tm
    per_expert = jnp.where(counts > 0, last_tile - first_tile + 1, 0)
    item_end = jnp.cumsum(per_expert)
    n_items = item_end[-1]
    w = jnp.arange(n_work, dtype=jnp.int32)
    wc = jnp.minimum(w, n_items - 1)
    expert = jnp.searchsorted(item_end, wc, side="right").astype(jnp.int32)
    tile = (first_tile[expert] + wc - (item_end[expert] - per_expert[expert])).astype(jnp.int32)
    starts_ext = jnp.concatenate([starts, ends[-1:]]).astype(jnp.int32)
    return tile, expert, n_items.reshape(1).astype(jnp.int32), starts_ext


def _combine_kernel(pos_hbm, h_ref, rinfo_ref, g_ref, ys_hbm, yp_ref, ysm_ref,
                    idx_ref, buf_ref, idx_sem, row_sem, *, n_prompt_tiles):
    i = pl.program_id(0)
    tm = h_ref.shape[0]
    idx_copy = pltpu.make_async_copy(pos_hbm.at[pl.ds(i, 1), :], idx_ref, idx_sem)
    idx_copy.start()
    idx_copy.wait()

    def row_copy(r, slot):
        src = idx_ref[0, slot * tm + r]
        return pltpu.make_async_copy(ys_hbm.at[pl.ds(src, 1), :], buf_ref.at[slot, pl.ds(r, 1), :], row_sem)

    def start(r, _):
        row_copy(r, 0).start()
        row_copy(r, 1).start()
        return 0

    def wait(r, _):
        row_copy(r, 0).wait()
        row_copy(r, 1).wait()
        return 0

    lax.fori_loop(0, tm, start, 0, unroll=8)
    lax.fori_loop(0, tm, wait, 0, unroll=8)

    rinfo = rinfo_ref[...]
    lane = lax.broadcasted_iota(jnp.int32, rinfo.shape, 1)
    w1 = _lane_pick(rinfo, lane, R_W1)
    w2 = _lane_pick(rinfo, lane, R_W2)
    hp = h_ref[...] + (w1 * buf_ref[0] + w2 * buf_ref[1])
    y = _rms(hp, g_ref[...])

    @pl.when(i < n_prompt_tiles)
    def _():
        yp_ref[...] = y

    @pl.when(i >= n_prompt_tiles)
    def _():
        ysm_ref[...] = y


def _combine(pos_tiles, h, rinfo, norm_g, ys, *, n_prompt, tm):
    n_all, d = h.shape
    nt = n_all // tm
    npt = n_prompt // tm
    assert n_all - n_prompt == tm
    kern = functools.partial(_combine_kernel, n_prompt_tiles=npt)
    return pl.pallas_call(
        kern,
        grid=(nt,),
        in_specs=[
            pl.BlockSpec(memory_space=pl.ANY),
            pl.BlockSpec((tm, d), lambda i: (i, 0)),
            pl.BlockSpec((tm, LANES), lambda i: (i, 0)),
            pl.BlockSpec((1, d), lambda i: (0, 0)),
            pl.BlockSpec(memory_space=pl.ANY),
        ],
        out_specs=[
            pl.BlockSpec((tm, d), lambda i: (jnp.minimum(i, npt - 1), 0)),
            pl.BlockSpec((tm, d), lambda i: (0, 0)),
        ],
        out_shape=[jax.ShapeDtypeStruct((n_prompt, d), F32), jax.ShapeDtypeStruct((tm, d), F32)],
        scratch_shapes=[pltpu.SMEM((1, 2 * tm), jnp.int32), pltpu.VMEM((2, tm, d), F32),
                        pltpu.SemaphoreType.DMA(()), pltpu.SemaphoreType.DMA(())],
        compiler_params=_params(1),
        name="combine",
    )(pos_tiles, h, rinfo, norm_g, ys)


ROW_TILE = 256
PROJ_ROWS = 512
PROJ_COLS = 512
ATTN_Q_ROWS = 256


def kernel(x_prompt, x_sample, cache_k, cache_v, state_conv, norm_mix, w_in, conv_w, lambda_q1,
           lambda_k1, lambda_q2, lambda_k2, head_norm, w_out, norm_ffn, w_router_group,
           b_router_group, w_router_expert, b_router_expert, w_expert_in, w_expert_out, norm_final):
    depth = w_in.shape[0]
    b, t, d = x_prompt.shape
    db, dt, _ = x_sample.shape
    past = cache_k.shape[2]
    d_conv = conv_w.shape[2]
    d_attn = d - d_conv
    n_p, n_s = b * t, db * dt
    n_all = n_p + n_s
    tm = ROW_TILE

    hp = x_prompt.reshape(n_p, d)
    hs = x_sample.reshape(n_s, d)
    outs = {name: [] for name in ("kp", "vp", "cp", "ks", "vs", "cs")}
    y_p = y_s = None
    for l in range(depth):
        lam_init = 0.8 - 0.6 * math.exp(-0.3 * l)
        w_in_bf = w_in[l].astype(BF16)
        w_out_bf = w_out[l].astype(BF16)
        w_ei_bf = w_expert_in[l].astype(BF16)
        w_eo_bf = w_expert_out[l].astype(BF16)
        g_mix = norm_mix[l].reshape(1, d)
        lams = tuple(p[l].reshape(1, -1) for p in (lambda_q1, lambda_k1, lambda_q2, lambda_k2))
        hnorm = head_norm[l].reshape(1, -1)
        pad = LANES - N_GROUPS - N_EXPERTS
        w_router = jnp.concatenate([w_router_group[l], w_router_expert[l], jnp.zeros((d, pad), F32)],
                                   axis=1).astype(BF16)
        b_router = jnp.concatenate([b_router_group[l], b_router_expert[l], jnp.zeros((pad,), F32)]).reshape(1, LANES)

        zero_prev = jnp.zeros((b, CONV_WIDTH - 1, d_conv), F32)
        a_p, q_p, k_p, v_p, tail_p = _in_proj(hp, g_mix, w_in_bf, conv_w[l], zero_prev,
                                              seq_len=t, tm=min(PROJ_ROWS, t), tn=PROJ_COLS)
        o_p = _attn_prompt(q_p, k_p, v_p, lams, hnorm, batch=b, seq_len=t, lam_init=lam_init,
                           tq=min(ATTN_Q_ROWS, t))
        prev_rows = jnp.stack([jnp.repeat(state_conv[l][:, 0], dt, axis=0),
                               jnp.repeat(state_conv[l][:, 1], dt, axis=0)])
        a_s, q_s, k_s, v_s, u_s = _in_proj(hs, g_mix, w_in_bf, conv_w[l], prev_rows,
                                           seq_len=dt, tm=n_s, tn=PROJ_COLS)
        o_s = _attn_sample(q_s, k_s, v_s, cache_k[l].reshape(db * past, d_attn),
                           cache_v[l].reshape(db * past, d_attn), lams, hnorm,
                           batch=db, seq_len=dt, past_len=past, lam_init=lam_init)

        h, hnp, rinfo, cnt = _out_proj(a_p, o_p, hp, a_s, o_s, hs, w_out_bf,
                                       norm_ffn[l].reshape(1, d), w_router, b_router, tm=tm)
        counts = cnt[0, :N_EXPERTS].astype(jnp.int32)
        starts = jnp.cumsum(counts) - counts
        e12 = rinfo[:, R_E1:R_E2 + 1].astype(jnp.int32)
        r12 = rinfo[:, R_RANK1:R_RANK2 + 1].astype(jnp.int32)
        pos = starts[e12] + r12
        pos_tiles = pos.reshape(n_all // tm, tm, 2).transpose(0, 2, 1).reshape(n_all // tm, 2 * tm)

        xs = _dispatch(pos_tiles, hnp, tm=tm)
        items = _expert_work_items(counts, 2 * n_all, tm)
        ys = _experts(*items, xs, w_ei_bf, w_eo_bf, tm=tm)
        last = l == depth - 1
        g_out = norm_final.reshape(1, d) if last else jnp.ones((1, d), F32)
        y_p, y_s = _combine(pos_tiles, h, rinfo, g_out, ys, n_prompt=n_p, tm=tm)
        assert last, "multi-layer stacks need an un-normalised combine output"

        outs["kp"].append(k_p.reshape(b, t, N_HEADS, 2, d_attn // (2 * N_HEADS)))
        outs["vp"].append(v_p.reshape(b, t, N_HEADS, d_attn // N_HEADS))
        tails = tail_p.reshape(b, t // min(PROJ_ROWS, t), SUBLANES, d_conv)
        outs["cp"].append(tails[:, -1, SUBLANES - (CONV_WIDTH - 1):, :])
        outs["ks"].append(k_s.reshape(db, dt, N_HEADS, 2, d_attn // (2 * N_HEADS)))
        outs["vs"].append(v_s.reshape(db, dt, N_HEADS, d_attn // N_HEADS))
        outs["cs"].append(u_s.reshape(db, dt, d_conv)[:, dt - (CONV_WIDTH - 1):, :])

    return (y_p.reshape(b, t, d), y_s.reshape(db, dt, d),
            jnp.stack(outs["kp"]), jnp.stack(outs["vp"]), jnp.stack(outs["cp"]),
            jnp.stack(outs["ks"]), jnp.stack(outs["vs"]), jnp.stack(outs["cs"]))
```

```python
import functools
import math

import jax
import jax.numpy as jnp
from jax import lax
from jax.experimental import pallas as pl
from jax.experimental.pallas import tpu as pltpu

EPS = 1e-6
N_HEADS = 8
CHUNK = 64
CONV_WIDTH = 3
N_GROUPS = 4
EXPERTS_PER_GROUP = 8
N_EXPERTS = N_GROUPS * EXPERTS_PER_GROUP
LANES = 128
SUBLANES = 8
VMEM_LIMIT = 56 * 1024 * 1024

F32 = jnp.float32
BF16 = jnp.bfloat16
U32 = jnp.uint32
NEG_INF = float("-inf")


def _params(n_axes, vmem=VMEM_LIMIT, **kw):
    return pltpu.CompilerParams(dimension_semantics=("arbitrary",) * n_axes,
                                vmem_limit_bytes=vmem, **kw)


def _rms(x, g):
    return x * lax.rsqrt(jnp.mean(x * x, axis=-1, keepdims=True) + EPS) * g


def _lane_pick(x, lane, idx):
    return jnp.sum(jnp.where(lane == idx, x, 0.0), axis=1, keepdims=True)


def _in_proj_kernel(x_ref, g_ref, wa_ref, wb_ref, wc_ref, cw_ref, prev_ref,
                    a_ref, q_ref, kb_ref, ko_ref, v_ref, u_ref, xn_ref, carry_ref,
                    *, seq_len, tiles_per_seq, rows_span_sequences):
    i = pl.program_id(0)
    part = pl.program_id(1)
    j = pl.program_id(2)
    tm = x_ref.shape[0]

    @pl.when((part == 0) & (j == 0))
    def _():
        xn_ref[...] = _rms(x_ref[...], g_ref[...]).astype(BF16)

    xn = xn_ref[...]
    za = jnp.dot(xn, wa_ref[...], preferred_element_type=F32)
    zb = jnp.dot(xn, wb_ref[...], preferred_element_type=F32)
    zc = jnp.dot(xn, wc_ref[...], preferred_element_type=F32)

    @pl.when(part == 0)
    def _():
        u = zb * zc
        row = lax.broadcasted_iota(jnp.int32, u.shape, 0)
        if rows_span_sequences:
            rmod = row % seq_len
            pm2, pm1 = prev_ref[0], prev_ref[1]
            first, second = rmod == 0, rmod == 1
            u_ref[...] = u
        else:
            seq_start = (i % tiles_per_seq) == 0
            tail = carry_ref[j]
            pm2 = jnp.where(seq_start, prev_ref[0, 0:1, :], tail[SUBLANES - 2:SUBLANES - 1, :])
            pm1 = jnp.where(seq_start, prev_ref[0, 1:2, :], tail[SUBLANES - 1:SUBLANES, :])
            first, second = row == 0, row == 1
            new_tail = u[tm - SUBLANES:, :]
            carry_ref[j] = new_tail
            u_ref[0] = new_tail
        u1 = jnp.where(first, pm1, pltpu.roll(u, 1, 0))
        u2 = jnp.where(first, pm2, jnp.where(second, pm1, pltpu.roll(u, 2, 0)))
        cw = cw_ref[...]
        y = cw[0:1, :] * u2 + cw[1:2, :] * u1 + cw[2:3, :] * u
        a_ref[...] = (za * y).astype(BF16)

    @pl.when(part == 1)
    def _():
        q_ref[...] = za.astype(BF16)
        kb_ref[...] = zb.astype(BF16)
        if rows_span_sequences:
            ko_ref[...] = zb
        else:
            ko_ref[0] = zb.T
        v_ref[...] = zc


def _in_proj(x2d, norm_g, w_in_bf, conv_w, prev, *, seq_len, tm, tn):
    n, d = x2d.shape
    d_conv = conv_w.shape[1]
    assert w_in_bf.shape == (d, 6 * d_conv), "conv and attention widths must match"
    assert n % tm == 0 and d_conv % tn == 0
    nj = d_conv // tn
    n_i = n // tm
    rows_span_sequences = tm > seq_len
    if rows_span_sequences:
        assert tm % seq_len == 0 and prev.shape == (2, n, d_conv)
        tiles_per_seq = 1
        prev_spec = pl.BlockSpec((2, tm, tn), lambda i, p, j: (0, i, j))
        u_shape = jax.ShapeDtypeStruct((n, d_conv), F32)
        u_spec = pl.BlockSpec((tm, tn), lambda i, p, j: (i, jnp.where(p == 0, j, nj - 1)))
        ko_shape = jax.ShapeDtypeStruct((n, d_conv), F32)
        ko_spec = pl.BlockSpec((tm, tn), lambda i, p, j: (i, jnp.where(p == 1, j, 0)))
    else:
        assert seq_len % tm == 0 and prev.shape == (n // seq_len, CONV_WIDTH - 1, d_conv)
        tiles_per_seq = seq_len // tm
        prev_spec = pl.BlockSpec((1, CONV_WIDTH - 1, tn), lambda i, p, j: (i // tiles_per_seq, 0, j))
        u_shape = jax.ShapeDtypeStruct((n_i, SUBLANES, d_conv), F32)
        u_spec = pl.BlockSpec((1, SUBLANES, tn), lambda i, p, j: (i, 0, jnp.where(p == 0, j, nj - 1)))
        ko_shape = jax.ShapeDtypeStruct((n // seq_len, d_conv, seq_len), F32)
        ko_spec = pl.BlockSpec((1, tn, tm), lambda i, p, j: (i // tiles_per_seq, jnp.where(p == 1, j, 0),
                                                             i % tiles_per_seq))

    def wspec(k):
        return pl.BlockSpec((d, tn), lambda i, p, j: (0, p * 3 * nj + k * nj + j))

    conv_out = lambda i, p, j: (i, jnp.where(p == 0, j, nj - 1))
    attn_out = lambda i, p, j: (i, jnp.where(p == 1, j, 0))
    kern = functools.partial(_in_proj_kernel, seq_len=seq_len, tiles_per_seq=tiles_per_seq,
                             rows_span_sequences=rows_span_sequences)
    return pl.pallas_call(
        kern,
        grid=(n_i, 2, nj),
        in_specs=[
            pl.BlockSpec((tm, d), lambda i, p, j: (i, 0)),
            pl.BlockSpec((1, d), lambda i, p, j: (0, 0)),
            wspec(0), wspec(1), wspec(2),
            pl.BlockSpec((CONV_WIDTH, tn), lambda i, p, j: (0, j)),
            prev_spec,
        ],
        out_specs=[
            pl.BlockSpec((tm, tn), conv_out),
            pl.BlockSpec((tm, tn), attn_out),
            pl.BlockSpec((tm, tn), attn_out),
            ko_spec,
            pl.BlockSpec((tm, tn), attn_out),
            u_spec,
        ],
        out_shape=[
            jax.ShapeDtypeStruct((n, d_conv), BF16),
            jax.ShapeDtypeStruct((n, d_conv), BF16),
            jax.ShapeDtypeStruct((n, d_conv), BF16),
            ko_shape,
            jax.ShapeDtypeStruct((n, d_conv), F32),
            u_shape,
        ],
        scratch_shapes=[pltpu.VMEM((tm, d), BF16), pltpu.VMEM((nj, SUBLANES, tn), F32)],
        compiler_params=_params(3),
        name="in_proj",
    )(x2d, norm_g, w_in_bf, w_in_bf, w_in_bf, conv_w, prev)


def _diff_lambda(lq1_ref, lk1_ref, lq2_ref, lk2_ref, lam_init):
    s1 = jnp.sum(lq1_ref[...] * lk1_ref[...], axis=1, keepdims=True)
    s2 = jnp.sum(lq2_ref[...] * lk2_ref[...], axis=1, keepdims=True)
    return jnp.exp(s1) - jnp.exp(s2) + lam_init


def _stacked_queries(q, head_dim):
    lane = lax.broadcasted_iota(jnp.int32, q.shape, 1)
    qs = q * jnp.asarray(head_dim ** -0.5, q.dtype)
    zero = jnp.zeros_like(qs)
    return jnp.concatenate([jnp.where(lane < head_dim, qs, zero),
                            jnp.where(lane >= head_dim, qs, zero)], axis=0)


def _scores(qq, kblk):
    return lax.dot_general(qq, kblk, (((1,), (1,)), ((), ())), preferred_element_type=F32)


def _finish_heads(acc, l, lam, hn, lam_init, tq):
    o = acc[:tq] / l[:tq] - lam * (acc[tq:] / l[tq:])
    return (_rms(o, hn) * (1.0 - lam_init)).astype(BF16)


def _attn_prompt_kernel(lq1_ref, lk1_ref, lq2_ref, lk2_ref, hn_ref, q_ref, k_ref, v_ref, o_ref,
                        qt1_ref, qt2_ref, vt_ref, s0_ref, s1_ref, p0_ref, p1_ref,
                        *, lam_init, head_dim, tq):
    seq_len, hw = q_ref.shape
    lam = _diff_lambda(lq1_ref, lk1_ref, lq2_ref, lk2_ref, lam_init)

    qt = (q_ref[...].astype(F32) * (head_dim ** -0.5)).T
    dim = lax.broadcasted_iota(jnp.int32, qt.shape, 0)
    qt1_ref[...] = jnp.where(dim < head_dim, qt, 0.0).astype(BF16)
    qt2_ref[...] = jnp.where(dim >= head_dim, qt, 0.0).astype(BF16)
    vt_ref[...] = v_ref[...].T.astype(BF16)

    key_chunk = lax.broadcasted_iota(jnp.int32, (tq, 2 * tq), 0) // CHUNK
    qry_chunk = (lax.broadcasted_iota(jnp.int32, (tq, 2 * tq), 1) % tq) // CHUNK
    visible = key_chunk <= qry_chunk

    bufs = ((s0_ref, p0_ref), (s1_ref, p1_ref))
    n_blocks = seq_len // tq

    def scores(qi):
        q0, n_keys, s_buf = qi * tq, (qi + 1) * tq, bufs[qi % 2][0]
        qqt = jnp.concatenate([qt1_ref[:, q0:n_keys], qt2_ref[:, q0:n_keys]], axis=1)
        s = jnp.dot(k_ref[q0:n_keys, :], qqt, preferred_element_type=F32)
        s = jnp.where(visible, s, NEG_INF)
        s_buf[q0:n_keys, :] = s
        m = jnp.max(s, axis=0, keepdims=True)
        if qi > 0:
            s = jnp.dot(k_ref[:q0, :], qqt, preferred_element_type=F32)
            s_buf[:q0, :] = s
            m = jnp.maximum(m, jnp.max(s, axis=0, keepdims=True))
        return m

    def finish(qi, m):
        q0, n_keys = qi * tq, (qi + 1) * tq
        s_buf, p_buf = bufs[qi % 2]
        p = jnp.exp(s_buf[:n_keys, :] - m)
        p_buf[:n_keys, :] = p.astype(BF16)
        l = jnp.sum(p, axis=0, keepdims=True)
        ot = jnp.dot(vt_ref[:, :n_keys], p_buf[:n_keys, :], preferred_element_type=F32)
        ot = ot / l
        o = (ot[:, :tq] - lam * ot[:, tq:]).T
        o_ref[q0:n_keys, :] = (_rms(o, hn_ref[...]) * (1.0 - lam_init)).astype(BF16)

    m = scores(0)
    for qi in range(n_blocks):
        m_next = scores(qi + 1) if qi + 1 < n_blocks else None
        finish(qi, m)
        m = m_next


def _attn_prompt(q, k, v, lams, head_norm, *, batch, seq_len, lam_init, tq):
    n, d_attn = q.shape
    hw = d_attn // N_HEADS
    assert seq_len % tq == 0 and tq % CHUNK == 0
    small = pl.BlockSpec((1, hw // 2), lambda b, h: (0, 0))
    head = pl.BlockSpec((seq_len, hw), lambda b, h: (b, h))
    kern = functools.partial(_attn_prompt_kernel, lam_init=lam_init, head_dim=hw // 2, tq=tq)
    return pl.pallas_call(
        kern,
        grid=(batch, N_HEADS),
        in_specs=[small, small, small, small, pl.BlockSpec((1, hw), lambda b, h: (0, 0)),
                  head, head, head],
        out_specs=head,
        out_shape=jax.ShapeDtypeStruct((n, d_attn), BF16),
        scratch_shapes=[pltpu.VMEM((hw, seq_len), BF16), pltpu.VMEM((hw, seq_len), BF16),
                        pltpu.VMEM((hw, seq_len), BF16),
                        pltpu.VMEM((seq_len, 2 * tq), F32), pltpu.VMEM((seq_len, 2 * tq), F32),
                        pltpu.VMEM((seq_len, 2 * tq), BF16), pltpu.VMEM((seq_len, 2 * tq), BF16)],
        compiler_params=_params(2),
        name="attn_prompt",
    )(*lams, head_norm, q, k, v)


def _attn_sample_kernel(lq1_ref, lk1_ref, lq2_ref, lk2_ref, hn_ref, q_ref, kc_ref, vc_ref,
                        kn_ref, vn_ref, o_ref, *, lam_init, head_dim):
    tq = q_ref.shape[0]
    lam = _diff_lambda(lq1_ref, lk1_ref, lq2_ref, lk2_ref, lam_init)
    qq = _stacked_queries(q_ref[...], head_dim)
    s_past = jnp.dot(qq, kc_ref[...].astype(BF16), preferred_element_type=F32)
    s_new = _scores(qq, kn_ref[...])
    m = jnp.maximum(jnp.max(s_past, axis=1, keepdims=True), jnp.max(s_new, axis=1, keepdims=True))
    p_past = jnp.exp(s_past - m)
    p_new = jnp.exp(s_new - m)
    l = jnp.sum(p_past, axis=1, keepdims=True) + jnp.sum(p_new, axis=1, keepdims=True)
    acc = (jnp.dot(p_past.astype(BF16), vc_ref[...].astype(BF16), preferred_element_type=F32)
           + jnp.dot(p_new.astype(BF16), vn_ref[...].astype(BF16), preferred_element_type=F32))
    o_ref[...] = _finish_heads(acc, l, lam, hn_ref[...], lam_init, tq)


def _attn_sample(q, k_new, v_new, k_past, v_past, lams, head_norm, *, batch, seq_len, past_len, lam_init):
    n, d_attn = q.shape
    hw = d_attn // N_HEADS
    small = pl.BlockSpec((1, hw // 2), lambda b, h: (0, 0))
    new_spec = pl.BlockSpec((seq_len, hw), lambda b, h: (b, h))
    past_spec = pl.BlockSpec((past_len, hw), lambda b, h: (b, h))
    past_t_spec = pl.BlockSpec((hw, past_len), lambda b, h: (b * N_HEADS + h, 0))
    kern = functools.partial(_attn_sample_kernel, lam_init=lam_init, head_dim=hw // 2)
    return pl.pallas_call(
        kern,
        grid=(batch, N_HEADS),
        in_specs=[small, small, small, small, pl.BlockSpec((1, hw), lambda b, h: (0, 0)),
                  new_spec, past_t_spec, past_spec, new_spec, new_spec],
        out_specs=new_spec,
        out_shape=jax.ShapeDtypeStruct((n, d_attn), BF16),
        compiler_params=_params(2),
        name="attn_sample",
    )(*lams, head_norm, q, k_past, v_past, k_new, v_new)


R_E1, R_E2, R_W1, R_W2, R_RANK1, R_RANK2 = 0, 1, 2, 3, 4, 5


def _pack_bf16_pairs(x):
    c = x.shape[1] // 2
    lo = lax.bitcast_convert_type(x[:, :c].astype(BF16).astype(F32), U32)
    hi = lax.bitcast_convert_type(x[:, c:].astype(BF16).astype(F32), U32)
    return (hi & jnp.uint32(0xFFFF0000)) | (lo >> 16)


def _unpack_bf16_pairs(w):
    lo = lax.bitcast_convert_type(w << 16, F32).astype(BF16)
    hi = lax.bitcast_convert_type(w & jnp.uint32(0xFFFF0000), F32).astype(BF16)
    return lo, hi


def _out_proj_kernel(ap_ref, op_ref, xp_ref, as_ref, os_ref, xs_ref, wt_ref, wb_ref, g_ref,
                     wr_ref, br_ref, h_ref, hnp_ref, rinfo_ref, cnt_ref, carry_ref, *, n_prompt_tiles):
    i = pl.program_id(0)
    is_prompt = i < n_prompt_tiles
    a = jnp.where(is_prompt, ap_ref[...], as_ref[...])
    o = jnp.where(is_prompt, op_ref[...], os_ref[...])
    x = jnp.where(is_prompt, xp_ref[...], xs_ref[...])
    tm = x.shape[0]

    @pl.when(i == 0)
    def _():
        carry_ref[...] = jnp.zeros_like(carry_ref)

    y = (jnp.dot(a, wt_ref[...], preferred_element_type=F32)
         + jnp.dot(o, wb_ref[...], preferred_element_type=F32))
    h = x + y
    h_ref[...] = h
    hn = _rms(h, g_ref[...])
    hnp_ref[...] = _pack_bf16_pairs(hn)

    logits = jnp.dot(hn.astype(BF16), wr_ref[...], preferred_element_type=F32) + br_ref[...]
    lane = lax.broadcasted_iota(jnp.int32, logits.shape, 1)
    lane_f = lane.astype(F32)
    lg = jnp.where(lane < N_GROUPS, logits, NEG_INF)
    mg = jnp.max(lg, axis=1, keepdims=True)
    grp = jnp.min(jnp.where(lg == mg, lane_f, float(LANES)), axis=1, keepdims=True)
    gate_g = 1.0 / jnp.sum(jnp.exp(lg - mg), axis=1, keepdims=True)
    ex = lane - N_GROUPS
    in_grp = (ex >= 0) & (ex < N_EXPERTS) & ((ex >> 3).astype(F32) == grp)
    le = jnp.where(in_grp, logits, NEG_INF)
    v1 = jnp.max(le, axis=1, keepdims=True)
    i1 = jnp.min(jnp.where(le == v1, lane_f, float(LANES)), axis=1, keepdims=True)
    le2 = jnp.where(lane_f == i1, NEG_INF, le)
    v2 = jnp.max(le2, axis=1, keepdims=True)
    i2 = jnp.min(jnp.where(le2 == v2, lane_f, float(LANES)), axis=1, keepdims=True)
    e21 = jnp.exp(v2 - v1)
    w1 = gate_g / (1.0 + e21)
    w2 = gate_g * e21 / (1.0 + e21)
    e1 = i1 - float(N_GROUPS)
    e2 = i2 - float(N_GROUPS)

    onehot = jnp.where((lane_f == e1) | (lane_f == e2), 1.0, 0.0)
    r = lax.broadcasted_iota(jnp.int32, (tm, tm), 0)
    c = lax.broadcasted_iota(jnp.int32, (tm, tm), 1)
    earlier = jnp.where(c < r, 1.0, 0.0).astype(BF16)
    before = jnp.dot(earlier, onehot.astype(BF16), preferred_element_type=F32) + carry_ref[0:1, :]
    rank1 = _lane_pick(before, lane_f, e1)
    rank2 = _lane_pick(before, lane_f, e2)
    new_carry = carry_ref[0:1, :] + jnp.sum(onehot, axis=0, keepdims=True)
    carry_ref[...] = jnp.broadcast_to(new_carry, carry_ref.shape)
    cnt_ref[...] = jnp.broadcast_to(new_carry, cnt_ref.shape)

    rec = jnp.zeros_like(logits)
    for idx, val in ((R_E1, e1), (R_E2, e2), (R_W1, w1), (R_W2, w2), (R_RANK1, rank1), (R_RANK2, rank2)):
        rec = jnp.where(lane == idx, val, rec)
    rinfo_ref[...] = rec


def _out_proj(a_p, o_p, x_p, a_s, o_s, x_s, w_out_bf, norm_g, w_router, b_router, *, tm):
    n_p, d = x_p.shape
    n_s = x_s.shape[0]
    dc = a_p.shape[1]
    assert n_p % tm == 0 and n_s == tm, "sample rows must fill exactly one tile"
    npt = n_p // tm
    n_all = n_p + n_s
    p_idx = lambda i: (jnp.minimum(i, npt - 1), 0)
    zero = lambda i: (0, 0)
    kern = functools.partial(_out_proj_kernel, n_prompt_tiles=npt)
    return pl.pallas_call(
        kern,
        grid=(npt + 1,),
        in_specs=[
            pl.BlockSpec((tm, dc), p_idx), pl.BlockSpec((tm, dc), p_idx), pl.BlockSpec((tm, d), p_idx),
            pl.BlockSpec((tm, dc), zero), pl.BlockSpec((tm, dc), zero), pl.BlockSpec((tm, d), zero),
            pl.BlockSpec((dc, d), zero), pl.BlockSpec((dc, d), lambda i: (1, 0)),
            pl.BlockSpec((1, d), zero),
            pl.BlockSpec((d, LANES), zero), pl.BlockSpec((1, LANES), zero),
        ],
        out_specs=[
            pl.BlockSpec((tm, d), lambda i: (i, 0)),
            pl.BlockSpec((tm, d // 2), lambda i: (i, 0)),
            pl.BlockSpec((tm, LANES), lambda i: (i, 0)),
            pl.BlockSpec((SUBLANES, LANES), zero),
        ],
        out_shape=[
            jax.ShapeDtypeStruct((n_all, d), F32),
            jax.ShapeDtypeStruct((n_all, d // 2), U32),
            jax.ShapeDtypeStruct((n_all, LANES), F32),
            jax.ShapeDtypeStruct((SUBLANES, LANES), F32),
        ],
        scratch_shapes=[pltpu.VMEM((SUBLANES, LANES), F32)],
        compiler_params=_params(1),
        name="out_proj",
    )(a_p, o_p, x_p, a_s, o_s, x_s, w_out_bf, w_out_bf, norm_g, w_router, b_router)


def _dispatch_kernel(pos_hbm, hnp_ref, xs_hbm, idx_ref, idx_sem, row_sem):
    i = pl.program_id(0)
    tm = hnp_ref.shape[0]
    idx_copy = pltpu.make_async_copy(pos_hbm.at[pl.ds(i, 1), :], idx_ref, idx_sem)
    idx_copy.start()
    idx_copy.wait()

    def row_copy(r, slot):
        dst = idx_ref[0, slot * tm + r]
        return pltpu.make_async_copy(hnp_ref.at[pl.ds(r, 1), :], xs_hbm.at[pl.ds(dst, 1), :], row_sem)

    def start(r, _):
        row_copy(r, 0).start()
        row_copy(r, 1).start()
        return 0

    def wait(r, _):
        row_copy(r, 0).wait()
        row_copy(r, 1).wait()
        return 0

    lax.fori_loop(0, tm, start, 0, unroll=8)
    lax.fori_loop(0, tm, wait, 0, unroll=8)


def _dispatch(pos_tiles, hnp, *, tm):
    n_all, dw = hnp.shape
    nt = n_all // tm
    assert n_all % tm == 0 and pos_tiles.shape == (nt, 2 * tm)
    return pl.pallas_call(
        _dispatch_kernel,
        grid=(nt,),
        in_specs=[pl.BlockSpec(memory_space=pl.ANY), pl.BlockSpec((tm, dw), lambda i: (i, 0))],
        out_specs=pl.BlockSpec(memory_space=pl.ANY),
        out_shape=jax.ShapeDtypeStruct((2 * n_all, dw), U32),
        scratch_shapes=[pltpu.SMEM((1, 2 * tm), jnp.int32), pltpu.SemaphoreType.DMA(()),
                        pltpu.SemaphoreType.DMA(())],
        compiler_params=_params(1),
        name="dispatch",
    )(pos_tiles, hnp)


def _experts_kernel(item_tile_ref, item_expert_ref, n_items_ref, starts_ref,
                    xs_ref, wi_ref, wo_ref, ys_ref):
    w = pl.program_id(0)
    tm = xs_ref.shape[0]
    t = item_tile_ref[w]
    e = item_expert_ref[w]

    @pl.when(w < n_items_ref[0])
    def _():
        lo, hi = _unpack_bf16_pairs(xs_ref[...])
        half = lo.shape[1]
        gu = (jnp.dot(lo, wi_ref[0, :half, :], preferred_element_type=F32)
              + jnp.dot(hi, wi_ref[0, half:, :], preferred_element_type=F32))
        f = gu.shape[1] // 2
        gate, up = gu[:, :f], gu[:, f:]
        act = (gate * jax.nn.sigmoid(gate) * up).astype(BF16)
        y = jnp.dot(act, wo_ref[0], preferred_element_type=F32)
        row = t * tm + lax.broadcasted_iota(jnp.int32, (tm, 1), 0)
        mine = (row >= starts_ref[e]) & (row < starts_ref[e + 1])
        first = (w == 0) | (item_tile_ref[jnp.maximum(w - 1, 0)] != t)

        @pl.when(first)
        def _():
            ys_ref[...] = jnp.where(mine, y, 0.0)

        @pl.when(jnp.logical_not(first))
        def _():
            ys_ref[...] = jnp.where(mine, y, ys_ref[...])


def _experts(item_tile, item_expert, n_items, starts, xs, w_ei_bf, w_eo_bf, *, tm):
    n2, dw = xs.shape
    n_e, d, f2 = w_ei_bf.shape
    n_work = item_tile.shape[0]
    grid_spec = pltpu.PrefetchScalarGridSpec(
        num_scalar_prefetch=4,
        grid=(n_work,),
        in_specs=[
            pl.BlockSpec((tm, dw), lambda w, it, ie, ni, st: (it[w], 0)),
            pl.BlockSpec((1, d, f2), lambda w, it, ie, ni, st: (ie[w], 0, 0)),
            pl.BlockSpec((1, f2 // 2, d), lambda w, it, ie, ni, st: (ie[w], 0, 0)),
        ],
        out_specs=pl.BlockSpec((tm, d), lambda w, it, ie, ni, st: (it[w], 0)),
    )
    return pl.pallas_call(
        _experts_kernel,
        grid_spec=grid_spec,
        out_shape=jax.ShapeDtypeStruct((n2, d), F32),
        compiler_params=_params(1),
        name="experts",
    )(item_tile, item_expert, n_items, starts, xs, w_ei_bf, w_eo_bf)


def _expert_work_items(counts, n_rows, tm):
    n_tiles = n_rows // tm
    n_work = n_tiles + N_EXPERTS - 1
    ends = jnp.cumsum(counts)
    starts = ends - counts
    first_tile = starts // tm
    last_tile = jnp.maximum(ends - 1, 0) // tm
    per_expert = jnp.where(counts > 0, last_tile - first_tile + 1, 0)
    item_end = jnp.cumsum(per_expert)
    n_items = item_end[-1]
    w = jnp.arange(n_work, dtype=jnp.int32)
    wc = jnp.minimum(w, n_items - 1)
    expert = jnp.sum(item_end[None, :] <= wc[:, None], axis=1).astype(jnp.int32)
    tile = (first_tile[expert] + wc - (item_end[expert] - per_expert[expert])).astype(jnp.int32)
    starts_ext = jnp.concatenate([starts, ends[-1:]]).astype(jnp.int32)
    return tile, expert, n_items.reshape(1).astype(jnp.int32), starts_ext


def _combine_kernel(pos_hbm, h_ref, rinfo_ref, g_ref, ys_hbm, yp_ref, ysm_ref,
                    idx_ref, buf_ref, idx_sem, row_sem, *, n_prompt_tiles):
    i = pl.program_id(0)
    tm = h_ref.shape[0]
    idx_copy = pltpu.make_async_copy(pos_hbm.at[pl.ds(i, 1), :], idx_ref, idx_sem)
    idx_copy.start()
    idx_copy.wait()

    def row_copy(r, slot):
        src = idx_ref[0, slot * tm + r]
        return pltpu.make_async_copy(ys_hbm.at[pl.ds(src, 1), :], buf_ref.at[slot, pl.ds(r, 1), :], row_sem)

    def start(r, _):
        row_copy(r, 0).start()
        row_copy(r, 1).start()
        return 0

    def wait(r, _):
        row_copy(r, 0).wait()
        row_copy(r, 1).wait()
        return 0

    lax.fori_loop(0, tm, start, 0, unroll=8)
    lax.fori_loop(0, tm, wait, 0, unroll=8)

    rinfo = rinfo_ref[...]
    lane = lax.broadcasted_iota(jnp.int32, rinfo.shape, 1)
    w1 = _lane_pick(rinfo, lane, R_W1)
    w2 = _lane_pick(rinfo, lane, R_W2)
    hp = h_ref[...] + (w1 * buf_ref[0] + w2 * buf_ref[1])
    y = _rms(hp, g_ref[...])

    @pl.when(i < n_prompt_tiles)
    def _():
        yp_ref[...] = y

    @pl.when(i >= n_prompt_tiles)
    def _():
        ysm_ref[...] = y


def _combine(pos_tiles, h, rinfo, norm_g, ys, *, n_prompt, tm):
    n_all, d = h.shape
    nt = n_all // tm
    npt = n_prompt // tm
    assert n_all - n_prompt == tm
    kern = functools.partial(_combine_kernel, n_prompt_tiles=npt)
    return pl.pallas_call(
        kern,
        grid=(nt,),
        in_specs=[
            pl.BlockSpec(memory_space=pl.ANY),
            pl.BlockSpec((tm, d), lambda i: (i, 0)),
            pl.BlockSpec((tm, LANES), lambda i: (i, 0)),
            pl.BlockSpec((1, d), lambda i: (0, 0)),
            pl.BlockSpec(memory_space=pl.ANY),
        ],
        out_specs=[
            pl.BlockSpec((tm, d), lambda i: (jnp.minimum(i, npt - 1), 0)),
            pl.BlockSpec((tm, d), lambda i: (0, 0)),
        ],
        out_shape=[jax.ShapeDtypeStruct((n_prompt, d), F32), jax.ShapeDtypeStruct((tm, d), F32)],
        scratch_shapes=[pltpu.SMEM((1, 2 * tm), jnp.int32), pltpu.VMEM((2, tm, d), F32),
                        pltpu.SemaphoreType.DMA(()), pltpu.SemaphoreType.DMA(())],
        compiler_params=_params(1),
        name="combine",
    )(pos_tiles, h, rinfo, norm_g, ys)


ROW_TILE = 256
PROJ_ROWS = 512
PROJ_COLS = 512
ATTN_Q_ROWS = 256


def kernel(x_prompt, x_sample, cache_k, cache_v, state_conv, norm_mix, w_in, conv_w, lambda_q1,
           lambda_k1, lambda_q2, lambda_k2, head_norm, w_out, norm_ffn, w_router_group,
           b_router_group, w_router_expert, b_router_expert, w_expert_in, w_expert_out, norm_final):
    depth = w_in.shape[0]
    b, t, d = x_prompt.shape
    db, dt, _ = x_sample.shape
    past = cache_k.shape[2]
    d_conv = conv_w.shape[2]
    d_attn = d - d_conv
    n_p, n_s = b * t, db * dt
    n_all = n_p + n_s
    tm = ROW_TILE

    hp = x_prompt.reshape(n_p, d)
    hs = x_sample.reshape(n_s, d)
    outs = {name: [] for name in ("kp", "vp", "cp", "ks", "vs", "cs")}
    y_p = y_s = None
    for l in range(depth):
        lam_init = 0.8 - 0.6 * math.exp(-0.3 * l)
        w_in_bf = w_in[l].astype(BF16)
        w_out_bf = w_out[l].astype(BF16)
        w_ei_bf = w_expert_in[l].astype(BF16)
        w_eo_bf = w_expert_out[l].astype(BF16)
        g_mix = norm_mix[l].reshape(1, d)
        lams = tuple(p[l].reshape(1, -1) for p in (lambda_q1, lambda_k1, lambda_q2, lambda_k2))
        hnorm = head_norm[l].reshape(1, -1)
        pad = LANES - N_GROUPS - N_EXPERTS
        w_router = jnp.concatenate([w_router_group[l], w_router_expert[l], jnp.zeros((d, pad), F32)],
                                   axis=1).astype(BF16)
        b_router = jnp.concatenate([b_router_group[l], b_router_expert[l], jnp.zeros((pad,), F32)]).reshape(1, LANES)

        zero_prev = jnp.zeros((b, CONV_WIDTH - 1, d_conv), F32)
        a_p, q_p, kb_p, kt_p, v_p, tail_p = _in_proj(hp, g_mix, w_in_bf, conv_w[l], zero_prev,
                                                     seq_len=t, tm=min(PROJ_ROWS, t), tn=PROJ_COLS)
        o_p = _attn_prompt(q_p, kb_p, v_p, lams, hnorm, batch=b, seq_len=t, lam_init=lam_init,
                           tq=min(ATTN_Q_ROWS, t))
        prev_rows = jnp.stack([jnp.repeat(state_conv[l][:, 0], dt, axis=0),
                               jnp.repeat(state_conv[l][:, 1], dt, axis=0)])
        a_s, q_s, kb_s, k_s, v_s, u_s = _in_proj(hs, g_mix, w_in_bf, conv_w[l], prev_rows,
                                                 seq_len=dt, tm=n_s, tn=PROJ_COLS)
        kt_past = cache_k[l].transpose(0, 2, 3, 4, 1).reshape(db * d_attn, past)
        o_s = _attn_sample(q_s, kb_s, v_s, kt_past, cache_v[l].reshape(db * past, d_attn), lams, hnorm,
                           batch=db, seq_len=dt, past_len=past, lam_init=lam_init)

        h, hnp, rinfo, cnt = _out_proj(a_p, o_p, hp, a_s, o_s, hs, w_out_bf,
                                       norm_ffn[l].reshape(1, d), w_router, b_router, tm=tm)
        counts = cnt[0, :N_EXPERTS].astype(jnp.int32)
        starts = jnp.cumsum(counts) - counts
        e12 = rinfo[:, R_E1:R_E2 + 1].astype(jnp.int32)
        r12 = rinfo[:, R_RANK1:R_RANK2 + 1].astype(jnp.int32)
        pos = starts[e12] + r12
        pos_tiles = pos.reshape(n_all // tm, tm, 2).transpose(0, 2, 1).reshape(n_all // tm, 2 * tm)

        xs = _dispatch(pos_tiles, hnp, tm=tm)
        items = _expert_work_items(counts, 2 * n_all, tm)
        ys = _experts(*items, xs, w_ei_bf, w_eo_bf, tm=tm)
        last = l == depth - 1
        g_out = norm_final.reshape(1, d) if last else jnp.ones((1, d), F32)
        y_p, y_s = _combine(pos_tiles, h, rinfo, g_out, ys, n_prompt=n_p, tm=tm)
        assert last, "multi-layer stacks need an un-normalised combine output"

        outs["kp"].append(kt_p.reshape(b, N_HEADS, 2, d_attn // (2 * N_HEADS), t).transpose(0, 4, 1, 2, 3))
        outs["vp"].append(v_p.reshape(b, t, N_HEADS, d_attn // N_HEADS))
        tails = tail_p.reshape(b, t // min(PROJ_ROWS, t), SUBLANES, d_conv)
        outs["cp"].append(tails[:, -1, SUBLANES - (CONV_WIDTH - 1):, :])
        outs["ks"].append(k_s.reshape(db, dt, N_HEADS, 2, d_attn // (2 * N_HEADS)))
        outs["vs"].append(v_s.reshape(db, dt, N_HEADS, d_attn // N_HEADS))
        outs["cs"].append(u_s.reshape(db, dt, d_conv)[:, dt - (CONV_WIDTH - 1):, :])

    return (y_p.reshape(b, t, d), y_s.reshape(db, dt, d),
            jnp.stack(outs["kp"]), jnp.stack(outs["vp"]), jnp.stack(outs["cp"]),
            jnp.stack(outs["ks"]), jnp.stack(outs["vs"]), jnp.stack(outs["cs"]))
```

```python
import functools
import math

import jax
import jax.numpy as jnp
from jax import lax
from jax.experimental import pallas as pl
from jax.experimental.pallas import tpu as pltpu

EPS = 1e-6
N_HEADS = 8
CHUNK = 64
CONV_WIDTH = 3
N_GROUPS = 4
EXPERTS_PER_GROUP = 8
N_EXPERTS = N_GROUPS * EXPERTS_PER_GROUP
LANES = 128
SUBLANES = 8
BF16_SUBLANES = 16
VMEM_LIMIT = 56 * 1024 * 1024

F32 = jnp.float32
BF16 = jnp.bfloat16
U32 = jnp.uint32
NEG_INF = float("-inf")


def _params(n_axes, vmem=VMEM_LIMIT, **kw):
    return pltpu.CompilerParams(dimension_semantics=("arbitrary",) * n_axes,
                                vmem_limit_bytes=vmem, **kw)


def _rms(x, g):
    return x * lax.rsqrt(jnp.mean(x * x, axis=-1, keepdims=True) + EPS) * g


def _lane_pick(x, lane, idx):
    return jnp.sum(jnp.where(lane == idx, x, 0.0), axis=1, keepdims=True)


def _in_proj_kernel(x_ref, g_ref, wa_ref, wb_ref, wc_ref, cw_ref, prev_ref,
                    a_ref, q_ref, kb_ref, ko_ref, v_ref, u_ref, xn_ref, carry_ref,
                    *, seq_len, tiles_per_seq, rows_span_sequences, q_scale):
    i = pl.program_id(0)
    part = pl.program_id(1)
    j = pl.program_id(2)
    tm = x_ref.shape[0]

    @pl.when((part == 0) & (j == 0))
    def _():
        xn_ref[...] = _rms(x_ref[...], g_ref[...]).astype(BF16)

    xn = xn_ref[...]
    za = jnp.dot(xn, wa_ref[...], preferred_element_type=F32)
    zb = jnp.dot(xn, wb_ref[...], preferred_element_type=F32)
    zc = jnp.dot(xn, wc_ref[...], preferred_element_type=F32)

    @pl.when(part == 0)
    def _():
        u = zb * zc
        row = lax.broadcasted_iota(jnp.int32, u.shape, 0)
        if rows_span_sequences:
            rmod = row % seq_len
            pm2, pm1 = prev_ref[0], prev_ref[1]
            first, second = rmod == 0, rmod == 1
            u_ref[...] = u
        else:
            seq_start = (i % tiles_per_seq) == 0
            tail = carry_ref[j]
            pm2 = jnp.where(seq_start, prev_ref[0, 0:1, :], tail[SUBLANES - 2:SUBLANES - 1, :])
            pm1 = jnp.where(seq_start, prev_ref[0, 1:2, :], tail[SUBLANES - 1:SUBLANES, :])
            first, second = row == 0, row == 1
            new_tail = u[tm - SUBLANES:, :]
            carry_ref[j] = new_tail
            u_ref[0] = new_tail
        u1 = jnp.where(first, pm1, pltpu.roll(u, 1, 0))
        u2 = jnp.where(first, pm2, jnp.where(second, pm1, pltpu.roll(u, 2, 0)))
        cw = cw_ref[...]
        y = cw[0:1, :] * u2 + cw[1:2, :] * u1 + cw[2:3, :] * u
        a_ref[...] = (za * y).astype(BF16)

    @pl.when(part == 1)
    def _():
        q_ref[...] = (za * q_scale).astype(BF16)
        kb_ref[...] = zb.astype(BF16)
        if rows_span_sequences:
            ko_ref[...] = zb
        else:
            ko_ref[0] = zb.T
        v_ref[...] = zc


def _in_proj(x2d, norm_g, w_in_bf, conv_w, prev, *, seq_len, tm, tn):
    n, d = x2d.shape
    d_conv = conv_w.shape[1]
    assert w_in_bf.shape == (d, 6 * d_conv), "conv and attention widths must match"
    assert n % tm == 0 and d_conv % tn == 0
    nj = d_conv // tn
    n_i = n // tm
    rows_span_sequences = tm > seq_len
    if rows_span_sequences:
        assert tm % seq_len == 0 and prev.shape == (2, n, d_conv)
        tiles_per_seq = 1
        prev_spec = pl.BlockSpec((2, tm, tn), lambda i, p, j: (0, i, j))
        u_shape = jax.ShapeDtypeStruct((n, d_conv), F32)
        u_spec = pl.BlockSpec((tm, tn), lambda i, p, j: (i, jnp.where(p == 0, j, nj - 1)))
        ko_shape = jax.ShapeDtypeStruct((n, d_conv), F32)
        ko_spec = pl.BlockSpec((tm, tn), lambda i, p, j: (i, jnp.where(p == 1, j, 0)))
    else:
        assert seq_len % tm == 0 and prev.shape == (n // seq_len, CONV_WIDTH - 1, d_conv)
        tiles_per_seq = seq_len // tm
        prev_spec = pl.BlockSpec((1, CONV_WIDTH - 1, tn), lambda i, p, j: (i // tiles_per_seq, 0, j))
        u_shape = jax.ShapeDtypeStruct((n_i, SUBLANES, d_conv), F32)
        u_spec = pl.BlockSpec((1, SUBLANES, tn), lambda i, p, j: (i, 0, jnp.where(p == 0, j, nj - 1)))
        ko_shape = jax.ShapeDtypeStruct((n // seq_len, d_conv, seq_len), F32)
        ko_spec = pl.BlockSpec((1, tn, tm), lambda i, p, j: (i // tiles_per_seq, jnp.where(p == 1, j, 0),
                                                             i % tiles_per_seq))

    def wspec(k):
        return pl.BlockSpec((d, tn), lambda i, p, j: (0, p * 3 * nj + k * nj + j))

    conv_out = lambda i, p, j: (i, jnp.where(p == 0, j, nj - 1))
    attn_out = lambda i, p, j: (i, jnp.where(p == 1, j, 0))
    head_dim = d_conv // (2 * N_HEADS)
    kern = functools.partial(_in_proj_kernel, seq_len=seq_len, tiles_per_seq=tiles_per_seq,
                             rows_span_sequences=rows_span_sequences,
                             q_scale=head_dim ** -0.5 * math.log2(math.e))
    return pl.pallas_call(
        kern,
        grid=(n_i, 2, nj),
        in_specs=[
            pl.BlockSpec((tm, d), lambda i, p, j: (i, 0)),
            pl.BlockSpec((1, d), lambda i, p, j: (0, 0)),
            wspec(0), wspec(1), wspec(2),
            pl.BlockSpec((CONV_WIDTH, tn), lambda i, p, j: (0, j)),
            prev_spec,
        ],
        out_specs=[
            pl.BlockSpec((tm, tn), conv_out),
            pl.BlockSpec((tm, tn), attn_out),
            pl.BlockSpec((tm, tn), attn_out),
            ko_spec,
            pl.BlockSpec((tm, tn), attn_out),
            u_spec,
        ],
        out_shape=[
            jax.ShapeDtypeStruct((n, d_conv), BF16),
            jax.ShapeDtypeStruct((n, d_conv), BF16),
            jax.ShapeDtypeStruct((n, d_conv), BF16),
            ko_shape,
            jax.ShapeDtypeStruct((n, d_conv), F32),
            u_shape,
        ],
        scratch_shapes=[pltpu.VMEM((tm, d), BF16), pltpu.VMEM((nj, SUBLANES, tn), F32)],
        compiler_params=_params(3),
        name="in_proj",
    )(x2d, norm_g, w_in_bf, w_in_bf, w_in_bf, conv_w, prev)


def _diff_lambda(lq1_ref, lk1_ref, lq2_ref, lk2_ref, lam_init):
    s1 = jnp.sum(lq1_ref[...] * lk1_ref[...], axis=1, keepdims=True)
    s2 = jnp.sum(lq2_ref[...] * lk2_ref[...], axis=1, keepdims=True)
    return jnp.exp(s1) - jnp.exp(s2) + lam_init


def _stacked_queries(q, head_dim):
    lane = lax.broadcasted_iota(jnp.int32, q.shape, 1)
    zero = jnp.zeros_like(q)
    return jnp.concatenate([jnp.where(lane < head_dim, q, zero),
                            jnp.where(lane >= head_dim, q, zero)], axis=0)


def _scores(qq, kblk):
    return lax.dot_general(qq, kblk, (((1,), (1,)), ((), ())), preferred_element_type=F32)


def _finish_heads(acc, l, lam, hn, lam_init, tq):
    o = acc[:tq] / l[:tq] - lam * (acc[tq:] / l[tq:])
    return (_rms(o, hn) * (1.0 - lam_init)).astype(BF16)


def _attn_prompt_kernel(lq1_ref, lk1_ref, lq2_ref, lk2_ref, hn_ref, q_ref, k_ref, v_ref, o_ref,
                        qt1_ref, qt2_ref, vt_ref, s0_ref, s1_ref, p0_ref, p1_ref,
                        *, lam_init, head_dim, tq):
    seq_len, hw = q_ref.shape
    lam = _diff_lambda(lq1_ref, lk1_ref, lq2_ref, lk2_ref, lam_init)

    qt = q_ref[...].astype(F32).T
    dim = lax.broadcasted_iota(jnp.int32, qt.shape, 0)
    qt1_ref[...] = jnp.where(dim < head_dim, qt, 0.0).astype(BF16)
    qt2_ref[...] = jnp.where(dim >= head_dim, qt, 0.0).astype(BF16)
    vt_ref[:hw, :] = v_ref[...].T.astype(BF16)
    vt_ref[hw:, :] = jnp.ones((vt_ref.shape[0] - hw, seq_len), BF16)

    key_chunk = lax.broadcasted_iota(jnp.int32, (tq, 2 * tq), 0) // CHUNK
    qry_chunk = (lax.broadcasted_iota(jnp.int32, (tq, 2 * tq), 1) % tq) // CHUNK
    visible = key_chunk <= qry_chunk

    bufs = ((s0_ref, p0_ref), (s1_ref, p1_ref))
    n_blocks = seq_len // tq

    def scores(qi):
        q0, n_keys, s_buf = qi * tq, (qi + 1) * tq, bufs[qi % 2][0]
        qqt = jnp.concatenate([qt1_ref[:, q0:n_keys], qt2_ref[:, q0:n_keys]], axis=1)
        s = jnp.dot(k_ref[q0:n_keys, :], qqt, preferred_element_type=F32)
        s = jnp.where(visible, s, NEG_INF)
        s_buf[q0:n_keys, :] = s
        m = jnp.max(s, axis=0, keepdims=True)
        if qi > 0:
            s = jnp.dot(k_ref[:q0, :], qqt, preferred_element_type=F32)
            s_buf[:q0, :] = s
            m = jnp.maximum(m, jnp.max(s, axis=0, keepdims=True))
        return m

    def finish(qi, m):
        q0, n_keys = qi * tq, (qi + 1) * tq
        s_buf, p_buf = bufs[qi % 2]
        p_buf[:n_keys, :] = jnp.exp2(s_buf[:n_keys, :] - m).astype(BF16)
        ot = jnp.dot(vt_ref[:, :n_keys], p_buf[:n_keys, :], preferred_element_type=F32)
        ot = ot[:hw] / ot[hw:hw + 1]
        o = (ot[:, :tq] - lam * ot[:, tq:]).T
        o_ref[q0:n_keys, :] = (_rms(o, hn_ref[...]) * (1.0 - lam_init)).astype(BF16)

    m = scores(0)
    for qi in range(n_blocks):
        m_next = scores(qi + 1) if qi + 1 < n_blocks else None
        finish(qi, m)
        m = m_next


def _attn_prompt(q, k, v, lams, head_norm, *, batch, seq_len, lam_init, tq):
    n, d_attn = q.shape
    hw = d_attn // N_HEADS
    assert seq_len % tq == 0 and tq % CHUNK == 0
    small = pl.BlockSpec((1, hw // 2), lambda b, h: (0, 0))
    head = pl.BlockSpec((seq_len, hw), lambda b, h: (b, h))
    kern = functools.partial(_attn_prompt_kernel, lam_init=lam_init, head_dim=hw // 2, tq=tq)
    return pl.pallas_call(
        kern,
        grid=(batch, N_HEADS),
        in_specs=[small, small, small, small, pl.BlockSpec((1, hw), lambda b, h: (0, 0)),
                  head, head, head],
        out_specs=head,
        out_shape=jax.ShapeDtypeStruct((n, d_attn), BF16),
        scratch_shapes=[pltpu.VMEM((hw, seq_len), BF16), pltpu.VMEM((hw, seq_len), BF16),
                        pltpu.VMEM((hw + BF16_SUBLANES, seq_len), BF16),
                        pltpu.VMEM((seq_len, 2 * tq), F32), pltpu.VMEM((seq_len, 2 * tq), F32),
                        pltpu.VMEM((seq_len, 2 * tq), BF16), pltpu.VMEM((seq_len, 2 * tq), BF16)],
        compiler_params=_params(2),
        name="attn_prompt",
    )(*lams, head_norm, q, k, v)


def _attn_sample_kernel(lq1_ref, lk1_ref, lq2_ref, lk2_ref, hn_ref, q_ref, kc_ref, vc_ref,
                        kn_ref, vn_ref, o_ref, *, lam_init, head_dim):
    tq = q_ref.shape[0]
    lam = _diff_lambda(lq1_ref, lk1_ref, lq2_ref, lk2_ref, lam_init)
    qq = _stacked_queries(q_ref[...], head_dim)
    s_past = jnp.dot(qq, kc_ref[...].astype(BF16), preferred_element_type=F32)
    s_new = _scores(qq, kn_ref[...])
    m = jnp.maximum(jnp.max(s_past, axis=1, keepdims=True), jnp.max(s_new, axis=1, keepdims=True))
    p_past = jnp.exp2(s_past - m)
    p_new = jnp.exp2(s_new - m)
    l = jnp.sum(p_past, axis=1, keepdims=True) + jnp.sum(p_new, axis=1, keepdims=True)
    acc = (jnp.dot(p_past.astype(BF16), vc_ref[...].astype(BF16), preferred_element_type=F32)
           + jnp.dot(p_new.astype(BF16), vn_ref[...].astype(BF16), preferred_element_type=F32))
    o_ref[...] = _finish_heads(acc, l, lam, hn_ref[...], lam_init, tq)


def _attn_sample(q, k_new, v_new, k_past, v_past, lams, head_norm, *, batch, seq_len, past_len, lam_init):
    n, d_attn = q.shape
    hw = d_attn // N_HEADS
    small = pl.BlockSpec((1, hw // 2), lambda b, h: (0, 0))
    new_spec = pl.BlockSpec((seq_len, hw), lambda b, h: (b, h))
    past_spec = pl.BlockSpec((past_len, hw), lambda b, h: (b, h))
    past_t_spec = pl.BlockSpec((hw, past_len), lambda b, h: (b * N_HEADS + h, 0))
    kern = functools.partial(_attn_sample_kernel, lam_init=lam_init, head_dim=hw // 2)
    return pl.pallas_call(
        kern,
        grid=(batch, N_HEADS),
        in_specs=[small, small, small, small, pl.BlockSpec((1, hw), lambda b, h: (0, 0)),
                  new_spec, past_t_spec, past_spec, new_spec, new_spec],
        out_specs=new_spec,
        out_shape=jax.ShapeDtypeStruct((n, d_attn), BF16),
        compiler_params=_params(2),
        name="attn_sample",
    )(*lams, head_norm, q, k_past, v_past, k_new, v_new)


R_E1, R_E2, R_W1, R_W2, R_RANK1, R_RANK2 = 0, 1, 2, 3, 4, 5
OUT_PROJ_CHUNKS = 4


def _pack_bf16_pairs(lo_bf, hi_bf):
    lo = lax.bitcast_convert_type(lo_bf.astype(F32), U32)
    hi = lax.bitcast_convert_type(hi_bf.astype(F32), U32)
    return (hi & jnp.uint32(0xFFFF0000)) | (lo >> 16)


def _unpack_bf16_pairs(w):
    lo = lax.bitcast_convert_type(w << 16, F32).astype(BF16)
    hi = lax.bitcast_convert_type(w & jnp.uint32(0xFFFF0000), F32).astype(BF16)
    return lo, hi


def _out_proj_kernel(ap_ref, op_ref, xp_ref, as_ref, os_ref, xs_ref, w_ref, g_ref,
                     wr_ref, br_ref, h_ref, hnp_ref, rinfo_ref, cnt_ref, carry_ref, *, n_prompt_tiles):
    i = pl.program_id(0)

    @pl.when(i == 0)
    def _():
        carry_ref[...] = jnp.zeros_like(carry_ref)

    @pl.when(i < n_prompt_tiles)
    def _():
        _out_proj_rows(ap_ref, op_ref, xp_ref, w_ref, g_ref, wr_ref, br_ref,
                       h_ref, hnp_ref, rinfo_ref, cnt_ref, carry_ref)

    @pl.when(i >= n_prompt_tiles)
    def _():
        _out_proj_rows(as_ref, os_ref, xs_ref, w_ref, g_ref, wr_ref, br_ref,
                       h_ref, hnp_ref, rinfo_ref, cnt_ref, carry_ref)


def _out_proj_rows(a_ref, o_ref, x_ref, w_ref, g_ref, wr_ref, br_ref,
                   h_ref, hnp_ref, rinfo_ref, cnt_ref, carry_ref):
    tm, d = x_ref.shape
    cw = d // OUT_PROJ_CHUNKS
    a, o = a_ref[...], o_ref[...]

    ss = jnp.zeros((tm, 1), F32)
    for c in range(OUT_PROJ_CHUNKS):
        cols = slice(c * cw, (c + 1) * cw)
        h = (x_ref[:, cols] + jnp.dot(a, w_ref[0, c], preferred_element_type=F32)
             + jnp.dot(o, w_ref[1, c], preferred_element_type=F32))
        h_ref[:, cols] = h
        ss = ss + jnp.sum(h * h, axis=1, keepdims=True)
    inv = lax.rsqrt(ss / d + EPS)

    logits = br_ref[...]
    for c in range(OUT_PROJ_CHUNKS // 2):
        halves = []
        for cc in (c, c + OUT_PROJ_CHUNKS // 2):
            cols = slice(cc * cw, (cc + 1) * cw)
            hb = (h_ref[:, cols] * inv * g_ref[:, cols]).astype(BF16)
            logits = logits + jnp.dot(hb, wr_ref[cols, :], preferred_element_type=F32)
            halves.append(hb)
        hnp_ref[:, c * cw:(c + 1) * cw] = _pack_bf16_pairs(*halves)

    lane = lax.broadcasted_iota(jnp.int32, logits.shape, 1)
    lane_f = lane.astype(F32)
    lg = jnp.where(lane < N_GROUPS, logits, NEG_INF)
    mg = jnp.max(lg, axis=1, keepdims=True)
    grp = jnp.min(jnp.where(lg == mg, lane_f, float(LANES)), axis=1, keepdims=True)
    gate_g = 1.0 / jnp.sum(jnp.exp(lg - mg), axis=1, keepdims=True)
    ex = lane - N_GROUPS
    in_grp = (ex >= 0) & (ex < N_EXPERTS) & ((ex >> 3).astype(F32) == grp)
    le = jnp.where(in_grp, logits, NEG_INF)
    v1 = jnp.max(le, axis=1, keepdims=True)
    i1 = jnp.min(jnp.where(le == v1, lane_f, float(LANES)), axis=1, keepdims=True)
    le2 = jnp.where(lane_f == i1, NEG_INF, le)
    v2 = jnp.max(le2, axis=1, keepdims=True)
    i2 = jnp.min(jnp.where(le2 == v2, lane_f, float(LANES)), axis=1, keepdims=True)
    e21 = jnp.exp(v2 - v1)
    w1 = gate_g / (1.0 + e21)
    w2 = gate_g * e21 / (1.0 + e21)
    e1 = i1 - float(N_GROUPS)
    e2 = i2 - float(N_GROUPS)

    onehot = jnp.where((lane_f == e1) | (lane_f == e2), 1.0, 0.0)
    r = lax.broadcasted_iota(jnp.int32, (tm, tm), 0)
    c = lax.broadcasted_iota(jnp.int32, (tm, tm), 1)
    earlier = jnp.where(c < r, 1.0, 0.0).astype(BF16)
    before = jnp.dot(earlier, onehot.astype(BF16), preferred_element_type=F32) + carry_ref[0:1, :]
    rank1 = _lane_pick(before, lane_f, e1)
    rank2 = _lane_pick(before, lane_f, e2)
    new_carry = carry_ref[0:1, :] + jnp.sum(onehot, axis=0, keepdims=True)
    carry_ref[...] = jnp.broadcast_to(new_carry, carry_ref.shape)
    cnt_ref[...] = jnp.broadcast_to(new_carry, cnt_ref.shape)

    rec = jnp.zeros_like(logits)
    for idx, val in ((R_E1, e1), (R_E2, e2), (R_W1, w1), (R_W2, w2), (R_RANK1, rank1), (R_RANK2, rank2)):
        rec = jnp.where(lane == idx, val, rec)
    rinfo_ref[...] = rec


def _out_proj(a_p, o_p, x_p, a_s, o_s, x_s, w_out_chunks, norm_g, w_router, b_router, *, tm):
    n_p, d = x_p.shape
    n_s = x_s.shape[0]
    dc = a_p.shape[1]
    assert n_p % tm == 0 and n_s == tm, "sample rows must fill exactly one tile"
    npt = n_p // tm
    n_all = n_p + n_s
    p_idx = lambda i: (jnp.minimum(i, npt - 1), 0)
    zero = lambda i: (0, 0)
    kern = functools.partial(_out_proj_kernel, n_prompt_tiles=npt)
    return pl.pallas_call(
        kern,
        grid=(npt + 1,),
        in_specs=[
            pl.BlockSpec((tm, dc), p_idx), pl.BlockSpec((tm, dc), p_idx), pl.BlockSpec((tm, d), p_idx),
            pl.BlockSpec((tm, dc), zero), pl.BlockSpec((tm, dc), zero), pl.BlockSpec((tm, d), zero),
            pl.BlockSpec(w_out_chunks.shape, lambda i: (0, 0, 0, 0)),
            pl.BlockSpec((1, d), zero),
            pl.BlockSpec((d, LANES), zero), pl.BlockSpec((1, LANES), zero),
        ],
        out_specs=[
            pl.BlockSpec((tm, d), lambda i: (i, 0)),
            pl.BlockSpec((tm, d // 2), lambda i: (i, 0)),
            pl.BlockSpec((tm, LANES), lambda i: (i, 0)),
            pl.BlockSpec((SUBLANES, LANES), zero),
        ],
        out_shape=[
            jax.ShapeDtypeStruct((n_all, d), F32),
            jax.ShapeDtypeStruct((n_all, d // 2), U32),
            jax.ShapeDtypeStruct((n_all, LANES), F32),
            jax.ShapeDtypeStruct((SUBLANES, LANES), F32),
        ],
        scratch_shapes=[pltpu.VMEM((SUBLANES, LANES), F32)],
        compiler_params=_params(1),
        name="out_proj",
    )(a_p, o_p, x_p, a_s, o_s, x_s, w_out_chunks, norm_g, w_router, b_router)


def _dispatch_kernel(pos_hbm, hnp_ref, xs_hbm, idx_ref, idx_sem, row_sem):
    i = pl.program_id(0)
    tm = hnp_ref.shape[0]
    idx_copy = pltpu.make_async_copy(pos_hbm.at[pl.ds(i, 1), :], idx_ref, idx_sem)
    idx_copy.start()
    idx_copy.wait()

    def row_copy(r, slot):
        dst = idx_ref[0, slot * tm + r]
        return pltpu.make_async_copy(hnp_ref.at[pl.ds(r, 1), :], xs_hbm.at[pl.ds(dst, 1), :], row_sem)

    def start(r, _):
        row_copy(r, 0).start()
        row_copy(r, 1).start()
        return 0

    def wait(r, _):
        row_copy(r, 0).wait()
        row_copy(r, 1).wait()
        return 0

    lax.fori_loop(0, tm, start, 0, unroll=8)
    lax.fori_loop(0, tm, wait, 0, unroll=8)


def _dispatch(pos_tiles, hnp, *, tm):
    n_all, dw = hnp.shape
    nt = n_all // tm
    assert n_all % tm == 0 and pos_tiles.shape == (nt, 2 * tm)
    return pl.pallas_call(
        _dispatch_kernel,
        grid=(nt,),
        in_specs=[pl.BlockSpec(memory_space=pl.ANY), pl.BlockSpec((tm, dw), lambda i: (i, 0))],
        out_specs=pl.BlockSpec(memory_space=pl.ANY),
        out_shape=jax.ShapeDtypeStruct((2 * n_all, dw), U32),
        scratch_shapes=[pltpu.SMEM((1, 2 * tm), jnp.int32), pltpu.SemaphoreType.DMA(()),
                        pltpu.SemaphoreType.DMA(())],
        compiler_params=_params(1),
        name="dispatch",
    )(pos_tiles, hnp)


def _experts_kernel(item_tile_ref, item_expert_ref, n_items_ref, starts_ref,
                    xs_ref, wi_ref, wo_ref, ys_ref):
    w = pl.program_id(0)
    tm = xs_ref.shape[0]
    t = item_tile_ref[w]
    e = item_expert_ref[w]

    @pl.when(w < n_items_ref[0])
    def _():
        lo, hi = _unpack_bf16_pairs(xs_ref[...])
        half = lo.shape[1]
        gu = (jnp.dot(lo, wi_ref[0, :half, :], preferred_element_type=F32)
              + jnp.dot(hi, wi_ref[0, half:, :], preferred_element_type=F32))
        f = gu.shape[1] // 2
        gate, up = gu[:, :f], gu[:, f:]
        act = (gate * jax.nn.sigmoid(gate) * up).astype(BF16)
        y = jnp.dot(act, wo_ref[0], preferred_element_type=F32)
        row = t * tm + lax.broadcasted_iota(jnp.int32, (tm, 1), 0)
        mine = (row >= starts_ref[e]) & (row < starts_ref[e + 1])
        first = (w == 0) | (item_tile_ref[jnp.maximum(w - 1, 0)] != t)

        @pl.when(first)
        def _():
            ys_ref[...] = jnp.where(mine, y, 0.0)

        @pl.when(jnp.logical_not(first))
        def _():
            ys_ref[...] = jnp.where(mine, y, ys_ref[...])


def _experts(item_tile, item_expert, n_items, starts, xs, w_ei_bf, w_eo_bf, *, tm):
    n2, dw = xs.shape
    n_e, d, f2 = w_ei_bf.shape
    n_work = item_tile.shape[0]
    grid_spec = pltpu.PrefetchScalarGridSpec(
        num_scalar_prefetch=4,
        grid=(n_work,),
        in_specs=[
            pl.BlockSpec((tm, dw), lambda w, it, ie, ni, st: (it[w], 0)),
            pl.BlockSpec((1, d, f2), lambda w, it, ie, ni, st: (ie[w], 0, 0)),
            pl.BlockSpec((1, f2 // 2, d), lambda w, it, ie, ni, st: (ie[w], 0, 0)),
        ],
        out_specs=pl.BlockSpec((tm, d), lambda w, it, ie, ni, st: (it[w], 0)),
    )
    return pl.pallas_call(
        _experts_kernel,
        grid_spec=grid_spec,
        out_shape=jax.ShapeDtypeStruct((n2, d), F32),
        compiler_params=_params(1),
        name="experts",
    )(item_tile, item_expert, n_items, starts, xs, w_ei_bf, w_eo_bf)


def _expert_work_items(counts, n_rows, tm):
    n_tiles = n_rows // tm
    n_work = n_tiles + N_EXPERTS - 1
    ends = jnp.cumsum(counts)
    starts = ends - counts
    first_tile = starts // tm
    last_tile = jnp.maximum(ends - 1, 0) // tm
    per_expert = jnp.where(counts > 0, last_tile - first_tile + 1, 0)
    item_end = jnp.cumsum(per_expert)
    n_items = item_end[-1]
    w = jnp.arange(n_work, dtype=jnp.int32)
    wc = jnp.minimum(w, n_items - 1)
    expert = jnp.sum(item_end[None, :] <= wc[:, None], axis=1).astype(jnp.int32)
    tile = (first_tile[expert] + wc - (item_end[expert] - per_expert[expert])).astype(jnp.int32)
    starts_ext = jnp.concatenate([starts, ends[-1:]]).astype(jnp.int32)
    return tile, expert, n_items.reshape(1).astype(jnp.int32), starts_ext


def _combine_kernel(pos_hbm, h_ref, rinfo_ref, g_ref, ys_hbm, yp_ref, ysm_ref,
                    idx_ref, buf_ref, idx_sem, row_sem, *, n_prompt_tiles):
    i = pl.program_id(0)
    tm = h_ref.shape[0]
    idx_copy = pltpu.make_async_copy(pos_hbm.at[pl.ds(i, 1), :], idx_ref, idx_sem)
    idx_copy.start()
    idx_copy.wait()

    def row_copy(r, slot):
        src = idx_ref[0, slot * tm + r]
        return pltpu.make_async_copy(ys_hbm.at[pl.ds(src, 1), :], buf_ref.at[slot, pl.ds(r, 1), :], row_sem)

    def start(r, _):
        row_copy(r, 0).start()
        row_copy(r, 1).start()
        return 0

    def wait(r, _):
        row_copy(r, 0).wait()
        row_copy(r, 1).wait()
        return 0

    lax.fori_loop(0, tm, start, 0, unroll=8)
    lax.fori_loop(0, tm, wait, 0, unroll=8)

    rinfo = rinfo_ref[...]
    lane = lax.broadcasted_iota(jnp.int32, rinfo.shape, 1)
    w1 = _lane_pick(rinfo, lane, R_W1)
    w2 = _lane_pick(rinfo, lane, R_W2)
    hp = h_ref[...] + (w1 * buf_ref[0] + w2 * buf_ref[1])
    y = _rms(hp, g_ref[...])

    @pl.when(i < n_prompt_tiles)
    def _():
        yp_ref[...] = y

    @pl.when(i >= n_prompt_tiles)
    def _():
        ysm_ref[...] = y


def _combine(pos_tiles, h, rinfo, norm_g, ys, *, n_prompt, tm):
    n_all, d = h.shape
    nt = n_all // tm
    npt = n_prompt // tm
    assert n_all - n_prompt == tm
    kern = functools.partial(_combine_kernel, n_prompt_tiles=npt)
    return pl.pallas_call(
        kern,
        grid=(nt,),
        in_specs=[
            pl.BlockSpec(memory_space=pl.ANY),
            pl.BlockSpec((tm, d), lambda i: (i, 0)),
            pl.BlockSpec((tm, LANES), lambda i: (i, 0)),
            pl.BlockSpec((1, d), lambda i: (0, 0)),
            pl.BlockSpec(memory_space=pl.ANY),
        ],
        out_specs=[
            pl.BlockSpec((tm, d), lambda i: (jnp.minimum(i, npt - 1), 0)),
            pl.BlockSpec((tm, d), lambda i: (0, 0)),
        ],
        out_shape=[jax.ShapeDtypeStruct((n_prompt, d), F32), jax.ShapeDtypeStruct((tm, d), F32)],
        scratch_shapes=[pltpu.SMEM((1, 2 * tm), jnp.int32), pltpu.VMEM((2, tm, d), F32),
                        pltpu.SemaphoreType.DMA(()), pltpu.SemaphoreType.DMA(())],
        compiler_params=_params(1),
        name="combine",
    )(pos_tiles, h, rinfo, norm_g, ys)


ROW_TILE = 256
PROJ_ROWS = 512
PROJ_COLS = 512
ATTN_Q_ROWS = 256


def kernel(x_prompt, x_sample, cache_k, cache_v, state_conv, norm_mix, w_in, conv_w, lambda_q1,
           lambda_k1, lambda_q2, lambda_k2, head_norm, w_out, norm_ffn, w_router_group,
           b_router_group, w_router_expert, b_router_expert, w_expert_in, w_expert_out, norm_final):
    depth = w_in.shape[0]
    b, t, d = x_prompt.shape
    db, dt, _ = x_sample.shape
    past = cache_k.shape[2]
    d_conv = conv_w.shape[2]
    d_attn = d - d_conv
    n_p, n_s = b * t, db * dt
    n_all = n_p + n_s
    tm = ROW_TILE

    hp = x_prompt.reshape(n_p, d)
    hs = x_sample.reshape(n_s, d)
    outs = {name: [] for name in ("kp", "vp", "cp", "ks", "vs", "cs")}
    y_p = y_s = None
    for l in range(depth):
        lam_init = 0.8 - 0.6 * math.exp(-0.3 * l)
        w_in_bf = w_in[l].astype(BF16)
        w_out_bf = w_out[l].astype(BF16).reshape(2, d_conv, OUT_PROJ_CHUNKS, d // OUT_PROJ_CHUNKS)
        w_out_bf = w_out_bf.transpose(0, 2, 1, 3)
        w_ei_bf = w_expert_in[l].astype(BF16)
        w_eo_bf = w_expert_out[l].astype(BF16)
        g_mix = norm_mix[l].reshape(1, d)
        lams = tuple(p[l].reshape(1, -1) for p in (lambda_q1, lambda_k1, lambda_q2, lambda_k2))
        hnorm = head_norm[l].reshape(1, -1)
        pad = LANES - N_GROUPS - N_EXPERTS
        w_router = jnp.concatenate([w_router_group[l], w_router_expert[l], jnp.zeros((d, pad), F32)],
                                   axis=1).astype(BF16)
        b_router = jnp.concatenate([b_router_group[l], b_router_expert[l], jnp.zeros((pad,), F32)]).reshape(1, LANES)

        zero_prev = jnp.zeros((b, CONV_WIDTH - 1, d_conv), F32)
        a_p, q_p, kb_p, kt_p, v_p, tail_p = _in_proj(hp, g_mix, w_in_bf, conv_w[l], zero_prev,
                                                     seq_len=t, tm=min(PROJ_ROWS, t), tn=PROJ_COLS)
        o_p = _attn_prompt(q_p, kb_p, v_p, lams, hnorm, batch=b, seq_len=t, lam_init=lam_init,
                           tq=min(ATTN_Q_ROWS, t))
        prev_rows = jnp.stack([jnp.repeat(state_conv[l][:, 0], dt, axis=0),
                               jnp.repeat(state_conv[l][:, 1], dt, axis=0)])
        a_s, q_s, kb_s, k_s, v_s, u_s = _in_proj(hs, g_mix, w_in_bf, conv_w[l], prev_rows,
                                                 seq_len=dt, tm=n_s, tn=PROJ_COLS)
        kt_past = cache_k[l].transpose(0, 2, 3, 4, 1).reshape(db * d_attn, past)
        o_s = _attn_sample(q_s, kb_s, v_s, kt_past, cache_v[l].reshape(db * past, d_attn), lams, hnorm,
                           batch=db, seq_len=dt, past_len=past, lam_init=lam_init)

        h, hnp, rinfo, cnt = _out_proj(a_p, o_p, hp, a_s, o_s, hs, w_out_bf,
                                       norm_ffn[l].reshape(1, d), w_router, b_router, tm=tm)
        counts = cnt[0, :N_EXPERTS].astype(jnp.int32)
        starts = jnp.cumsum(counts) - counts
        e12 = rinfo[:, R_E1:R_E2 + 1].astype(jnp.int32)
        r12 = rinfo[:, R_RANK1:R_RANK2 + 1].astype(jnp.int32)
        pos = starts[e12] + r12
        pos_tiles = pos.reshape(n_all // tm, tm, 2).transpose(0, 2, 1).reshape(n_all // tm, 2 * tm)

        xs = _dispatch(pos_tiles, hnp, tm=tm)
        items = _expert_work_items(counts, 2 * n_all, tm)
        ys = _experts(*items, xs, w_ei_bf, w_eo_bf, tm=tm)
        last = l == depth - 1
        g_out = norm_final.reshape(1, d) if last else jnp.ones((1, d), F32)
        y_p, y_s = _combine(pos_tiles, h, rinfo, g_out, ys, n_prompt=n_p, tm=tm)
        assert last, "multi-layer stacks need an un-normalised combine output"

        outs["kp"].append(kt_p.reshape(b, N_HEADS, 2, d_attn // (2 * N_HEADS), t).transpose(0, 4, 1, 2, 3))
        outs["vp"].append(v_p.reshape(b, t, N_HEADS, d_attn // N_HEADS))
        tails = tail_p.reshape(b, t // min(PROJ_ROWS, t), SUBLANES, d_conv)
        outs["cp"].append(tails[:, -1, SUBLANES - (CONV_WIDTH - 1):, :])
        outs["ks"].append(k_s.reshape(db, dt, N_HEADS, 2, d_attn // (2 * N_HEADS)))
        outs["vs"].append(v_s.reshape(db, dt, N_HEADS, d_attn // N_HEADS))
        outs["cs"].append(u_s.reshape(db, dt, d_conv)[:, dt - (CONV_WIDTH - 1):, :])

    return (y_p.reshape(b, t, d), y_s.reshape(db, dt, d),
            jnp.stack(outs["kp"]), jnp.stack(outs["vp"]), jnp.stack(outs["cp"]),
            jnp.stack(outs["ks"]), jnp.stack(outs["vs"]), jnp.stack(outs["cs"]))
```

```python
import functools
import math

import jax
import jax.numpy as jnp
from jax import lax
from jax.experimental import pallas as pl
from jax.experimental.pallas import tpu as pltpu

EPS = 1e-6
N_HEADS = 8
CHUNK = 64
CONV_WIDTH = 3
N_GROUPS = 4
EXPERTS_PER_GROUP = 8
N_EXPERTS = N_GROUPS * EXPERTS_PER_GROUP
LANES = 128
SUBLANES = 8
BF16_SUBLANES = 16
VMEM_LIMIT = 56 * 1024 * 1024

F32 = jnp.float32
BF16 = jnp.bfloat16
U32 = jnp.uint32
NEG_INF = float("-inf")


def _params(n_axes, vmem=VMEM_LIMIT, **kw):
    return pltpu.CompilerParams(dimension_semantics=("arbitrary",) * n_axes,
                                vmem_limit_bytes=vmem, **kw)


def _rms(x, g):
    return x * lax.rsqrt(jnp.mean(x * x, axis=-1, keepdims=True) + EPS) * g


def _lane_pick(x, lane, idx):
    return jnp.sum(jnp.where(lane == idx, x, 0.0), axis=1, keepdims=True)


def _in_proj_kernel(x_ref, g_ref, wa_ref, wb_ref, wc_ref, cw_ref, prev_ref,
                    a_ref, q_ref, kb_ref, ko_ref, v_ref, u_ref, xn_ref, carry_ref,
                    *, seq_len, tiles_per_seq, rows_span_sequences, q_scale):
    i = pl.program_id(0)
    part = pl.program_id(1)
    j = pl.program_id(2)
    tm = x_ref.shape[0]

    @pl.when((part == 0) & (j == 0))
    def _():
        xn_ref[...] = _rms(x_ref[...], g_ref[...]).astype(BF16)

    xn = xn_ref[...]
    za = jnp.dot(xn, wa_ref[...], preferred_element_type=F32)
    zb = jnp.dot(xn, wb_ref[...], preferred_element_type=F32)
    zc = jnp.dot(xn, wc_ref[...], preferred_element_type=F32)

    @pl.when(part == 0)
    def _():
        u = zb * zc
        row = lax.broadcasted_iota(jnp.int32, u.shape, 0)
        if rows_span_sequences:
            rmod = row % seq_len
            pm2, pm1 = prev_ref[0], prev_ref[1]
            first, second = rmod == 0, rmod == 1
            u_ref[...] = u
        else:
            seq_start = (i % tiles_per_seq) == 0
            tail = carry_ref[j]
            pm2 = jnp.where(seq_start, prev_ref[0, 0:1, :], tail[SUBLANES - 2:SUBLANES - 1, :])
            pm1 = jnp.where(seq_start, prev_ref[0, 1:2, :], tail[SUBLANES - 1:SUBLANES, :])
            first, second = row == 0, row == 1
            new_tail = u[tm - SUBLANES:, :]
            carry_ref[j] = new_tail
            u_ref[0] = new_tail
        u1 = jnp.where(first, pm1, pltpu.roll(u, 1, 0))
        u2 = jnp.where(first, pm2, jnp.where(second, pm1, pltpu.roll(u, 2, 0)))
        cw = cw_ref[...]
        y = cw[0:1, :] * u2 + cw[1:2, :] * u1 + cw[2:3, :] * u
        a_ref[...] = (za * y).astype(BF16)

    @pl.when(part == 1)
    def _():
        q_ref[...] = (za * q_scale).astype(BF16)
        kb_ref[...] = zb.astype(BF16)
        if rows_span_sequences:
            ko_ref[...] = zb
        else:
            ko_ref[0] = zb.T
        v_ref[...] = zc


def _in_proj(x2d, norm_g, w_in_bf, conv_w, prev, *, seq_len, tm, tn):
    n, d = x2d.shape
    d_conv = conv_w.shape[1]
    assert w_in_bf.shape == (d, 6 * d_conv), "conv and attention widths must match"
    assert n % tm == 0 and d_conv % tn == 0
    nj = d_conv // tn
    n_i = n // tm
    rows_span_sequences = tm > seq_len
    if rows_span_sequences:
        assert tm % seq_len == 0 and prev.shape == (2, n, d_conv)
        tiles_per_seq = 1
        prev_spec = pl.BlockSpec((2, tm, tn), lambda i, p, j: (0, i, j))
        u_shape = jax.ShapeDtypeStruct((n, d_conv), F32)
        u_spec = pl.BlockSpec((tm, tn), lambda i, p, j: (i, jnp.where(p == 0, j, nj - 1)))
        ko_shape = jax.ShapeDtypeStruct((n, d_conv), F32)
        ko_spec = pl.BlockSpec((tm, tn), lambda i, p, j: (i, jnp.where(p == 1, j, 0)))
    else:
        assert seq_len % tm == 0 and prev.shape == (n // seq_len, CONV_WIDTH - 1, d_conv)
        tiles_per_seq = seq_len // tm
        prev_spec = pl.BlockSpec((1, CONV_WIDTH - 1, tn), lambda i, p, j: (i // tiles_per_seq, 0, j))
        u_shape = jax.ShapeDtypeStruct((n_i, SUBLANES, d_conv), F32)
        u_spec = pl.BlockSpec((1, SUBLANES, tn), lambda i, p, j: (i, 0, jnp.where(p == 0, j, nj - 1)))
        ko_shape = jax.ShapeDtypeStruct((n // seq_len, d_conv, seq_len), F32)
        ko_spec = pl.BlockSpec((1, tn, tm), lambda i, p, j: (i // tiles_per_seq, jnp.where(p == 1, j, 0),
                                                             i % tiles_per_seq))

    def wspec(k):
        return pl.BlockSpec((d, tn), lambda i, p, j: (0, p * 3 * nj + k * nj + j))

    conv_out = lambda i, p, j: (i, jnp.where(p == 0, j, nj - 1))
    attn_out = lambda i, p, j: (i, jnp.where(p == 1, j, 0))
    head_dim = d_conv // (2 * N_HEADS)
    kern = functools.partial(_in_proj_kernel, seq_len=seq_len, tiles_per_seq=tiles_per_seq,
                             rows_span_sequences=rows_span_sequences,
                             q_scale=head_dim ** -0.5 * math.log2(math.e))
    return pl.pallas_call(
        kern,
        grid=(n_i, 2, nj),
        in_specs=[
            pl.BlockSpec((tm, d), lambda i, p, j: (i, 0)),
            pl.BlockSpec((1, d), lambda i, p, j: (0, 0)),
            wspec(0), wspec(1), wspec(2),
            pl.BlockSpec((CONV_WIDTH, tn), lambda i, p, j: (0, j)),
            prev_spec,
        ],
        out_specs=[
            pl.BlockSpec((tm, tn), conv_out),
            pl.BlockSpec((tm, tn), attn_out),
            pl.BlockSpec((tm, tn), attn_out),
            ko_spec,
            pl.BlockSpec((tm, tn), attn_out),
            u_spec,
        ],
        out_shape=[
            jax.ShapeDtypeStruct((n, d_conv), BF16),
            jax.ShapeDtypeStruct((n, d_conv), BF16),
            jax.ShapeDtypeStruct((n, d_conv), BF16),
            ko_shape,
            jax.ShapeDtypeStruct((n, d_conv), F32),
            u_shape,
        ],
        scratch_shapes=[pltpu.VMEM((tm, d), BF16), pltpu.VMEM((nj, SUBLANES, tn), F32)],
        compiler_params=_params(3),
        name="in_proj",
    )(x2d, norm_g, w_in_bf, w_in_bf, w_in_bf, conv_w, prev)


def _diff_lambda(lq1_ref, lk1_ref, lq2_ref, lk2_ref, lam_init):
    s1 = jnp.sum(lq1_ref[...] * lk1_ref[...], axis=1, keepdims=True)
    s2 = jnp.sum(lq2_ref[...] * lk2_ref[...], axis=1, keepdims=True)
    return jnp.exp(s1) - jnp.exp(s2) + lam_init


def _stacked_queries(q, head_dim):
    lane = lax.broadcasted_iota(jnp.int32, q.shape, 1)
    zero = jnp.zeros_like(q)
    return jnp.concatenate([jnp.where(lane < head_dim, q, zero),
                            jnp.where(lane >= head_dim, q, zero)], axis=0)


def _scores(qq, kblk):
    return lax.dot_general(qq, kblk, (((1,), (1,)), ((), ())), preferred_element_type=F32)


def _finish_heads(acc, l, lam, hn, lam_init, tq):
    o = acc[:tq] / l[:tq] - lam * (acc[tq:] / l[tq:])
    return (_rms(o, hn) * (1.0 - lam_init)).astype(BF16)


def _attn_prompt_kernel(lq1_ref, lk1_ref, lq2_ref, lk2_ref, hn_ref, q_ref, k_ref, v_ref, o_ref,
                        qt1_ref, qt2_ref, vt_ref, s0_ref, s1_ref, p0_ref, p1_ref,
                        *, lam_init, head_dim, tq):
    seq_len, hw = q_ref.shape
    lam = _diff_lambda(lq1_ref, lk1_ref, lq2_ref, lk2_ref, lam_init)

    qt = q_ref[...].astype(F32).T
    dim = lax.broadcasted_iota(jnp.int32, qt.shape, 0)
    qt1_ref[...] = jnp.where(dim < head_dim, qt, 0.0).astype(BF16)
    qt2_ref[...] = jnp.where(dim >= head_dim, qt, 0.0).astype(BF16)
    vt_ref[:hw, :] = v_ref[...].T.astype(BF16)
    vt_ref[hw:, :] = jnp.ones((vt_ref.shape[0] - hw, seq_len), BF16)

    key_chunk = lax.broadcasted_iota(jnp.int32, (tq, 2 * tq), 0) // CHUNK
    qry_chunk = (lax.broadcasted_iota(jnp.int32, (tq, 2 * tq), 1) % tq) // CHUNK
    visible = key_chunk <= qry_chunk

    bufs = ((s0_ref, p0_ref), (s1_ref, p1_ref))
    n_blocks = seq_len // tq

    def scores(qi):
        q0, n_keys, s_buf = qi * tq, (qi + 1) * tq, bufs[qi % 2][0]
        qqt = jnp.concatenate([qt1_ref[:, q0:n_keys], qt2_ref[:, q0:n_keys]], axis=1)
        s = jnp.dot(k_ref[q0:n_keys, :], qqt, preferred_element_type=F32)
        s = jnp.where(visible, s, NEG_INF)
        s_buf[q0:n_keys, :] = s
        m = jnp.max(s, axis=0, keepdims=True)
        if qi > 0:
            s = jnp.dot(k_ref[:q0, :], qqt, preferred_element_type=F32)
            s_buf[:q0, :] = s
            m = jnp.maximum(m, jnp.max(s, axis=0, keepdims=True))
        return m

    def finish(qi, m):
        q0, n_keys = qi * tq, (qi + 1) * tq
        s_buf, p_buf = bufs[qi % 2]
        p_buf[:n_keys, :] = jnp.exp2(s_buf[:n_keys, :] - m).astype(BF16)
        ot = jnp.dot(vt_ref[:, :n_keys], p_buf[:n_keys, :], preferred_element_type=F32)
        ot = ot[:hw] / ot[hw:hw + 1]
        o = (ot[:, :tq] - lam * ot[:, tq:]).T
        o_ref[q0:n_keys, :] = (_rms(o, hn_ref[...]) * (1.0 - lam_init)).astype(BF16)

    m = scores(0)
    for qi in range(n_blocks):
        m_next = scores(qi + 1) if qi + 1 < n_blocks else None
        finish(qi, m)
        m = m_next


def _attn_prompt(q, k, v, lams, head_norm, *, batch, seq_len, lam_init, tq):
    n, d_attn = q.shape
    hw = d_attn // N_HEADS
    assert seq_len % tq == 0 and tq % CHUNK == 0
    small = pl.BlockSpec((1, hw // 2), lambda b, h: (0, 0))
    head = pl.BlockSpec((seq_len, hw), lambda b, h: (b, h))
    kern = functools.partial(_attn_prompt_kernel, lam_init=lam_init, head_dim=hw // 2, tq=tq)
    return pl.pallas_call(
        kern,
        grid=(batch, N_HEADS),
        in_specs=[small, small, small, small, pl.BlockSpec((1, hw), lambda b, h: (0, 0)),
                  head, head, head],
        out_specs=head,
        out_shape=jax.ShapeDtypeStruct((n, d_attn), BF16),
        scratch_shapes=[pltpu.VMEM((hw, seq_len), BF16), pltpu.VMEM((hw, seq_len), BF16),
                        pltpu.VMEM((hw + BF16_SUBLANES, seq_len), BF16),
                        pltpu.VMEM((seq_len, 2 * tq), F32), pltpu.VMEM((seq_len, 2 * tq), F32),
                        pltpu.VMEM((seq_len, 2 * tq), BF16), pltpu.VMEM((seq_len, 2 * tq), BF16)],
        compiler_params=_params(2),
        name="attn_prompt",
    )(*lams, head_norm, q, k, v)


def _attn_sample_kernel(lq1_ref, lk1_ref, lq2_ref, lk2_ref, hn_ref, q_ref, kc_ref, vc_ref,
                        kn_ref, vn_ref, o_ref, *, lam_init, head_dim):
    tq = q_ref.shape[0]
    lam = _diff_lambda(lq1_ref, lk1_ref, lq2_ref, lk2_ref, lam_init)
    qq = _stacked_queries(q_ref[...], head_dim)
    s_past = jnp.dot(qq, kc_ref[...].astype(BF16), preferred_element_type=F32)
    s_new = _scores(qq, kn_ref[...])
    m = jnp.maximum(jnp.max(s_past, axis=1, keepdims=True), jnp.max(s_new, axis=1, keepdims=True))
    p_past = jnp.exp2(s_past - m)
    p_new = jnp.exp2(s_new - m)
    l = jnp.sum(p_past, axis=1, keepdims=True) + jnp.sum(p_new, axis=1, keepdims=True)
    acc = (jnp.dot(p_past.astype(BF16), vc_ref[...].astype(BF16), preferred_element_type=F32)
           + jnp.dot(p_new.astype(BF16), vn_ref[...].astype(BF16), preferred_element_type=F32))
    o_ref[...] = _finish_heads(acc, l, lam, hn_ref[...], lam_init, tq)


def _attn_sample(q, k_new, v_new, k_past, v_past, lams, head_norm, *, batch, seq_len, past_len, lam_init):
    n, d_attn = q.shape
    hw = d_attn // N_HEADS
    small = pl.BlockSpec((1, hw // 2), lambda b, h: (0, 0))
    new_spec = pl.BlockSpec((seq_len, hw), lambda b, h: (b, h))
    past_spec = pl.BlockSpec((past_len, hw), lambda b, h: (b, h))
    past_t_spec = pl.BlockSpec((hw, past_len), lambda b, h: (b * N_HEADS + h, 0))
    kern = functools.partial(_attn_sample_kernel, lam_init=lam_init, head_dim=hw // 2)
    return pl.pallas_call(
        kern,
        grid=(batch, N_HEADS),
        in_specs=[small, small, small, small, pl.BlockSpec((1, hw), lambda b, h: (0, 0)),
                  new_spec, past_t_spec, past_spec, new_spec, new_spec],
        out_specs=new_spec,
        out_shape=jax.ShapeDtypeStruct((n, d_attn), BF16),
        compiler_params=_params(2),
        name="attn_sample",
    )(*lams, head_norm, q, k_past, v_past, k_new, v_new)


R_E1, R_E2, R_W1, R_W2, R_RANK1, R_RANK2 = 0, 1, 2, 3, 4, 5
OUT_PROJ_CHUNKS = 4


def _pack_bf16_pairs(lo_bf, hi_bf):
    lo = lax.bitcast_convert_type(lo_bf.astype(F32), U32)
    hi = lax.bitcast_convert_type(hi_bf.astype(F32), U32)
    return (hi & jnp.uint32(0xFFFF0000)) | (lo >> 16)


def _unpack_bf16_pairs(w):
    lo = lax.bitcast_convert_type(w << 16, F32).astype(BF16)
    hi = lax.bitcast_convert_type(w & jnp.uint32(0xFFFF0000), F32).astype(BF16)
    return lo, hi


def _out_proj_kernel(ap_ref, op_ref, xp_ref, as_ref, os_ref, xs_ref, w_ref, g_ref,
                     wr_ref, br_ref, h_ref, hnp_ref, rinfo_ref, cnt_ref, carry_ref, *, n_prompt_tiles):
    i = pl.program_id(0)

    @pl.when(i == 0)
    def _():
        carry_ref[...] = jnp.zeros_like(carry_ref)

    @pl.when(i < n_prompt_tiles)
    def _():
        _out_proj_rows(ap_ref, op_ref, xp_ref, w_ref, g_ref, wr_ref, br_ref,
                       h_ref, hnp_ref, rinfo_ref, cnt_ref, carry_ref)

    @pl.when(i >= n_prompt_tiles)
    def _():
        _out_proj_rows(as_ref, os_ref, xs_ref, w_ref, g_ref, wr_ref, br_ref,
                       h_ref, hnp_ref, rinfo_ref, cnt_ref, carry_ref)


def _out_proj_rows(a_ref, o_ref, x_ref, w_ref, g_ref, wr_ref, br_ref,
                   h_ref, hnp_ref, rinfo_ref, cnt_ref, carry_ref):
    tm, d = x_ref.shape
    cw = d // OUT_PROJ_CHUNKS
    a, o = a_ref[...], o_ref[...]

    ss = jnp.zeros((tm, 1), F32)
    for c in range(OUT_PROJ_CHUNKS):
        cols = slice(c * cw, (c + 1) * cw)
        h = (x_ref[:, cols] + jnp.dot(a, w_ref[0, c], preferred_element_type=F32)
             + jnp.dot(o, w_ref[1, c], preferred_element_type=F32))
        h_ref[:, cols] = h
        ss = ss + jnp.sum(h * h, axis=1, keepdims=True)
    inv = lax.rsqrt(ss / d + EPS)

    logits = br_ref[...]
    for c in range(OUT_PROJ_CHUNKS // 2):
        halves = []
        for cc in (c, c + OUT_PROJ_CHUNKS // 2):
            cols = slice(cc * cw, (cc + 1) * cw)
            hb = (h_ref[:, cols] * inv * g_ref[:, cols]).astype(BF16)
            logits = logits + jnp.dot(hb, wr_ref[cols, :], preferred_element_type=F32)
            halves.append(hb)
        hnp_ref[:, c * cw:(c + 1) * cw] = _pack_bf16_pairs(*halves)

    lane = lax.broadcasted_iota(jnp.int32, logits.shape, 1)
    lane_f = lane.astype(F32)
    lg = jnp.where(lane < N_GROUPS, logits, NEG_INF)
    mg = jnp.max(lg, axis=1, keepdims=True)
    grp = jnp.min(jnp.where(lg == mg, lane_f, float(LANES)), axis=1, keepdims=True)
    gate_g = 1.0 / jnp.sum(jnp.exp(lg - mg), axis=1, keepdims=True)
    ex = lane - N_GROUPS
    in_grp = (ex >= 0) & (ex < N_EXPERTS) & ((ex >> 3).astype(F32) == grp)
    le = jnp.where(in_grp, logits, NEG_INF)
    v1 = jnp.max(le, axis=1, keepdims=True)
    i1 = jnp.min(jnp.where(le == v1, lane_f, float(LANES)), axis=1, keepdims=True)
    le2 = jnp.where(lane_f == i1, NEG_INF, le)
    v2 = jnp.max(le2, axis=1, keepdims=True)
    i2 = jnp.min(jnp.where(le2 == v2, lane_f, float(LANES)), axis=1, keepdims=True)
    e21 = jnp.exp(v2 - v1)
    w1 = gate_g / (1.0 + e21)
    w2 = gate_g * e21 / (1.0 + e21)
    e1 = i1 - float(N_GROUPS)
    e2 = i2 - float(N_GROUPS)

    onehot = jnp.where((lane_f == e1) | (lane_f == e2), 1.0, 0.0)
    r = lax.broadcasted_iota(jnp.int32, (tm, tm), 0)
    c = lax.broadcasted_iota(jnp.int32, (tm, tm), 1)
    earlier = jnp.where(c < r, 1.0, 0.0).astype(BF16)
    before = jnp.dot(earlier, onehot.astype(BF16), preferred_element_type=F32) + carry_ref[0:1, :]
    rank1 = _lane_pick(before, lane_f, e1)
    rank2 = _lane_pick(before, lane_f, e2)
    new_carry = carry_ref[0:1, :] + jnp.sum(onehot, axis=0, keepdims=True)
    carry_ref[...] = jnp.broadcast_to(new_carry, carry_ref.shape)
    cnt_ref[...] = jnp.broadcast_to(new_carry, cnt_ref.shape)

    rec = jnp.zeros_like(logits)
    for idx, val in ((R_E1, e1), (R_E2, e2), (R_W1, w1), (R_W2, w2), (R_RANK1, rank1), (R_RANK2, rank2)):
        rec = jnp.where(lane == idx, val, rec)
    rinfo_ref[...] = rec


def _out_proj(a_p, o_p, x_p, a_s, o_s, x_s, w_out_chunks, norm_g, w_router, b_router, *, tm):
    n_p, d = x_p.shape
    n_s = x_s.shape[0]
    dc = a_p.shape[1]
    assert n_p % tm == 0 and n_s == tm, "sample rows must fill exactly one tile"
    npt = n_p // tm
    n_all = n_p + n_s
    p_idx = lambda i: (jnp.minimum(i, npt - 1), 0)
    zero = lambda i: (0, 0)
    kern = functools.partial(_out_proj_kernel, n_prompt_tiles=npt)
    return pl.pallas_call(
        kern,
        grid=(npt + 1,),
        in_specs=[
            pl.BlockSpec((tm, dc), p_idx), pl.BlockSpec((tm, dc), p_idx), pl.BlockSpec((tm, d), p_idx),
            pl.BlockSpec((tm, dc), zero), pl.BlockSpec((tm, dc), zero), pl.BlockSpec((tm, d), zero),
            pl.BlockSpec(w_out_chunks.shape, lambda i: (0, 0, 0, 0)),
            pl.BlockSpec((1, d), zero),
            pl.BlockSpec((d, LANES), zero), pl.BlockSpec((1, LANES), zero),
        ],
        out_specs=[
            pl.BlockSpec((tm, d), lambda i: (i, 0)),
            pl.BlockSpec((tm, d // 2), lambda i: (i, 0)),
            pl.BlockSpec((tm, LANES), lambda i: (i, 0)),
            pl.BlockSpec((SUBLANES, LANES), zero),
        ],
        out_shape=[
            jax.ShapeDtypeStruct((n_all, d), F32),
            jax.ShapeDtypeStruct((n_all, d // 2), U32),
            jax.ShapeDtypeStruct((n_all, LANES), F32),
            jax.ShapeDtypeStruct((SUBLANES, LANES), F32),
        ],
        scratch_shapes=[pltpu.VMEM((SUBLANES, LANES), F32)],
        compiler_params=_params(1),
        name="out_proj",
    )(a_p, o_p, x_p, a_s, o_s, x_s, w_out_chunks, norm_g, w_router, b_router)


EXPERT_DMA_GROUPS = 8


def _experts_kernel(tile_expert_ref, n_used_ref, src_hbm, dst_hbm, hnp_hbm, wi_ref, wo_ref, y2_hbm,
                    src_ref, dst_ref, xs_ref, ys_ref, src_sem, dst_sem, gather_sem, scatter_sem):
    w = pl.program_id(0)
    n_used = n_used_ref[0]
    tm = xs_ref.shape[1]
    last_row = src_hbm.shape[0] - 1
    group = tm // EXPERT_DMA_GROUPS

    def src_copy(tile, slot):
        return pltpu.make_async_copy(src_hbm.at[pl.ds(tile, 1), :], src_ref.at[pl.ds(slot, 1), :],
                                     src_sem.at[slot])

    def dst_copy(tile, slot):
        return pltpu.make_async_copy(dst_hbm.at[pl.ds(tile, 1), :], dst_ref.at[pl.ds(slot, 1), :],
                                     dst_sem.at[slot])

    def gather_row(slot, r):
        return pltpu.make_async_copy(hnp_hbm.at[pl.ds(src_ref[slot, r], 1), :],
                                     xs_ref.at[slot, pl.ds(r, 1), :], gather_sem.at[slot])

    def scatter_row(slot, r):
        return pltpu.make_async_copy(ys_ref.at[slot, pl.ds(r, 1), :],
                                     y2_hbm.at[pl.ds(dst_ref[slot, r], 1), :], scatter_sem.at[slot])

    def wait_gathers(slot):
        pltpu.make_async_copy(xs_ref.at[slot], xs_ref.at[slot], gather_sem.at[slot]).wait()

    def wait_scatters(slot):
        pltpu.make_async_copy(ys_ref.at[slot], ys_ref.at[slot], scatter_sem.at[slot]).wait()

    @pl.when(w == 0)
    def _():
        src_copy(0, 0).start()
        src_copy(1, 1).start()
        dst_copy(last_row, 1).start()
        src_copy(0, 0).wait()
        for r in range(tm):
            gather_row(0, r).start()
        ys_ref[1] = jnp.zeros(ys_ref.shape[1:], ys_ref.dtype)
        for blk in range(2):
            spare = pltpu.make_async_copy(ys_ref.at[1], y2_hbm.at[pl.ds(y2_hbm.shape[0] - (blk + 1) * tm, tm), :],
                                          scatter_sem.at[1])
            spare.start()
            spare.wait()

    def compute_tile(cur):
        nxt = 1 - cur
        wait_gathers(cur)
        src_copy(w + 1, nxt).wait()
        dst_copy(jnp.maximum(w - 1, 0), nxt).wait()
        src_copy(w + 2, cur).start()
        dst_copy(w, cur).start()

        def move_rows(g):
            for r in range(g * group, (g + 1) * group):
                gather_row(nxt, r).start()
                scatter_row(nxt, r).start()

        lo, hi = _unpack_bf16_pairs(xs_ref[cur])
        half = lo.shape[1]
        n_chunks = EXPERT_DMA_GROUPS // 2
        f2 = wi_ref.shape[2]
        cw = f2 // n_chunks
        gu = []
        for c in range(n_chunks):
            cols = slice(c * cw, (c + 1) * cw)
            gu.append(jnp.dot(lo, wi_ref[0, :half, cols], preferred_element_type=F32)
                      + jnp.dot(hi, wi_ref[0, half:, cols], preferred_element_type=F32))
            move_rows(c)
        gate = jnp.concatenate(gu[:n_chunks // 2], axis=1)
        up = jnp.concatenate(gu[n_chunks // 2:], axis=1)
        act = (gate * jax.nn.sigmoid(gate) * up).astype(BF16)
        d = wo_ref.shape[2]
        ow = d // n_chunks
        for c in range(n_chunks):
            cols = slice(c * ow, (c + 1) * ow)
            ys_ref[cur, :, cols] = jnp.dot(act, wo_ref[0, :, cols], preferred_element_type=F32)
            move_rows(n_chunks + c)

    def drain(cur):
        nxt = 1 - cur
        wait_gathers(cur)
        src_copy(w + 1, nxt).wait()
        dst_copy(w - 1, nxt).wait()
        for r in range(tm):
            scatter_row(nxt, r).start()
        wait_scatters(nxt)

    for parity in (0, 1):
        mine = (w % 2) == parity

        @pl.when(mine & (w >= 1) & (w <= n_used))
        def _():
            wait_scatters(parity)

        pl.when(mine & (w < n_used))(functools.partial(compute_tile, parity))
        pl.when(mine & (w == n_used))(functools.partial(drain, parity))


def _experts(tile_expert, n_used, src_tab, dst_tab, hnp, w_ei_bf, w_eo_bf, *, n_out_rows, tm):
    n_e, d, f2 = w_ei_bf.shape
    dw = hnp.shape[1]
    n_steps = src_tab.shape[0] - 1
    assert tm % EXPERT_DMA_GROUPS == 0 and src_tab.shape == dst_tab.shape == (n_steps + 1, tm)

    def expert_of(w, te, nu):
        return te[jnp.minimum(w, nu[0] - 1)]

    grid_spec = pltpu.PrefetchScalarGridSpec(
        num_scalar_prefetch=2,
        grid=(n_steps,),
        in_specs=[
            pl.BlockSpec(memory_space=pl.ANY),
            pl.BlockSpec(memory_space=pl.ANY),
            pl.BlockSpec(memory_space=pl.ANY),
            pl.BlockSpec((1, d, f2), lambda w, te, nu: (expert_of(w, te, nu), 0, 0)),
            pl.BlockSpec((1, f2 // 2, d), lambda w, te, nu: (expert_of(w, te, nu), 0, 0)),
        ],
        out_specs=pl.BlockSpec(memory_space=pl.ANY),
        scratch_shapes=[pltpu.SMEM((2, tm), jnp.int32), pltpu.SMEM((2, tm), jnp.int32),
                        pltpu.VMEM((2, tm, dw), U32), pltpu.VMEM((2, tm, d), F32),
                        pltpu.SemaphoreType.DMA((2,)), pltpu.SemaphoreType.DMA((2,)),
                        pltpu.SemaphoreType.DMA((2,)), pltpu.SemaphoreType.DMA((2,))],
    )
    return pl.pallas_call(
        _experts_kernel,
        grid_spec=grid_spec,
        out_shape=jax.ShapeDtypeStruct((n_out_rows, d), F32),
        compiler_params=_params(1),
        name="experts",
    )(tile_expert, n_used, src_tab, dst_tab, hnp, w_ei_bf, w_eo_bf)


def _expert_tiles(counts, e12, r12, n_all, tm):
    max_tiles = (2 * n_all) // tm + N_EXPERTS
    n_rows = max_tiles + 2
    tiles_per_expert = (counts + tm - 1) // tm
    tile_end = jnp.cumsum(tiles_per_expert)
    n_used = tile_end[-1]
    row_start = (tile_end - tiles_per_expert) * tm
    t = jnp.minimum(jnp.arange(n_rows, dtype=jnp.int32), n_used - 1)
    tile_expert = jnp.sum(tile_end[None, :] <= t[:, None], axis=1).astype(jnp.int32)

    pos = (row_start[e12] + r12).reshape(-1)
    pair = jnp.arange(2 * n_all, dtype=jnp.int32)
    filled = jnp.full((n_rows * tm,), -1, jnp.int32).at[pos].set(pair, unique_indices=True)
    filled = filled.reshape(n_rows, tm)
    tile = jnp.arange(n_rows, dtype=jnp.int32)[:, None]
    spare = 2 * n_all + (tile % 2) * tm + jnp.arange(tm, dtype=jnp.int32)[None, :]
    src = jnp.where(filled >= 0, filled // 2, 0)
    dst = jnp.where(filled >= 0, filled, spare)
    return tile_expert, n_used.reshape(1).astype(jnp.int32), src, dst


def _combine_kernel(h_ref, rinfo_ref, g_ref, y2_ref, yp_ref, ysm_ref, *, n_prompt_tiles):
    i = pl.program_id(0)
    d = h_ref.shape[1]
    rinfo = rinfo_ref[...]
    lane = lax.broadcasted_iota(jnp.int32, rinfo.shape, 1)
    w1 = _lane_pick(rinfo, lane, R_W1)
    w2 = _lane_pick(rinfo, lane, R_W2)
    hp = h_ref[...] + (w1 * y2_ref[:, :d] + w2 * y2_ref[:, d:])
    y = _rms(hp, g_ref[...])

    @pl.when(i < n_prompt_tiles)
    def _():
        yp_ref[...] = y

    @pl.when(i >= n_prompt_tiles)
    def _():
        ysm_ref[...] = y


def _combine(h, rinfo, norm_g, y2, *, n_prompt, tm):
    n_all, d = h.shape
    nt = n_all // tm
    npt = n_prompt // tm
    assert n_all - n_prompt == tm and y2.shape[1] == 2 * d
    kern = functools.partial(_combine_kernel, n_prompt_tiles=npt)
    return pl.pallas_call(
        kern,
        grid=(nt,),
        in_specs=[
            pl.BlockSpec((tm, d), lambda i: (i, 0)),
            pl.BlockSpec((tm, LANES), lambda i: (i, 0)),
            pl.BlockSpec((1, d), lambda i: (0, 0)),
            pl.BlockSpec((tm, 2 * d), lambda i: (i, 0)),
        ],
        out_specs=[
            pl.BlockSpec((tm, d), lambda i: (jnp.minimum(i, npt - 1), 0)),
            pl.BlockSpec((tm, d), lambda i: (0, 0)),
        ],
        out_shape=[jax.ShapeDtypeStruct((n_prompt, d), F32), jax.ShapeDtypeStruct((tm, d), F32)],
        compiler_params=_params(1),
        name="combine",
    )(h, rinfo, norm_g, y2)


ROW_TILE = 256
PROJ_ROWS = 512
PROJ_COLS = 512
ATTN_Q_ROWS = 256


def kernel(x_prompt, x_sample, cache_k, cache_v, state_conv, norm_mix, w_in, conv_w, lambda_q1,
           lambda_k1, lambda_q2, lambda_k2, head_norm, w_out, norm_ffn, w_router_group,
           b_router_group, w_router_expert, b_router_expert, w_expert_in, w_expert_out, norm_final):
    depth = w_in.shape[0]
    b, t, d = x_prompt.shape
    db, dt, _ = x_sample.shape
    past = cache_k.shape[2]
    d_conv = conv_w.shape[2]
    d_attn = d - d_conv
    n_p, n_s = b * t, db * dt
    n_all = n_p + n_s
    tm = ROW_TILE

    hp = x_prompt.reshape(n_p, d)
    hs = x_sample.reshape(n_s, d)
    outs = {name: [] for name in ("kp", "vp", "cp", "ks", "vs", "cs")}
    y_p = y_s = None
    for l in range(depth):
        lam_init = 0.8 - 0.6 * math.exp(-0.3 * l)
        w_in_bf = w_in[l].astype(BF16)
        w_out_bf = w_out[l].astype(BF16).reshape(2, d_conv, OUT_PROJ_CHUNKS, d // OUT_PROJ_CHUNKS)
        w_out_bf = w_out_bf.transpose(0, 2, 1, 3)
        w_ei_bf = w_expert_in[l].astype(BF16)
        w_eo_bf = w_expert_out[l].astype(BF16)
        g_mix = norm_mix[l].reshape(1, d)
        lams = tuple(p[l].reshape(1, -1) for p in (lambda_q1, lambda_k1, lambda_q2, lambda_k2))
        hnorm = head_norm[l].reshape(1, -1)
        pad = LANES - N_GROUPS - N_EXPERTS
        w_router = jnp.concatenate([w_router_group[l], w_router_expert[l], jnp.zeros((d, pad), F32)],
                                   axis=1).astype(BF16)
        b_router = jnp.concatenate([b_router_group[l], b_router_expert[l], jnp.zeros((pad,), F32)]).reshape(1, LANES)

        zero_prev = jnp.zeros((b, CONV_WIDTH - 1, d_conv), F32)
        a_p, q_p, kb_p, kt_p, v_p, tail_p = _in_proj(hp, g_mix, w_in_bf, conv_w[l], zero_prev,
                                                     seq_len=t, tm=min(PROJ_ROWS, t), tn=PROJ_COLS)
        o_p = _attn_prompt(q_p, kb_p, v_p, lams, hnorm, batch=b, seq_len=t, lam_init=lam_init,
                           tq=min(ATTN_Q_ROWS, t))
        prev_rows = jnp.stack([jnp.repeat(state_conv[l][:, 0], dt, axis=0),
                               jnp.repeat(state_conv[l][:, 1], dt, axis=0)])
        a_s, q_s, kb_s, k_s, v_s, u_s = _in_proj(hs, g_mix, w_in_bf, conv_w[l], prev_rows,
                                                 seq_len=dt, tm=n_s, tn=PROJ_COLS)
        kt_past = cache_k[l].transpose(0, 2, 3, 4, 1).reshape(db * d_attn, past)
        o_s = _attn_sample(q_s, kb_s, v_s, kt_past, cache_v[l].reshape(db * past, d_attn), lams, hnorm,
                           batch=db, seq_len=dt, past_len=past, lam_init=lam_init)

        h, hnp, rinfo, cnt = _out_proj(a_p, o_p, hp, a_s, o_s, hs, w_out_bf,
                                       norm_ffn[l].reshape(1, d), w_router, b_router, tm=tm)
        counts = cnt[0, :N_EXPERTS].astype(jnp.int32)
        e12 = rinfo[:, R_E1:R_E2 + 1].astype(jnp.int32)
        r12 = rinfo[:, R_RANK1:R_RANK2 + 1].astype(jnp.int32)
        tile_expert, n_used, src_tab, dst_tab = _expert_tiles(counts, e12, r12, n_all, tm)
        y2 = _experts(tile_expert, n_used, src_tab, dst_tab, hnp, w_ei_bf, w_eo_bf,
                      n_out_rows=2 * n_all + 2 * tm, tm=tm)
        last = l == depth - 1
        g_out = norm_final.reshape(1, d) if last else jnp.ones((1, d), F32)
        y_p, y_s = _combine(h, rinfo, g_out, y2.reshape(n_all + tm, 2 * d), n_prompt=n_p, tm=tm)
        assert last, "multi-layer stacks need an un-normalised combine output"

        outs["kp"].append(kt_p.reshape(b, N_HEADS, 2, d_attn // (2 * N_HEADS), t).transpose(0, 4, 1, 2, 3))
        outs["vp"].append(v_p.reshape(b, t, N_HEADS, d_attn // N_HEADS))
        tails = tail_p.reshape(b, t // min(PROJ_ROWS, t), SUBLANES, d_conv)
        outs["cp"].append(tails[:, -1, SUBLANES - (CONV_WIDTH - 1):, :])
        outs["ks"].append(k_s.reshape(db, dt, N_HEADS, 2, d_attn // (2 * N_HEADS)))
        outs["vs"].append(v_s.reshape(db, dt, N_HEADS, d_attn // N_HEADS))
        outs["cs"].append(u_s.reshape(db, dt, d_conv)[:, dt - (CONV_WIDTH - 1):, :])

    return (y_p.reshape(b, t, d), y_s.reshape(db, dt, d),
            jnp.stack(outs["kp"]), jnp.stack(outs["vp"]), jnp.stack(outs["cp"]),
            jnp.stack(outs["ks"]), jnp.stack(outs["vs"]), jnp.stack(outs["cs"]))
```

```python
import functools
import math

import jax
import jax.numpy as jnp
from jax import lax
from jax.experimental import pallas as pl
from jax.experimental.pallas import tpu as pltpu

EPS = 1e-6
N_HEADS = 8
CHUNK = 64
CONV_WIDTH = 3
N_GROUPS = 4
EXPERTS_PER_GROUP = 8
N_EXPERTS = N_GROUPS * EXPERTS_PER_GROUP
LANES = 128
SUBLANES = 8
BF16_SUBLANES = 16
VMEM_LIMIT = 56 * 1024 * 1024

F32 = jnp.float32
BF16 = jnp.bfloat16
U32 = jnp.uint32
NEG_INF = float("-inf")


def _params(n_axes, vmem=VMEM_LIMIT, **kw):
    return pltpu.CompilerParams(dimension_semantics=("arbitrary",) * n_axes,
                                vmem_limit_bytes=vmem, **kw)


def _rms(x, g):
    return x * lax.rsqrt(jnp.mean(x * x, axis=-1, keepdims=True) + EPS) * g


def _lane_pick(x, lane, idx):
    return jnp.sum(jnp.where(lane == idx, x, 0.0), axis=1, keepdims=True)


def _in_proj_kernel(x_ref, g_ref, wa_ref, wb_ref, wc_ref, cw_ref, prev_ref,
                    a_ref, q_ref, kb_ref, ko_ref, v_ref, u_ref, xn_ref, carry_ref,
                    *, seq_len, tiles_per_seq, rows_span_sequences, q_scale):
    i = pl.program_id(0)
    part = pl.program_id(1)
    j = pl.program_id(2)
    tm = x_ref.shape[0]

    @pl.when((part == 0) & (j == 0))
    def _():
        xn_ref[...] = _rms(x_ref[...], g_ref[...]).astype(BF16)

    xn = xn_ref[...]
    za = jnp.dot(xn, wa_ref[...], preferred_element_type=F32)
    zb = jnp.dot(xn, wb_ref[...], preferred_element_type=F32)
    zc = jnp.dot(xn, wc_ref[...], preferred_element_type=F32)

    @pl.when(part == 0)
    def _():
        u = zb * zc
        row = lax.broadcasted_iota(jnp.int32, u.shape, 0)
        if rows_span_sequences:
            rmod = row % seq_len
            pm2, pm1 = prev_ref[0], prev_ref[1]
            first, second = rmod == 0, rmod == 1
            u_ref[...] = u
        else:
            seq_start = (i % tiles_per_seq) == 0
            tail = carry_ref[j]
            pm2 = jnp.where(seq_start, prev_ref[0, 0:1, :], tail[SUBLANES - 2:SUBLANES - 1, :])
            pm1 = jnp.where(seq_start, prev_ref[0, 1:2, :], tail[SUBLANES - 1:SUBLANES, :])
            first, second = row == 0, row == 1
            new_tail = u[tm - SUBLANES:, :]
            carry_ref[j] = new_tail
            u_ref[0] = new_tail
        u1 = jnp.where(first, pm1, pltpu.roll(u, 1, 0))
        u2 = jnp.where(first, pm2, jnp.where(second, pm1, pltpu.roll(u, 2, 0)))
        cw = cw_ref[...]
        y = cw[0:1, :] * u2 + cw[1:2, :] * u1 + cw[2:3, :] * u
        a_ref[...] = (za * y).astype(BF16)

    @pl.when(part == 1)
    def _():
        q_ref[...] = (za * q_scale).astype(BF16)
        kb_ref[...] = zb.astype(BF16)
        if rows_span_sequences:
            ko_ref[...] = zb
        else:
            ko_ref[0] = zb.T
        v_ref[...] = zc


def _in_proj(x2d, norm_g, w_in_bf, conv_w, prev, *, seq_len, tm, tn):
    n, d = x2d.shape
    d_conv = conv_w.shape[1]
    assert w_in_bf.shape == (d, 6 * d_conv), "conv and attention widths must match"
    assert n % tm == 0 and d_conv % tn == 0
    nj = d_conv // tn
    n_i = n // tm
    rows_span_sequences = tm > seq_len
    if rows_span_sequences:
        assert tm % seq_len == 0 and prev.shape == (2, n, d_conv)
        tiles_per_seq = 1
        prev_spec = pl.BlockSpec((2, tm, tn), lambda i, p, j: (0, i, j))
        u_shape = jax.ShapeDtypeStruct((n, d_conv), F32)
        u_spec = pl.BlockSpec((tm, tn), lambda i, p, j: (i, jnp.where(p == 0, j, nj - 1)))
        ko_shape = jax.ShapeDtypeStruct((n, d_conv), F32)
        ko_spec = pl.BlockSpec((tm, tn), lambda i, p, j: (i, jnp.where(p == 1, j, 0)))
    else:
        assert seq_len % tm == 0 and prev.shape == (n // seq_len, CONV_WIDTH - 1, d_conv)
        tiles_per_seq = seq_len // tm
        prev_spec = pl.BlockSpec((1, CONV_WIDTH - 1, tn), lambda i, p, j: (i // tiles_per_seq, 0, j))
        u_shape = jax.ShapeDtypeStruct((n_i, SUBLANES, d_conv), F32)
        u_spec = pl.BlockSpec((1, SUBLANES, tn), lambda i, p, j: (i, 0, jnp.where(p == 0, j, nj - 1)))
        ko_shape = jax.ShapeDtypeStruct((n // seq_len, d_conv, seq_len), F32)
        ko_spec = pl.BlockSpec((1, tn, tm), lambda i, p, j: (i // tiles_per_seq, jnp.where(p == 1, j, 0),
                                                             i % tiles_per_seq))

    def wspec(k):
        return pl.BlockSpec((d, tn), lambda i, p, j: (0, p * 3 * nj + k * nj + j))

    conv_out = lambda i, p, j: (i, jnp.where(p == 0, j, nj - 1))
    attn_out = lambda i, p, j: (i, jnp.where(p == 1, j, 0))
    head_dim = d_conv // (2 * N_HEADS)
    kern = functools.partial(_in_proj_kernel, seq_len=seq_len, tiles_per_seq=tiles_per_seq,
                             rows_span_sequences=rows_span_sequences,
                             q_scale=head_dim ** -0.5 * math.log2(math.e))
    return pl.pallas_call(
        kern,
        grid=(n_i, 2, nj),
        in_specs=[
            pl.BlockSpec((tm, d), lambda i, p, j: (i, 0)),
            pl.BlockSpec((1, d), lambda i, p, j: (0, 0)),
            wspec(0), wspec(1), wspec(2),
            pl.BlockSpec((CONV_WIDTH, tn), lambda i, p, j: (0, j)),
            prev_spec,
        ],
        out_specs=[
            pl.BlockSpec((tm, tn), conv_out),
            pl.BlockSpec((tm, tn), attn_out),
            pl.BlockSpec((tm, tn), attn_out),
            ko_spec,
            pl.BlockSpec((tm, tn), attn_out),
            u_spec,
        ],
        out_shape=[
            jax.ShapeDtypeStruct((n, d_conv), BF16),
            jax.ShapeDtypeStruct((n, d_conv), BF16),
            jax.ShapeDtypeStruct((n, d_conv), BF16),
            ko_shape,
            jax.ShapeDtypeStruct((n, d_conv), F32),
            u_shape,
        ],
        scratch_shapes=[pltpu.VMEM((tm, d), BF16), pltpu.VMEM((nj, SUBLANES, tn), F32)],
        compiler_params=_params(3),
        name="in_proj",
    )(x2d, norm_g, w_in_bf, w_in_bf, w_in_bf, conv_w, prev)


def _diff_lambda(lq1_ref, lk1_ref, lq2_ref, lk2_ref, lam_init):
    s1 = jnp.sum(lq1_ref[...] * lk1_ref[...], axis=1, keepdims=True)
    s2 = jnp.sum(lq2_ref[...] * lk2_ref[...], axis=1, keepdims=True)
    return jnp.exp(s1) - jnp.exp(s2) + lam_init


def _stacked_queries(q, head_dim):
    lane = lax.broadcasted_iota(jnp.int32, q.shape, 1)
    zero = jnp.zeros_like(q)
    return jnp.concatenate([jnp.where(lane < head_dim, q, zero),
                            jnp.where(lane >= head_dim, q, zero)], axis=0)


def _scores(qq, kblk):
    return lax.dot_general(qq, kblk, (((1,), (1,)), ((), ())), preferred_element_type=F32)


def _finish_heads(acc, l, lam, hn, lam_init, tq):
    o = acc[:tq] / l[:tq] - lam * (acc[tq:] / l[tq:])
    return (_rms(o, hn) * (1.0 - lam_init)).astype(BF16)


def _attn_prompt_kernel(lq1_ref, lk1_ref, lq2_ref, lk2_ref, hn_ref, q_ref, k_ref, v_ref, o_ref,
                        qt1_ref, qt2_ref, vt_ref, s0_ref, s1_ref, p0_ref, p1_ref,
                        *, lam_init, head_dim, tq):
    seq_len, hw = q_ref.shape
    lam = _diff_lambda(lq1_ref, lk1_ref, lq2_ref, lk2_ref, lam_init)

    qt = q_ref[...].astype(F32).T
    dim = lax.broadcasted_iota(jnp.int32, qt.shape, 0)
    qt1_ref[...] = jnp.where(dim < head_dim, qt, 0.0).astype(BF16)
    qt2_ref[...] = jnp.where(dim >= head_dim, qt, 0.0).astype(BF16)
    vt_ref[:hw, :] = v_ref[...].T.astype(BF16)
    vt_ref[hw:, :] = jnp.ones((vt_ref.shape[0] - hw, seq_len), BF16)

    key_chunk = lax.broadcasted_iota(jnp.int32, (tq, 2 * tq), 0) // CHUNK
    qry_chunk = (lax.broadcasted_iota(jnp.int32, (tq, 2 * tq), 1) % tq) // CHUNK
    visible = key_chunk <= qry_chunk

    bufs = ((s0_ref, p0_ref), (s1_ref, p1_ref))
    n_blocks = seq_len // tq

    def scores(qi):
        q0, n_keys, s_buf = qi * tq, (qi + 1) * tq, bufs[qi % 2][0]
        qqt = jnp.concatenate([qt1_ref[:, q0:n_keys], qt2_ref[:, q0:n_keys]], axis=1)
        s = jnp.dot(k_ref[q0:n_keys, :], qqt, preferred_element_type=F32)
        s = jnp.where(visible, s, NEG_INF)
        s_buf[q0:n_keys, :] = s
        m = jnp.max(s, axis=0, keepdims=True)
        if qi > 0:
            s = jnp.dot(k_ref[:q0, :], qqt, preferred_element_type=F32)
            s_buf[:q0, :] = s
            m = jnp.maximum(m, jnp.max(s, axis=0, keepdims=True))
        return m

    def finish(qi, m):
        q0, n_keys = qi * tq, (qi + 1) * tq
        s_buf, p_buf = bufs[qi % 2]
        p_buf[:n_keys, :] = jnp.exp2(s_buf[:n_keys, :] - m).astype(BF16)
        ot = jnp.dot(vt_ref[:, :n_keys], p_buf[:n_keys, :], preferred_element_type=F32)
        ot = ot[:hw] / ot[hw:hw + 1]
        o = (ot[:, :tq] - lam * ot[:, tq:]).T
        o_ref[q0:n_keys, :] = (_rms(o, hn_ref[...]) * (1.0 - lam_init)).astype(BF16)

    m = scores(0)
    for qi in range(n_blocks):
        m_next = scores(qi + 1) if qi + 1 < n_blocks else None
        finish(qi, m)
        m = m_next


def _attn_prompt(q, k, v, lams, head_norm, *, batch, seq_len, lam_init, tq):
    n, d_attn = q.shape
    hw = d_attn // N_HEADS
    assert seq_len % tq == 0 and tq % CHUNK == 0
    small = pl.BlockSpec((1, hw // 2), lambda b, h: (0, 0))
    head = pl.BlockSpec((seq_len, hw), lambda b, h: (b, h))
    kern = functools.partial(_attn_prompt_kernel, lam_init=lam_init, head_dim=hw // 2, tq=tq)
    return pl.pallas_call(
        kern,
        grid=(batch, N_HEADS),
        in_specs=[small, small, small, small, pl.BlockSpec((1, hw), lambda b, h: (0, 0)),
                  head, head, head],
        out_specs=head,
        out_shape=jax.ShapeDtypeStruct((n, d_attn), BF16),
        scratch_shapes=[pltpu.VMEM((hw, seq_len), BF16), pltpu.VMEM((hw, seq_len), BF16),
                        pltpu.VMEM((hw + BF16_SUBLANES, seq_len), BF16),
                        pltpu.VMEM((seq_len, 2 * tq), F32), pltpu.VMEM((seq_len, 2 * tq), F32),
                        pltpu.VMEM((seq_len, 2 * tq), BF16), pltpu.VMEM((seq_len, 2 * tq), BF16)],
        compiler_params=_params(2),
        name="attn_prompt",
    )(*lams, head_norm, q, k, v)


def _attn_sample_kernel(lq1_ref, lk1_ref, lq2_ref, lk2_ref, hn_ref, q_ref, kc_ref, vc_ref,
                        kn_ref, vn_ref, o_ref, *, lam_init, head_dim):
    tq = q_ref.shape[0]
    lam = _diff_lambda(lq1_ref, lk1_ref, lq2_ref, lk2_ref, lam_init)
    qq = _stacked_queries(q_ref[...], head_dim)
    s_past = jnp.dot(qq, kc_ref[...].astype(BF16), preferred_element_type=F32)
    s_new = _scores(qq, kn_ref[...])
    m = jnp.maximum(jnp.max(s_past, axis=1, keepdims=True), jnp.max(s_new, axis=1, keepdims=True))
    p_past = jnp.exp2(s_past - m)
    p_new = jnp.exp2(s_new - m)
    l = jnp.sum(p_past, axis=1, keepdims=True) + jnp.sum(p_new, axis=1, keepdims=True)
    acc = (jnp.dot(p_past.astype(BF16), vc_ref[...].astype(BF16), preferred_element_type=F32)
           + jnp.dot(p_new.astype(BF16), vn_ref[...].astype(BF16), preferred_element_type=F32))
    o_ref[...] = _finish_heads(acc, l, lam, hn_ref[...], lam_init, tq)


def _attn_sample(q, k_new, v_new, k_past, v_past, lams, head_norm, *, batch, seq_len, past_len, lam_init):
    n, d_attn = q.shape
    hw = d_attn // N_HEADS
    small = pl.BlockSpec((1, hw // 2), lambda b, h: (0, 0))
    new_spec = pl.BlockSpec((seq_len, hw), lambda b, h: (b, h))
    past_spec = pl.BlockSpec((past_len, hw), lambda b, h: (b, h))
    past_t_spec = pl.BlockSpec((hw, past_len), lambda b, h: (b * N_HEADS + h, 0))
    kern = functools.partial(_attn_sample_kernel, lam_init=lam_init, head_dim=hw // 2)
    return pl.pallas_call(
        kern,
        grid=(batch, N_HEADS),
        in_specs=[small, small, small, small, pl.BlockSpec((1, hw), lambda b, h: (0, 0)),
                  new_spec, past_t_spec, past_spec, new_spec, new_spec],
        out_specs=new_spec,
        out_shape=jax.ShapeDtypeStruct((n, d_attn), BF16),
        compiler_params=_params(2),
        name="attn_sample",
    )(*lams, head_norm, q, k_past, v_past, k_new, v_new)


R_E1, R_E2, R_W1, R_W2, R_RANK1, R_RANK2 = 0, 1, 2, 3, 4, 5
OUT_PROJ_CHUNKS = 4


def _pack_bf16_pairs(lo_bf, hi_bf):
    lo = lax.bitcast_convert_type(lo_bf.astype(F32), U32)
    hi = lax.bitcast_convert_type(hi_bf.astype(F32), U32)
    return (hi & jnp.uint32(0xFFFF0000)) | (lo >> 16)


def _unpack_bf16_pairs(w):
    lo = lax.bitcast_convert_type(w << 16, F32).astype(BF16)
    hi = lax.bitcast_convert_type(w & jnp.uint32(0xFFFF0000), F32).astype(BF16)
    return lo, hi


def _out_proj_kernel(ap_ref, op_ref, xp_ref, as_ref, os_ref, xs_ref, w_ref, g_ref,
                     wr_ref, br_ref, h_ref, hnp_ref, rinfo_ref, cnt_ref, carry_ref, *, n_prompt_tiles):
    i = pl.program_id(0)

    @pl.when(i == 0)
    def _():
        carry_ref[...] = jnp.zeros_like(carry_ref)

    @pl.when(i < n_prompt_tiles)
    def _():
        _out_proj_rows(ap_ref, op_ref, xp_ref, w_ref, g_ref, wr_ref, br_ref,
                       h_ref, hnp_ref, rinfo_ref, cnt_ref, carry_ref)

    @pl.when(i >= n_prompt_tiles)
    def _():
        _out_proj_rows(as_ref, os_ref, xs_ref, w_ref, g_ref, wr_ref, br_ref,
                       h_ref, hnp_ref, rinfo_ref, cnt_ref, carry_ref)


def _out_proj_rows(a_ref, o_ref, x_ref, w_ref, g_ref, wr_ref, br_ref,
                   h_ref, hnp_ref, rinfo_ref, cnt_ref, carry_ref):
    tm, d = x_ref.shape
    cw = d // OUT_PROJ_CHUNKS
    a, o = a_ref[...], o_ref[...]

    ss = jnp.zeros((tm, 1), F32)
    for c in range(OUT_PROJ_CHUNKS):
        cols = slice(c * cw, (c + 1) * cw)
        h = (x_ref[:, cols] + jnp.dot(a, w_ref[0, c], preferred_element_type=F32)
             + jnp.dot(o, w_ref[1, c], preferred_element_type=F32))
        h_ref[:, cols] = h
        ss = ss + jnp.sum(h * h, axis=1, keepdims=True)
    inv = lax.rsqrt(ss / d + EPS)

    logits = br_ref[...]
    for c in range(OUT_PROJ_CHUNKS // 2):
        halves = []
        for cc in (c, c + OUT_PROJ_CHUNKS // 2):
            cols = slice(cc * cw, (cc + 1) * cw)
            hb = (h_ref[:, cols] * inv * g_ref[:, cols]).astype(BF16)
            logits = logits + jnp.dot(hb, wr_ref[cols, :], preferred_element_type=F32)
            halves.append(hb)
        hnp_ref[:, c * cw:(c + 1) * cw] = _pack_bf16_pairs(*halves)

    lane = lax.broadcasted_iota(jnp.int32, logits.shape, 1)
    lane_f = lane.astype(F32)
    lg = jnp.where(lane < N_GROUPS, logits, NEG_INF)
    mg = jnp.max(lg, axis=1, keepdims=True)
    grp = jnp.min(jnp.where(lg == mg, lane_f, float(LANES)), axis=1, keepdims=True)
    gate_g = 1.0 / jnp.sum(jnp.exp(lg - mg), axis=1, keepdims=True)
    ex = lane - N_GROUPS
    in_grp = (ex >= 0) & (ex < N_EXPERTS) & ((ex >> 3).astype(F32) == grp)
    le = jnp.where(in_grp, logits, NEG_INF)
    v1 = jnp.max(le, axis=1, keepdims=True)
    i1 = jnp.min(jnp.where(le == v1, lane_f, float(LANES)), axis=1, keepdims=True)
    le2 = jnp.where(lane_f == i1, NEG_INF, le)
    v2 = jnp.max(le2, axis=1, keepdims=True)
    i2 = jnp.min(jnp.where(le2 == v2, lane_f, float(LANES)), axis=1, keepdims=True)
    e21 = jnp.exp(v2 - v1)
    w1 = gate_g / (1.0 + e21)
    w2 = gate_g * e21 / (1.0 + e21)
    e1 = i1 - float(N_GROUPS)
    e2 = i2 - float(N_GROUPS)

    onehot = jnp.where((lane_f == e1) | (lane_f == e2), 1.0, 0.0)
    r = lax.broadcasted_iota(jnp.int32, (tm, tm), 0)
    c = lax.broadcasted_iota(jnp.int32, (tm, tm), 1)
    earlier = jnp.where(c < r, 1.0, 0.0).astype(BF16)
    before = jnp.dot(earlier, onehot.astype(BF16), preferred_element_type=F32) + carry_ref[0:1, :]
    rank1 = _lane_pick(before, lane_f, e1)
    rank2 = _lane_pick(before, lane_f, e2)
    new_carry = carry_ref[0:1, :] + jnp.sum(onehot, axis=0, keepdims=True)
    carry_ref[...] = jnp.broadcast_to(new_carry, carry_ref.shape)
    cnt_ref[...] = jnp.broadcast_to(new_carry, cnt_ref.shape)

    rec = jnp.zeros_like(logits)
    for idx, val in ((R_E1, e1), (R_E2, e2), (R_W1, w1), (R_W2, w2), (R_RANK1, rank1), (R_RANK2, rank2)):
        rec = jnp.where(lane == idx, val, rec)
    rinfo_ref[...] = rec


def _out_proj(a_p, o_p, x_p, a_s, o_s, x_s, w_out_chunks, norm_g, w_router, b_router, *, tm):
    n_p, d = x_p.shape
    n_s = x_s.shape[0]
    dc = a_p.shape[1]
    assert n_p % tm == 0 and n_s == tm, "sample rows must fill exactly one tile"
    npt = n_p // tm
    n_all = n_p + n_s
    p_idx = lambda i: (jnp.minimum(i, npt - 1), 0)
    zero = lambda i: (0, 0)
    kern = functools.partial(_out_proj_kernel, n_prompt_tiles=npt)
    return pl.pallas_call(
        kern,
        grid=(npt + 1,),
        in_specs=[
            pl.BlockSpec((tm, dc), p_idx), pl.BlockSpec((tm, dc), p_idx), pl.BlockSpec((tm, d), p_idx),
            pl.BlockSpec((tm, dc), zero), pl.BlockSpec((tm, dc), zero), pl.BlockSpec((tm, d), zero),
            pl.BlockSpec(w_out_chunks.shape, lambda i: (0, 0, 0, 0)),
            pl.BlockSpec((1, d), zero),
            pl.BlockSpec((d, LANES), zero), pl.BlockSpec((1, LANES), zero),
        ],
        out_specs=[
            pl.BlockSpec((tm, d), lambda i: (i, 0)),
            pl.BlockSpec((tm, d // 2), lambda i: (i, 0)),
            pl.BlockSpec((tm, LANES), lambda i: (i, 0)),
            pl.BlockSpec((SUBLANES, LANES), zero),
        ],
        out_shape=[
            jax.ShapeDtypeStruct((n_all, d), F32),
            jax.ShapeDtypeStruct((n_all, d // 2), U32),
            jax.ShapeDtypeStruct((n_all, LANES), F32),
            jax.ShapeDtypeStruct((SUBLANES, LANES), F32),
        ],
        scratch_shapes=[pltpu.VMEM((SUBLANES, LANES), F32)],
        compiler_params=_params(1),
        name="out_proj",
    )(a_p, o_p, x_p, a_s, o_s, x_s, w_out_chunks, norm_g, w_router, b_router)


EXPERT_DMA_GROUPS = 8


def _experts_kernel(tile_expert_ref, n_used_ref, src_hbm, dst_hbm, hnp_hbm, wi_ref, wo_ref, y2_hbm,
                    src_ref, dst_ref, xs_ref, ys_ref, src_sem, dst_sem, gather_sem, scatter_sem):
    w = pl.program_id(0)
    n_used = n_used_ref[0]
    tm = xs_ref.shape[1]
    last_row = src_hbm.shape[0] - 1
    group = tm // EXPERT_DMA_GROUPS

    def src_copy(tile, slot):
        return pltpu.make_async_copy(src_hbm.at[pl.ds(tile, 1), :], src_ref.at[pl.ds(slot, 1), :],
                                     src_sem.at[slot])

    def dst_copy(tile, slot):
        return pltpu.make_async_copy(dst_hbm.at[pl.ds(tile, 1), :], dst_ref.at[pl.ds(slot, 1), :],
                                     dst_sem.at[slot])

    def gather_row(slot, r):
        return pltpu.make_async_copy(hnp_hbm.at[pl.ds(src_ref[slot, r], 1), :],
                                     xs_ref.at[slot, pl.ds(r, 1), :], gather_sem.at[slot])

    def scatter_row(slot, r):
        return pltpu.make_async_copy(ys_ref.at[slot, pl.ds(r, 1), :],
                                     y2_hbm.at[pl.ds(dst_ref[slot, r], 1), :], scatter_sem.at[slot])

    def wait_gathers(slot):
        pltpu.make_async_copy(xs_ref.at[slot], xs_ref.at[slot], gather_sem.at[slot]).wait()

    def wait_scatters(slot):
        pltpu.make_async_copy(ys_ref.at[slot], ys_ref.at[slot], scatter_sem.at[slot]).wait()

    @pl.when(w == 0)
    def _():
        src_copy(0, 0).start()
        src_copy(1, 1).start()
        dst_copy(last_row, 1).start()
        src_copy(0, 0).wait()
        for r in range(tm):
            gather_row(0, r).start(priority=r % 2)
        ys_ref[1] = jnp.zeros(ys_ref.shape[1:], ys_ref.dtype)
        half = y2_hbm.shape[0] // 2
        for blk in range(2):
            spare = pltpu.make_async_copy(ys_ref.at[1], y2_hbm.at[pl.ds((blk + 1) * half - tm, tm), :],
                                          scatter_sem.at[1])
            spare.start()
            spare.wait()

    def compute_tile(cur):
        nxt = 1 - cur
        wait_gathers(cur)
        src_copy(w + 1, nxt).wait()
        dst_copy(jnp.maximum(w - 1, 0), nxt).wait()
        src_copy(w + 2, cur).start()
        dst_copy(w, cur).start()

        def move_rows(g):
            for r in range(g * group, (g + 1) * group):
                gather_row(nxt, r).start(priority=r % 2)
                scatter_row(nxt, r).start(priority=r % 2)

        lo, hi = _unpack_bf16_pairs(xs_ref[cur])
        half = lo.shape[1]
        n_chunks = EXPERT_DMA_GROUPS // 2
        f2 = wi_ref.shape[2]
        cw = f2 // n_chunks
        gu = []
        for c in range(n_chunks):
            cols = slice(c * cw, (c + 1) * cw)
            gu.append(jnp.dot(lo, wi_ref[0, :half, cols], preferred_element_type=F32)
                      + jnp.dot(hi, wi_ref[0, half:, cols], preferred_element_type=F32))
            move_rows(c)
        gate = jnp.concatenate(gu[:n_chunks // 2], axis=1)
        up = jnp.concatenate(gu[n_chunks // 2:], axis=1)
        act = (gate * jax.nn.sigmoid(gate) * up).astype(BF16)
        d = wo_ref.shape[2]
        ow = d // n_chunks
        for c in range(n_chunks):
            cols = slice(c * ow, (c + 1) * ow)
            ys_ref[cur, :, cols] = jnp.dot(act, wo_ref[0, :, cols], preferred_element_type=F32)
            move_rows(n_chunks + c)

    def drain(cur):
        nxt = 1 - cur
        wait_gathers(cur)
        src_copy(w + 1, nxt).wait()
        dst_copy(w - 1, nxt).wait()
        for r in range(tm):
            scatter_row(nxt, r).start(priority=r % 2)
        wait_scatters(nxt)

    for parity in (0, 1):
        mine = (w % 2) == parity

        @pl.when(mine & (w >= 1) & (w <= n_used))
        def _():
            wait_scatters(parity)

        pl.when(mine & (w < n_used))(functools.partial(compute_tile, parity))
        pl.when(mine & (w == n_used))(functools.partial(drain, parity))


def _experts(tile_expert, n_used, src_tab, dst_tab, hnp, w_ei_bf, w_eo_bf, *, n_out_rows, tm):
    n_e, d, f2 = w_ei_bf.shape
    dw = hnp.shape[1]
    n_steps = src_tab.shape[0] - 1
    assert tm % EXPERT_DMA_GROUPS == 0 and src_tab.shape == dst_tab.shape == (n_steps + 1, tm)

    def expert_of(w, te, nu):
        return te[jnp.minimum(w, nu[0] - 1)]

    grid_spec = pltpu.PrefetchScalarGridSpec(
        num_scalar_prefetch=2,
        grid=(n_steps,),
        in_specs=[
            pl.BlockSpec(memory_space=pl.ANY),
            pl.BlockSpec(memory_space=pl.ANY),
            pl.BlockSpec(memory_space=pl.ANY),
            pl.BlockSpec((1, d, f2), lambda w, te, nu: (expert_of(w, te, nu), 0, 0)),
            pl.BlockSpec((1, f2 // 2, d), lambda w, te, nu: (expert_of(w, te, nu), 0, 0)),
        ],
        out_specs=pl.BlockSpec(memory_space=pl.ANY),
        scratch_shapes=[pltpu.SMEM((2, tm), jnp.int32), pltpu.SMEM((2, tm), jnp.int32),
                        pltpu.VMEM((2, tm, dw), U32), pltpu.VMEM((2, tm, d), F32),
                        pltpu.SemaphoreType.DMA((2,)), pltpu.SemaphoreType.DMA((2,)),
                        pltpu.SemaphoreType.DMA((2,)), pltpu.SemaphoreType.DMA((2,))],
    )
    return pl.pallas_call(
        _experts_kernel,
        grid_spec=grid_spec,
        out_shape=jax.ShapeDtypeStruct((n_out_rows, d), F32),
        compiler_params=_params(1),
        name="experts",
    )(tile_expert, n_used, src_tab, dst_tab, hnp, w_ei_bf, w_eo_bf)


def _expert_tiles(counts, e12, r12, n_all, tm):
    max_tiles = (2 * n_all) // tm + N_EXPERTS
    n_rows = max_tiles + 2
    tiles_per_expert = (counts + tm - 1) // tm
    tile_end = jnp.cumsum(tiles_per_expert)
    n_used = tile_end[-1]
    row_start = (tile_end - tiles_per_expert) * tm
    t = jnp.minimum(jnp.arange(n_rows, dtype=jnp.int32), n_used - 1)
    tile_expert = jnp.sum(tile_end[None, :] <= t[:, None], axis=1).astype(jnp.int32)

    pos = (row_start[e12] + r12).reshape(-1)
    pair = jnp.arange(2 * n_all, dtype=jnp.int32)
    filled = jnp.full((n_rows * tm,), -1, jnp.int32).at[pos].set(pair, unique_indices=True, mode="promise_in_bounds")
    filled = filled.reshape(n_rows, tm)
    tile = jnp.arange(n_rows, dtype=jnp.int32)[:, None]
    spare = (tile % 2) * (n_all + tm) + n_all + jnp.arange(tm, dtype=jnp.int32)[None, :]
    src = jnp.where(filled >= 0, filled // 2, 0)
    dst = jnp.where(filled >= 0, (filled % 2) * (n_all + tm) + filled // 2, spare)
    return tile_expert, n_used.reshape(1).astype(jnp.int32), src, dst


def _combine_kernel(h_ref, rinfo_ref, g_ref, ya_ref, yb_ref, yp_ref, ysm_ref, *, n_prompt_tiles):
    i = pl.program_id(0)
    rinfo = rinfo_ref[...]
    lane = lax.broadcasted_iota(jnp.int32, rinfo.shape, 1)
    w1 = _lane_pick(rinfo, lane, R_W1)
    w2 = _lane_pick(rinfo, lane, R_W2)
    hp = h_ref[...] + (w1 * ya_ref[...] + w2 * yb_ref[...])
    y = _rms(hp, g_ref[...])

    @pl.when(i < n_prompt_tiles)
    def _():
        yp_ref[...] = y

    @pl.when(i >= n_prompt_tiles)
    def _():
        ysm_ref[...] = y


def _combine(h, rinfo, norm_g, y2, *, n_prompt, tm):
    n_all, d = h.shape
    nt = n_all // tm
    npt = n_prompt // tm
    assert n_all - n_prompt == tm and y2.shape == (2 * (n_all + tm), d)
    kern = functools.partial(_combine_kernel, n_prompt_tiles=npt)
    return pl.pallas_call(
        kern,
        grid=(nt,),
        in_specs=[
            pl.BlockSpec((tm, d), lambda i: (i, 0)),
            pl.BlockSpec((tm, LANES), lambda i: (i, 0)),
            pl.BlockSpec((1, d), lambda i: (0, 0)),
            pl.BlockSpec((tm, d), lambda i: (i, 0)),
            pl.BlockSpec((tm, d), lambda i: (nt + 1 + i, 0)),
        ],
        out_specs=[
            pl.BlockSpec((tm, d), lambda i: (jnp.minimum(i, npt - 1), 0)),
            pl.BlockSpec((tm, d), lambda i: (0, 0)),
        ],
        out_shape=[jax.ShapeDtypeStruct((n_prompt, d), F32), jax.ShapeDtypeStruct((tm, d), F32)],
        compiler_params=_params(1),
        name="combine",
    )(h, rinfo, norm_g, y2, y2)


ROW_TILE = 256
PROJ_ROWS = 512
PROJ_COLS = 512
ATTN_Q_ROWS = 256


def kernel(x_prompt, x_sample, cache_k, cache_v, state_conv, norm_mix, w_in, conv_w, lambda_q1,
           lambda_k1, lambda_q2, lambda_k2, head_norm, w_out, norm_ffn, w_router_group,
           b_router_group, w_router_expert, b_router_expert, w_expert_in, w_expert_out, norm_final):
    depth = w_in.shape[0]
    b, t, d = x_prompt.shape
    db, dt, _ = x_sample.shape
    past = cache_k.shape[2]
    d_conv = conv_w.shape[2]
    d_attn = d - d_conv
    n_p, n_s = b * t, db * dt
    n_all = n_p + n_s
    tm = ROW_TILE

    hp = x_prompt.reshape(n_p, d)
    hs = x_sample.reshape(n_s, d)
    outs = {name: [] for name in ("kp", "vp", "cp", "ks", "vs", "cs")}
    y_p = y_s = None
    for l in range(depth):
        lam_init = 0.8 - 0.6 * math.exp(-0.3 * l)
        w_in_bf = w_in[l].astype(BF16)
        w_out_bf = w_out[l].astype(BF16).reshape(2, d_conv, OUT_PROJ_CHUNKS, d // OUT_PROJ_CHUNKS)
        w_out_bf = w_out_bf.transpose(0, 2, 1, 3)
        w_ei_bf = w_expert_in[l].astype(BF16)
        w_eo_bf = w_expert_out[l].astype(BF16)
        g_mix = norm_mix[l].reshape(1, d)
        lams = tuple(p[l].reshape(1, -1) for p in (lambda_q1, lambda_k1, lambda_q2, lambda_k2))
        hnorm = head_norm[l].reshape(1, -1)
        pad = LANES - N_GROUPS - N_EXPERTS
        w_router = jnp.concatenate([w_router_group[l], w_router_expert[l], jnp.zeros((d, pad), F32)],
                                   axis=1).astype(BF16)
        b_router = jnp.concatenate([b_router_group[l], b_router_expert[l], jnp.zeros((pad,), F32)]).reshape(1, LANES)

        zero_prev = jnp.zeros((b, CONV_WIDTH - 1, d_conv), F32)
        a_p, q_p, kb_p, kt_p, v_p, tail_p = _in_proj(hp, g_mix, w_in_bf, conv_w[l], zero_prev,
                                                     seq_len=t, tm=min(PROJ_ROWS, t), tn=PROJ_COLS)
        o_p = _attn_prompt(q_p, kb_p, v_p, lams, hnorm, batch=b, seq_len=t, lam_init=lam_init,
                           tq=min(ATTN_Q_ROWS, t))
        prev_rows = jnp.stack([jnp.repeat(state_conv[l][:, 0], dt, axis=0),
                               jnp.repeat(state_conv[l][:, 1], dt, axis=0)])
        a_s, q_s, kb_s, k_s, v_s, u_s = _in_proj(hs, g_mix, w_in_bf, conv_w[l], prev_rows,
                                                 seq_len=dt, tm=n_s, tn=PROJ_COLS)
        kt_past = cache_k[l].transpose(0, 2, 3, 4, 1).reshape(db * d_attn, past)
        o_s = _attn_sample(q_s, kb_s, v_s, kt_past, cache_v[l].reshape(db * past, d_attn), lams, hnorm,
                           batch=db, seq_len=dt, past_len=past, lam_init=lam_init)

        h, hnp, rinfo, cnt = _out_proj(a_p, o_p, hp, a_s, o_s, hs, w_out_bf,
                                       norm_ffn[l].reshape(1, d), w_router, b_router, tm=tm)
        counts = cnt[0, :N_EXPERTS].astype(jnp.int32)
        e12 = rinfo[:, R_E1:R_E2 + 1].astype(jnp.int32)
        r12 = rinfo[:, R_RANK1:R_RANK2 + 1].astype(jnp.int32)
        tile_expert, n_used, src_tab, dst_tab = _expert_tiles(counts, e12, r12, n_all, tm)
        y2 = _experts(tile_expert, n_used, src_tab, dst_tab, hnp, w_ei_bf, w_eo_bf,
                      n_out_rows=2 * n_all + 2 * tm, tm=tm)
        last = l == depth - 1
        g_out = norm_final.reshape(1, d) if last else jnp.ones((1, d), F32)
        y_p, y_s = _combine(h, rinfo, g_out, y2, n_prompt=n_p, tm=tm)
        assert last, "multi-layer stacks need an un-normalised combine output"

        outs["kp"].append(kt_p.reshape(b, N_HEADS, 2, d_attn // (2 * N_HEADS), t).transpose(0, 4, 1, 2, 3))
        outs["vp"].append(v_p.reshape(b, t, N_HEADS, d_attn // N_HEADS))
        tails = tail_p.reshape(b, t // min(PROJ_ROWS, t), SUBLANES, d_conv)
        outs["cp"].append(tails[:, -1, SUBLANES - (CONV_WIDTH - 1):, :])
        outs["ks"].append(k_s.reshape(db, dt, N_HEADS, 2, d_attn // (2 * N_HEADS)))
        outs["vs"].append(v_s.reshape(db, dt, N_HEADS, d_attn // N_HEADS))
        outs["cs"].append(u_s.reshape(db, dt, d_conv)[:, dt - (CONV_WIDTH - 1):, :])

    return (y_p.reshape(b, t, d), y_s.reshape(db, dt, d),
            jnp.stack(outs["kp"]), jnp.stack(outs["vp"]), jnp.stack(outs["cp"]),
            jnp.stack(outs["ks"]), jnp.stack(outs["vs"]), jnp.stack(outs["cs"]))
```

```python
import functools
import math

import jax
import jax.numpy as jnp
from jax import lax
from jax.experimental import pallas as pl
from jax.experimental.pallas import tpu as pltpu

EPS = 1e-6
N_HEADS = 8
CHUNK = 64
CONV_WIDTH = 3
N_GROUPS = 4
EXPERTS_PER_GROUP = 8
N_EXPERTS = N_GROUPS * EXPERTS_PER_GROUP
LANES = 128
SUBLANES = 8
BF16_SUBLANES = 16
VMEM_LIMIT = 56 * 1024 * 1024

F32 = jnp.float32
BF16 = jnp.bfloat16
U32 = jnp.uint32
NEG_INF = float("-inf")


def _params(n_axes, vmem=VMEM_LIMIT, **kw):
    return pltpu.CompilerParams(dimension_semantics=("arbitrary",) * n_axes,
                                vmem_limit_bytes=vmem, **kw)


def _rms(x, g):
    return x * lax.rsqrt(jnp.mean(x * x, axis=-1, keepdims=True) + EPS) * g


def _lane_pick(x, lane, idx):
    return jnp.sum(jnp.where(lane == idx, x, 0.0), axis=1, keepdims=True)


def _in_proj_kernel(x_ref, g_ref, wa_ref, wb_ref, wc_ref, cw_ref, prev_ref,
                    a_ref, q_ref, kb_ref, ko_ref, v_ref, u_ref, xn_ref, carry_ref,
                    *, seq_len, tiles_per_seq, rows_span_sequences, q_scale):
    i = pl.program_id(0)
    part = pl.program_id(1)
    j = pl.program_id(2)
    tm = x_ref.shape[0]

    @pl.when((part == 0) & (j == 0))
    def _():
        xn_ref[...] = _rms(x_ref[...], g_ref[...]).astype(BF16)

    xn = xn_ref[...]
    za = jnp.dot(xn, wa_ref[...], preferred_element_type=F32)
    zb = jnp.dot(xn, wb_ref[...], preferred_element_type=F32)
    zc = jnp.dot(xn, wc_ref[...], preferred_element_type=F32)

    @pl.when(part == 0)
    def _():
        u = zb * zc
        row = lax.broadcasted_iota(jnp.int32, u.shape, 0)
        if rows_span_sequences:
            rmod = row % seq_len
            pm2, pm1 = prev_ref[0], prev_ref[1]
            first, second = rmod == 0, rmod == 1
            u_ref[...] = u
        else:
            seq_start = (i % tiles_per_seq) == 0
            tail = carry_ref[j]
            pm2 = jnp.where(seq_start, prev_ref[0, 0:1, :], tail[SUBLANES - 2:SUBLANES - 1, :])
            pm1 = jnp.where(seq_start, prev_ref[0, 1:2, :], tail[SUBLANES - 1:SUBLANES, :])
            first, second = row == 0, row == 1
            new_tail = u[tm - SUBLANES:, :]
            carry_ref[j] = new_tail
            u_ref[0] = new_tail
        u1 = jnp.where(first, pm1, pltpu.roll(u, 1, 0))
        u2 = jnp.where(first, pm2, jnp.where(second, pm1, pltpu.roll(u, 2, 0)))
        cw = cw_ref[...]
        y = cw[0:1, :] * u2 + cw[1:2, :] * u1 + cw[2:3, :] * u
        a_ref[...] = (za * y).astype(BF16)

    @pl.when(part == 1)
    def _():
        q_ref[...] = (za * q_scale).astype(BF16)
        kb_ref[...] = zb.astype(BF16)
        if rows_span_sequences:
            ko_ref[...] = zb
        else:
            ko_ref[0] = zb.T
        v_ref[...] = zc


def _in_proj(x2d, norm_g, w_in_bf, conv_w, prev, *, seq_len, tm, tn):
    n, d = x2d.shape
    d_conv = conv_w.shape[1]
    assert w_in_bf.shape == (d, 6 * d_conv), "conv and attention widths must match"
    assert n % tm == 0 and d_conv % tn == 0
    nj = d_conv // tn
    n_i = n // tm
    rows_span_sequences = tm > seq_len
    if rows_span_sequences:
        assert tm % seq_len == 0 and prev.shape == (2, n, d_conv)
        tiles_per_seq = 1
        prev_spec = pl.BlockSpec((2, tm, tn), lambda i, p, j: (0, i, j))
        u_shape = jax.ShapeDtypeStruct((n, d_conv), F32)
        u_spec = pl.BlockSpec((tm, tn), lambda i, p, j: (i, jnp.where(p == 0, j, nj - 1)))
        ko_shape = jax.ShapeDtypeStruct((n, d_conv), F32)
        ko_spec = pl.BlockSpec((tm, tn), lambda i, p, j: (i, jnp.where(p == 1, j, 0)))
    else:
        assert seq_len % tm == 0 and prev.shape == (n // seq_len, CONV_WIDTH - 1, d_conv)
        tiles_per_seq = seq_len // tm
        prev_spec = pl.BlockSpec((1, CONV_WIDTH - 1, tn), lambda i, p, j: (i // tiles_per_seq, 0, j))
        u_shape = jax.ShapeDtypeStruct((n_i, SUBLANES, d_conv), F32)
        u_spec = pl.BlockSpec((1, SUBLANES, tn), lambda i, p, j: (i, 0, jnp.where(p == 0, j, nj - 1)))
        ko_shape = jax.ShapeDtypeStruct((n // seq_len, d_conv, seq_len), F32)
        ko_spec = pl.BlockSpec((1, tn, tm), lambda i, p, j: (i // tiles_per_seq, jnp.where(p == 1, j, 0),
                                                             i % tiles_per_seq))

    def wspec(k):
        return pl.BlockSpec((d, tn), lambda i, p, j: (0, p * 3 * nj + k * nj + j))

    conv_out = lambda i, p, j: (i, jnp.where(p == 0, j, nj - 1))
    attn_out = lambda i, p, j: (i, jnp.where(p == 1, j, 0))
    head_dim = d_conv // (2 * N_HEADS)
    kern = functools.partial(_in_proj_kernel, seq_len=seq_len, tiles_per_seq=tiles_per_seq,
                             rows_span_sequences=rows_span_sequences,
                             q_scale=head_dim ** -0.5 * math.log2(math.e))
    return pl.pallas_call(
        kern,
        grid=(n_i, 2, nj),
        in_specs=[
            pl.BlockSpec((tm, d), lambda i, p, j: (i, 0)),
            pl.BlockSpec((1, d), lambda i, p, j: (0, 0)),
            wspec(0), wspec(1), wspec(2),
            pl.BlockSpec((CONV_WIDTH, tn), lambda i, p, j: (0, j)),
            prev_spec,
        ],
        out_specs=[
            pl.BlockSpec((tm, tn), conv_out),
            pl.BlockSpec((tm, tn), attn_out),
            pl.BlockSpec((tm, tn), attn_out),
            ko_spec,
            pl.BlockSpec((tm, tn), attn_out),
            u_spec,
        ],
        out_shape=[
            jax.ShapeDtypeStruct((n, d_conv), BF16),
            jax.ShapeDtypeStruct((n, d_conv), BF16),
            jax.ShapeDtypeStruct((n, d_conv), BF16),
            ko_shape,
            jax.ShapeDtypeStruct((n, d_conv), F32),
            u_shape,
        ],
        scratch_shapes=[pltpu.VMEM((tm, d), BF16), pltpu.VMEM((nj, SUBLANES, tn), F32)],
        compiler_params=_params(3),
        name="in_proj",
    )(x2d, norm_g, w_in_bf, w_in_bf, w_in_bf, conv_w, prev)


def _diff_lambda(lq1_ref, lk1_ref, lq2_ref, lk2_ref, lam_init):
    s1 = jnp.sum(lq1_ref[...] * lk1_ref[...], axis=1, keepdims=True)
    s2 = jnp.sum(lq2_ref[...] * lk2_ref[...], axis=1, keepdims=True)
    return jnp.exp(s1) - jnp.exp(s2) + lam_init


def _stacked_queries(q, head_dim):
    lane = lax.broadcasted_iota(jnp.int32, q.shape, 1)
    zero = jnp.zeros_like(q)
    return jnp.concatenate([jnp.where(lane < head_dim, q, zero),
                            jnp.where(lane >= head_dim, q, zero)], axis=0)


def _scores(qq, kblk):
    return lax.dot_general(qq, kblk, (((1,), (1,)), ((), ())), preferred_element_type=F32)


def _finish_heads(acc, l, lam, hn, lam_init, tq):
    o = acc[:tq] / l[:tq] - lam * (acc[tq:] / l[tq:])
    return (_rms(o, hn) * (1.0 - lam_init)).astype(BF16)


def _attn_prompt_kernel(lq1_ref, lk1_ref, lq2_ref, lk2_ref, hn_ref, q_ref, k_ref, v_ref, o_ref,
                        qt1_ref, qt2_ref, vt_ref, s0_ref, s1_ref, p0_ref, p1_ref,
                        *, lam_init, head_dim, tq):
    seq_len, hw = q_ref.shape
    lam = _diff_lambda(lq1_ref, lk1_ref, lq2_ref, lk2_ref, lam_init)

    qt = q_ref[...].astype(F32).T
    dim = lax.broadcasted_iota(jnp.int32, qt.shape, 0)
    qt1_ref[...] = jnp.where(dim < head_dim, qt, 0.0).astype(BF16)
    qt2_ref[...] = jnp.where(dim >= head_dim, qt, 0.0).astype(BF16)
    vt_ref[:hw, :] = v_ref[...].T.astype(BF16)
    vt_ref[hw:, :] = jnp.ones((vt_ref.shape[0] - hw, seq_len), BF16)

    key_chunk = lax.broadcasted_iota(jnp.int32, (tq, 2 * tq), 0) // CHUNK
    qry_chunk = (lax.broadcasted_iota(jnp.int32, (tq, 2 * tq), 1) % tq) // CHUNK
    visible = key_chunk <= qry_chunk

    bufs = ((s0_ref, p0_ref), (s1_ref, p1_ref))
    n_blocks = seq_len // tq

    def scores(qi):
        q0, n_keys, s_buf = qi * tq, (qi + 1) * tq, bufs[qi % 2][0]
        qqt = jnp.concatenate([qt1_ref[:, q0:n_keys], qt2_ref[:, q0:n_keys]], axis=1)
        s = jnp.dot(k_ref[q0:n_keys, :], qqt, preferred_element_type=F32)
        s = jnp.where(visible, s, NEG_INF)
        s_buf[q0:n_keys, :] = s
        m = jnp.max(s, axis=0, keepdims=True)
        if qi > 0:
            s = jnp.dot(k_ref[:q0, :], qqt, preferred_element_type=F32)
            s_buf[:q0, :] = s
            m = jnp.maximum(m, jnp.max(s, axis=0, keepdims=True))
        return m

    def finish(qi, m):
        q0, n_keys = qi * tq, (qi + 1) * tq
        s_buf, p_buf = bufs[qi % 2]
        p_buf[:n_keys, :] = jnp.exp2(s_buf[:n_keys, :] - m).astype(BF16)
        ot = jnp.dot(vt_ref[:, :n_keys], p_buf[:n_keys, :], preferred_element_type=F32)
        ot = ot[:hw] / ot[hw:hw + 1]
        o = (ot[:, :tq] - lam * ot[:, tq:]).T
        o_ref[q0:n_keys, :] = (_rms(o, hn_ref[...]) * (1.0 - lam_init)).astype(BF16)

    m = scores(0)
    for qi in range(n_blocks):
        m_next = scores(qi + 1) if qi + 1 < n_blocks else None
        finish(qi, m)
        m = m_next


def _attn_prompt(q, k, v, lams, head_norm, *, batch, seq_len, lam_init, tq):
    n, d_attn = q.shape
    hw = d_attn // N_HEADS
    assert seq_len % tq == 0 and tq % CHUNK == 0
    small = pl.BlockSpec((1, hw // 2), lambda b, h: (0, 0))
    head = pl.BlockSpec((seq_len, hw), lambda b, h: (b, h))
    kern = functools.partial(_attn_prompt_kernel, lam_init=lam_init, head_dim=hw // 2, tq=tq)
    return pl.pallas_call(
        kern,
        grid=(batch, N_HEADS),
        in_specs=[small, small, small, small, pl.BlockSpec((1, hw), lambda b, h: (0, 0)),
                  head, head, head],
        out_specs=head,
        out_shape=jax.ShapeDtypeStruct((n, d_attn), BF16),
        scratch_shapes=[pltpu.VMEM((hw, seq_len), BF16), pltpu.VMEM((hw, seq_len), BF16),
                        pltpu.VMEM((hw + BF16_SUBLANES, seq_len), BF16),
                        pltpu.VMEM((seq_len, 2 * tq), F32), pltpu.VMEM((seq_len, 2 * tq), F32),
                        pltpu.VMEM((seq_len, 2 * tq), BF16), pltpu.VMEM((seq_len, 2 * tq), BF16)],
        compiler_params=_params(2),
        name="attn_prompt",
    )(*lams, head_norm, q, k, v)


def _attn_sample_kernel(lq1_ref, lk1_ref, lq2_ref, lk2_ref, hn_ref, q_ref, kc_ref, vc_ref,
                        kn_ref, vn_ref, o_ref, *, lam_init, head_dim):
    tq = q_ref.shape[0]
    lam = _diff_lambda(lq1_ref, lk1_ref, lq2_ref, lk2_ref, lam_init)
    qq = _stacked_queries(q_ref[...], head_dim)
    s_past = jnp.dot(qq, kc_ref[...].astype(BF16), preferred_element_type=F32)
    s_new = _scores(qq, kn_ref[...])
    m = jnp.maximum(jnp.max(s_past, axis=1, keepdims=True), jnp.max(s_new, axis=1, keepdims=True))
    p_past = jnp.exp2(s_past - m)
    p_new = jnp.exp2(s_new - m)
    l = jnp.sum(p_past, axis=1, keepdims=True) + jnp.sum(p_new, axis=1, keepdims=True)
    acc = (jnp.dot(p_past.astype(BF16), vc_ref[...].astype(BF16), preferred_element_type=F32)
           + jnp.dot(p_new.astype(BF16), vn_ref[...].astype(BF16), preferred_element_type=F32))
    o_ref[...] = _finish_heads(acc, l, lam, hn_ref[...], lam_init, tq)


def _attn_sample(q, k_new, v_new, k_past, v_past, lams, head_norm, *, batch, seq_len, past_len, lam_init):
    n, d_attn = q.shape
    hw = d_attn // N_HEADS
    small = pl.BlockSpec((1, hw // 2), lambda b, h: (0, 0))
    new_spec = pl.BlockSpec((seq_len, hw), lambda b, h: (b, h))
    past_spec = pl.BlockSpec((past_len, hw), lambda b, h: (b, h))
    past_t_spec = pl.BlockSpec((hw, past_len), lambda b, h: (b * N_HEADS + h, 0))
    kern = functools.partial(_attn_sample_kernel, lam_init=lam_init, head_dim=hw // 2)
    return pl.pallas_call(
        kern,
        grid=(batch, N_HEADS),
        in_specs=[small, small, small, small, pl.BlockSpec((1, hw), lambda b, h: (0, 0)),
                  new_spec, past_t_spec, past_spec, new_spec, new_spec],
        out_specs=new_spec,
        out_shape=jax.ShapeDtypeStruct((n, d_attn), BF16),
        compiler_params=_params(2),
        name="attn_sample",
    )(*lams, head_norm, q, k_past, v_past, k_new, v_new)


R_E1, R_E2, R_W1, R_W2, R_RANK1, R_RANK2 = 0, 1, 2, 3, 4, 5
OUT_PROJ_CHUNKS = 4


def _pack_bf16_pairs(lo_bf, hi_bf):
    lo = lax.bitcast_convert_type(lo_bf.astype(F32), U32)
    hi = lax.bitcast_convert_type(hi_bf.astype(F32), U32)
    return (hi & jnp.uint32(0xFFFF0000)) | (lo >> 16)


def _unpack_bf16_pairs(w):
    lo = lax.bitcast_convert_type(w << 16, F32).astype(BF16)
    hi = lax.bitcast_convert_type(w & jnp.uint32(0xFFFF0000), F32).astype(BF16)
    return lo, hi


def _out_proj_kernel(ap_ref, op_ref, xp_ref, as_ref, os_ref, xs_ref, w_ref, g_ref,
                     wr_ref, br_ref, h_ref, hnp_ref, rinfo_ref, cnt_ref, carry_ref, *, n_prompt_tiles):
    i = pl.program_id(0)

    @pl.when(i == 0)
    def _():
        carry_ref[...] = jnp.zeros_like(carry_ref)

    @pl.when(i < n_prompt_tiles)
    def _():
        _out_proj_rows(ap_ref, op_ref, xp_ref, w_ref, g_ref, wr_ref, br_ref,
                       h_ref, hnp_ref, rinfo_ref, cnt_ref, carry_ref)

    @pl.when(i >= n_prompt_tiles)
    def _():
        _out_proj_rows(as_ref, os_ref, xs_ref, w_ref, g_ref, wr_ref, br_ref,
                       h_ref, hnp_ref, rinfo_ref, cnt_ref, carry_ref)


def _out_proj_rows(a_ref, o_ref, x_ref, w_ref, g_ref, wr_ref, br_ref,
                   h_ref, hnp_ref, rinfo_ref, cnt_ref, carry_ref):
    tm, d = x_ref.shape
    cw = d // OUT_PROJ_CHUNKS
    a, o = a_ref[...], o_ref[...]

    ss = jnp.zeros((tm, 1), F32)
    for c in range(OUT_PROJ_CHUNKS):
        cols = slice(c * cw, (c + 1) * cw)
        h = (x_ref[:, cols] + jnp.dot(a, w_ref[0, c], preferred_element_type=F32)
             + jnp.dot(o, w_ref[1, c], preferred_element_type=F32))
        h_ref[:, cols] = h
        ss = ss + jnp.sum(h * h, axis=1, keepdims=True)
    inv = lax.rsqrt(ss / d + EPS)

    logits = br_ref[...]
    for c in range(OUT_PROJ_CHUNKS // 2):
        halves = []
        for cc in (c, c + OUT_PROJ_CHUNKS // 2):
            cols = slice(cc * cw, (cc + 1) * cw)
            hb = (h_ref[:, cols] * inv * g_ref[:, cols]).astype(BF16)
            logits = logits + jnp.dot(hb, wr_ref[cols, :], preferred_element_type=F32)
            halves.append(hb)
        hnp_ref[:, c * cw:(c + 1) * cw] = _pack_bf16_pairs(*halves)

    lane = lax.broadcasted_iota(jnp.int32, logits.shape, 1)
    lane_f = lane.astype(F32)
    lg = jnp.where(lane < N_GROUPS, logits, NEG_INF)
    mg = jnp.max(lg, axis=1, keepdims=True)
    grp = jnp.min(jnp.where(lg == mg, lane_f, float(LANES)), axis=1, keepdims=True)
    gate_g = 1.0 / jnp.sum(jnp.exp(lg - mg), axis=1, keepdims=True)
    ex = lane - N_GROUPS
    in_grp = (ex >= 0) & (ex < N_EXPERTS) & ((ex >> 3).astype(F32) == grp)
    le = jnp.where(in_grp, logits, NEG_INF)
    v1 = jnp.max(le, axis=1, keepdims=True)
    i1 = jnp.min(jnp.where(le == v1, lane_f, float(LANES)), axis=1, keepdims=True)
    le2 = jnp.where(lane_f == i1, NEG_INF, le)
    v2 = jnp.max(le2, axis=1, keepdims=True)
    i2 = jnp.min(jnp.where(le2 == v2, lane_f, float(LANES)), axis=1, keepdims=True)
    e21 = jnp.exp(v2 - v1)
    w1 = gate_g / (1.0 + e21)
    w2 = gate_g * e21 / (1.0 + e21)
    e1 = i1 - float(N_GROUPS)
    e2 = i2 - float(N_GROUPS)

    onehot = jnp.where((lane_f == e1) | (lane_f == e2), 1.0, 0.0)
    r = lax.broadcasted_iota(jnp.int32, (tm, tm), 0)
    c = lax.broadcasted_iota(jnp.int32, (tm, tm), 1)
    earlier = jnp.where(c < r, 1.0, 0.0).astype(BF16)
    before = jnp.dot(earlier, onehot.astype(BF16), preferred_element_type=F32) + carry_ref[0:1, :]
    rank1 = _lane_pick(before, lane_f, e1)
    rank2 = _lane_pick(before, lane_f, e2)
    new_carry = carry_ref[0:1, :] + jnp.sum(onehot, axis=0, keepdims=True)
    carry_ref[...] = jnp.broadcast_to(new_carry, carry_ref.shape)
    cnt_ref[...] = jnp.broadcast_to(new_carry, cnt_ref.shape)

    rec = jnp.zeros_like(logits)
    for idx, val in ((R_E1, e1), (R_E2, e2), (R_W1, w1), (R_W2, w2), (R_RANK1, rank1), (R_RANK2, rank2)):
        rec = jnp.where(lane == idx, val, rec)
    rinfo_ref[...] = rec


def _out_proj(a_p, o_p, x_p, a_s, o_s, x_s, w_out_chunks, norm_g, w_router, b_router, *, tm):
    n_p, d = x_p.shape
    n_s = x_s.shape[0]
    dc = a_p.shape[1]
    assert n_p % tm == 0 and n_s == tm, "sample rows must fill exactly one tile"
    npt = n_p // tm
    n_all = n_p + n_s
    p_idx = lambda i: (jnp.minimum(i, npt - 1), 0)
    zero = lambda i: (0, 0)
    kern = functools.partial(_out_proj_kernel, n_prompt_tiles=npt)
    return pl.pallas_call(
        kern,
        grid=(npt + 1,),
        in_specs=[
            pl.BlockSpec((tm, dc), p_idx), pl.BlockSpec((tm, dc), p_idx), pl.BlockSpec((tm, d), p_idx),
            pl.BlockSpec((tm, dc), zero), pl.BlockSpec((tm, dc), zero), pl.BlockSpec((tm, d), zero),
            pl.BlockSpec(w_out_chunks.shape, lambda i: (0, 0, 0, 0)),
            pl.BlockSpec((1, d), zero),
            pl.BlockSpec((d, LANES), zero), pl.BlockSpec((1, LANES), zero),
        ],
        out_specs=[
            pl.BlockSpec((tm, d), lambda i: (i, 0)),
            pl.BlockSpec((tm, d // 2), lambda i: (i, 0)),
            pl.BlockSpec((tm, LANES), lambda i: (i, 0)),
            pl.BlockSpec((SUBLANES, LANES), zero),
        ],
        out_shape=[
            jax.ShapeDtypeStruct((n_all, d), F32),
            jax.ShapeDtypeStruct((n_all, d // 2), U32),
            jax.ShapeDtypeStruct((n_all, LANES), F32),
            jax.ShapeDtypeStruct((SUBLANES, LANES), F32),
        ],
        scratch_shapes=[pltpu.VMEM((SUBLANES, LANES), F32)],
        compiler_params=_params(1),
        name="out_proj",
    )(a_p, o_p, x_p, a_s, o_s, x_s, w_out_chunks, norm_g, w_router, b_router)


EXPERT_DMA_GROUPS = 8


def _experts_kernel(tile_expert_ref, n_used_ref, src_hbm, hnp_hbm, wi_ref, wo_ref, ys_ref,
                    src_ref, xs_ref, src_sem, gather_sem):
    w = pl.program_id(0)
    n_used = n_used_ref[0]
    tm = xs_ref.shape[1]
    n_tiles = pl.num_programs(0) - 1
    group = tm // EXPERT_DMA_GROUPS

    def src_copy(tile, slot):
        return pltpu.make_async_copy(src_hbm.at[pl.ds(tile, 1), :], src_ref.at[pl.ds(slot, 1), :],
                                     src_sem.at[slot])

    def gather_row(slot, r):
        return pltpu.make_async_copy(hnp_hbm.at[pl.ds(src_ref[slot, r], 1), :],
                                     xs_ref.at[slot, pl.ds(r, 1), :], gather_sem.at[slot])

    def wait_gathers(slot):
        pltpu.make_async_copy(xs_ref.at[slot], xs_ref.at[slot], gather_sem.at[slot]).wait()

    @pl.when(w == 0)
    def _():
        src_copy(0, 0).start()
        src_copy(1, 1).start()
        src_copy(0, 0).wait()
        for r in range(tm):
            gather_row(0, r).start(priority=r % 2)

    def compute_tile(cur):
        nxt = 1 - cur
        wait_gathers(cur)
        src_copy(w + 1, nxt).wait()
        src_copy(w + 2, cur).start()

        def move_rows(g):
            for r in range(g * group, (g + 1) * group):
                gather_row(nxt, r).start(priority=r % 2)

        lo, hi = _unpack_bf16_pairs(xs_ref[cur])
        half = lo.shape[1]
        n_chunks = EXPERT_DMA_GROUPS // 2
        f2 = wi_ref.shape[2]
        cw = f2 // n_chunks
        gu = []
        for c in range(n_chunks):
            cols = slice(c * cw, (c + 1) * cw)
            gu.append(jnp.dot(lo, wi_ref[0, :half, cols], preferred_element_type=F32)
                      + jnp.dot(hi, wi_ref[0, half:, cols], preferred_element_type=F32))
            move_rows(c)
        gate = jnp.concatenate(gu[:n_chunks // 2], axis=1)
        up = jnp.concatenate(gu[n_chunks // 2:], axis=1)
        act = (gate * jax.nn.sigmoid(gate) * up).astype(BF16)
        d = wo_ref.shape[2]
        ow = d // n_chunks
        for c in range(n_chunks):
            cols = slice(c * ow, (c + 1) * ow)
            ys_ref[:, cols] = jnp.dot(act, wo_ref[0, :, cols], preferred_element_type=F32)
            move_rows(n_chunks + c)

    def drain(cur):
        wait_gathers(cur)
        src_copy(w + 1, 1 - cur).wait()

    for parity in (0, 1):
        mine = (w % 2) == parity
        pl.when(mine & (w < n_used))(functools.partial(compute_tile, parity))
        pl.when(mine & (w == n_used))(functools.partial(drain, parity))

    @pl.when((w >= n_used) & (w < n_tiles))
    def _():
        ys_ref[...] = jnp.zeros_like(ys_ref)


def _experts(tile_expert, n_used, src_tab, hnp, w_ei_bf, w_eo_bf, *, tm):
    n_e, d, f2 = w_ei_bf.shape
    dw = hnp.shape[1]
    n_steps = src_tab.shape[0] - 1
    n_tiles = n_steps - 1
    assert tm % EXPERT_DMA_GROUPS == 0 and src_tab.shape == (n_steps + 1, tm)

    def expert_of(w, te, nu):
        return te[jnp.minimum(w, nu[0] - 1)]

    grid_spec = pltpu.PrefetchScalarGridSpec(
        num_scalar_prefetch=2,
        grid=(n_steps,),
        in_specs=[
            pl.BlockSpec(memory_space=pl.ANY),
            pl.BlockSpec(memory_space=pl.ANY),
            pl.BlockSpec((1, d, f2), lambda w, te, nu: (expert_of(w, te, nu), 0, 0)),
            pl.BlockSpec((1, f2 // 2, d), lambda w, te, nu: (expert_of(w, te, nu), 0, 0)),
        ],
        out_specs=pl.BlockSpec((tm, d), lambda w, te, nu: (jnp.minimum(w, n_tiles - 1), 0)),
        scratch_shapes=[pltpu.SMEM((2, tm), jnp.int32), pltpu.VMEM((2, tm, dw), U32),
                        pltpu.SemaphoreType.DMA((2,)), pltpu.SemaphoreType.DMA((2,))],
    )
    return pl.pallas_call(
        _experts_kernel,
        grid_spec=grid_spec,
        out_shape=jax.ShapeDtypeStruct((n_tiles * tm, d), F32),
        compiler_params=_params(1),
        name="experts",
    )(tile_expert, n_used, src_tab, hnp, w_ei_bf, w_eo_bf)


def _expert_tiles(counts, e12, r12, n_all, tm):
    max_tiles = (2 * n_all) // tm + N_EXPERTS
    n_rows = max_tiles + 2
    tiles_per_expert = (counts + tm - 1) // tm
    tile_end = jnp.cumsum(tiles_per_expert)
    n_used = tile_end[-1]
    row_start = (tile_end - tiles_per_expert) * tm
    t = jnp.minimum(jnp.arange(n_rows, dtype=jnp.int32), n_used - 1)
    tile_expert = jnp.sum(tile_end[None, :] <= t[:, None], axis=1).astype(jnp.int32)

    pos = row_start[e12] + r12
    token = jnp.repeat(jnp.arange(n_all, dtype=jnp.int32), 2)
    src = jnp.zeros((n_rows * tm,), jnp.int32).at[pos.reshape(-1)].set(
        token, unique_indices=True, mode="promise_in_bounds").reshape(n_rows, tm)
    pos_tiles = pos.reshape(n_all // tm, tm, 2).transpose(0, 2, 1).reshape(n_all // tm, 2 * tm)
    return tile_expert, n_used.reshape(1).astype(jnp.int32), src, pos_tiles.astype(jnp.int32)


def _combine_kernel(pos_hbm, h_ref, rinfo_ref, g_ref, ys_hbm, yp_ref, ysm_ref,
                    idx_ref, buf_ref, idx_sem, row_sem, *, n_prompt_tiles):
    i = pl.program_id(0)
    nt = pl.num_programs(0)
    tm = h_ref.shape[0]

    def idx_copy(tile, slot):
        return pltpu.make_async_copy(pos_hbm.at[pl.ds(tile, 1), :], idx_ref.at[pl.ds(slot, 1), :],
                                     idx_sem.at[slot])

    def start_rows(slot):
        def body(r, _):
            for k in range(2):
                pltpu.make_async_copy(ys_hbm.at[pl.ds(idx_ref[slot, k * tm + r], 1), :],
                                      buf_ref.at[slot, k, pl.ds(r, 1), :], row_sem.at[slot]).start()
            return 0
        lax.fori_loop(0, tm, body, 0, unroll=8)

    def wait_rows(slot):
        pltpu.make_async_copy(buf_ref.at[slot], buf_ref.at[slot], row_sem.at[slot]).wait()

    @pl.when(i == 0)
    def _():
        idx_copy(0, 0).start()
        idx_copy(0, 0).wait()
        start_rows(0)

        @pl.when(nt > 1)
        def _():
            idx_copy(1, 1).start()

    def tile(cur):
        nxt = 1 - cur

        @pl.when(i + 1 < nt)
        def _():
            idx_copy(i + 1, nxt).wait()
            start_rows(nxt)

        @pl.when(i + 2 < nt)
        def _():
            idx_copy(i + 2, cur).start()

        wait_rows(cur)
        rinfo = rinfo_ref[...]
        lane = lax.broadcasted_iota(jnp.int32, rinfo.shape, 1)
        w1 = _lane_pick(rinfo, lane, R_W1)
        w2 = _lane_pick(rinfo, lane, R_W2)
        hp = h_ref[...] + (w1 * buf_ref[cur, 0] + w2 * buf_ref[cur, 1])
        y = _rms(hp, g_ref[...])

        @pl.when(i < n_prompt_tiles)
        def _():
            yp_ref[...] = y

        @pl.when(i >= n_prompt_tiles)
        def _():
            ysm_ref[...] = y

    for parity in (0, 1):
        pl.when((i % 2) == parity)(functools.partial(tile, parity))


def _combine(pos_tiles, h, rinfo, norm_g, ys, *, n_prompt, tm):
    n_all, d = h.shape
    nt = n_all // tm
    npt = n_prompt // tm
    assert n_all - n_prompt == tm and pos_tiles.shape == (nt, 2 * tm)
    kern = functools.partial(_combine_kernel, n_prompt_tiles=npt)
    return pl.pallas_call(
        kern,
        grid=(nt,),
        in_specs=[
            pl.BlockSpec(memory_space=pl.ANY),
            pl.BlockSpec((tm, d), lambda i: (i, 0)),
            pl.BlockSpec((tm, LANES), lambda i: (i, 0)),
            pl.BlockSpec((1, d), lambda i: (0, 0)),
            pl.BlockSpec(memory_space=pl.ANY),
        ],
        out_specs=[
            pl.BlockSpec((tm, d), lambda i: (jnp.minimum(i, npt - 1), 0)),
            pl.BlockSpec((tm, d), lambda i: (0, 0)),
        ],
        out_shape=[jax.ShapeDtypeStruct((n_prompt, d), F32), jax.ShapeDtypeStruct((tm, d), F32)],
        scratch_shapes=[pltpu.SMEM((2, 2 * tm), jnp.int32), pltpu.VMEM((2, 2, tm, d), F32),
                        pltpu.SemaphoreType.DMA((2,)), pltpu.SemaphoreType.DMA((2,))],
        compiler_params=_params(1),
        name="combine",
    )(pos_tiles, h, rinfo, norm_g, ys)


ROW_TILE = 256
PROJ_ROWS = 512
PROJ_COLS = 512
ATTN_Q_ROWS = 256


def kernel(x_prompt, x_sample, cache_k, cache_v, state_conv, norm_mix, w_in, conv_w, lambda_q1,
           lambda_k1, lambda_q2, lambda_k2, head_norm, w_out, norm_ffn, w_router_group,
           b_router_group, w_router_expert, b_router_expert, w_expert_in, w_expert_out, norm_final):
    depth = w_in.shape[0]
    b, t, d = x_prompt.shape
    db, dt, _ = x_sample.shape
    past = cache_k.shape[2]
    d_conv = conv_w.shape[2]
    d_attn = d - d_conv
    n_p, n_s = b * t, db * dt
    n_all = n_p + n_s
    tm = ROW_TILE

    hp = x_prompt.reshape(n_p, d)
    hs = x_sample.reshape(n_s, d)
    outs = {name: [] for name in ("kp", "vp", "cp", "ks", "vs", "cs")}
    y_p = y_s = None
    for l in range(depth):
        lam_init = 0.8 - 0.6 * math.exp(-0.3 * l)
        w_in_bf = w_in[l].astype(BF16)
        w_out_bf = w_out[l].astype(BF16).reshape(2, d_conv, OUT_PROJ_CHUNKS, d // OUT_PROJ_CHUNKS)
        w_out_bf = w_out_bf.transpose(0, 2, 1, 3)
        w_ei_bf = w_expert_in[l].astype(BF16)
        w_eo_bf = w_expert_out[l].astype(BF16)
        g_mix = norm_mix[l].reshape(1, d)
        lams = tuple(p[l].reshape(1, -1) for p in (lambda_q1, lambda_k1, lambda_q2, lambda_k2))
        hnorm = head_norm[l].reshape(1, -1)
        pad = LANES - N_GROUPS - N_EXPERTS
        w_router = jnp.concatenate([w_router_group[l], w_router_expert[l], jnp.zeros((d, pad), F32)],
                                   axis=1).astype(BF16)
        b_router = jnp.concatenate([b_router_group[l], b_router_expert[l], jnp.zeros((pad,), F32)]).reshape(1, LANES)

        zero_prev = jnp.zeros((b, CONV_WIDTH - 1, d_conv), F32)
        a_p, q_p, kb_p, kt_p, v_p, tail_p = _in_proj(hp, g_mix, w_in_bf, conv_w[l], zero_prev,
                                                     seq_len=t, tm=min(PROJ_ROWS, t), tn=PROJ_COLS)
        o_p = _attn_prompt(q_p, kb_p, v_p, lams, hnorm, batch=b, seq_len=t, lam_init=lam_init,
                           tq=min(ATTN_Q_ROWS, t))
        prev_rows = jnp.stack([jnp.repeat(state_conv[l][:, 0], dt, axis=0),
                               jnp.repeat(state_conv[l][:, 1], dt, axis=0)])
        a_s, q_s, kb_s, k_s, v_s, u_s = _in_proj(hs, g_mix, w_in_bf, conv_w[l], prev_rows,
                                                 seq_len=dt, tm=n_s, tn=PROJ_COLS)
        kt_past = cache_k[l].transpose(0, 2, 3, 4, 1).reshape(db * d_attn, past)
        o_s = _attn_sample(q_s, kb_s, v_s, kt_past, cache_v[l].reshape(db * past, d_attn), lams, hnorm,
                           batch=db, seq_len=dt, past_len=past, lam_init=lam_init)

        h, hnp, rinfo, cnt = _out_proj(a_p, o_p, hp, a_s, o_s, hs, w_out_bf,
                                       norm_ffn[l].reshape(1, d), w_router, b_router, tm=tm)
        counts = cnt[0, :N_EXPERTS].astype(jnp.int32)
        e12 = rinfo[:, R_E1:R_E2 + 1].astype(jnp.int32)
        r12 = rinfo[:, R_RANK1:R_RANK2 + 1].astype(jnp.int32)
        tile_expert, n_used, src_tab, pos_tiles = _expert_tiles(counts, e12, r12, n_all, tm)
        ys = _experts(tile_expert, n_used, src_tab, hnp, w_ei_bf, w_eo_bf, tm=tm)
        last = l == depth - 1
        g_out = norm_final.reshape(1, d) if last else jnp.ones((1, d), F32)
        y_p, y_s = _combine(pos_tiles, h, rinfo, g_out, ys, n_prompt=n_p, tm=tm)
        assert last, "multi-layer stacks need an un-normalised combine output"

        outs["kp"].append(kt_p.reshape(b, N_HEADS, 2, d_attn // (2 * N_HEADS), t).transpose(0, 4, 1, 2, 3))
        outs["vp"].append(v_p.reshape(b, t, N_HEADS, d_attn // N_HEADS))
        tails = tail_p.reshape(b, t // min(PROJ_ROWS, t), SUBLANES, d_conv)
        outs["cp"].append(tails[:, -1, SUBLANES - (CONV_WIDTH - 1):, :])
        outs["ks"].append(k_s.reshape(db, dt, N_HEADS, 2, d_attn // (2 * N_HEADS)))
        outs["vs"].append(v_s.reshape(db, dt, N_HEADS, d_attn // N_HEADS))
        outs["cs"].append(u_s.reshape(db, dt, d_conv)[:, dt - (CONV_WIDTH - 1):, :])

    return (y_p.reshape(b, t, d), y_s.reshape(db, dt, d),
            jnp.stack(outs["kp"]), jnp.stack(outs["vp"]), jnp.stack(outs["cp"]),
            jnp.stack(outs["ks"]), jnp.stack(outs["vs"]), jnp.stack(outs["cs"]))
```

```python
import functools
import math

import jax
import jax.numpy as jnp
from jax import lax
from jax.experimental import pallas as pl
from jax.experimental.pallas import tpu as pltpu

EPS = 1e-6
N_HEADS = 8
CHUNK = 64
CONV_WIDTH = 3
N_GROUPS = 4
EXPERTS_PER_GROUP = 8
N_EXPERTS = N_GROUPS * EXPERTS_PER_GROUP
LANES = 128
SUBLANES = 8
BF16_SUBLANES = 16
VMEM_LIMIT = 56 * 1024 * 1024

F32 = jnp.float32
BF16 = jnp.bfloat16
U32 = jnp.uint32
NEG_INF = float("-inf")


def _params(n_axes, vmem=VMEM_LIMIT, **kw):
    return pltpu.CompilerParams(dimension_semantics=("arbitrary",) * n_axes,
                                vmem_limit_bytes=vmem, **kw)


def _rms(x, g):
    return x * lax.rsqrt(jnp.mean(x * x, axis=-1, keepdims=True) + EPS) * g


def _lane_pick(x, lane, idx):
    return jnp.sum(jnp.where(lane == idx, x, 0.0), axis=1, keepdims=True)


def _in_proj_kernel(x_ref, g_ref, wa_ref, wb_ref, wc_ref, cw_ref, prev_ref,
                    a_ref, q_ref, kb_ref, ko_ref, v_ref, u_ref, xn_ref, carry_ref,
                    *, seq_len, tiles_per_seq, rows_span_sequences, q_scale):
    i = pl.program_id(0)
    part = pl.program_id(1)
    j = pl.program_id(2)
    tm = x_ref.shape[0]

    @pl.when((part == 0) & (j == 0))
    def _():
        xn_ref[...] = _rms(x_ref[...], g_ref[...]).astype(BF16)

    xn = xn_ref[...]
    za = jnp.dot(xn, wa_ref[...], preferred_element_type=F32)
    zb = jnp.dot(xn, wb_ref[...], preferred_element_type=F32)
    zc = jnp.dot(xn, wc_ref[...], preferred_element_type=F32)

    @pl.when(part == 0)
    def _():
        u = zb * zc
        row = lax.broadcasted_iota(jnp.int32, u.shape, 0)
        if rows_span_sequences:
            rmod = row % seq_len
            pm2, pm1 = prev_ref[0], prev_ref[1]
            first, second = rmod == 0, rmod == 1
            u_ref[...] = u
        else:
            seq_start = (i % tiles_per_seq) == 0
            tail = carry_ref[j]
            pm2 = jnp.where(seq_start, prev_ref[0, 0:1, :], tail[SUBLANES - 2:SUBLANES - 1, :])
            pm1 = jnp.where(seq_start, prev_ref[0, 1:2, :], tail[SUBLANES - 1:SUBLANES, :])
            first, second = row == 0, row == 1
            new_tail = u[tm - SUBLANES:, :]
            carry_ref[j] = new_tail
            u_ref[0] = new_tail
        u1 = jnp.where(first, pm1, pltpu.roll(u, 1, 0))
        u2 = jnp.where(first, pm2, jnp.where(second, pm1, pltpu.roll(u, 2, 0)))
        cw = cw_ref[...]
        y = cw[0:1, :] * u2 + cw[1:2, :] * u1 + cw[2:3, :] * u
        a_ref[...] = (za * y).astype(BF16)

    @pl.when(part == 1)
    def _():
        q_ref[...] = (za * q_scale).astype(BF16)
        kb_ref[...] = zb.astype(BF16)
        if rows_span_sequences:
            ko_ref[...] = zb
        else:
            ko_ref[0] = zb.T
        v_ref[...] = zc


def _in_proj(x2d, norm_g, w_in_bf, conv_w, prev, *, seq_len, tm, tn):
    n, d = x2d.shape
    d_conv = conv_w.shape[1]
    assert w_in_bf.shape == (d, 6 * d_conv), "conv and attention widths must match"
    assert n % tm == 0 and d_conv % tn == 0
    nj = d_conv // tn
    n_i = n // tm
    rows_span_sequences = tm > seq_len
    if rows_span_sequences:
        assert tm % seq_len == 0 and prev.shape == (2, n, d_conv)
        tiles_per_seq = 1
        prev_spec = pl.BlockSpec((2, tm, tn), lambda i, p, j: (0, i, j))
        u_shape = jax.ShapeDtypeStruct((n, d_conv), F32)
        u_spec = pl.BlockSpec((tm, tn), lambda i, p, j: (i, jnp.where(p == 0, j, nj - 1)))
        ko_shape = jax.ShapeDtypeStruct((n, d_conv), F32)
        ko_spec = pl.BlockSpec((tm, tn), lambda i, p, j: (i, jnp.where(p == 1, j, 0)))
    else:
        assert seq_len % tm == 0 and prev.shape == (n // seq_len, CONV_WIDTH - 1, d_conv)
        tiles_per_seq = seq_len // tm
        prev_spec = pl.BlockSpec((1, CONV_WIDTH - 1, tn), lambda i, p, j: (i // tiles_per_seq, 0, j))
        u_shape = jax.ShapeDtypeStruct((n_i, SUBLANES, d_conv), F32)
        u_spec = pl.BlockSpec((1, SUBLANES, tn), lambda i, p, j: (i, 0, jnp.where(p == 0, j, nj - 1)))
        ko_shape = jax.ShapeDtypeStruct((n // seq_len, d_conv, seq_len), F32)
        ko_spec = pl.BlockSpec((1, tn, tm), lambda i, p, j: (i // tiles_per_seq, jnp.where(p == 1, j, 0),
                                                             i % tiles_per_seq))

    def wspec(k):
        return pl.BlockSpec((d, tn), lambda i, p, j: (0, p * 3 * nj + k * nj + j))

    conv_out = lambda i, p, j: (i, jnp.where(p == 0, j, nj - 1))
    attn_out = lambda i, p, j: (i, jnp.where(p == 1, j, 0))
    head_dim = d_conv // (2 * N_HEADS)
    kern = functools.partial(_in_proj_kernel, seq_len=seq_len, tiles_per_seq=tiles_per_seq,
                             rows_span_sequences=rows_span_sequences,
                             q_scale=head_dim ** -0.5 * math.log2(math.e))
    return pl.pallas_call(
        kern,
        grid=(n_i, 2, nj),
        in_specs=[
            pl.BlockSpec((tm, d), lambda i, p, j: (i, 0)),
            pl.BlockSpec((1, d), lambda i, p, j: (0, 0)),
            wspec(0), wspec(1), wspec(2),
            pl.BlockSpec((CONV_WIDTH, tn), lambda i, p, j: (0, j)),
            prev_spec,
        ],
        out_specs=[
            pl.BlockSpec((tm, tn), conv_out),
            pl.BlockSpec((tm, tn), attn_out),
            pl.BlockSpec((tm, tn), attn_out),
            ko_spec,
            pl.BlockSpec((tm, tn), attn_out),
            u_spec,
        ],
        out_shape=[
            jax.ShapeDtypeStruct((n, d_conv), BF16),
            jax.ShapeDtypeStruct((n, d_conv), BF16),
            jax.ShapeDtypeStruct((n, d_conv), BF16),
            ko_shape,
            jax.ShapeDtypeStruct((n, d_conv), F32),
            u_shape,
        ],
        scratch_shapes=[pltpu.VMEM((tm, d), BF16), pltpu.VMEM((nj, SUBLANES, tn), F32)],
        compiler_params=_params(3),
        name="in_proj",
    )(x2d, norm_g, w_in_bf, w_in_bf, w_in_bf, conv_w, prev)


def _diff_lambda(lq1_ref, lk1_ref, lq2_ref, lk2_ref, lam_init):
    s1 = jnp.sum(lq1_ref[...] * lk1_ref[...], axis=1, keepdims=True)
    s2 = jnp.sum(lq2_ref[...] * lk2_ref[...], axis=1, keepdims=True)
    return jnp.exp(s1) - jnp.exp(s2) + lam_init


def _stacked_queries(q, head_dim):
    lane = lax.broadcasted_iota(jnp.int32, q.shape, 1)
    zero = jnp.zeros_like(q)
    return jnp.concatenate([jnp.where(lane < head_dim, q, zero),
                            jnp.where(lane >= head_dim, q, zero)], axis=0)


def _scores(qq, kblk):
    return lax.dot_general(qq, kblk, (((1,), (1,)), ((), ())), preferred_element_type=F32)


def _finish_heads(acc, l, lam, hn, lam_init, tq):
    o = acc[:tq] / l[:tq] - lam * (acc[tq:] / l[tq:])
    return (_rms(o, hn) * (1.0 - lam_init)).astype(BF16)


def _attn_prompt_kernel(lq1_ref, lk1_ref, lq2_ref, lk2_ref, hn_ref, q_ref, k_ref, v_ref, o_ref,
                        qt1_ref, qt2_ref, vt_ref, s0_ref, s1_ref, p0_ref, p1_ref,
                        *, lam_init, head_dim, tq):
    seq_len, hw = q_ref.shape
    lam = _diff_lambda(lq1_ref, lk1_ref, lq2_ref, lk2_ref, lam_init)

    qt = q_ref[...].astype(F32).T
    dim = lax.broadcasted_iota(jnp.int32, qt.shape, 0)
    qt1_ref[...] = jnp.where(dim < head_dim, qt, 0.0).astype(BF16)
    qt2_ref[...] = jnp.where(dim >= head_dim, qt, 0.0).astype(BF16)
    vt_ref[:hw, :] = v_ref[...].T.astype(BF16)
    vt_ref[hw:, :] = jnp.ones((vt_ref.shape[0] - hw, seq_len), BF16)

    key_chunk = lax.broadcasted_iota(jnp.int32, (tq, 2 * tq), 0) // CHUNK
    qry_chunk = (lax.broadcasted_iota(jnp.int32, (tq, 2 * tq), 1) % tq) // CHUNK
    visible = key_chunk <= qry_chunk

    bufs = ((s0_ref, p0_ref), (s1_ref, p1_ref))
    n_blocks = seq_len // tq

    def scores(qi):
        q0, n_keys, s_buf = qi * tq, (qi + 1) * tq, bufs[qi % 2][0]
        qqt = jnp.concatenate([qt1_ref[:, q0:n_keys], qt2_ref[:, q0:n_keys]], axis=1)
        s = jnp.dot(k_ref[q0:n_keys, :], qqt, preferred_element_type=F32)
        s = jnp.where(visible, s, NEG_INF)
        s_buf[q0:n_keys, :] = s
        m = jnp.max(s, axis=0, keepdims=True)
        if qi > 0:
            s = jnp.dot(k_ref[:q0, :], qqt, preferred_element_type=F32)
            s_buf[:q0, :] = s
            m = jnp.maximum(m, jnp.max(s, axis=0, keepdims=True))
        return m

    def finish(qi, m):
        q0, n_keys = qi * tq, (qi + 1) * tq
        s_buf, p_buf = bufs[qi % 2]
        p_buf[:n_keys, :] = jnp.exp2(s_buf[:n_keys, :] - m).astype(BF16)
        ot = jnp.dot(vt_ref[:, :n_keys], p_buf[:n_keys, :], preferred_element_type=F32)
        ot = ot[:hw] / ot[hw:hw + 1]
        o = (ot[:, :tq] - lam * ot[:, tq:]).T
        o_ref[q0:n_keys, :] = (_rms(o, hn_ref[...]) * (1.0 - lam_init)).astype(BF16)

    m = scores(0)
    for qi in range(n_blocks):
        m_next = scores(qi + 1) if qi + 1 < n_blocks else None
        finish(qi, m)
        m = m_next


def _attn_prompt(q, k, v, lams, head_norm, *, batch, seq_len, lam_init, tq):
    n, d_attn = q.shape
    hw = d_attn // N_HEADS
    assert seq_len % tq == 0 and tq % CHUNK == 0
    small = pl.BlockSpec((1, hw // 2), lambda b, h: (0, 0))
    head = pl.BlockSpec((seq_len, hw), lambda b, h: (b, h))
    kern = functools.partial(_attn_prompt_kernel, lam_init=lam_init, head_dim=hw // 2, tq=tq)
    return pl.pallas_call(
        kern,
        grid=(batch, N_HEADS),
        in_specs=[small, small, small, small, pl.BlockSpec((1, hw), lambda b, h: (0, 0)),
                  head, head, head],
        out_specs=head,
        out_shape=jax.ShapeDtypeStruct((n, d_attn), BF16),
        scratch_shapes=[pltpu.VMEM((hw, seq_len), BF16), pltpu.VMEM((hw, seq_len), BF16),
                        pltpu.VMEM((hw + BF16_SUBLANES, seq_len), BF16),
                        pltpu.VMEM((seq_len, 2 * tq), F32), pltpu.VMEM((seq_len, 2 * tq), F32),
                        pltpu.VMEM((seq_len, 2 * tq), BF16), pltpu.VMEM((seq_len, 2 * tq), BF16)],
        compiler_params=_params(2),
        name="attn_prompt",
    )(*lams, head_norm, q, k, v)


def _attn_sample_kernel(lq1_ref, lk1_ref, lq2_ref, lk2_ref, hn_ref, q_ref, kc_ref, vc_ref,
                        kn_ref, vn_ref, o_ref, *, lam_init, head_dim):
    tq = q_ref.shape[0]
    lam = _diff_lambda(lq1_ref, lk1_ref, lq2_ref, lk2_ref, lam_init)
    qq = _stacked_queries(q_ref[...], head_dim)
    s_past = jnp.dot(qq, kc_ref[...].astype(BF16), preferred_element_type=F32)
    s_new = _scores(qq, kn_ref[...])
    m = jnp.maximum(jnp.max(s_past, axis=1, keepdims=True), jnp.max(s_new, axis=1, keepdims=True))
    p_past = jnp.exp2(s_past - m)
    p_new = jnp.exp2(s_new - m)
    l = jnp.sum(p_past, axis=1, keepdims=True) + jnp.sum(p_new, axis=1, keepdims=True)
    acc = (jnp.dot(p_past.astype(BF16), vc_ref[...].astype(BF16), preferred_element_type=F32)
           + jnp.dot(p_new.astype(BF16), vn_ref[...].astype(BF16), preferred_element_type=F32))
    o_ref[...] = _finish_heads(acc, l, lam, hn_ref[...], lam_init, tq)


def _attn_sample(q, k_new, v_new, k_past, v_past, lams, head_norm, *, batch, seq_len, past_len, lam_init):
    n, d_attn = q.shape
    hw = d_attn // N_HEADS
    small = pl.BlockSpec((1, hw // 2), lambda b, h: (0, 0))
    new_spec = pl.BlockSpec((seq_len, hw), lambda b, h: (b, h))
    past_spec = pl.BlockSpec((past_len, hw), lambda b, h: (b, h))
    past_t_spec = pl.BlockSpec((hw, past_len), lambda b, h: (b * N_HEADS + h, 0))
    kern = functools.partial(_attn_sample_kernel, lam_init=lam_init, head_dim=hw // 2)
    return pl.pallas_call(
        kern,
        grid=(batch, N_HEADS),
        in_specs=[small, small, small, small, pl.BlockSpec((1, hw), lambda b, h: (0, 0)),
                  new_spec, past_t_spec, past_spec, new_spec, new_spec],
        out_specs=new_spec,
        out_shape=jax.ShapeDtypeStruct((n, d_attn), BF16),
        compiler_params=_params(2),
        name="attn_sample",
    )(*lams, head_norm, q, k_past, v_past, k_new, v_new)


R_E1, R_E2, R_W1, R_W2, R_RANK1, R_RANK2 = 0, 1, 2, 3, 4, 5
OUT_PROJ_CHUNKS = 4


def _pack_bf16_pairs(lo_bf, hi_bf):
    lo = lax.bitcast_convert_type(lo_bf.astype(F32), U32)
    hi = lax.bitcast_convert_type(hi_bf.astype(F32), U32)
    return (hi & jnp.uint32(0xFFFF0000)) | (lo >> 16)


def _unpack_bf16_pairs(w):
    lo = lax.bitcast_convert_type(w << 16, F32).astype(BF16)
    hi = lax.bitcast_convert_type(w & jnp.uint32(0xFFFF0000), F32).astype(BF16)
    return lo, hi


def _out_proj_kernel(ap_ref, op_ref, xp_ref, as_ref, os_ref, xs_ref, w_ref, g_ref,
                     wr_ref, br_ref, h_ref, hnp_ref, rinfo_ref, cnt_ref, carry_ref, *, n_prompt_tiles):
    i = pl.program_id(0)

    @pl.when(i == 0)
    def _():
        carry_ref[...] = jnp.zeros_like(carry_ref)

    @pl.when(i < n_prompt_tiles)
    def _():
        _out_proj_rows(ap_ref, op_ref, xp_ref, w_ref, g_ref, wr_ref, br_ref,
                       h_ref, hnp_ref, rinfo_ref, cnt_ref, carry_ref)

    @pl.when(i >= n_prompt_tiles)
    def _():
        _out_proj_rows(as_ref, os_ref, xs_ref, w_ref, g_ref, wr_ref, br_ref,
                       h_ref, hnp_ref, rinfo_ref, cnt_ref, carry_ref)


def _out_proj_rows(a_ref, o_ref, x_ref, w_ref, g_ref, wr_ref, br_ref,
                   h_ref, hnp_ref, rinfo_ref, cnt_ref, carry_ref):
    tm, d = x_ref.shape
    cw = d // OUT_PROJ_CHUNKS
    a, o = a_ref[...], o_ref[...]

    ss = jnp.zeros((tm, 1), F32)
    for c in range(OUT_PROJ_CHUNKS):
        cols = slice(c * cw, (c + 1) * cw)
        h = (x_ref[:, cols] + jnp.dot(a, w_ref[0, c], preferred_element_type=F32)
             + jnp.dot(o, w_ref[1, c], preferred_element_type=F32))
        h_ref[:, cols] = h
        ss = ss + jnp.sum(h * h, axis=1, keepdims=True)
    inv = lax.rsqrt(ss / d + EPS)

    logits = br_ref[...]
    for c in range(OUT_PROJ_CHUNKS // 2):
        halves = []
        for cc in (c, c + OUT_PROJ_CHUNKS // 2):
            cols = slice(cc * cw, (cc + 1) * cw)
            hb = (h_ref[:, cols] * inv * g_ref[:, cols]).astype(BF16)
            logits = logits + jnp.dot(hb, wr_ref[cols, :], preferred_element_type=F32)
            halves.append(hb)
        hnp_ref[:, c * cw:(c + 1) * cw] = _pack_bf16_pairs(*halves)

    lane = lax.broadcasted_iota(jnp.int32, logits.shape, 1)
    lane_f = lane.astype(F32)
    lg = jnp.where(lane < N_GROUPS, logits, NEG_INF)
    mg = jnp.max(lg, axis=1, keepdims=True)
    grp = jnp.min(jnp.where(lg == mg, lane_f, float(LANES)), axis=1, keepdims=True)
    gate_g = 1.0 / jnp.sum(jnp.exp(lg - mg), axis=1, keepdims=True)
    ex = lane - N_GROUPS
    in_grp = (ex >= 0) & (ex < N_EXPERTS) & ((ex >> 3).astype(F32) == grp)
    le = jnp.where(in_grp, logits, NEG_INF)
    v1 = jnp.max(le, axis=1, keepdims=True)
    i1 = jnp.min(jnp.where(le == v1, lane_f, float(LANES)), axis=1, keepdims=True)
    le2 = jnp.where(lane_f == i1, NEG_INF, le)
    v2 = jnp.max(le2, axis=1, keepdims=True)
    i2 = jnp.min(jnp.where(le2 == v2, lane_f, float(LANES)), axis=1, keepdims=True)
    e21 = jnp.exp(v2 - v1)
    w1 = gate_g / (1.0 + e21)
    w2 = gate_g * e21 / (1.0 + e21)
    e1 = i1 - float(N_GROUPS)
    e2 = i2 - float(N_GROUPS)

    onehot = jnp.where((lane_f == e1) | (lane_f == e2), 1.0, 0.0)
    r = lax.broadcasted_iota(jnp.int32, (tm, tm), 0)
    c = lax.broadcasted_iota(jnp.int32, (tm, tm), 1)
    earlier = jnp.where(c < r, 1.0, 0.0).astype(BF16)
    before = jnp.dot(earlier, onehot.astype(BF16), preferred_element_type=F32) + carry_ref[0:1, :]
    rank1 = _lane_pick(before, lane_f, e1)
    rank2 = _lane_pick(before, lane_f, e2)
    new_carry = carry_ref[0:1, :] + jnp.sum(onehot, axis=0, keepdims=True)
    carry_ref[...] = jnp.broadcast_to(new_carry, carry_ref.shape)
    cnt_ref[...] = jnp.broadcast_to(new_carry, cnt_ref.shape)

    rec = jnp.zeros_like(logits)
    for idx, val in ((R_E1, e1), (R_E2, e2), (R_W1, w1), (R_W2, w2), (R_RANK1, rank1), (R_RANK2, rank2)):
        rec = jnp.where(lane == idx, val, rec)
    rinfo_ref[...] = rec


def _out_proj(a_p, o_p, x_p, a_s, o_s, x_s, w_out_chunks, norm_g, w_router, b_router, *, tm):
    n_p, d = x_p.shape
    n_s = x_s.shape[0]
    dc = a_p.shape[1]
    assert n_p % tm == 0 and n_s == tm, "sample rows must fill exactly one tile"
    npt = n_p // tm
    n_all = n_p + n_s
    p_idx = lambda i: (jnp.minimum(i, npt - 1), 0)
    zero = lambda i: (0, 0)
    kern = functools.partial(_out_proj_kernel, n_prompt_tiles=npt)
    return pl.pallas_call(
        kern,
        grid=(npt + 1,),
        in_specs=[
            pl.BlockSpec((tm, dc), p_idx), pl.BlockSpec((tm, dc), p_idx), pl.BlockSpec((tm, d), p_idx),
            pl.BlockSpec((tm, dc), zero), pl.BlockSpec((tm, dc), zero), pl.BlockSpec((tm, d), zero),
            pl.BlockSpec(w_out_chunks.shape, lambda i: (0, 0, 0, 0)),
            pl.BlockSpec((1, d), zero),
            pl.BlockSpec((d, LANES), zero), pl.BlockSpec((1, LANES), zero),
        ],
        out_specs=[
            pl.BlockSpec((tm, d), lambda i: (i, 0)),
            pl.BlockSpec((tm, d // 2), lambda i: (i, 0)),
            pl.BlockSpec((tm, LANES), lambda i: (i, 0)),
            pl.BlockSpec((SUBLANES, LANES), zero),
        ],
        out_shape=[
            jax.ShapeDtypeStruct((n_all, d), F32),
            jax.ShapeDtypeStruct((n_all, d // 2), U32),
            jax.ShapeDtypeStruct((n_all, LANES), F32),
            jax.ShapeDtypeStruct((SUBLANES, LANES), F32),
        ],
        scratch_shapes=[pltpu.VMEM((SUBLANES, LANES), F32)],
        compiler_params=_params(1),
        name="out_proj",
    )(a_p, o_p, x_p, a_s, o_s, x_s, w_out_chunks, norm_g, w_router, b_router)


EXPERT_DMA_GROUPS = 16


def _experts_kernel(tile_expert_ref, n_used_ref, src_hbm, hnp_hbm, wi_f32_ref, wo_f32_ref, ys_ref,
                    src_ref, xs_ref, gu_ref, wi_ref, wo_ref, src_sem, gather_sem):
    w = pl.program_id(0)
    n_used = n_used_ref[0]
    tm = xs_ref.shape[1]
    n_tiles = pl.num_programs(0) - 1
    group = tm // EXPERT_DMA_GROUPS

    def src_copy(tile, slot):
        return pltpu.make_async_copy(src_hbm.at[pl.ds(tile, 1), :], src_ref.at[pl.ds(slot, 1), :],
                                     src_sem.at[slot])

    def gather_row(slot, r):
        return pltpu.make_async_copy(hnp_hbm.at[pl.ds(src_ref[slot, r], 1), :],
                                     xs_ref.at[slot, pl.ds(r, 1), :], gather_sem.at[slot])

    def wait_gathers(slot):
        pltpu.make_async_copy(xs_ref.at[slot], xs_ref.at[slot], gather_sem.at[slot]).wait()

    @pl.when(w == 0)
    def _():
        src_copy(0, 0).start()
        src_copy(1, 1).start()
        src_copy(0, 0).wait()
        for r in range(tm):
            gather_row(0, r).start(priority=r % 2)

    def compute_tile(cur):
        nxt = 1 - cur
        wait_gathers(cur)
        src_copy(w + 1, nxt).wait()
        src_copy(w + 2, cur).start()

        def move_rows(g):
            for r in range(g * group, (g + 1) * group):
                gather_row(nxt, r).start(priority=r % 2)

        @pl.when((w == 0) | (tile_expert_ref[w] != tile_expert_ref[jnp.maximum(w - 1, 0)]))
        def _():
            wi_ref[...] = wi_f32_ref[0].astype(BF16)
            wo_ref[...] = wo_f32_ref[0].astype(BF16)

        lo, hi = _unpack_bf16_pairs(xs_ref[cur])
        half = lo.shape[1]
        n_chunks = EXPERT_DMA_GROUPS // 4
        f2 = wi_ref.shape[1]
        cw = f2 // n_chunks
        for c in range(n_chunks):
            cols = slice(c * cw, (c + 1) * cw)
            part = jnp.dot(lo, wi_ref[:half, cols], preferred_element_type=F32)
            move_rows(2 * c)
            gu_ref[:, cols] = part + jnp.dot(hi, wi_ref[half:, cols], preferred_element_type=F32)
            move_rows(2 * c + 1)
        gate, up = gu_ref[:, :f2 // 2], gu_ref[:, f2 // 2:]
        act = (gate * jax.nn.sigmoid(gate) * up).astype(BF16)
        fh = act.shape[1] // 2
        d = wo_ref.shape[1]
        ow = d // n_chunks
        for c in range(n_chunks):
            cols = slice(c * ow, (c + 1) * ow)
            part = jnp.dot(act[:, :fh], wo_ref[:fh, cols], preferred_element_type=F32)
            move_rows(2 * (n_chunks + c))
            ys_ref[:, cols] = part + jnp.dot(act[:, fh:], wo_ref[fh:, cols], preferred_element_type=F32)
            move_rows(2 * (n_chunks + c) + 1)

    def drain(cur):
        wait_gathers(cur)
        src_copy(w + 1, 1 - cur).wait()

    for parity in (0, 1):
        mine = (w % 2) == parity
        pl.when(mine & (w < n_used))(functools.partial(compute_tile, parity))
        pl.when(mine & (w == n_used))(functools.partial(drain, parity))

    @pl.when((w >= n_used) & (w < n_tiles))
    def _():
        ys_ref[...] = jnp.zeros_like(ys_ref)


def _experts(tile_expert, n_used, src_tab, hnp, w_ei, w_eo, *, tm):
    n_e, d, f2 = w_ei.shape
    dw = hnp.shape[1]
    n_steps = src_tab.shape[0] - 1
    n_tiles = n_steps - 1
    assert tm % EXPERT_DMA_GROUPS == 0 and src_tab.shape == (n_steps + 1, tm)

    def expert_of(w, te, nu):
        return te[jnp.minimum(w, nu[0] - 1)]

    grid_spec = pltpu.PrefetchScalarGridSpec(
        num_scalar_prefetch=2,
        grid=(n_steps,),
        in_specs=[
            pl.BlockSpec(memory_space=pl.ANY),
            pl.BlockSpec(memory_space=pl.ANY),
            pl.BlockSpec((1, d, f2), lambda w, te, nu: (expert_of(w, te, nu), 0, 0)),
            pl.BlockSpec((1, f2 // 2, d), lambda w, te, nu: (expert_of(w, te, nu), 0, 0)),
        ],
        out_specs=pl.BlockSpec((tm, d), lambda w, te, nu: (jnp.minimum(w, n_tiles - 1), 0)),
        scratch_shapes=[pltpu.SMEM((2, tm), jnp.int32), pltpu.VMEM((2, tm, dw), U32),
                        pltpu.VMEM((tm, f2), F32),
                        pltpu.VMEM((d, f2), BF16), pltpu.VMEM((f2 // 2, d), BF16),
                        pltpu.SemaphoreType.DMA((2,)), pltpu.SemaphoreType.DMA((2,))],
    )
    return pl.pallas_call(
        _experts_kernel,
        grid_spec=grid_spec,
        out_shape=jax.ShapeDtypeStruct((n_tiles * tm, d), F32),
        compiler_params=_params(1),
        name="experts",
    )(tile_expert, n_used, src_tab, hnp, w_ei, w_eo)


def _expert_tiles(counts, e12, r12, n_all, tm):
    max_tiles = (2 * n_all) // tm + N_EXPERTS
    n_rows = max_tiles + 2
    tiles_per_expert = (counts + tm - 1) // tm
    tile_end = jnp.cumsum(tiles_per_expert)
    n_used = tile_end[-1]
    row_start = (tile_end - tiles_per_expert) * tm
    t = jnp.minimum(jnp.arange(n_rows, dtype=jnp.int32), n_used - 1)
    tile_expert = jnp.sum(tile_end[None, :] <= t[:, None], axis=1).astype(jnp.int32)

    pos = row_start[e12] + r12
    token = jnp.repeat(jnp.arange(n_all, dtype=jnp.int32), 2)
    src = jnp.zeros((n_rows * tm,), jnp.int32).at[pos.reshape(-1)].set(
        token, unique_indices=True, mode="promise_in_bounds").reshape(n_rows, tm)
    pos_tiles = pos.reshape(n_all // tm, tm, 2).transpose(0, 2, 1).reshape(n_all // tm, 2 * tm)
    return tile_expert, n_used.reshape(1).astype(jnp.int32), src, pos_tiles.astype(jnp.int32)


def _combine_kernel(pos_hbm, h_ref, rinfo_ref, g_ref, ys_hbm, yp_ref, ysm_ref,
                    idx_ref, buf_ref, idx_sem, row_sem, *, n_prompt_tiles):
    i = pl.program_id(0)
    nt = pl.num_programs(0)
    tm = h_ref.shape[0]

    def idx_copy(tile, slot):
        return pltpu.make_async_copy(pos_hbm.at[pl.ds(tile, 1), :], idx_ref.at[pl.ds(slot, 1), :],
                                     idx_sem.at[slot])

    def start_rows(slot):
        def body(r, _):
            for k in range(2):
                pltpu.make_async_copy(ys_hbm.at[pl.ds(idx_ref[slot, k * tm + r], 1), :],
                                      buf_ref.at[slot, k, pl.ds(r, 1), :], row_sem.at[slot]).start()
            return 0
        lax.fori_loop(0, tm, body, 0, unroll=8)

    def wait_rows(slot):
        pltpu.make_async_copy(buf_ref.at[slot], buf_ref.at[slot], row_sem.at[slot]).wait()

    @pl.when(i == 0)
    def _():
        idx_copy(0, 0).start()
        idx_copy(0, 0).wait()
        start_rows(0)

        @pl.when(nt > 1)
        def _():
            idx_copy(1, 1).start()

    def tile(cur):
        nxt = 1 - cur

        @pl.when(i + 1 < nt)
        def _():
            idx_copy(i + 1, nxt).wait()
            start_rows(nxt)

        @pl.when(i + 2 < nt)
        def _():
            idx_copy(i + 2, cur).start()

        wait_rows(cur)
        rinfo = rinfo_ref[...]
        lane = lax.broadcasted_iota(jnp.int32, rinfo.shape, 1)
        w1 = _lane_pick(rinfo, lane, R_W1)
        w2 = _lane_pick(rinfo, lane, R_W2)
        hp = h_ref[...] + (w1 * buf_ref[cur, 0] + w2 * buf_ref[cur, 1])
        y = _rms(hp, g_ref[...])

        @pl.when(i < n_prompt_tiles)
        def _():
            yp_ref[...] = y

        @pl.when(i >= n_prompt_tiles)
        def _():
            ysm_ref[...] = y

    for parity in (0, 1):
        pl.when((i % 2) == parity)(functools.partial(tile, parity))


def _combine(pos_tiles, h, rinfo, norm_g, ys, *, n_prompt, tm):
    n_all, d = h.shape
    nt = n_all // tm
    npt = n_prompt // tm
    assert n_all - n_prompt == tm and pos_tiles.shape == (nt, 2 * tm)
    kern = functools.partial(_combine_kernel, n_prompt_tiles=npt)
    return pl.pallas_call(
        kern,
        grid=(nt,),
        in_specs=[
            pl.BlockSpec(memory_space=pl.ANY),
            pl.BlockSpec((tm, d), lambda i: (i, 0)),
            pl.BlockSpec((tm, LANES), lambda i: (i, 0)),
            pl.BlockSpec((1, d), lambda i: (0, 0)),
            pl.BlockSpec(memory_space=pl.ANY),
        ],
        out_specs=[
            pl.BlockSpec((tm, d), lambda i: (jnp.minimum(i, npt - 1), 0)),
            pl.BlockSpec((tm, d), lambda i: (0, 0)),
        ],
        out_shape=[jax.ShapeDtypeStruct((n_prompt, d), F32), jax.ShapeDtypeStruct((tm, d), F32)],
        scratch_shapes=[pltpu.SMEM((2, 2 * tm), jnp.int32), pltpu.VMEM((2, 2, tm, d), F32),
                        pltpu.SemaphoreType.DMA((2,)), pltpu.SemaphoreType.DMA((2,))],
        compiler_params=_params(1),
        name="combine",
    )(pos_tiles, h, rinfo, norm_g, ys)


ROW_TILE = 256
PROJ_ROWS = 512
PROJ_COLS = 512
ATTN_Q_ROWS = 256


def kernel(x_prompt, x_sample, cache_k, cache_v, state_conv, norm_mix, w_in, conv_w, lambda_q1,
           lambda_k1, lambda_q2, lambda_k2, head_norm, w_out, norm_ffn, w_router_group,
           b_router_group, w_router_expert, b_router_expert, w_expert_in, w_expert_out, norm_final):
    depth = w_in.shape[0]
    b, t, d = x_prompt.shape
    db, dt, _ = x_sample.shape
    past = cache_k.shape[2]
    d_conv = conv_w.shape[2]
    d_attn = d - d_conv
    n_p, n_s = b * t, db * dt
    n_all = n_p + n_s
    tm = ROW_TILE

    hp = x_prompt.reshape(n_p, d)
    hs = x_sample.reshape(n_s, d)
    outs = {name: [] for name in ("kp", "vp", "cp", "ks", "vs", "cs")}
    y_p = y_s = None
    for l in range(depth):
        lam_init = 0.8 - 0.6 * math.exp(-0.3 * l)
        w_in_bf = w_in[l].astype(BF16)
        w_out_bf = w_out[l].astype(BF16).reshape(2, d_conv, OUT_PROJ_CHUNKS, d // OUT_PROJ_CHUNKS)
        w_out_bf = w_out_bf.transpose(0, 2, 1, 3)
        g_mix = norm_mix[l].reshape(1, d)
        lams = tuple(p[l].reshape(1, -1) for p in (lambda_q1, lambda_k1, lambda_q2, lambda_k2))
        hnorm = head_norm[l].reshape(1, -1)
        pad = LANES - N_GROUPS - N_EXPERTS
        w_router = jnp.concatenate([w_router_group[l], w_router_expert[l], jnp.zeros((d, pad), F32)],
                                   axis=1).astype(BF16)
        b_router = jnp.concatenate([b_router_group[l], b_router_expert[l], jnp.zeros((pad,), F32)]).reshape(1, LANES)

        zero_prev = jnp.zeros((b, CONV_WIDTH - 1, d_conv), F32)
        a_p, q_p, kb_p, kt_p, v_p, tail_p = _in_proj(hp, g_mix, w_in_bf, conv_w[l], zero_prev,
                                                     seq_len=t, tm=min(PROJ_ROWS, t), tn=PROJ_COLS)
        o_p = _attn_prompt(q_p, kb_p, v_p, lams, hnorm, batch=b, seq_len=t, lam_init=lam_init,
                           tq=min(ATTN_Q_ROWS, t))
        prev_rows = jnp.stack([jnp.repeat(state_conv[l][:, 0], dt, axis=0),
                               jnp.repeat(state_conv[l][:, 1], dt, axis=0)])
        a_s, q_s, kb_s, k_s, v_s, u_s = _in_proj(hs, g_mix, w_in_bf, conv_w[l], prev_rows,
                                                 seq_len=dt, tm=n_s, tn=PROJ_COLS)
        kt_past = cache_k[l].transpose(0, 2, 3, 4, 1).reshape(db * d_attn, past)
        o_s = _attn_sample(q_s, kb_s, v_s, kt_past, cache_v[l].reshape(db * past, d_attn), lams, hnorm,
                           batch=db, seq_len=dt, past_len=past, lam_init=lam_init)

        h, hnp, rinfo, cnt = _out_proj(a_p, o_p, hp, a_s, o_s, hs, w_out_bf,
                                       norm_ffn[l].reshape(1, d), w_router, b_router, tm=tm)
        counts = cnt[0, :N_EXPERTS].astype(jnp.int32)
        e12 = rinfo[:, R_E1:R_E2 + 1].astype(jnp.int32)
        r12 = rinfo[:, R_RANK1:R_RANK2 + 1].astype(jnp.int32)
        tile_expert, n_used, src_tab, pos_tiles = _expert_tiles(counts, e12, r12, n_all, tm)
        ys = _experts(tile_expert, n_used, src_tab, hnp, w_expert_in[l], w_expert_out[l], tm=tm)
        last = l == depth - 1
        g_out = norm_final.reshape(1, d) if last else jnp.ones((1, d), F32)
        y_p, y_s = _combine(pos_tiles, h, rinfo, g_out, ys, n_prompt=n_p, tm=tm)
        assert last, "multi-layer stacks need an un-normalised combine output"

        outs["kp"].append(kt_p.reshape(b, N_HEADS, 2, d_attn // (2 * N_HEADS), t).transpose(0, 4, 1, 2, 3))
        outs["vp"].append(v_p.reshape(b, t, N_HEADS, d_attn // N_HEADS))
        tails = tail_p.reshape(b, t // min(PROJ_ROWS, t), SUBLANES, d_conv)
        outs["cp"].append(tails[:, -1, SUBLANES - (CONV_WIDTH - 1):, :])
        outs["ks"].append(k_s.reshape(db, dt, N_HEADS, 2, d_attn // (2 * N_HEADS)))
        outs["vs"].append(v_s.reshape(db, dt, N_HEADS, d_attn // N_HEADS))
        outs["cs"].append(u_s.reshape(db, dt, d_conv)[:, dt - (CONV_WIDTH - 1):, :])

    return (y_p.reshape(b, t, d), y_s.reshape(db, dt, d),
            jnp.stack(outs["kp"]), jnp.stack(outs["vp"]), jnp.stack(outs["cp"]),
            jnp.stack(outs["ks"]), jnp.stack(outs["vs"]), jnp.stack(outs["cs"]))
```

```python
import functools
import math

import jax
import jax.numpy as jnp
from jax import lax
from jax.experimental import pallas as pl
from jax.experimental.pallas import tpu as pltpu

EPS = 1e-6
N_HEADS = 8
CHUNK = 64
CONV_WIDTH = 3
N_GROUPS = 4
EXPERTS_PER_GROUP = 8
N_EXPERTS = N_GROUPS * EXPERTS_PER_GROUP
LANES = 128
SUBLANES = 8
BF16_SUBLANES = 16
VMEM_LIMIT = 56 * 1024 * 1024

F32 = jnp.float32
BF16 = jnp.bfloat16
U32 = jnp.uint32
NEG_INF = float("-inf")


def _params(n_axes, vmem=VMEM_LIMIT, **kw):
    return pltpu.CompilerParams(dimension_semantics=("arbitrary",) * n_axes,
                                vmem_limit_bytes=vmem, **kw)


def _rms(x, g):
    return x * lax.rsqrt(jnp.mean(x * x, axis=-1, keepdims=True) + EPS) * g


def _lane_pick(x, lane, idx):
    return jnp.sum(jnp.where(lane == idx, x, 0.0), axis=1, keepdims=True)


def _in_proj_kernel(x_ref, g_ref, wa_ref, wb_ref, wc_ref, cw_ref, prev_ref,
                    a_ref, q_ref, kb_ref, ko_ref, v_ref, u_ref, xn_ref, carry_ref,
                    *, seq_len, tiles_per_seq, rows_span_sequences, q_scale):
    i = pl.program_id(0)
    part = pl.program_id(1)
    j = pl.program_id(2)
    tm = x_ref.shape[0]

    @pl.when((part == 0) & (j == 0))
    def _():
        xn_ref[...] = _rms(x_ref[...], g_ref[...]).astype(BF16)

    xn = xn_ref[...]
    za = jnp.dot(xn, wa_ref[...], preferred_element_type=F32)
    zb = jnp.dot(xn, wb_ref[...], preferred_element_type=F32)
    zc = jnp.dot(xn, wc_ref[...], preferred_element_type=F32)

    @pl.when(part == 0)
    def _():
        u = zb * zc
        row = lax.broadcasted_iota(jnp.int32, u.shape, 0)
        if rows_span_sequences:
            rmod = row % seq_len
            pm2, pm1 = prev_ref[0], prev_ref[1]
            first, second = rmod == 0, rmod == 1
            u_ref[...] = u
        else:
            seq_start = (i % tiles_per_seq) == 0
            tail = carry_ref[j]
            pm2 = jnp.where(seq_start, prev_ref[0, 0:1, :], tail[SUBLANES - 2:SUBLANES - 1, :])
            pm1 = jnp.where(seq_start, prev_ref[0, 1:2, :], tail[SUBLANES - 1:SUBLANES, :])
            first, second = row == 0, row == 1
            new_tail = u[tm - SUBLANES:, :]
            carry_ref[j] = new_tail
            u_ref[0] = new_tail
        u1 = jnp.where(first, pm1, pltpu.roll(u, 1, 0))
        u2 = jnp.where(first, pm2, jnp.where(second, pm1, pltpu.roll(u, 2, 0)))
        cw = cw_ref[...]
        y = cw[0:1, :] * u2 + cw[1:2, :] * u1 + cw[2:3, :] * u
        a_ref[...] = (za * y).astype(BF16)

    @pl.when(part == 1)
    def _():
        q_ref[...] = (za * q_scale).astype(BF16)
        kb_ref[...] = zb.astype(BF16)
        if rows_span_sequences:
            ko_ref[...] = zb
        else:
            ko_ref[0] = zb.T
        v_ref[...] = zc


def _in_proj(x2d, norm_g, w_in_bf, conv_w, prev, *, seq_len, tm, tn):
    n, d = x2d.shape
    d_conv = conv_w.shape[1]
    assert w_in_bf.shape == (d, 6 * d_conv), "conv and attention widths must match"
    assert n % tm == 0 and d_conv % tn == 0
    nj = d_conv // tn
    n_i = n // tm
    rows_span_sequences = tm > seq_len
    if rows_span_sequences:
        assert tm % seq_len == 0 and prev.shape == (2, n, d_conv)
        tiles_per_seq = 1
        prev_spec = pl.BlockSpec((2, tm, tn), lambda i, p, j: (0, i, j))
        u_shape = jax.ShapeDtypeStruct((n, d_conv), F32)
        u_spec = pl.BlockSpec((tm, tn), lambda i, p, j: (i, jnp.where(p == 0, j, nj - 1)))
        ko_shape = jax.ShapeDtypeStruct((n, d_conv), F32)
        ko_spec = pl.BlockSpec((tm, tn), lambda i, p, j: (i, jnp.where(p == 1, j, 0)))
    else:
        assert seq_len % tm == 0 and prev.shape == (n // seq_len, CONV_WIDTH - 1, d_conv)
        tiles_per_seq = seq_len // tm
        prev_spec = pl.BlockSpec((1, CONV_WIDTH - 1, tn), lambda i, p, j: (i // tiles_per_seq, 0, j))
        u_shape = jax.ShapeDtypeStruct((n_i, SUBLANES, d_conv), F32)
        u_spec = pl.BlockSpec((1, SUBLANES, tn), lambda i, p, j: (i, 0, jnp.where(p == 0, j, nj - 1)))
        ko_shape = jax.ShapeDtypeStruct((n // seq_len, d_conv, seq_len), F32)
        ko_spec = pl.BlockSpec((1, tn, tm), lambda i, p, j: (i // tiles_per_seq, jnp.where(p == 1, j, 0),
                                                             i % tiles_per_seq))

    def wspec(k):
        return pl.BlockSpec((d, tn), lambda i, p, j: (0, p * 3 * nj + k * nj + j))

    conv_out = lambda i, p, j: (i, jnp.where(p == 0, j, nj - 1))
    attn_out = lambda i, p, j: (i, jnp.where(p == 1, j, 0))
    head_dim = d_conv // (2 * N_HEADS)
    kern = functools.partial(_in_proj_kernel, seq_len=seq_len, tiles_per_seq=tiles_per_seq,
                             rows_span_sequences=rows_span_sequences,
                             q_scale=head_dim ** -0.5 * math.log2(math.e))
    return pl.pallas_call(
        kern,
        grid=(n_i, 2, nj),
        in_specs=[
            pl.BlockSpec((tm, d), lambda i, p, j: (i, 0)),
            pl.BlockSpec((1, d), lambda i, p, j: (0, 0)),
            wspec(0), wspec(1), wspec(2),
            pl.BlockSpec((CONV_WIDTH, tn), lambda i, p, j: (0, j)),
            prev_spec,
        ],
        out_specs=[
            pl.BlockSpec((tm, tn), conv_out),
            pl.BlockSpec((tm, tn), attn_out),
            pl.BlockSpec((tm, tn), attn_out),
            ko_spec,
            pl.BlockSpec((tm, tn), attn_out),
            u_spec,
        ],
        out_shape=[
            jax.ShapeDtypeStruct((n, d_conv), BF16),
            jax.ShapeDtypeStruct((n, d_conv), BF16),
            jax.ShapeDtypeStruct((n, d_conv), BF16),
            ko_shape,
            jax.ShapeDtypeStruct((n, d_conv), F32),
            u_shape,
        ],
        scratch_shapes=[pltpu.VMEM((tm, d), BF16), pltpu.VMEM((nj, SUBLANES, tn), F32)],
        compiler_params=_params(3),
        name="in_proj",
    )(x2d, norm_g, w_in_bf, w_in_bf, w_in_bf, conv_w, prev)


def _diff_lambda(lq1_ref, lk1_ref, lq2_ref, lk2_ref, lam_init):
    s1 = jnp.sum(lq1_ref[...] * lk1_ref[...], axis=1, keepdims=True)
    s2 = jnp.sum(lq2_ref[...] * lk2_ref[...], axis=1, keepdims=True)
    return jnp.exp(s1) - jnp.exp(s2) + lam_init


def _stacked_queries(q, head_dim):
    lane = lax.broadcasted_iota(jnp.int32, q.shape, 1)
    zero = jnp.zeros_like(q)
    return jnp.concatenate([jnp.where(lane < head_dim, q, zero),
                            jnp.where(lane >= head_dim, q, zero)], axis=0)


def _scores(qq, kblk):
    return lax.dot_general(qq, kblk, (((1,), (1,)), ((), ())), preferred_element_type=F32)


def _finish_heads(acc, l, lam, hn, lam_init, tq):
    o = acc[:tq] / l[:tq] - lam * (acc[tq:] / l[tq:])
    return (_rms(o, hn) * (1.0 - lam_init)).astype(BF16)


def _attn_prompt_kernel(lq1_ref, lk1_ref, lq2_ref, lk2_ref, hn_ref, q_ref, k_ref, v_ref, o_ref,
                        qt1_ref, qt2_ref, vt_ref, s0_ref, s1_ref, p0_ref, p1_ref,
                        *, lam_init, head_dim, tq):
    seq_len, hw = q_ref.shape
    lam = _diff_lambda(lq1_ref, lk1_ref, lq2_ref, lk2_ref, lam_init)

    qt = q_ref[...].astype(F32).T
    dim = lax.broadcasted_iota(jnp.int32, qt.shape, 0)
    qt1_ref[...] = jnp.where(dim < head_dim, qt, 0.0).astype(BF16)
    qt2_ref[...] = jnp.where(dim >= head_dim, qt, 0.0).astype(BF16)
    vt_ref[:hw, :] = v_ref[...].T.astype(BF16)
    vt_ref[hw:, :] = jnp.ones((vt_ref.shape[0] - hw, seq_len), BF16)

    key_chunk = lax.broadcasted_iota(jnp.int32, (tq, 2 * tq), 0) // CHUNK
    qry_chunk = (lax.broadcasted_iota(jnp.int32, (tq, 2 * tq), 1) % tq) // CHUNK
    visible = key_chunk <= qry_chunk

    bufs = ((s0_ref, p0_ref), (s1_ref, p1_ref))
    n_blocks = seq_len // tq

    def scores(qi):
        q0, n_keys, s_buf = qi * tq, (qi + 1) * tq, bufs[qi % 2][0]
        qqt = jnp.concatenate([qt1_ref[:, q0:n_keys], qt2_ref[:, q0:n_keys]], axis=1)
        s = jnp.dot(k_ref[q0:n_keys, :], qqt, preferred_element_type=F32)
        s = jnp.where(visible, s, NEG_INF)
        s_buf[q0:n_keys, :] = s
        m = jnp.max(s, axis=0, keepdims=True)
        if qi > 0:
            s = jnp.dot(k_ref[:q0, :], qqt, preferred_element_type=F32)
            s_buf[:q0, :] = s
            m = jnp.maximum(m, jnp.max(s, axis=0, keepdims=True))
        return m

    def finish(qi, m):
        q0, n_keys = qi * tq, (qi + 1) * tq
        s_buf, p_buf = bufs[qi % 2]
        p_buf[:n_keys, :] = jnp.exp2(s_buf[:n_keys, :] - m).astype(BF16)
        ot = jnp.dot(vt_ref[:, :n_keys], p_buf[:n_keys, :], preferred_element_type=F32)
        ot = ot[:hw] / ot[hw:hw + 1]
        o = (ot[:, :tq] - lam * ot[:, tq:]).T
        o_ref[q0:n_keys, :] = (_rms(o, hn_ref[...]) * (1.0 - lam_init)).astype(BF16)

    m = scores(0)
    for qi in range(n_blocks):
        m_next = scores(qi + 1) if qi + 1 < n_blocks else None
        finish(qi, m)
        m = m_next


def _attn_prompt(q, k, v, lams, head_norm, *, batch, seq_len, lam_init, tq):
    n, d_attn = q.shape
    hw = d_attn // N_HEADS
    assert seq_len % tq == 0 and tq % CHUNK == 0
    small = pl.BlockSpec((1, hw // 2), lambda b, h: (0, 0))
    head = pl.BlockSpec((seq_len, hw), lambda b, h: (b, h))
    kern = functools.partial(_attn_prompt_kernel, lam_init=lam_init, head_dim=hw // 2, tq=tq)
    return pl.pallas_call(
        kern,
        grid=(batch, N_HEADS),
        in_specs=[small, small, small, small, pl.BlockSpec((1, hw), lambda b, h: (0, 0)),
                  head, head, head],
        out_specs=head,
        out_shape=jax.ShapeDtypeStruct((n, d_attn), BF16),
        scratch_shapes=[pltpu.VMEM((hw, seq_len), BF16), pltpu.VMEM((hw, seq_len), BF16),
                        pltpu.VMEM((hw + BF16_SUBLANES, seq_len), BF16),
                        pltpu.VMEM((seq_len, 2 * tq), F32), pltpu.VMEM((seq_len, 2 * tq), F32),
                        pltpu.VMEM((seq_len, 2 * tq), BF16), pltpu.VMEM((seq_len, 2 * tq), BF16)],
        compiler_params=_params(2),
        name="attn_prompt",
    )(*lams, head_norm, q, k, v)


def _attn_sample_kernel(lq1_ref, lk1_ref, lq2_ref, lk2_ref, hn_ref, q_ref, kc_ref, vc_ref,
                        kn_ref, vn_ref, o_ref, *, lam_init, head_dim):
    tq = q_ref.shape[0]
    lam = _diff_lambda(lq1_ref, lk1_ref, lq2_ref, lk2_ref, lam_init)
    qq = _stacked_queries(q_ref[...], head_dim)
    s_past = jnp.dot(qq, kc_ref[...].astype(BF16), preferred_element_type=F32)
    s_new = _scores(qq, kn_ref[...])
    m = jnp.maximum(jnp.max(s_past, axis=1, keepdims=True), jnp.max(s_new, axis=1, keepdims=True))
    p_past = jnp.exp2(s_past - m)
    p_new = jnp.exp2(s_new - m)
    l = jnp.sum(p_past, axis=1, keepdims=True) + jnp.sum(p_new, axis=1, keepdims=True)
    acc = (jnp.dot(p_past.astype(BF16), vc_ref[...].astype(BF16), preferred_element_type=F32)
           + jnp.dot(p_new.astype(BF16), vn_ref[...].astype(BF16), preferred_element_type=F32))
    o_ref[...] = _finish_heads(acc, l, lam, hn_ref[...], lam_init, tq)


def _attn_sample(q, k_new, v_new, k_past, v_past, lams, head_norm, *, batch, seq_len, past_len, lam_init):
    n, d_attn = q.shape
    hw = d_attn // N_HEADS
    small = pl.BlockSpec((1, hw // 2), lambda b, h: (0, 0))
    new_spec = pl.BlockSpec((seq_len, hw), lambda b, h: (b, h))
    past_spec = pl.BlockSpec((past_len, hw), lambda b, h: (b, h))
    past_t_spec = pl.BlockSpec((hw, past_len), lambda b, h: (b * N_HEADS + h, 0))
    kern = functools.partial(_attn_sample_kernel, lam_init=lam_init, head_dim=hw // 2)
    return pl.pallas_call(
        kern,
        grid=(batch, N_HEADS),
        in_specs=[small, small, small, small, pl.BlockSpec((1, hw), lambda b, h: (0, 0)),
                  new_spec, past_t_spec, past_spec, new_spec, new_spec],
        out_specs=new_spec,
        out_shape=jax.ShapeDtypeStruct((n, d_attn), BF16),
        compiler_params=_params(2),
        name="attn_sample",
    )(*lams, head_norm, q, k_past, v_past, k_new, v_new)


R_E1, R_E2, R_W1, R_W2, R_RANK1, R_RANK2 = 0, 1, 2, 3, 4, 5
OUT_PROJ_CHUNKS = 4


def _pack_bf16_pairs(lo_bf, hi_bf):
    lo = lax.bitcast_convert_type(lo_bf.astype(F32), U32)
    hi = lax.bitcast_convert_type(hi_bf.astype(F32), U32)
    return (hi & jnp.uint32(0xFFFF0000)) | (lo >> 16)


def _unpack_bf16_pairs(w):
    lo = lax.bitcast_convert_type(w << 16, F32).astype(BF16)
    hi = lax.bitcast_convert_type(w & jnp.uint32(0xFFFF0000), F32).astype(BF16)
    return lo, hi


def _out_proj_kernel(ap_ref, op_ref, xp_ref, as_ref, os_ref, xs_ref, w_ref, g_ref,
                     wr_ref, br_ref, h_ref, hnp_ref, rinfo_ref, cnt_ref, carry_ref, *, n_prompt_tiles):
    i = pl.program_id(0)

    @pl.when(i == 0)
    def _():
        carry_ref[...] = jnp.zeros_like(carry_ref)

    @pl.when(i < n_prompt_tiles)
    def _():
        _out_proj_rows(ap_ref, op_ref, xp_ref, w_ref, g_ref, wr_ref, br_ref,
                       h_ref, hnp_ref, rinfo_ref, cnt_ref, carry_ref)

    @pl.when(i >= n_prompt_tiles)
    def _():
        _out_proj_rows(as_ref, os_ref, xs_ref, w_ref, g_ref, wr_ref, br_ref,
                       h_ref, hnp_ref, rinfo_ref, cnt_ref, carry_ref)


def _out_proj_rows(a_ref, o_ref, x_ref, w_ref, g_ref, wr_ref, br_ref,
                   h_ref, hnp_ref, rinfo_ref, cnt_ref, carry_ref):
    tm, d = x_ref.shape
    cw = d // OUT_PROJ_CHUNKS
    a, o = a_ref[...], o_ref[...]

    ss = jnp.zeros((tm, 1), F32)
    for c in range(OUT_PROJ_CHUNKS):
        cols = slice(c * cw, (c + 1) * cw)
        h = (x_ref[:, cols] + jnp.dot(a, w_ref[0, c], preferred_element_type=F32)
             + jnp.dot(o, w_ref[1, c], preferred_element_type=F32))
        h_ref[:, cols] = h
        ss = ss + jnp.sum(h * h, axis=1, keepdims=True)
    inv = lax.rsqrt(ss / d + EPS)

    logits = br_ref[...]
    for c in range(OUT_PROJ_CHUNKS // 2):
        halves = []
        for cc in (c, c + OUT_PROJ_CHUNKS // 2):
            cols = slice(cc * cw, (cc + 1) * cw)
            hb = (h_ref[:, cols] * inv * g_ref[:, cols]).astype(BF16)
            logits = logits + jnp.dot(hb, wr_ref[cols, :], preferred_element_type=F32)
            halves.append(hb)
        hnp_ref[:, c * cw:(c + 1) * cw] = _pack_bf16_pairs(*halves)

    lane = lax.broadcasted_iota(jnp.int32, logits.shape, 1)
    lane_f = lane.astype(F32)
    lg = jnp.where(lane < N_GROUPS, logits, NEG_INF)
    mg = jnp.max(lg, axis=1, keepdims=True)
    grp = jnp.min(jnp.where(lg == mg, lane_f, float(LANES)), axis=1, keepdims=True)
    gate_g = 1.0 / jnp.sum(jnp.exp(lg - mg), axis=1, keepdims=True)
    ex = lane - N_GROUPS
    in_grp = (ex >= 0) & (ex < N_EXPERTS) & ((ex >> 3).astype(F32) == grp)
    le = jnp.where(in_grp, logits, NEG_INF)
    v1 = jnp.max(le, axis=1, keepdims=True)
    i1 = jnp.min(jnp.where(le == v1, lane_f, float(LANES)), axis=1, keepdims=True)
    le2 = jnp.where(lane_f == i1, NEG_INF, le)
    v2 = jnp.max(le2, axis=1, keepdims=True)
    i2 = jnp.min(jnp.where(le2 == v2, lane_f, float(LANES)), axis=1, keepdims=True)
    e21 = jnp.exp(v2 - v1)
    w1 = gate_g / (1.0 + e21)
    w2 = gate_g * e21 / (1.0 + e21)
    e1 = i1 - float(N_GROUPS)
    e2 = i2 - float(N_GROUPS)

    onehot = jnp.where((lane_f == e1) | (lane_f == e2), 1.0, 0.0)
    r = lax.broadcasted_iota(jnp.int32, (tm, tm), 0)
    c = lax.broadcasted_iota(jnp.int32, (tm, tm), 1)
    earlier = jnp.where(c < r, 1.0, 0.0).astype(BF16)
    before = jnp.dot(earlier, onehot.astype(BF16), preferred_element_type=F32) + carry_ref[0:1, :]
    rank1 = _lane_pick(before, lane_f, e1)
    rank2 = _lane_pick(before, lane_f, e2)
    new_carry = carry_ref[0:1, :] + jnp.sum(onehot, axis=0, keepdims=True)
    carry_ref[...] = jnp.broadcast_to(new_carry, carry_ref.shape)
    cnt_ref[...] = jnp.broadcast_to(new_carry, cnt_ref.shape)

    rec = jnp.zeros_like(logits)
    for idx, val in ((R_E1, e1), (R_E2, e2), (R_W1, w1), (R_W2, w2), (R_RANK1, rank1), (R_RANK2, rank2)):
        rec = jnp.where(lane == idx, val, rec)
    rinfo_ref[...] = rec


def _out_proj(a_p, o_p, x_p, a_s, o_s, x_s, w_out_chunks, norm_g, w_router, b_router, *, tm):
    n_p, d = x_p.shape
    n_s = x_s.shape[0]
    dc = a_p.shape[1]
    assert n_p % tm == 0 and n_s == tm, "sample rows must fill exactly one tile"
    npt = n_p // tm
    n_all = n_p + n_s
    p_idx = lambda i: (jnp.minimum(i, npt - 1), 0)
    zero = lambda i: (0, 0)
    kern = functools.partial(_out_proj_kernel, n_prompt_tiles=npt)
    return pl.pallas_call(
        kern,
        grid=(npt + 1,),
        in_specs=[
            pl.BlockSpec((tm, dc), p_idx), pl.BlockSpec((tm, dc), p_idx), pl.BlockSpec((tm, d), p_idx),
            pl.BlockSpec((tm, dc), zero), pl.BlockSpec((tm, dc), zero), pl.BlockSpec((tm, d), zero),
            pl.BlockSpec(w_out_chunks.shape, lambda i: (0, 0, 0, 0)),
            pl.BlockSpec((1, d), zero),
            pl.BlockSpec((d, LANES), zero), pl.BlockSpec((1, LANES), zero),
        ],
        out_specs=[
            pl.BlockSpec((tm, d), lambda i: (i, 0)),
            pl.BlockSpec((tm, d // 2), lambda i: (i, 0)),
            pl.BlockSpec((tm, LANES), lambda i: (i, 0)),
            pl.BlockSpec((SUBLANES, LANES), zero),
        ],
        out_shape=[
            jax.ShapeDtypeStruct((n_all, d), F32),
            jax.ShapeDtypeStruct((n_all, d // 2), U32),
            jax.ShapeDtypeStruct((n_all, LANES), F32),
            jax.ShapeDtypeStruct((SUBLANES, LANES), F32),
        ],
        scratch_shapes=[pltpu.VMEM((SUBLANES, LANES), F32)],
        compiler_params=_params(1),
        name="out_proj",
    )(a_p, o_p, x_p, a_s, o_s, x_s, w_out_chunks, norm_g, w_router, b_router)


EXPERT_DMA_GROUPS = 16
EXPERT_SLOTS = 3


def _experts_kernel(tile_expert_ref, n_used_ref, src_hbm, hnp_hbm, wi_f32_ref, wo_f32_ref, ys_ref,
                    src_ref, xs_ref, gu_ref, wi_ref, wo_ref, src_sem, gather_sem):
    w = pl.program_id(0)
    n_used = n_used_ref[0]
    tm = xs_ref.shape[1]
    n_tiles = pl.num_programs(0) - 1
    group = tm // EXPERT_DMA_GROUPS

    def src_copy(tile, slot):
        return pltpu.make_async_copy(src_hbm.at[pl.ds(tile, 1), :], src_ref.at[pl.ds(slot, 1), :],
                                     src_sem.at[slot])

    def gather_row(slot, r):
        return pltpu.make_async_copy(hnp_hbm.at[pl.ds(src_ref[slot, r], 1), :],
                                     xs_ref.at[slot, pl.ds(r, 1), :], gather_sem.at[slot])

    def wait_gathers(slot):
        pltpu.make_async_copy(xs_ref.at[slot], xs_ref.at[slot], gather_sem.at[slot]).wait()

    @pl.when(w == 0)
    def _():
        for t in range(EXPERT_SLOTS):
            src_copy(t, t).start()
        for t in range(EXPERT_SLOTS - 1):
            src_copy(t, t).wait()
            for r in range(tm):
                gather_row(t, r).start(priority=r % 2)

    def compute_tile(cur):
        ahead = (cur + EXPERT_SLOTS - 1) % EXPERT_SLOTS
        wait_gathers(cur)
        src_copy(w + 2, ahead).wait()
        src_copy(w + 3, cur).start()

        def move_rows(g):
            for r in range(g * group, (g + 1) * group):
                gather_row(ahead, r).start(priority=r % 2)

        @pl.when((w == 0) | (tile_expert_ref[w] != tile_expert_ref[jnp.maximum(w - 1, 0)]))
        def _():
            wi_ref[...] = wi_f32_ref[0].astype(BF16)
            wo_ref[...] = wo_f32_ref[0].astype(BF16)

        lo, hi = _unpack_bf16_pairs(xs_ref[cur])
        half = lo.shape[1]
        n_chunks = EXPERT_DMA_GROUPS // 4
        f2 = wi_ref.shape[1]
        cw = f2 // n_chunks
        for c in range(n_chunks):
            cols = slice(c * cw, (c + 1) * cw)
            part = jnp.dot(lo, wi_ref[:half, cols], preferred_element_type=F32)
            move_rows(2 * c)
            gu_ref[:, cols] = part + jnp.dot(hi, wi_ref[half:, cols], preferred_element_type=F32)
            move_rows(2 * c + 1)
        gate, up = gu_ref[:, :f2 // 2], gu_ref[:, f2 // 2:]
        act = (gate * jax.nn.sigmoid(gate) * up).astype(BF16)
        fh = act.shape[1] // 2
        d = wo_ref.shape[1]
        ow = d // n_chunks
        for c in range(n_chunks):
            cols = slice(c * ow, (c + 1) * ow)
            part = jnp.dot(act[:, :fh], wo_ref[:fh, cols], preferred_element_type=F32)
            move_rows(2 * (n_chunks + c))
            ys_ref[:, cols] = part + jnp.dot(act[:, fh:], wo_ref[fh:, cols], preferred_element_type=F32)
            move_rows(2 * (n_chunks + c) + 1)

    def drain(cur):
        wait_gathers(cur)
        wait_gathers((cur + 1) % EXPERT_SLOTS)
        src_copy(w + 2, (cur + EXPERT_SLOTS - 1) % EXPERT_SLOTS).wait()

    for key in range(EXPERT_SLOTS):
        mine = (w % EXPERT_SLOTS) == key
        pl.when(mine & (w < n_used))(functools.partial(compute_tile, key))
        pl.when(mine & (w == n_used))(functools.partial(drain, key))

    @pl.when((w >= n_used) & (w < n_tiles))
    def _():
        ys_ref[...] = jnp.zeros_like(ys_ref)


def _experts(tile_expert, n_used, src_tab, hnp, w_ei, w_eo, *, tm):
    n_e, d, f2 = w_ei.shape
    dw = hnp.shape[1]
    n_tiles = src_tab.shape[0] - EXPERT_SLOTS
    n_steps = n_tiles + 1
    assert tm % EXPERT_DMA_GROUPS == 0 and src_tab.shape[1] == tm

    def expert_of(w, te, nu):
        return te[jnp.minimum(w, nu[0] - 1)]

    grid_spec = pltpu.PrefetchScalarGridSpec(
        num_scalar_prefetch=2,
        grid=(n_steps,),
        in_specs=[
            pl.BlockSpec(memory_space=pl.ANY),
            pl.BlockSpec(memory_space=pl.ANY),
            pl.BlockSpec((1, d, f2), lambda w, te, nu: (expert_of(w, te, nu), 0, 0)),
            pl.BlockSpec((1, f2 // 2, d), lambda w, te, nu: (expert_of(w, te, nu), 0, 0)),
        ],
        out_specs=pl.BlockSpec((tm, d), lambda w, te, nu: (jnp.minimum(w, n_tiles - 1), 0)),
        scratch_shapes=[pltpu.SMEM((EXPERT_SLOTS, tm), jnp.int32), pltpu.VMEM((EXPERT_SLOTS, tm, dw), U32),
                        pltpu.VMEM((tm, f2), F32),
                        pltpu.VMEM((d, f2), BF16), pltpu.VMEM((f2 // 2, d), BF16),
                        pltpu.SemaphoreType.DMA((EXPERT_SLOTS,)), pltpu.SemaphoreType.DMA((EXPERT_SLOTS,))],
    )
    return pl.pallas_call(
        _experts_kernel,
        grid_spec=grid_spec,
        out_shape=jax.ShapeDtypeStruct((n_tiles * tm, d), F32),
        compiler_params=_params(1),
        name="experts",
    )(tile_expert, n_used, src_tab, hnp, w_ei, w_eo)


def _expert_tiles(counts, e12, r12, n_all, tm):
    max_tiles = (2 * n_all) // tm + N_EXPERTS
    n_rows = max_tiles + EXPERT_SLOTS
    tiles_per_expert = (counts + tm - 1) // tm
    tile_end = jnp.cumsum(tiles_per_expert)
    n_used = tile_end[-1]
    row_start = (tile_end - tiles_per_expert) * tm
    t = jnp.minimum(jnp.arange(n_rows, dtype=jnp.int32), n_used - 1)
    tile_expert = jnp.sum(tile_end[None, :] <= t[:, None], axis=1).astype(jnp.int32)

    pos = row_start[e12] + r12
    token = jnp.repeat(jnp.arange(n_all, dtype=jnp.int32), 2)
    src = jnp.zeros((n_rows * tm,), jnp.int32).at[pos.reshape(-1)].set(
        token, unique_indices=True, mode="promise_in_bounds").reshape(n_rows, tm)
    pos_tiles = pos.reshape(n_all // tm, tm, 2).transpose(0, 2, 1).reshape(n_all // tm, 2 * tm)
    return tile_expert, n_used.reshape(1).astype(jnp.int32), src, pos_tiles.astype(jnp.int32)


def _combine_kernel(pos_hbm, h_ref, rinfo_ref, g_ref, ys_hbm, yp_ref, ysm_ref,
                    idx_ref, buf_ref, idx_sem, row_sem, *, n_prompt_tiles):
    i = pl.program_id(0)
    nt = pl.num_programs(0)
    tm = h_ref.shape[0]

    def idx_copy(tile, slot):
        return pltpu.make_async_copy(pos_hbm.at[pl.ds(tile, 1), :], idx_ref.at[pl.ds(slot, 1), :],
                                     idx_sem.at[slot])

    def start_rows(slot):
        def body(r, _):
            for k in range(2):
                pltpu.make_async_copy(ys_hbm.at[pl.ds(idx_ref[slot, k * tm + r], 1), :],
                                      buf_ref.at[slot, k, pl.ds(r, 1), :], row_sem.at[slot]).start()
            return 0
        lax.fori_loop(0, tm, body, 0, unroll=8)

    def wait_rows(slot):
        pltpu.make_async_copy(buf_ref.at[slot], buf_ref.at[slot], row_sem.at[slot]).wait()

    @pl.when(i == 0)
    def _():
        idx_copy(0, 0).start()
        idx_copy(0, 0).wait()
        start_rows(0)

        @pl.when(nt > 1)
        def _():
            idx_copy(1, 1).start()

    def tile(cur):
        nxt = 1 - cur

        @pl.when(i + 1 < nt)
        def _():
            idx_copy(i + 1, nxt).wait()
            start_rows(nxt)

        @pl.when(i + 2 < nt)
        def _():
            idx_copy(i + 2, cur).start()

        wait_rows(cur)
        rinfo = rinfo_ref[...]
        lane = lax.broadcasted_iota(jnp.int32, rinfo.shape, 1)
        w1 = _lane_pick(rinfo, lane, R_W1)
        w2 = _lane_pick(rinfo, lane, R_W2)
        hp = h_ref[...] + (w1 * buf_ref[cur, 0] + w2 * buf_ref[cur, 1])
        y = _rms(hp, g_ref[...])

        @pl.when(i < n_prompt_tiles)
        def _():
            yp_ref[...] = y

        @pl.when(i >= n_prompt_tiles)
        def _():
            ysm_ref[...] = y

    for parity in (0, 1):
        pl.when((i % 2) == parity)(functools.partial(tile, parity))


def _combine(pos_tiles, h, rinfo, norm_g, ys, *, n_prompt, tm):
    n_all, d = h.shape
    nt = n_all // tm
    npt = n_prompt // tm
    assert n_all - n_prompt == tm and pos_tiles.shape == (nt, 2 * tm)
    kern = functools.partial(_combine_kernel, n_prompt_tiles=npt)
    return pl.pallas_call(
        kern,
        grid=(nt,),
        in_specs=[
            pl.BlockSpec(memory_space=pl.ANY),
            pl.BlockSpec((tm, d), lambda i: (i, 0)),
            pl.BlockSpec((tm, LANES), lambda i: (i, 0)),
            pl.BlockSpec((1, d), lambda i: (0, 0)),
            pl.BlockSpec(memory_space=pl.ANY),
        ],
        out_specs=[
            pl.BlockSpec((tm, d), lambda i: (jnp.minimum(i, npt - 1), 0)),
            pl.BlockSpec((tm, d), lambda i: (0, 0)),
        ],
        out_shape=[jax.ShapeDtypeStruct((n_prompt, d), F32), jax.ShapeDtypeStruct((tm, d), F32)],
        scratch_shapes=[pltpu.SMEM((2, 2 * tm), jnp.int32), pltpu.VMEM((2, 2, tm, d), F32),
                        pltpu.SemaphoreType.DMA((2,)), pltpu.SemaphoreType.DMA((2,))],
        compiler_params=_params(1),
        name="combine",
    )(pos_tiles, h, rinfo, norm_g, ys)


ROW_TILE = 256
PROJ_ROWS = 512
PROJ_COLS = 512
ATTN_Q_ROWS = 256


def kernel(x_prompt, x_sample, cache_k, cache_v, state_conv, norm_mix, w_in, conv_w, lambda_q1,
           lambda_k1, lambda_q2, lambda_k2, head_norm, w_out, norm_ffn, w_router_group,
           b_router_group, w_router_expert, b_router_expert, w_expert_in, w_expert_out, norm_final):
    depth = w_in.shape[0]
    b, t, d = x_prompt.shape
    db, dt, _ = x_sample.shape
    past = cache_k.shape[2]
    d_conv = conv_w.shape[2]
    d_attn = d - d_conv
    n_p, n_s = b * t, db * dt
    n_all = n_p + n_s
    tm = ROW_TILE

    hp = x_prompt.reshape(n_p, d)
    hs = x_sample.reshape(n_s, d)
    outs = {name: [] for name in ("kp", "vp", "cp", "ks", "vs", "cs")}
    y_p = y_s = None
    for l in range(depth):
        lam_init = 0.8 - 0.6 * math.exp(-0.3 * l)
        w_in_bf = w_in[l].astype(BF16)
        w_out_bf = w_out[l].astype(BF16).reshape(2, d_conv, OUT_PROJ_CHUNKS, d // OUT_PROJ_CHUNKS)
        w_out_bf = w_out_bf.transpose(0, 2, 1, 3)
        g_mix = norm_mix[l].reshape(1, d)
        lams = tuple(p[l].reshape(1, -1) for p in (lambda_q1, lambda_k1, lambda_q2, lambda_k2))
        hnorm = head_norm[l].reshape(1, -1)
        pad = LANES - N_GROUPS - N_EXPERTS
        w_router = jnp.concatenate([w_router_group[l], w_router_expert[l], jnp.zeros((d, pad), F32)],
                                   axis=1).astype(BF16)
        b_router = jnp.concatenate([b_router_group[l], b_router_expert[l], jnp.zeros((pad,), F32)]).reshape(1, LANES)

        zero_prev = jnp.zeros((b, CONV_WIDTH - 1, d_conv), F32)
        a_p, q_p, kb_p, kt_p, v_p, tail_p = _in_proj(hp, g_mix, w_in_bf, conv_w[l], zero_prev,
                                                     seq_len=t, tm=min(PROJ_ROWS, t), tn=PROJ_COLS)
        o_p = _attn_prompt(q_p, kb_p, v_p, lams, hnorm, batch=b, seq_len=t, lam_init=lam_init,
                           tq=min(ATTN_Q_ROWS, t))
        prev_rows = jnp.stack([jnp.repeat(state_conv[l][:, 0], dt, axis=0),
                               jnp.repeat(state_conv[l][:, 1], dt, axis=0)])
        a_s, q_s, kb_s, k_s, v_s, u_s = _in_proj(hs, g_mix, w_in_bf, conv_w[l], prev_rows,
                                                 seq_len=dt, tm=n_s, tn=PROJ_COLS)
        kt_past = cache_k[l].transpose(0, 2, 3, 4, 1).reshape(db * d_attn, past)
        o_s = _attn_sample(q_s, kb_s, v_s, kt_past, cache_v[l].reshape(db * past, d_attn), lams, hnorm,
                           batch=db, seq_len=dt, past_len=past, lam_init=lam_init)

        h, hnp, rinfo, cnt = _out_proj(a_p, o_p, hp, a_s, o_s, hs, w_out_bf,
                                       norm_ffn[l].reshape(1, d), w_router, b_router, tm=tm)
        counts = cnt[0, :N_EXPERTS].astype(jnp.int32)
        e12 = rinfo[:, R_E1:R_E2 + 1].astype(jnp.int32)
        r12 = rinfo[:, R_RANK1:R_RANK2 + 1].astype(jnp.int32)
        tile_expert, n_used, src_tab, pos_tiles = _expert_tiles(counts, e12, r12, n_all, tm)
        ys = _experts(tile_expert, n_used, src_tab, hnp, w_expert_in[l], w_expert_out[l], tm=tm)
        last = l == depth - 1
        g_out = norm_final.reshape(1, d) if last else jnp.ones((1, d), F32)
        y_p, y_s = _combine(pos_tiles, h, rinfo, g_out, ys, n_prompt=n_p, tm=tm)
        assert last, "multi-layer stacks need an un-normalised combine output"

        outs["kp"].append(kt_p.reshape(b, N_HEADS, 2, d_attn // (2 * N_HEADS), t).transpose(0, 4, 1, 2, 3))
        outs["vp"].append(v_p.reshape(b, t, N_HEADS, d_attn // N_HEADS))
        tails = tail_p.reshape(b, t // min(PROJ_ROWS, t), SUBLANES, d_conv)
        outs["cp"].append(tails[:, -1, SUBLANES - (CONV_WIDTH - 1):, :])
        outs["ks"].append(k_s.reshape(db, dt, N_HEADS, 2, d_attn // (2 * N_HEADS)))
        outs["vs"].append(v_s.reshape(db, dt, N_HEADS, d_attn // N_HEADS))
        outs["cs"].append(u_s.reshape(db, dt, d_conv)[:, dt - (CONV_WIDTH - 1):, :])

    return (y_p.reshape(b, t, d), y_s.reshape(db, dt, d),
            jnp.stack(outs["kp"]), jnp.stack(outs["vp"]), jnp.stack(outs["cp"]),
            jnp.stack(outs["ks"]), jnp.stack(outs["vs"]), jnp.stack(outs["cs"]))
```

```python
import functools
import math

import jax
import jax.numpy as jnp
from jax import lax
from jax.experimental import pallas as pl
from jax.experimental.pallas import tpu as pltpu

EPS = 1e-6
N_HEADS = 8
CHUNK = 64
CONV_WIDTH = 3
N_GROUPS = 4
EXPERTS_PER_GROUP = 8
N_EXPERTS = N_GROUPS * EXPERTS_PER_GROUP
LANES = 128
SUBLANES = 8
BF16_SUBLANES = 16
VMEM_LIMIT = 56 * 1024 * 1024

F32 = jnp.float32
BF16 = jnp.bfloat16
U32 = jnp.uint32
NEG_INF = float("-inf")


def _params(n_axes, vmem=VMEM_LIMIT, **kw):
    return pltpu.CompilerParams(dimension_semantics=("arbitrary",) * n_axes,
                                vmem_limit_bytes=vmem, **kw)


def _rms(x, g):
    return x * lax.rsqrt(jnp.mean(x * x, axis=-1, keepdims=True) + EPS) * g


def _lane_pick(x, lane, idx):
    return jnp.sum(jnp.where(lane == idx, x, 0.0), axis=1, keepdims=True)


def _in_proj_kernel(x_ref, g_ref, wgb_ref, wgc_ref, wx_ref, wq_ref, wk_ref, wv_ref, cw_ref, prev_ref,
                    a_ref, q_ref, kb_ref, ko_ref, v_ref, u_ref, xn_ref, carry_ref,
                    *, seq_len, tiles_per_seq, rows_span_sequences, q_scale):
    i = pl.program_id(0)
    j = pl.program_id(1)
    tm = x_ref.shape[0]

    @pl.when(j == 0)
    def _():
        xn_ref[...] = _rms(x_ref[...], g_ref[...]).astype(BF16)

    xn = xn_ref[...]

    gate_b = jnp.dot(xn, wgb_ref[...], preferred_element_type=F32)
    u = (jnp.dot(xn, wgc_ref[...], preferred_element_type=F32)
         * jnp.dot(xn, wx_ref[...], preferred_element_type=F32))
    row = lax.broadcasted_iota(jnp.int32, u.shape, 0)
    if rows_span_sequences:
        rmod = row % seq_len
        pm2, pm1 = prev_ref[0], prev_ref[1]
        first, second = rmod == 0, rmod == 1
        u_ref[...] = u
    else:
        seq_start = (i % tiles_per_seq) == 0
        tail = carry_ref[j]
        pm2 = jnp.where(seq_start, prev_ref[0, 0:1, :], tail[SUBLANES - 2:SUBLANES - 1, :])
        pm1 = jnp.where(seq_start, prev_ref[0, 1:2, :], tail[SUBLANES - 1:SUBLANES, :])
        first, second = row == 0, row == 1
        new_tail = u[tm - SUBLANES:, :]
        carry_ref[j] = new_tail
        u_ref[0] = new_tail
    u1 = jnp.where(first, pm1, pltpu.roll(u, 1, 0))
    u2 = jnp.where(first, pm2, jnp.where(second, pm1, pltpu.roll(u, 2, 0)))
    cw = cw_ref[...]
    y = cw[0:1, :] * u2 + cw[1:2, :] * u1 + cw[2:3, :] * u
    a_ref[...] = (gate_b * y).astype(BF16)

    q_ref[...] = (jnp.dot(xn, wq_ref[...], preferred_element_type=F32) * q_scale).astype(BF16)
    k = jnp.dot(xn, wk_ref[...], preferred_element_type=F32)
    kb_ref[...] = k.astype(BF16)
    if rows_span_sequences:
        ko_ref[...] = k
    else:
        ko_ref[0] = k.T
    v_ref[...] = jnp.dot(xn, wv_ref[...], preferred_element_type=F32)


def _in_proj(x2d, norm_g, w_in_bf, conv_w, prev, *, seq_len, tm, tn):
    n, d = x2d.shape
    d_conv = conv_w.shape[1]
    assert w_in_bf.shape == (d, 6 * d_conv), "conv and attention widths must match"
    assert n % tm == 0 and d_conv % tn == 0
    nj = d_conv // tn
    n_i = n // tm
    rows_span_sequences = tm > seq_len
    if rows_span_sequences:
        assert tm % seq_len == 0 and prev.shape == (2, n, d_conv)
        tiles_per_seq = 1
        prev_spec = pl.BlockSpec((2, tm, tn), lambda i, j: (0, i, j))
        u_shape = jax.ShapeDtypeStruct((n, d_conv), F32)
        u_spec = pl.BlockSpec((tm, tn), lambda i, j: (i, j))
        ko_shape = jax.ShapeDtypeStruct((n, d_conv), F32)
        ko_spec = pl.BlockSpec((tm, tn), lambda i, j: (i, j))
    else:
        assert seq_len % tm == 0 and prev.shape == (n // seq_len, CONV_WIDTH - 1, d_conv)
        tiles_per_seq = seq_len // tm
        prev_spec = pl.BlockSpec((1, CONV_WIDTH - 1, tn), lambda i, j: (i // tiles_per_seq, 0, j))
        u_shape = jax.ShapeDtypeStruct((n_i, SUBLANES, d_conv), F32)
        u_spec = pl.BlockSpec((1, SUBLANES, tn), lambda i, j: (i, 0, j))
        ko_shape = jax.ShapeDtypeStruct((n // seq_len, d_conv, seq_len), F32)
        ko_spec = pl.BlockSpec((1, tn, tm), lambda i, j: (i // tiles_per_seq, j, i % tiles_per_seq))

    def wspec(k):
        return pl.BlockSpec((d, tn), lambda i, j: (0, k * nj + j))

    tile = pl.BlockSpec((tm, tn), lambda i, j: (i, j))
    head_dim = d_conv // (2 * N_HEADS)
    kern = functools.partial(_in_proj_kernel, seq_len=seq_len, tiles_per_seq=tiles_per_seq,
                             rows_span_sequences=rows_span_sequences,
                             q_scale=head_dim ** -0.5 * math.log2(math.e))
    return pl.pallas_call(
        kern,
        grid=(n_i, nj),
        in_specs=[
            pl.BlockSpec((tm, d), lambda i, j: (i, 0)),
            pl.BlockSpec((1, d), lambda i, j: (0, 0)),
            wspec(0), wspec(1), wspec(2), wspec(3), wspec(4), wspec(5),
            pl.BlockSpec((CONV_WIDTH, tn), lambda i, j: (0, j)),
            prev_spec,
        ],
        out_specs=[tile, tile, tile, ko_spec, tile, u_spec],
        out_shape=[
            jax.ShapeDtypeStruct((n, d_conv), BF16),
            jax.ShapeDtypeStruct((n, d_conv), BF16),
            jax.ShapeDtypeStruct((n, d_conv), BF16),
            ko_shape,
            jax.ShapeDtypeStruct((n, d_conv), F32),
            u_shape,
        ],
        scratch_shapes=[pltpu.VMEM((tm, d), BF16), pltpu.VMEM((nj, SUBLANES, tn), F32)],
        compiler_params=_params(2),
        name="in_proj",
    )(x2d, norm_g, *([w_in_bf] * 6), conv_w, prev)


def _diff_lambda(lq1_ref, lk1_ref, lq2_ref, lk2_ref, lam_init):
    s1 = jnp.sum(lq1_ref[...] * lk1_ref[...], axis=1, keepdims=True)
    s2 = jnp.sum(lq2_ref[...] * lk2_ref[...], axis=1, keepdims=True)
    return jnp.exp(s1) - jnp.exp(s2) + lam_init


def _stacked_queries(q, head_dim):
    lane = lax.broadcasted_iota(jnp.int32, q.shape, 1)
    zero = jnp.zeros_like(q)
    return jnp.concatenate([jnp.where(lane < head_dim, q, zero),
                            jnp.where(lane >= head_dim, q, zero)], axis=0)


def _scores(qq, kblk):
    return lax.dot_general(qq, kblk, (((1,), (1,)), ((), ())), preferred_element_type=F32)


def _finish_heads(acc, l, lam, hn, lam_init, tq):
    o = acc[:tq] / l[:tq] - lam * (acc[tq:] / l[tq:])
    return (_rms(o, hn) * (1.0 - lam_init)).astype(BF16)


def _attn_prompt_kernel(lq1_ref, lk1_ref, lq2_ref, lk2_ref, hn_ref, q_ref, k_ref, v_ref, o_ref,
                        qt1_ref, qt2_ref, vt_ref, s0_ref, s1_ref, p0_ref, p1_ref,
                        *, lam_init, head_dim, tq):
    seq_len, hw = q_ref.shape
    lam = _diff_lambda(lq1_ref, lk1_ref, lq2_ref, lk2_ref, lam_init)

    qt = q_ref[...].astype(F32).T
    dim = lax.broadcasted_iota(jnp.int32, qt.shape, 0)
    qt1_ref[...] = jnp.where(dim < head_dim, qt, 0.0).astype(BF16)
    qt2_ref[...] = jnp.where(dim >= head_dim, qt, 0.0).astype(BF16)
    vt_ref[:hw, :] = v_ref[...].T.astype(BF16)
    vt_ref[hw:, :] = jnp.ones((vt_ref.shape[0] - hw, seq_len), BF16)

    key_chunk = lax.broadcasted_iota(jnp.int32, (tq, 2 * tq), 0) // CHUNK
    qry_chunk = (lax.broadcasted_iota(jnp.int32, (tq, 2 * tq), 1) % tq) // CHUNK
    visible = key_chunk <= qry_chunk

    bufs = ((s0_ref, p0_ref), (s1_ref, p1_ref))
    n_blocks = seq_len // tq

    def scores(qi):
        q0, n_keys, s_buf = qi * tq, (qi + 1) * tq, bufs[qi % 2][0]
        qqt = jnp.concatenate([qt1_ref[:, q0:n_keys], qt2_ref[:, q0:n_keys]], axis=1)
        s = jnp.dot(k_ref[q0:n_keys, :], qqt, preferred_element_type=F32)
        s = jnp.where(visible, s, NEG_INF)
        s_buf[q0:n_keys, :] = s
        m = jnp.max(s, axis=0, keepdims=True)
        if qi > 0:
            s = jnp.dot(k_ref[:q0, :], qqt, preferred_element_type=F32)
            s_buf[:q0, :] = s
            m = jnp.maximum(m, jnp.max(s, axis=0, keepdims=True))
        return m

    def finish(qi, m):
        q0, n_keys = qi * tq, (qi + 1) * tq
        s_buf, p_buf = bufs[qi % 2]
        p_buf[:n_keys, :] = jnp.exp2(s_buf[:n_keys, :] - m).astype(BF16)
        ot = jnp.dot(vt_ref[:, :n_keys], p_buf[:n_keys, :], preferred_element_type=F32)
        ot = ot[:hw] / ot[hw:hw + 1]
        o = (ot[:, :tq] - lam * ot[:, tq:]).T
        o_ref[q0:n_keys, :] = (_rms(o, hn_ref[...]) * (1.0 - lam_init)).astype(BF16)

    m = scores(0)
    for qi in range(n_blocks):
        m_next = scores(qi + 1) if qi + 1 < n_blocks else None
        finish(qi, m)
        m = m_next


def _attn_prompt(q, k, v, lams, head_norm, *, batch, seq_len, lam_init, tq):
    n, d_attn = q.shape
    hw = d_attn // N_HEADS
    assert seq_len % tq == 0 and tq % CHUNK == 0
    small = pl.BlockSpec((1, hw // 2), lambda b, h: (0, 0))
    head = pl.BlockSpec((seq_len, hw), lambda b, h: (b, h))
    kern = functools.partial(_attn_prompt_kernel, lam_init=lam_init, head_dim=hw // 2, tq=tq)
    return pl.pallas_call(
        kern,
        grid=(batch, N_HEADS),
        in_specs=[small, small, small, small, pl.BlockSpec((1, hw), lambda b, h: (0, 0)),
                  head, head, head],
        out_specs=head,
        out_shape=jax.ShapeDtypeStruct((n, d_attn), BF16),
        scratch_shapes=[pltpu.VMEM((hw, seq_len), BF16), pltpu.VMEM((hw, seq_len), BF16),
                        pltpu.VMEM((hw + BF16_SUBLANES, seq_len), BF16),
                        pltpu.VMEM((seq_len, 2 * tq), F32), pltpu.VMEM((seq_len, 2 * tq), F32),
                        pltpu.VMEM((seq_len, 2 * tq), BF16), pltpu.VMEM((seq_len, 2 * tq), BF16)],
        compiler_params=_params(2),
        name="attn_prompt",
    )(*lams, head_norm, q, k, v)


def _attn_sample_kernel(lq1_ref, lk1_ref, lq2_ref, lk2_ref, hn_ref, q_ref, kc_ref, vc_ref,
                        kn_ref, vn_ref, o_ref, *, lam_init, head_dim):
    tq = q_ref.shape[0]
    lam = _diff_lambda(lq1_ref, lk1_ref, lq2_ref, lk2_ref, lam_init)
    qq = _stacked_queries(q_ref[...], head_dim)
    s_past = jnp.dot(qq, kc_ref[...].astype(BF16), preferred_element_type=F32)
    s_new = _scores(qq, kn_ref[...])
    m = jnp.maximum(jnp.max(s_past, axis=1, keepdims=True), jnp.max(s_new, axis=1, keepdims=True))
    p_past = jnp.exp2(s_past - m)
    p_new = jnp.exp2(s_new - m)
    l = jnp.sum(p_past, axis=1, keepdims=True) + jnp.sum(p_new, axis=1, keepdims=True)
    acc = (jnp.dot(p_past.astype(BF16), vc_ref[...].astype(BF16), preferred_element_type=F32)
           + jnp.dot(p_new.astype(BF16), vn_ref[...].astype(BF16), preferred_element_type=F32))
    o_ref[...] = _finish_heads(acc, l, lam, hn_ref[...], lam_init, tq)


def _attn_sample(q, k_new, v_new, k_past, v_past, lams, head_norm, *, batch, seq_len, past_len, lam_init):
    n, d_attn = q.shape
    hw = d_attn // N_HEADS
    small = pl.BlockSpec((1, hw // 2), lambda b, h: (0, 0))
    new_spec = pl.BlockSpec((seq_len, hw), lambda b, h: (b, h))
    past_spec = pl.BlockSpec((past_len, hw), lambda b, h: (b, h))
    past_t_spec = pl.BlockSpec((hw, past_len), lambda b, h: (b * N_HEADS + h, 0))
    kern = functools.partial(_attn_sample_kernel, lam_init=lam_init, head_dim=hw // 2)
    return pl.pallas_call(
        kern,
        grid=(batch, N_HEADS),
        in_specs=[small, small, small, small, pl.BlockSpec((1, hw), lambda b, h: (0, 0)),
                  new_spec, past_t_spec, past_spec, new_spec, new_spec],
        out_specs=new_spec,
        out_shape=jax.ShapeDtypeStruct((n, d_attn), BF16),
        compiler_params=_params(2),
        name="attn_sample",
    )(*lams, head_norm, q, k_past, v_past, k_new, v_new)


R_E1, R_E2, R_W1, R_W2, R_RANK1, R_RANK2 = 0, 1, 2, 3, 4, 5
OUT_PROJ_CHUNKS = 4


def _pack_bf16_pairs(lo_bf, hi_bf):
    lo = lax.bitcast_convert_type(lo_bf.astype(F32), U32)
    hi = lax.bitcast_convert_type(hi_bf.astype(F32), U32)
    return (hi & jnp.uint32(0xFFFF0000)) | (lo >> 16)


def _unpack_bf16_pairs(w):
    lo = lax.bitcast_convert_type(w << 16, F32).astype(BF16)
    hi = lax.bitcast_convert_type(w & jnp.uint32(0xFFFF0000), F32).astype(BF16)
    return lo, hi


def _out_proj_kernel(ap_ref, op_ref, xp_ref, as_ref, os_ref, xs_ref, w_ref, g_ref,
                     wr_ref, br_ref, h_ref, hnp_ref, rinfo_ref, cnt_ref, carry_ref, *, n_prompt_tiles):
    i = pl.program_id(0)

    @pl.when(i == 0)
    def _():
        carry_ref[...] = jnp.zeros_like(carry_ref)

    @pl.when(i < n_prompt_tiles)
    def _():
        _out_proj_rows(ap_ref, op_ref, xp_ref, w_ref, g_ref, wr_ref, br_ref,
                       h_ref, hnp_ref, rinfo_ref, cnt_ref, carry_ref)

    @pl.when(i >= n_prompt_tiles)
    def _():
        _out_proj_rows(as_ref, os_ref, xs_ref, w_ref, g_ref, wr_ref, br_ref,
                       h_ref, hnp_ref, rinfo_ref, cnt_ref, carry_ref)


def _out_proj_rows(a_ref, o_ref, x_ref, w_ref, g_ref, wr_ref, br_ref,
                   h_ref, hnp_ref, rinfo_ref, cnt_ref, carry_ref):
    tm, d = x_ref.shape
    cw = d // OUT_PROJ_CHUNKS
    a, o = a_ref[...], o_ref[...]

    ss = jnp.zeros((tm, 1), F32)
    for c in range(OUT_PROJ_CHUNKS):
        cols = slice(c * cw, (c + 1) * cw)
        h = (x_ref[:, cols] + jnp.dot(a, w_ref[0, c], preferred_element_type=F32)
             + jnp.dot(o, w_ref[1, c], preferred_element_type=F32))
        h_ref[:, cols] = h
        ss = ss + jnp.sum(h * h, axis=1, keepdims=True)
    inv = lax.rsqrt(ss / d + EPS)

    logits = br_ref[...]
    for c in range(OUT_PROJ_CHUNKS // 2):
        halves = []
        for cc in (c, c + OUT_PROJ_CHUNKS // 2):
            cols = slice(cc * cw, (cc + 1) * cw)
            hb = (h_ref[:, cols] * inv * g_ref[:, cols]).astype(BF16)
            logits = logits + jnp.dot(hb, wr_ref[cols, :], preferred_element_type=F32)
            halves.append(hb)
        hnp_ref[:, c * cw:(c + 1) * cw] = _pack_bf16_pairs(*halves)

    lane = lax.broadcasted_iota(jnp.int32, logits.shape, 1)
    lane_f = lane.astype(F32)
    lg = jnp.where(lane < N_GROUPS, logits, NEG_INF)
    mg = jnp.max(lg, axis=1, keepdims=True)
    grp = jnp.min(jnp.where(lg == mg, lane_f, float(LANES)), axis=1, keepdims=True)
    gate_g = 1.0 / jnp.sum(jnp.exp(lg - mg), axis=1, keepdims=True)
    ex = lane - N_GROUPS
    in_grp = (ex >= 0) & (ex < N_EXPERTS) & ((ex >> 3).astype(F32) == grp)
    le = jnp.where(in_grp, logits, NEG_INF)
    v1 = jnp.max(le, axis=1, keepdims=True)
    i1 = jnp.min(jnp.where(le == v1, lane_f, float(LANES)), axis=1, keepdims=True)
    le2 = jnp.where(lane_f == i1, NEG_INF, le)
    v2 = jnp.max(le2, axis=1, keepdims=True)
    i2 = jnp.min(jnp.where(le2 == v2, lane_f, float(LANES)), axis=1, keepdims=True)
    e21 = jnp.exp(v2 - v1)
    w1 = gate_g / (1.0 + e21)
    w2 = gate_g * e21 / (1.0 + e21)
    e1 = i1 - float(N_GROUPS)
    e2 = i2 - float(N_GROUPS)

    onehot = jnp.where((lane_f == e1) | (lane_f == e2), 1.0, 0.0)
    r = lax.broadcasted_iota(jnp.int32, (tm, tm), 0)
    c = lax.broadcasted_iota(jnp.int32, (tm, tm), 1)
    earlier = jnp.where(c < r, 1.0, 0.0).astype(BF16)
    before = jnp.dot(earlier, onehot.astype(BF16), preferred_element_type=F32) + carry_ref[0:1, :]
    rank1 = _lane_pick(before, lane_f, e1)
    rank2 = _lane_pick(before, lane_f, e2)
    new_carry = carry_ref[0:1, :] + jnp.sum(onehot, axis=0, keepdims=True)
    carry_ref[...] = jnp.broadcast_to(new_carry, carry_ref.shape)
    cnt_ref[...] = jnp.broadcast_to(new_carry, cnt_ref.shape)

    rec = jnp.zeros_like(logits)
    for idx, val in ((R_E1, e1), (R_E2, e2), (R_W1, w1), (R_W2, w2), (R_RANK1, rank1), (R_RANK2, rank2)):
        rec = jnp.where(lane == idx, val, rec)
    rinfo_ref[...] = rec


def _out_proj(a_p, o_p, x_p, a_s, o_s, x_s, w_out_chunks, norm_g, w_router, b_router, *, tm):
    n_p, d = x_p.shape
    n_s = x_s.shape[0]
    dc = a_p.shape[1]
    assert n_p % tm == 0 and n_s == tm, "sample rows must fill exactly one tile"
    npt = n_p // tm
    n_all = n_p + n_s
    p_idx = lambda i: (jnp.minimum(i, npt - 1), 0)
    zero = lambda i: (0, 0)
    kern = functools.partial(_out_proj_kernel, n_prompt_tiles=npt)
    return pl.pallas_call(
        kern,
        grid=(npt + 1,),
        in_specs=[
            pl.BlockSpec((tm, dc), p_idx), pl.BlockSpec((tm, dc), p_idx), pl.BlockSpec((tm, d), p_idx),
            pl.BlockSpec((tm, dc), zero), pl.BlockSpec((tm, dc), zero), pl.BlockSpec((tm, d), zero),
            pl.BlockSpec(w_out_chunks.shape, lambda i: (0, 0, 0, 0)),
            pl.BlockSpec((1, d), zero),
            pl.BlockSpec((d, LANES), zero), pl.BlockSpec((1, LANES), zero),
        ],
        out_specs=[
            pl.BlockSpec((tm, d), lambda i: (i, 0)),
            pl.BlockSpec((tm, d // 2), lambda i: (i, 0)),
            pl.BlockSpec((tm, LANES), lambda i: (i, 0)),
            pl.BlockSpec((SUBLANES, LANES), zero),
        ],
        out_shape=[
            jax.ShapeDtypeStruct((n_all, d), F32),
            jax.ShapeDtypeStruct((n_all, d // 2), U32),
            jax.ShapeDtypeStruct((n_all, LANES), F32),
            jax.ShapeDtypeStruct((SUBLANES, LANES), F32),
        ],
        scratch_shapes=[pltpu.VMEM((SUBLANES, LANES), F32)],
        compiler_params=_params(1),
        name="out_proj",
    )(a_p, o_p, x_p, a_s, o_s, x_s, w_out_chunks, norm_g, w_router, b_router)


EXPERT_DMA_GROUPS = 16
EXPERT_SLOTS = 3


def _experts_kernel(tile_expert_ref, n_used_ref, src_hbm, hnp_hbm, wi_f32_ref, wo_f32_ref, ys_ref,
                    src_ref, xs_ref, gu_ref, wi_ref, wo_ref, src_sem, gather_sem):
    w = pl.program_id(0)
    n_used = n_used_ref[0]
    tm = xs_ref.shape[1]
    n_tiles = pl.num_programs(0) - 1
    group = tm // EXPERT_DMA_GROUPS

    def src_copy(tile, slot):
        return pltpu.make_async_copy(src_hbm.at[pl.ds(tile, 1), :], src_ref.at[pl.ds(slot, 1), :],
                                     src_sem.at[slot])

    def gather_row(slot, r):
        return pltpu.make_async_copy(hnp_hbm.at[pl.ds(src_ref[slot, r], 1), :],
                                     xs_ref.at[slot, pl.ds(r, 1), :], gather_sem.at[slot])

    def wait_gathers(slot):
        pltpu.make_async_copy(xs_ref.at[slot], xs_ref.at[slot], gather_sem.at[slot]).wait()

    @pl.when(w == 0)
    def _():
        for t in range(EXPERT_SLOTS):
            src_copy(t, t).start()
        for t in range(EXPERT_SLOTS - 1):
            src_copy(t, t).wait()
            for r in range(tm):
                gather_row(t, r).start(priority=r % 2)

    def compute_tile(cur):
        ahead = (cur + EXPERT_SLOTS - 1) % EXPERT_SLOTS
        wait_gathers(cur)
        src_copy(w + 2, ahead).wait()
        src_copy(w + 3, cur).start()

        def move_rows(g):
            for r in range(g * group, (g + 1) * group):
                gather_row(ahead, r).start(priority=r % 2)

        @pl.when((w == 0) | (tile_expert_ref[w] != tile_expert_ref[jnp.maximum(w - 1, 0)]))
        def _():
            wi_ref[...] = wi_f32_ref[0].astype(BF16)
            wo_ref[...] = wo_f32_ref[0].astype(BF16)

        lo, hi = _unpack_bf16_pairs(xs_ref[cur])
        half = lo.shape[1]
        n_chunks = EXPERT_DMA_GROUPS // 4
        f2 = wi_ref.shape[1]
        cw = f2 // n_chunks
        for c in range(n_chunks):
            cols = slice(c * cw, (c + 1) * cw)
            part = jnp.dot(lo, wi_ref[:half, cols], preferred_element_type=F32)
            move_rows(2 * c)
            gu_ref[:, cols] = part + jnp.dot(hi, wi_ref[half:, cols], preferred_element_type=F32)
            move_rows(2 * c + 1)
        gate, up = gu_ref[:, :f2 // 2], gu_ref[:, f2 // 2:]
        act = (gate * jax.nn.sigmoid(gate) * up).astype(BF16)
        fh = act.shape[1] // 2
        d = wo_ref.shape[1]
        ow = d // n_chunks
        for c in range(n_chunks):
            cols = slice(c * ow, (c + 1) * ow)
            part = jnp.dot(act[:, :fh], wo_ref[:fh, cols], preferred_element_type=F32)
            move_rows(2 * (n_chunks + c))
            ys_ref[:, cols] = part + jnp.dot(act[:, fh:], wo_ref[fh:, cols], preferred_element_type=F32)
            move_rows(2 * (n_chunks + c) + 1)

    def drain(cur):
        wait_gathers(cur)
        wait_gathers((cur + 1) % EXPERT_SLOTS)
        src_copy(w + 2, (cur + EXPERT_SLOTS - 1) % EXPERT_SLOTS).wait()

    for key in range(EXPERT_SLOTS):
        mine = (w % EXPERT_SLOTS) == key
        pl.when(mine & (w < n_used))(functools.partial(compute_tile, key))
        pl.when(mine & (w == n_used))(functools.partial(drain, key))

    @pl.when((w >= n_used) & (w < n_tiles))
    def _():
        ys_ref[...] = jnp.zeros_like(ys_ref)


def _experts(tile_expert, n_used, src_tab, hnp, w_ei, w_eo, *, tm):
    n_e, d, f2 = w_ei.shape
    dw = hnp.shape[1]
    n_tiles = src_tab.shape[0] - EXPERT_SLOTS
    n_steps = n_tiles + 1
    assert tm % EXPERT_DMA_GROUPS == 0 and src_tab.shape[1] == tm

    def expert_of(w, te, nu):
        return te[jnp.minimum(w, nu[0] - 1)]

    grid_spec = pltpu.PrefetchScalarGridSpec(
        num_scalar_prefetch=2,
        grid=(n_steps,),
        in_specs=[
            pl.BlockSpec(memory_space=pl.ANY),
            pl.BlockSpec(memory_space=pl.ANY),
            pl.BlockSpec((1, d, f2), lambda w, te, nu: (expert_of(w, te, nu), 0, 0)),
            pl.BlockSpec((1, f2 // 2, d), lambda w, te, nu: (expert_of(w, te, nu), 0, 0)),
        ],
        out_specs=pl.BlockSpec((tm, d), lambda w, te, nu: (jnp.minimum(w, n_tiles - 1), 0)),
        scratch_shapes=[pltpu.SMEM((EXPERT_SLOTS, tm), jnp.int32), pltpu.VMEM((EXPERT_SLOTS, tm, dw), U32),
                        pltpu.VMEM((tm, f2), F32),
                        pltpu.VMEM((d, f2), BF16), pltpu.VMEM((f2 // 2, d), BF16),
                        pltpu.SemaphoreType.DMA((EXPERT_SLOTS,)), pltpu.SemaphoreType.DMA((EXPERT_SLOTS,))],
    )
    return pl.pallas_call(
        _experts_kernel,
        grid_spec=grid_spec,
        out_shape=jax.ShapeDtypeStruct((n_tiles * tm, d), F32),
        compiler_params=_params(1),
        name="experts",
    )(tile_expert, n_used, src_tab, hnp, w_ei, w_eo)


def _expert_tiles(counts, e12, r12, n_all, tm):
    max_tiles = (2 * n_all) // tm + N_EXPERTS
    n_rows = max_tiles + EXPERT_SLOTS
    tiles_per_expert = (counts + tm - 1) // tm
    tile_end = jnp.cumsum(tiles_per_expert)
    n_used = tile_end[-1]
    row_start = (tile_end - tiles_per_expert) * tm
    t = jnp.minimum(jnp.arange(n_rows, dtype=jnp.int32), n_used - 1)
    tile_expert = jnp.sum(tile_end[None, :] <= t[:, None], axis=1).astype(jnp.int32)

    pos = row_start[e12] + r12
    token = jnp.repeat(jnp.arange(n_all, dtype=jnp.int32), 2)
    src = jnp.zeros((n_rows * tm,), jnp.int32).at[pos.reshape(-1)].set(
        token, unique_indices=True, mode="promise_in_bounds").reshape(n_rows, tm)
    pos_tiles = pos.reshape(n_all // tm, tm, 2).transpose(0, 2, 1).reshape(n_all // tm, 2 * tm)
    return tile_expert, n_used.reshape(1).astype(jnp.int32), src, pos_tiles.astype(jnp.int32)


def _combine_kernel(pos_hbm, h_ref, rinfo_ref, g_ref, ys_hbm, yp_ref, ysm_ref,
                    idx_ref, buf_ref, idx_sem, row_sem, *, n_prompt_tiles):
    i = pl.program_id(0)
    nt = pl.num_programs(0)
    tm = h_ref.shape[0]

    def idx_copy(tile, slot):
        return pltpu.make_async_copy(pos_hbm.at[pl.ds(tile, 1), :], idx_ref.at[pl.ds(slot, 1), :],
                                     idx_sem.at[slot])

    def start_rows(slot):
        def body(r, _):
            for k in range(2):
                pltpu.make_async_copy(ys_hbm.at[pl.ds(idx_ref[slot, k * tm + r], 1), :],
                                      buf_ref.at[slot, k, pl.ds(r, 1), :], row_sem.at[slot]).start()
            return 0
        lax.fori_loop(0, tm, body, 0, unroll=8)

    def wait_rows(slot):
        pltpu.make_async_copy(buf_ref.at[slot], buf_ref.at[slot], row_sem.at[slot]).wait()

    @pl.when(i == 0)
    def _():
        idx_copy(0, 0).start()
        idx_copy(0, 0).wait()
        start_rows(0)

        @pl.when(nt > 1)
        def _():
            idx_copy(1, 1).start()

    def tile(cur):
        nxt = 1 - cur

        @pl.when(i + 1 < nt)
        def _():
            idx_copy(i + 1, nxt).wait()
            start_rows(nxt)

        @pl.when(i + 2 < nt)
        def _():
            idx_copy(i + 2, cur).start()

        wait_rows(cur)
        rinfo = rinfo_ref[...]
        lane = lax.broadcasted_iota(jnp.int32, rinfo.shape, 1)
        w1 = _lane_pick(rinfo, lane, R_W1)
        w2 = _lane_pick(rinfo, lane, R_W2)
        hp = h_ref[...] + (w1 * buf_ref[cur, 0] + w2 * buf_ref[cur, 1])
        y = _rms(hp, g_ref[...])

        @pl.when(i < n_prompt_tiles)
        def _():
            yp_ref[...] = y

        @pl.when(i >= n_prompt_tiles)
        def _():
            ysm_ref[...] = y

    for parity in (0, 1):
        pl.when((i % 2) == parity)(functools.partial(tile, parity))


def _combine(pos_tiles, h, rinfo, norm_g, ys, *, n_prompt, tm):
    n_all, d = h.shape
    nt = n_all // tm
    npt = n_prompt // tm
    assert n_all - n_prompt == tm and pos_tiles.shape == (nt, 2 * tm)
    kern = functools.partial(_combine_kernel, n_prompt_tiles=npt)
    return pl.pallas_call(
        kern,
        grid=(nt,),
        in_specs=[
            pl.BlockSpec(memory_space=pl.ANY),
            pl.BlockSpec((tm, d), lambda i: (i, 0)),
            pl.BlockSpec((tm, LANES), lambda i: (i, 0)),
            pl.BlockSpec((1, d), lambda i: (0, 0)),
            pl.BlockSpec(memory_space=pl.ANY),
        ],
        out_specs=[
            pl.BlockSpec((tm, d), lambda i: (jnp.minimum(i, npt - 1), 0)),
            pl.BlockSpec((tm, d), lambda i: (0, 0)),
        ],
        out_shape=[jax.ShapeDtypeStruct((n_prompt, d), F32), jax.ShapeDtypeStruct((tm, d), F32)],
        scratch_shapes=[pltpu.SMEM((2, 2 * tm), jnp.int32), pltpu.VMEM((2, 2, tm, d), F32),
                        pltpu.SemaphoreType.DMA((2,)), pltpu.SemaphoreType.DMA((2,))],
        compiler_params=_params(1),
        name="combine",
    )(pos_tiles, h, rinfo, norm_g, ys)


ROW_TILE = 256
PROJ_ROWS = 512
PROJ_COLS = 256
ATTN_Q_ROWS = 256


def kernel(x_prompt, x_sample, cache_k, cache_v, state_conv, norm_mix, w_in, conv_w, lambda_q1,
           lambda_k1, lambda_q2, lambda_k2, head_norm, w_out, norm_ffn, w_router_group,
           b_router_group, w_router_expert, b_router_expert, w_expert_in, w_expert_out, norm_final):
    depth = w_in.shape[0]
    b, t, d = x_prompt.shape
    db, dt, _ = x_sample.shape
    past = cache_k.shape[2]
    d_conv = conv_w.shape[2]
    d_attn = d - d_conv
    n_p, n_s = b * t, db * dt
    n_all = n_p + n_s
    tm = ROW_TILE

    hp = x_prompt.reshape(n_p, d)
    hs = x_sample.reshape(n_s, d)
    outs = {name: [] for name in ("kp", "vp", "cp", "ks", "vs", "cs")}
    y_p = y_s = None
    for l in range(depth):
        lam_init = 0.8 - 0.6 * math.exp(-0.3 * l)
        w_in_bf = w_in[l].astype(BF16)
        w_out_bf = w_out[l].astype(BF16).reshape(2, d_conv, OUT_PROJ_CHUNKS, d // OUT_PROJ_CHUNKS)
        w_out_bf = w_out_bf.transpose(0, 2, 1, 3)
        g_mix = norm_mix[l].reshape(1, d)
        lams = tuple(p[l].reshape(1, -1) for p in (lambda_q1, lambda_k1, lambda_q2, lambda_k2))
        hnorm = head_norm[l].reshape(1, -1)
        pad = LANES - N_GROUPS - N_EXPERTS
        w_router = jnp.concatenate([w_router_group[l], w_router_expert[l], jnp.zeros((d, pad), F32)],
                                   axis=1).astype(BF16)
        b_router = jnp.concatenate([b_router_group[l], b_router_expert[l], jnp.zeros((pad,), F32)]).reshape(1, LANES)

        zero_prev = jnp.zeros((b, CONV_WIDTH - 1, d_conv), F32)
        a_p, q_p, kb_p, kt_p, v_p, tail_p = _in_proj(hp, g_mix, w_in_bf, conv_w[l], zero_prev,
                                                     seq_len=t, tm=min(PROJ_ROWS, t), tn=PROJ_COLS)
        o_p = _attn_prompt(q_p, kb_p, v_p, lams, hnorm, batch=b, seq_len=t, lam_init=lam_init,
                           tq=min(ATTN_Q_ROWS, t))
        prev_rows = jnp.stack([jnp.repeat(state_conv[l][:, 0], dt, axis=0),
                               jnp.repeat(state_conv[l][:, 1], dt, axis=0)])
        a_s, q_s, kb_s, k_s, v_s, u_s = _in_proj(hs, g_mix, w_in_bf, conv_w[l], prev_rows,
                                                 seq_len=dt, tm=n_s, tn=PROJ_COLS)
        kt_past = cache_k[l].transpose(0, 2, 3, 4, 1).reshape(db * d_attn, past)
        o_s = _attn_sample(q_s, kb_s, v_s, kt_past, cache_v[l].reshape(db * past, d_attn), lams, hnorm,
                           batch=db, seq_len=dt, past_len=past, lam_init=lam_init)

        h, hnp, rinfo, cnt = _out_proj(a_p, o_p, hp, a_s, o_s, hs, w_out_bf,
                                       norm_ffn[l].reshape(1, d), w_router, b_router, tm=tm)
        counts = cnt[0, :N_EXPERTS].astype(jnp.int32)
        e12 = rinfo[:, R_E1:R_E2 + 1].astype(jnp.int32)
        r12 = rinfo[:, R_RANK1:R_RANK2 + 1].astype(jnp.int32)
        tile_expert, n_used, src_tab, pos_tiles = _expert_tiles(counts, e12, r12, n_all, tm)
        ys = _experts(tile_expert, n_used, src_tab, hnp, w_expert_in[l], w_expert_out[l], tm=tm)
        last = l == depth - 1
        g_out = norm_final.reshape(1, d) if last else jnp.ones((1, d), F32)
        y_p, y_s = _combine(pos_tiles, h, rinfo, g_out, ys, n_prompt=n_p, tm=tm)
        assert last, "multi-layer stacks need an un-normalised combine output"

        outs["kp"].append(kt_p.reshape(b, N_HEADS, 2, d_attn // (2 * N_HEADS), t).transpose(0, 4, 1, 2, 3))
        outs["vp"].append(v_p.reshape(b, t, N_HEADS, d_attn // N_HEADS))
        tails = tail_p.reshape(b, t // min(PROJ_ROWS, t), SUBLANES, d_conv)
        outs["cp"].append(tails[:, -1, SUBLANES - (CONV_WIDTH - 1):, :])
        outs["ks"].append(k_s.reshape(db, dt, N_HEADS, 2, d_attn // (2 * N_HEADS)))
        outs["vs"].append(v_s.reshape(db, dt, N_HEADS, d_attn // N_HEADS))
        outs["cs"].append(u_s.reshape(db, dt, d_conv)[:, dt - (CONV_WIDTH - 1):, :])

    return (y_p.reshape(b, t, d), y_s.reshape(db, dt, d),
            jnp.stack(outs["kp"]), jnp.stack(outs["vp"]), jnp.stack(outs["cp"]),
            jnp.stack(outs["ks"]), jnp.stack(outs["vs"]), jnp.stack(outs["cs"]))
```

```python
import functools
import math

import jax
import jax.numpy as jnp
from jax import lax
from jax.experimental import pallas as pl
from jax.experimental.pallas import tpu as pltpu

EPS = 1e-6
N_HEADS = 8
CHUNK = 64
CONV_WIDTH = 3
N_GROUPS = 4
EXPERTS_PER_GROUP = 8
N_EXPERTS = N_GROUPS * EXPERTS_PER_GROUP
LANES = 128
SUBLANES = 8
BF16_SUBLANES = 16
VMEM_LIMIT = 56 * 1024 * 1024

F32 = jnp.float32
BF16 = jnp.bfloat16
U32 = jnp.uint32
NEG_INF = float("-inf")


def _params(n_axes, vmem=VMEM_LIMIT, **kw):
    return pltpu.CompilerParams(dimension_semantics=("arbitrary",) * n_axes,
                                vmem_limit_bytes=vmem, **kw)


def _rms(x, g):
    return x * lax.rsqrt(jnp.mean(x * x, axis=-1, keepdims=True) + EPS) * g


def _lane_pick(x, lane, idx):
    return jnp.sum(jnp.where(lane == idx, x, 0.0), axis=1, keepdims=True)


def _in_proj_kernel(x_ref, g_ref, wgb_ref, wgc_ref, wx_ref, wq_ref, wk_ref, wv_ref, cw_ref, prev_ref,
                    a_ref, q_ref, kb_ref, ko_ref, v_ref, u_ref, xn_ref, carry_ref,
                    *, seq_len, tiles_per_seq, rows_span_sequences, q_scale):
    i = pl.program_id(0)
    j = pl.program_id(1)
    tm = x_ref.shape[0]

    @pl.when(j == 0)
    def _():
        xn_ref[...] = _rms(x_ref[...], g_ref[...]).astype(BF16)

    xn = xn_ref[...]

    gate_b = jnp.dot(xn, wgb_ref[...], preferred_element_type=F32)
    u = (jnp.dot(xn, wgc_ref[...], preferred_element_type=F32)
         * jnp.dot(xn, wx_ref[...], preferred_element_type=F32))
    row = lax.broadcasted_iota(jnp.int32, u.shape, 0)
    if rows_span_sequences:
        rmod = row % seq_len
        pm2, pm1 = prev_ref[0], prev_ref[1]
        first, second = rmod == 0, rmod == 1
        u_ref[...] = u
    else:
        seq_start = (i % tiles_per_seq) == 0
        tail = carry_ref[j]
        pm2 = jnp.where(seq_start, prev_ref[0, 0:1, :], tail[SUBLANES - 2:SUBLANES - 1, :])
        pm1 = jnp.where(seq_start, prev_ref[0, 1:2, :], tail[SUBLANES - 1:SUBLANES, :])
        first, second = row == 0, row == 1
        new_tail = u[tm - SUBLANES:, :]
        carry_ref[j] = new_tail
        u_ref[0] = new_tail
    u1 = jnp.where(first, pm1, pltpu.roll(u, 1, 0))
    u2 = jnp.where(first, pm2, jnp.where(second, pm1, pltpu.roll(u, 2, 0)))
    cw = cw_ref[...]
    y = cw[0:1, :] * u2 + cw[1:2, :] * u1 + cw[2:3, :] * u
    a_ref[...] = (gate_b * y).astype(BF16)

    q_ref[...] = (jnp.dot(xn, wq_ref[...], preferred_element_type=F32) * q_scale).astype(BF16)
    k = jnp.dot(xn, wk_ref[...], preferred_element_type=F32)
    kb_ref[...] = k.astype(BF16)
    if rows_span_sequences:
        ko_ref[...] = k
    else:
        ko_ref[0] = k.T
    v_ref[...] = jnp.dot(xn, wv_ref[...], preferred_element_type=F32)


def _in_proj(x2d, norm_g, w_in_bf, conv_w, prev, *, seq_len, tm, tn):
    n, d = x2d.shape
    d_conv = conv_w.shape[1]
    assert w_in_bf.shape == (d, 6 * d_conv), "conv and attention widths must match"
    assert n % tm == 0 and d_conv % tn == 0
    nj = d_conv // tn
    n_i = n // tm
    rows_span_sequences = tm > seq_len
    if rows_span_sequences:
        assert tm % seq_len == 0 and prev.shape == (2, n, d_conv)
        tiles_per_seq = 1
        prev_spec = pl.BlockSpec((2, tm, tn), lambda i, j: (0, i, j))
        u_shape = jax.ShapeDtypeStruct((n, d_conv), F32)
        u_spec = pl.BlockSpec((tm, tn), lambda i, j: (i, j))
        ko_shape = jax.ShapeDtypeStruct((n, d_conv), F32)
        ko_spec = pl.BlockSpec((tm, tn), lambda i, j: (i, j))
    else:
        assert seq_len % tm == 0 and prev.shape == (n // seq_len, CONV_WIDTH - 1, d_conv)
        tiles_per_seq = seq_len // tm
        prev_spec = pl.BlockSpec((1, CONV_WIDTH - 1, tn), lambda i, j: (i // tiles_per_seq, 0, j))
        u_shape = jax.ShapeDtypeStruct((n_i, SUBLANES, d_conv), F32)
        u_spec = pl.BlockSpec((1, SUBLANES, tn), lambda i, j: (i, 0, j))
        ko_shape = jax.ShapeDtypeStruct((n // seq_len, d_conv, seq_len), F32)
        ko_spec = pl.BlockSpec((1, tn, tm), lambda i, j: (i // tiles_per_seq, j, i % tiles_per_seq))

    def wspec(k):
        return pl.BlockSpec((d, tn), lambda i, j: (0, k * nj + j))

    tile = pl.BlockSpec((tm, tn), lambda i, j: (i, j))
    head_dim = d_conv // (2 * N_HEADS)
    kern = functools.partial(_in_proj_kernel, seq_len=seq_len, tiles_per_seq=tiles_per_seq,
                             rows_span_sequences=rows_span_sequences,
                             q_scale=head_dim ** -0.5 * math.log2(math.e))
    return pl.pallas_call(
        kern,
        grid=(n_i, nj),
        in_specs=[
            pl.BlockSpec((tm, d), lambda i, j: (i, 0)),
            pl.BlockSpec((1, d), lambda i, j: (0, 0)),
            wspec(0), wspec(1), wspec(2), wspec(3), wspec(4), wspec(5),
            pl.BlockSpec((CONV_WIDTH, tn), lambda i, j: (0, j)),
            prev_spec,
        ],
        out_specs=[tile, tile, tile, ko_spec, tile, u_spec],
        out_shape=[
            jax.ShapeDtypeStruct((n, d_conv), BF16),
            jax.ShapeDtypeStruct((n, d_conv), BF16),
            jax.ShapeDtypeStruct((n, d_conv), BF16),
            ko_shape,
            jax.ShapeDtypeStruct((n, d_conv), F32),
            u_shape,
        ],
        scratch_shapes=[pltpu.VMEM((tm, d), BF16), pltpu.VMEM((nj, SUBLANES, tn), F32)],
        compiler_params=_params(2),
        name="in_proj",
    )(x2d, norm_g, *([w_in_bf] * 6), conv_w, prev)


def _diff_lambda(lq1_ref, lk1_ref, lq2_ref, lk2_ref, lam_init):
    s1 = jnp.sum(lq1_ref[...] * lk1_ref[...], axis=1, keepdims=True)
    s2 = jnp.sum(lq2_ref[...] * lk2_ref[...], axis=1, keepdims=True)
    return jnp.exp(s1) - jnp.exp(s2) + lam_init


def _stacked_queries(q, head_dim):
    lane = lax.broadcasted_iota(jnp.int32, q.shape, 1)
    zero = jnp.zeros_like(q)
    return jnp.concatenate([jnp.where(lane < head_dim, q, zero),
                            jnp.where(lane >= head_dim, q, zero)], axis=0)


def _scores(qq, kblk):
    return lax.dot_general(qq, kblk, (((1,), (1,)), ((), ())), preferred_element_type=F32)


def _finish_heads(acc, l, lam, hn, lam_init, tq):
    o = acc[:tq] / l[:tq] - lam * (acc[tq:] / l[tq:])
    return (_rms(o, hn) * (1.0 - lam_init)).astype(BF16)


def _attn_prompt_kernel(lq1_ref, lk1_ref, lq2_ref, lk2_ref, hn_ref, q_ref, k_ref, v_ref, o_ref,
                        qt1_ref, qt2_ref, vt_ref, s0_ref, s1_ref, p0_ref, p1_ref,
                        *, lam_init, head_dim, tq):
    seq_len, hw = q_ref.shape
    lam = _diff_lambda(lq1_ref, lk1_ref, lq2_ref, lk2_ref, lam_init)

    qt = q_ref[...].astype(F32).T
    dim = lax.broadcasted_iota(jnp.int32, qt.shape, 0)
    qt1_ref[...] = jnp.where(dim < head_dim, qt, 0.0).astype(BF16)
    qt2_ref[...] = jnp.where(dim >= head_dim, qt, 0.0).astype(BF16)
    vt_ref[:hw, :] = v_ref[...].T.astype(BF16)
    vt_ref[hw:, :] = jnp.ones((vt_ref.shape[0] - hw, seq_len), BF16)

    key_chunk = lax.broadcasted_iota(jnp.int32, (tq, 2 * tq), 0) // CHUNK
    qry_chunk = (lax.broadcasted_iota(jnp.int32, (tq, 2 * tq), 1) % tq) // CHUNK
    visible = key_chunk <= qry_chunk

    bufs = ((s0_ref, p0_ref), (s1_ref, p1_ref))
    n_blocks = seq_len // tq

    def scores(qi):
        q0, n_keys, s_buf = qi * tq, (qi + 1) * tq, bufs[qi % 2][0]
        qqt = jnp.concatenate([qt1_ref[:, q0:n_keys], qt2_ref[:, q0:n_keys]], axis=1)
        s = jnp.dot(k_ref[q0:n_keys, :], qqt, preferred_element_type=F32)
        s = jnp.where(visible, s, NEG_INF)
        s_buf[q0:n_keys, :] = s
        m = jnp.max(s, axis=0, keepdims=True)
        if qi > 0:
            s = jnp.dot(k_ref[:q0, :], qqt, preferred_element_type=F32)
            s_buf[:q0, :] = s
            m = jnp.maximum(m, jnp.max(s, axis=0, keepdims=True))
        return m

    def finish(qi, m):
        q0, n_keys = qi * tq, (qi + 1) * tq
        s_buf, p_buf = bufs[qi % 2]
        p_buf[:n_keys, :] = jnp.exp2(s_buf[:n_keys, :] - m).astype(BF16)
        ot = jnp.dot(vt_ref[:, :n_keys], p_buf[:n_keys, :], preferred_element_type=F32)
        ot = ot[:hw] / ot[hw:hw + 1]
        o = (ot[:, :tq] - lam * ot[:, tq:]).T
        o_ref[q0:n_keys, :] = (_rms(o, hn_ref[...]) * (1.0 - lam_init)).astype(BF16)

    m = scores(0)
    for qi in range(n_blocks):
        m_next = scores(qi + 1) if qi + 1 < n_blocks else None
        finish(qi, m)
        m = m_next


def _attn_prompt(q, k, v, lams, head_norm, *, batch, seq_len, lam_init, tq):
    n, d_attn = q.shape
    hw = d_attn // N_HEADS
    assert seq_len % tq == 0 and tq % CHUNK == 0
    small = pl.BlockSpec((1, hw // 2), lambda b, h: (0, 0))
    head = pl.BlockSpec((seq_len, hw), lambda b, h: (b, h))
    kern = functools.partial(_attn_prompt_kernel, lam_init=lam_init, head_dim=hw // 2, tq=tq)
    return pl.pallas_call(
        kern,
        grid=(batch, N_HEADS),
        in_specs=[small, small, small, small, pl.BlockSpec((1, hw), lambda b, h: (0, 0)),
                  head, head, head],
        out_specs=head,
        out_shape=jax.ShapeDtypeStruct((n, d_attn), BF16),
        scratch_shapes=[pltpu.VMEM((hw, seq_len), BF16), pltpu.VMEM((hw, seq_len), BF16),
                        pltpu.VMEM((hw + BF16_SUBLANES, seq_len), BF16),
                        pltpu.VMEM((seq_len, 2 * tq), F32), pltpu.VMEM((seq_len, 2 * tq), F32),
                        pltpu.VMEM((seq_len, 2 * tq), BF16), pltpu.VMEM((seq_len, 2 * tq), BF16)],
        compiler_params=_params(2),
        name="attn_prompt",
    )(*lams, head_norm, q, k, v)


def _attn_sample_kernel(lq1_ref, lk1_ref, lq2_ref, lk2_ref, hn_ref, q_ref, kc_ref, vc_ref,
                        kn_ref, vn_ref, o_ref, *, lam_init, head_dim):
    tq = q_ref.shape[0]
    hw = 2 * head_dim
    lam = _diff_lambda(lq1_ref, lk1_ref, lq2_ref, lk2_ref, lam_init)
    for h in range(q_ref.shape[1] // hw):
        cols = slice(h * hw, (h + 1) * hw)
        qq = _stacked_queries(q_ref[:, cols], head_dim)
        s_past = jnp.dot(qq, kc_ref[cols, :].astype(BF16), preferred_element_type=F32)
        s_new = _scores(qq, kn_ref[:, cols])
        m = jnp.maximum(jnp.max(s_past, axis=1, keepdims=True), jnp.max(s_new, axis=1, keepdims=True))
        p_past = jnp.exp2(s_past - m)
        p_new = jnp.exp2(s_new - m)
        l = jnp.sum(p_past, axis=1, keepdims=True) + jnp.sum(p_new, axis=1, keepdims=True)
        acc = (jnp.dot(p_past.astype(BF16), vc_ref[:, cols].astype(BF16), preferred_element_type=F32)
               + jnp.dot(p_new.astype(BF16), vn_ref[:, cols].astype(BF16), preferred_element_type=F32))
        o_ref[:, cols] = _finish_heads(acc, l, lam, hn_ref[...], lam_init, tq)


def _attn_sample(q, k_new, v_new, k_past, v_past, lams, head_norm, *, batch, seq_len, past_len, lam_init):
    n, d_attn = q.shape
    hw = d_attn // N_HEADS
    small = pl.BlockSpec((1, hw // 2), lambda b: (0, 0))
    new_spec = pl.BlockSpec((seq_len, d_attn), lambda b: (b, 0))
    past_spec = pl.BlockSpec((past_len, d_attn), lambda b: (b, 0))
    past_t_spec = pl.BlockSpec((d_attn, past_len), lambda b: (b, 0))
    kern = functools.partial(_attn_sample_kernel, lam_init=lam_init, head_dim=hw // 2)
    return pl.pallas_call(
        kern,
        grid=(batch,),
        in_specs=[small, small, small, small, pl.BlockSpec((1, hw), lambda b: (0, 0)),
                  new_spec, past_t_spec, past_spec, new_spec, new_spec],
        out_specs=new_spec,
        out_shape=jax.ShapeDtypeStruct((n, d_attn), BF16),
        compiler_params=_params(1),
        name="attn_sample",
    )(*lams, head_norm, q, k_past, v_past, k_new, v_new)


R_E1, R_E2, R_W1, R_W2, R_RANK1, R_RANK2 = 0, 1, 2, 3, 4, 5
OUT_PROJ_CHUNKS = 4


def _pack_bf16_pairs(lo_bf, hi_bf):
    lo = lax.bitcast_convert_type(lo_bf.astype(F32), U32)
    hi = lax.bitcast_convert_type(hi_bf.astype(F32), U32)
    return (hi & jnp.uint32(0xFFFF0000)) | (lo >> 16)


def _unpack_bf16_pairs(w):
    lo = lax.bitcast_convert_type(w << 16, F32).astype(BF16)
    hi = lax.bitcast_convert_type(w & jnp.uint32(0xFFFF0000), F32).astype(BF16)
    return lo, hi


def _out_proj_kernel(*refs, n_prompt_tiles):
    ap_ref, op_ref, xp_ref, as_ref, os_ref, xs_ref = refs[:6]
    n_w = 2 * OUT_PROJ_CHUNKS
    w_refs = (refs[6:6 + OUT_PROJ_CHUNKS], refs[6 + OUT_PROJ_CHUNKS:6 + n_w])
    rest = refs[6 + n_w:]
    carry_ref = rest[-1]
    i = pl.program_id(0)

    @pl.when(i == 0)
    def _():
        carry_ref[...] = jnp.zeros_like(carry_ref)

    @pl.when(i < n_prompt_tiles)
    def _():
        _out_proj_rows(ap_ref, op_ref, xp_ref, w_refs, *rest)

    @pl.when(i >= n_prompt_tiles)
    def _():
        _out_proj_rows(as_ref, os_ref, xs_ref, w_refs, *rest)


def _out_proj_rows(a_ref, o_ref, x_ref, w_refs, g_ref, wr_ref, br_ref,
                   h_ref, hnp_ref, rinfo_ref, cnt_ref, carry_ref):
    tm, d = x_ref.shape
    cw = d // OUT_PROJ_CHUNKS
    a, o = a_ref[...], o_ref[...]

    ss = jnp.zeros((tm, 1), F32)
    for c in range(OUT_PROJ_CHUNKS):
        cols = slice(c * cw, (c + 1) * cw)
        h = (x_ref[:, cols] + jnp.dot(a, w_refs[0][c][...], preferred_element_type=F32)
             + jnp.dot(o, w_refs[1][c][...], preferred_element_type=F32))
        h_ref[:, cols] = h
        ss = ss + jnp.sum(h * h, axis=1, keepdims=True)
    inv = lax.rsqrt(ss / d + EPS)

    logits = br_ref[...]
    for c in range(OUT_PROJ_CHUNKS // 2):
        halves = []
        for cc in (c, c + OUT_PROJ_CHUNKS // 2):
            cols = slice(cc * cw, (cc + 1) * cw)
            hb = (h_ref[:, cols] * inv * g_ref[:, cols]).astype(BF16)
            logits = logits + jnp.dot(hb, wr_ref[cols, :], preferred_element_type=F32)
            halves.append(hb)
        hnp_ref[:, c * cw:(c + 1) * cw] = _pack_bf16_pairs(*halves)

    lane = lax.broadcasted_iota(jnp.int32, logits.shape, 1)
    lane_f = lane.astype(F32)
    lg = jnp.where(lane < N_GROUPS, logits, NEG_INF)
    mg = jnp.max(lg, axis=1, keepdims=True)
    grp = jnp.min(jnp.where(lg == mg, lane_f, float(LANES)), axis=1, keepdims=True)
    gate_g = 1.0 / jnp.sum(jnp.exp(lg - mg), axis=1, keepdims=True)
    ex = lane - N_GROUPS
    in_grp = (ex >= 0) & (ex < N_EXPERTS) & ((ex >> 3).astype(F32) == grp)
    le = jnp.where(in_grp, logits, NEG_INF)
    v1 = jnp.max(le, axis=1, keepdims=True)
    i1 = jnp.min(jnp.where(le == v1, lane_f, float(LANES)), axis=1, keepdims=True)
    le2 = jnp.where(lane_f == i1, NEG_INF, le)
    v2 = jnp.max(le2, axis=1, keepdims=True)
    i2 = jnp.min(jnp.where(le2 == v2, lane_f, float(LANES)), axis=1, keepdims=True)
    e21 = jnp.exp(v2 - v1)
    w1 = gate_g / (1.0 + e21)
    w2 = gate_g * e21 / (1.0 + e21)
    e1 = i1 - float(N_GROUPS)
    e2 = i2 - float(N_GROUPS)

    onehot = jnp.where((lane_f == e1) | (lane_f == e2), 1.0, 0.0)
    r = lax.broadcasted_iota(jnp.int32, (tm, tm), 0)
    c = lax.broadcasted_iota(jnp.int32, (tm, tm), 1)
    earlier = jnp.where(c < r, 1.0, 0.0).astype(BF16)
    before = jnp.dot(earlier, onehot.astype(BF16), preferred_element_type=F32) + carry_ref[0:1, :]
    rank1 = _lane_pick(before, lane_f, e1)
    rank2 = _lane_pick(before, lane_f, e2)
    new_carry = carry_ref[0:1, :] + jnp.sum(onehot, axis=0, keepdims=True)
    carry_ref[...] = jnp.broadcast_to(new_carry, carry_ref.shape)
    cnt_ref[...] = jnp.broadcast_to(new_carry, cnt_ref.shape)

    rec = jnp.zeros_like(logits)
    for idx, val in ((R_E1, e1), (R_E2, e2), (R_W1, w1), (R_W2, w2), (R_RANK1, rank1), (R_RANK2, rank2)):
        rec = jnp.where(lane == idx, val, rec)
    rinfo_ref[...] = rec


def _out_proj(a_p, o_p, x_p, a_s, o_s, x_s, w_out_bf, norm_g, w_router, b_router, *, tm):
    n_p, d = x_p.shape
    n_s = x_s.shape[0]
    dc = a_p.shape[1]
    assert n_p % tm == 0 and n_s == tm, "sample rows must fill exactly one tile"
    assert w_out_bf.shape == (2 * dc, d)
    npt = n_p // tm
    n_all = n_p + n_s
    cw = d // OUT_PROJ_CHUNKS
    p_idx = lambda i: (jnp.minimum(i, npt - 1), 0)
    zero = lambda i: (0, 0)

    def wspec(half, c):
        return pl.BlockSpec((dc, cw), lambda i: (half, c))

    kern = functools.partial(_out_proj_kernel, n_prompt_tiles=npt)
    return pl.pallas_call(
        kern,
        grid=(npt + 1,),
        in_specs=[
            pl.BlockSpec((tm, dc), p_idx), pl.BlockSpec((tm, dc), p_idx), pl.BlockSpec((tm, d), p_idx),
            pl.BlockSpec((tm, dc), zero), pl.BlockSpec((tm, dc), zero), pl.BlockSpec((tm, d), zero),
            *[wspec(half, c) for half in range(2) for c in range(OUT_PROJ_CHUNKS)],
            pl.BlockSpec((1, d), zero),
            pl.BlockSpec((d, LANES), zero), pl.BlockSpec((1, LANES), zero),
        ],
        out_specs=[
            pl.BlockSpec((tm, d), lambda i: (i, 0)),
            pl.BlockSpec((tm, d // 2), lambda i: (i, 0)),
            pl.BlockSpec((tm, LANES), lambda i: (i, 0)),
            pl.BlockSpec((SUBLANES, LANES), zero),
        ],
        out_shape=[
            jax.ShapeDtypeStruct((n_all, d), F32),
            jax.ShapeDtypeStruct((n_all, d // 2), U32),
            jax.ShapeDtypeStruct((n_all, LANES), F32),
            jax.ShapeDtypeStruct((SUBLANES, LANES), F32),
        ],
        scratch_shapes=[pltpu.VMEM((SUBLANES, LANES), F32)],
        compiler_params=_params(1),
        name="out_proj",
    )(a_p, o_p, x_p, a_s, o_s, x_s, *([w_out_bf] * (2 * OUT_PROJ_CHUNKS)), norm_g, w_router, b_router)


EXPERT_DMA_GROUPS = 16
EXPERT_SLOTS = 3


def _experts_kernel(tile_expert_ref, n_used_ref, src_hbm, hnp_hbm, wi_f32_ref, wo_f32_ref, ys_ref,
                    src_ref, xs_ref, gu_ref, wi_ref, wo_ref, src_sem, gather_sem):
    w = pl.program_id(0)
    n_used = n_used_ref[0]
    tm = xs_ref.shape[1]
    n_tiles = pl.num_programs(0) - 1
    group = tm // EXPERT_DMA_GROUPS

    def src_copy(tile, slot):
        return pltpu.make_async_copy(src_hbm.at[pl.ds(tile, 1), :], src_ref.at[pl.ds(slot, 1), :],
                                     src_sem.at[slot])

    def gather_row(slot, r):
        return pltpu.make_async_copy(hnp_hbm.at[pl.ds(src_ref[slot, r], 1), :],
                                     xs_ref.at[slot, pl.ds(r, 1), :], gather_sem.at[slot])

    def wait_gathers(slot):
        pltpu.make_async_copy(xs_ref.at[slot], xs_ref.at[slot], gather_sem.at[slot]).wait()

    @pl.when(w == 0)
    def _():
        for t in range(EXPERT_SLOTS):
            src_copy(t, t).start()
        for t in range(EXPERT_SLOTS - 1):
            src_copy(t, t).wait()
            for r in range(tm):
                gather_row(t, r).start(priority=r % 2)

    def compute_tile(cur):
        ahead = (cur + EXPERT_SLOTS - 1) % EXPERT_SLOTS
        wait_gathers(cur)
        src_copy(w + 2, ahead).wait()
        src_copy(w + 3, cur).start()

        def move_rows(g):
            for r in range(g * group, (g + 1) * group):
                gather_row(ahead, r).start(priority=r % 2)

        @pl.when((w == 0) | (tile_expert_ref[w] != tile_expert_ref[jnp.maximum(w - 1, 0)]))
        def _():
            wi_ref[...] = wi_f32_ref[0].astype(BF16)
            wo_ref[...] = wo_f32_ref[0].astype(BF16)

        lo, hi = _unpack_bf16_pairs(xs_ref[cur])
        half = lo.shape[1]
        n_chunks = EXPERT_DMA_GROUPS // 4
        f2 = wi_ref.shape[1]
        cw = f2 // n_chunks
        for c in range(n_chunks):
            cols = slice(c * cw, (c + 1) * cw)
            part = jnp.dot(lo, wi_ref[:half, cols], preferred_element_type=F32)
            move_rows(2 * c)
            gu_ref[:, cols] = part + jnp.dot(hi, wi_ref[half:, cols], preferred_element_type=F32)
            move_rows(2 * c + 1)
        gate, up = gu_ref[:, :f2 // 2], gu_ref[:, f2 // 2:]
        act = (gate * jax.nn.sigmoid(gate) * up).astype(BF16)
        fh = act.shape[1] // 2
        d = wo_ref.shape[1]
        ow = d // n_chunks
        for c in range(n_chunks):
            cols = slice(c * ow, (c + 1) * ow)
            part = jnp.dot(act[:, :fh], wo_ref[:fh, cols], preferred_element_type=F32)
            move_rows(2 * (n_chunks + c))
            ys_ref[:, cols] = part + jnp.dot(act[:, fh:], wo_ref[fh:, cols], preferred_element_type=F32)
            move_rows(2 * (n_chunks + c) + 1)

    def drain(cur):
        wait_gathers(cur)
        wait_gathers((cur + 1) % EXPERT_SLOTS)
        src_copy(w + 2, (cur + EXPERT_SLOTS - 1) % EXPERT_SLOTS).wait()

    for key in range(EXPERT_SLOTS):
        mine = (w % EXPERT_SLOTS) == key
        pl.when(mine & (w < n_used))(functools.partial(compute_tile, key))
        pl.when(mine & (w == n_used))(functools.partial(drain, key))

    @pl.when((w >= n_used) & (w < n_tiles))
    def _():
        ys_ref[...] = jnp.zeros_like(ys_ref)


def _experts(tile_expert, n_used, src_tab, hnp, w_ei, w_eo, *, tm):
    n_e, d, f2 = w_ei.shape
    dw = hnp.shape[1]
    n_tiles = src_tab.shape[0] - EXPERT_SLOTS
    n_steps = n_tiles + 1
    assert tm % EXPERT_DMA_GROUPS == 0 and src_tab.shape[1] == tm

    def expert_of(w, te, nu):
        return te[jnp.minimum(w, nu[0] - 1)]

    grid_spec = pltpu.PrefetchScalarGridSpec(
        num_scalar_prefetch=2,
        grid=(n_steps,),
        in_specs=[
            pl.BlockSpec(memory_space=pl.ANY),
            pl.BlockSpec(memory_space=pl.ANY),
            pl.BlockSpec((1, d, f2), lambda w, te, nu: (expert_of(w, te, nu), 0, 0)),
            pl.BlockSpec((1, f2 // 2, d), lambda w, te, nu: (expert_of(w, te, nu), 0, 0)),
        ],
        out_specs=pl.BlockSpec((tm, d), lambda w, te, nu: (jnp.minimum(w, n_tiles - 1), 0)),
        scratch_shapes=[pltpu.SMEM((EXPERT_SLOTS, tm), jnp.int32), pltpu.VMEM((EXPERT_SLOTS, tm, dw), U32),
                        pltpu.VMEM((tm, f2), F32),
                        pltpu.VMEM((d, f2), BF16), pltpu.VMEM((f2 // 2, d), BF16),
                        pltpu.SemaphoreType.DMA((EXPERT_SLOTS,)), pltpu.SemaphoreType.DMA((EXPERT_SLOTS,))],
    )
    return pl.pallas_call(
        _experts_kernel,
        grid_spec=grid_spec,
        out_shape=jax.ShapeDtypeStruct((n_tiles * tm, d), F32),
        compiler_params=_params(1),
        name="experts",
    )(tile_expert, n_used, src_tab, hnp, w_ei, w_eo)


def _expert_tiles(counts, e12, r12, n_all, tm):
    max_tiles = (2 * n_all) // tm + N_EXPERTS
    n_rows = max_tiles + EXPERT_SLOTS
    tiles_per_expert = (counts + tm - 1) // tm
    tile_end = jnp.cumsum(tiles_per_expert)
    n_used = tile_end[-1]
    row_start = (tile_end - tiles_per_expert) * tm
    t = jnp.minimum(jnp.arange(n_rows, dtype=jnp.int32), n_used - 1)
    tile_expert = jnp.sum(tile_end[None, :] <= t[:, None], axis=1).astype(jnp.int32)

    pos = row_start[e12] + r12
    token = jnp.repeat(jnp.arange(n_all, dtype=jnp.int32), 2)
    src = jnp.zeros((n_rows * tm,), jnp.int32).at[pos.reshape(-1)].set(
        token, unique_indices=True, mode="promise_in_bounds").reshape(n_rows, tm)
    pos_tiles = pos.reshape(n_all // tm, tm, 2).transpose(0, 2, 1).reshape(n_all // tm, 2 * tm)
    return tile_expert, n_used.reshape(1).astype(jnp.int32), src, pos_tiles.astype(jnp.int32)


def _combine_kernel(pos_hbm, h_ref, rinfo_ref, g_ref, ys_hbm, yp_ref, ysm_ref,
                    idx_ref, buf_ref, idx_sem, row_sem, *, n_prompt_tiles):
    i = pl.program_id(0)
    nt = pl.num_programs(0)
    tm = h_ref.shape[0]

    def idx_copy(tile, slot):
        return pltpu.make_async_copy(pos_hbm.at[pl.ds(tile, 1), :], idx_ref.at[pl.ds(slot, 1), :],
                                     idx_sem.at[slot])

    def start_rows(slot):
        def body(r, _):
            for k in range(2):
                pltpu.make_async_copy(ys_hbm.at[pl.ds(idx_ref[slot, k * tm + r], 1), :],
                                      buf_ref.at[slot, k, pl.ds(r, 1), :], row_sem.at[slot]).start(priority=k)
            return 0
        lax.fori_loop(0, tm, body, 0, unroll=8)

    def wait_rows(slot):
        pltpu.make_async_copy(buf_ref.at[slot], buf_ref.at[slot], row_sem.at[slot]).wait()

    @pl.when(i == 0)
    def _():
        idx_copy(0, 0).start()
        idx_copy(0, 0).wait()
        start_rows(0)

        @pl.when(nt > 1)
        def _():
            idx_copy(1, 1).start()

    def tile(cur):
        nxt = 1 - cur

        @pl.when(i + 1 < nt)
        def _():
            idx_copy(i + 1, nxt).wait()
            start_rows(nxt)

        @pl.when(i + 2 < nt)
        def _():
            idx_copy(i + 2, cur).start()

        wait_rows(cur)
        rinfo = rinfo_ref[...]
        lane = lax.broadcasted_iota(jnp.int32, rinfo.shape, 1)
        w1 = _lane_pick(rinfo, lane, R_W1)
        w2 = _lane_pick(rinfo, lane, R_W2)
        hp = h_ref[...] + (w1 * buf_ref[cur, 0] + w2 * buf_ref[cur, 1])
        y = _rms(hp, g_ref[...])

        @pl.when(i < n_prompt_tiles)
        def _():
            yp_ref[...] = y

        @pl.when(i >= n_prompt_tiles)
        def _():
            ysm_ref[...] = y

    for parity in (0, 1):
        pl.when((i % 2) == parity)(functools.partial(tile, parity))


def _combine(pos_tiles, h, rinfo, norm_g, ys, *, n_prompt, tm):
    n_all, d = h.shape
    nt = n_all // tm
    npt = n_prompt // tm
    assert n_all - n_prompt == tm and pos_tiles.shape == (nt, 2 * tm)
    kern = functools.partial(_combine_kernel, n_prompt_tiles=npt)
    return pl.pallas_call(
        kern,
        grid=(nt,),
        in_specs=[
            pl.BlockSpec(memory_space=pl.ANY),
            pl.BlockSpec((tm, d), lambda i: (i, 0)),
            pl.BlockSpec((tm, LANES), lambda i: (i, 0)),
            pl.BlockSpec((1, d), lambda i: (0, 0)),
            pl.BlockSpec(memory_space=pl.ANY),
        ],
        out_specs=[
            pl.BlockSpec((tm, d), lambda i: (jnp.minimum(i, npt - 1), 0)),
            pl.BlockSpec((tm, d), lambda i: (0, 0)),
        ],
        out_shape=[jax.ShapeDtypeStruct((n_prompt, d), F32), jax.ShapeDtypeStruct((tm, d), F32)],
        scratch_shapes=[pltpu.SMEM((2, 2 * tm), jnp.int32), pltpu.VMEM((2, 2, tm, d), F32),
                        pltpu.SemaphoreType.DMA((2,)), pltpu.SemaphoreType.DMA((2,))],
        compiler_params=_params(1),
        name="combine",
    )(pos_tiles, h, rinfo, norm_g, ys)


ROW_TILE = 256
PROJ_ROWS = 512
PROJ_COLS = 256
ATTN_Q_ROWS = 256


def kernel(x_prompt, x_sample, cache_k, cache_v, state_conv, norm_mix, w_in, conv_w, lambda_q1,
           lambda_k1, lambda_q2, lambda_k2, head_norm, w_out, norm_ffn, w_router_group,
           b_router_group, w_router_expert, b_router_expert, w_expert_in, w_expert_out, norm_final):
    depth = w_in.shape[0]
    b, t, d = x_prompt.shape
    db, dt, _ = x_sample.shape
    past = cache_k.shape[2]
    d_conv = conv_w.shape[2]
    d_attn = d - d_conv
    n_p, n_s = b * t, db * dt
    n_all = n_p + n_s
    tm = ROW_TILE

    hp = x_prompt.reshape(n_p, d)
    hs = x_sample.reshape(n_s, d)
    outs = {name: [] for name in ("kp", "vp", "cp", "ks", "vs", "cs")}
    y_p = y_s = None
    for l in range(depth):
        lam_init = 0.8 - 0.6 * math.exp(-0.3 * l)
        w_in_bf = w_in[l].astype(BF16)
        w_out_bf = w_out[l].astype(BF16)
        g_mix = norm_mix[l].reshape(1, d)
        lams = tuple(p[l].reshape(1, -1) for p in (lambda_q1, lambda_k1, lambda_q2, lambda_k2))
        hnorm = head_norm[l].reshape(1, -1)
        pad = LANES - N_GROUPS - N_EXPERTS
        w_router = jnp.concatenate([w_router_group[l], w_router_expert[l], jnp.zeros((d, pad), F32)],
                                   axis=1).astype(BF16)
        b_router = jnp.concatenate([b_router_group[l], b_router_expert[l], jnp.zeros((pad,), F32)]).reshape(1, LANES)

        zero_prev = jnp.zeros((b, CONV_WIDTH - 1, d_conv), F32)
        a_p, q_p, kb_p, kt_p, v_p, tail_p = _in_proj(hp, g_mix, w_in_bf, conv_w[l], zero_prev,
                                                     seq_len=t, tm=min(PROJ_ROWS, t), tn=PROJ_COLS)
        o_p = _attn_prompt(q_p, kb_p, v_p, lams, hnorm, batch=b, seq_len=t, lam_init=lam_init,
                           tq=min(ATTN_Q_ROWS, t))
        prev_rows = jnp.stack([jnp.repeat(state_conv[l][:, 0], dt, axis=0),
                               jnp.repeat(state_conv[l][:, 1], dt, axis=0)])
        a_s, q_s, kb_s, k_s, v_s, u_s = _in_proj(hs, g_mix, w_in_bf, conv_w[l], prev_rows,
                                                 seq_len=dt, tm=n_s, tn=PROJ_COLS)
        kt_past = cache_k[l].transpose(0, 2, 3, 4, 1).reshape(db * d_attn, past)
        o_s = _attn_sample(q_s, kb_s, v_s, kt_past, cache_v[l].reshape(db * past, d_attn), lams, hnorm,
                           batch=db, seq_len=dt, past_len=past, lam_init=lam_init)

        h, hnp, rinfo, cnt = _out_proj(a_p, o_p, hp, a_s, o_s, hs, w_out_bf,
                                       norm_ffn[l].reshape(1, d), w_router, b_router, tm=tm)
        counts = cnt[0, :N_EXPERTS].astype(jnp.int32)
        e12 = rinfo[:, R_E1:R_E2 + 1].astype(jnp.int32)
        r12 = rinfo[:, R_RANK1:R_RANK2 + 1].astype(jnp.int32)
        tile_expert, n_used, src_tab, pos_tiles = _expert_tiles(counts, e12, r12, n_all, tm)
        ys = _experts(tile_expert, n_used, src_tab, hnp, w_expert_in[l], w_expert_out[l], tm=tm)
        last = l == depth - 1
        g_out = norm_final.reshape(1, d) if last else jnp.ones((1, d), F32)
        y_p, y_s = _combine(pos_tiles, h, rinfo, g_out, ys, n_prompt=n_p, tm=tm)
        assert last, "multi-layer stacks need an un-normalised combine output"

        outs["kp"].append(kt_p.reshape(b, N_HEADS, 2, d_attn // (2 * N_HEADS), t).transpose(0, 4, 1, 2, 3))
        outs["vp"].append(v_p.reshape(b, t, N_HEADS, d_attn // N_HEADS))
        tails = tail_p.reshape(b, t // min(PROJ_ROWS, t), SUBLANES, d_conv)
        outs["cp"].append(tails[:, -1, SUBLANES - (CONV_WIDTH - 1):, :])
        outs["ks"].append(k_s.reshape(db, dt, N_HEADS, 2, d_attn // (2 * N_HEADS)))
        outs["vs"].append(v_s.reshape(db, dt, N_HEADS, d_attn // N_HEADS))
        outs["cs"].append(u_s.reshape(db, dt, d_conv)[:, dt - (CONV_WIDTH - 1):, :])

    return (y_p.reshape(b, t, d), y_s.reshape(db, dt, d),
            jnp.stack(outs["kp"]), jnp.stack(outs["vp"]), jnp.stack(outs["cp"]),
            jnp.stack(outs["ks"]), jnp.stack(outs["vs"]), jnp.stack(outs["cs"]))
```

```python
import functools
import math

import jax
import jax.numpy as jnp
from jax import lax
from jax.experimental import pallas as pl
from jax.experimental.pallas import tpu as pltpu

EPS = 1e-6
N_HEADS = 8
CHUNK = 64
CONV_WIDTH = 3
N_GROUPS = 4
EXPERTS_PER_GROUP = 8
N_EXPERTS = N_GROUPS * EXPERTS_PER_GROUP
LANES = 128
SUBLANES = 8
BF16_SUBLANES = 16
VMEM_LIMIT = 56 * 1024 * 1024

F32 = jnp.float32
BF16 = jnp.bfloat16
U32 = jnp.uint32
NEG_INF = float("-inf")


def _params(n_axes, vmem=VMEM_LIMIT, **kw):
    return pltpu.CompilerParams(dimension_semantics=("arbitrary",) * n_axes,
                                vmem_limit_bytes=vmem, **kw)


def _rms(x, g):
    return x * lax.rsqrt(jnp.mean(x * x, axis=-1, keepdims=True) + EPS) * g


def _lane_pick(x, lane, idx):
    return jnp.sum(jnp.where(lane == idx, x, 0.0), axis=1, keepdims=True)


def _in_proj_kernel(x_ref, g_ref, wgb_ref, wgc_ref, wx_ref, wq_ref, wk_ref, wv_ref, cw_ref, prev_ref,
                    a_ref, q_ref, kb_ref, ko_ref, v_ref, u_ref, xn_ref, carry_ref,
                    *, seq_len, tiles_per_seq, rows_span_sequences, q_scale):
    i = pl.program_id(0)
    j = pl.program_id(1)
    tm = x_ref.shape[0]

    @pl.when(j == 0)
    def _():
        xn_ref[...] = _rms(x_ref[...], g_ref[...]).astype(BF16)

    xn = xn_ref[...]

    gate_b = jnp.dot(xn, wgb_ref[...], preferred_element_type=F32)
    u = (jnp.dot(xn, wgc_ref[...], preferred_element_type=F32)
         * jnp.dot(xn, wx_ref[...], preferred_element_type=F32))
    row = lax.broadcasted_iota(jnp.int32, u.shape, 0)
    if rows_span_sequences:
        rmod = row % seq_len
        pm2, pm1 = prev_ref[0], prev_ref[1]
        first, second = rmod == 0, rmod == 1
        u_ref[...] = u
    else:
        seq_start = (i % tiles_per_seq) == 0
        tail = carry_ref[j]
        pm2 = jnp.where(seq_start, prev_ref[0, 0:1, :], tail[SUBLANES - 2:SUBLANES - 1, :])
        pm1 = jnp.where(seq_start, prev_ref[0, 1:2, :], tail[SUBLANES - 1:SUBLANES, :])
        first, second = row == 0, row == 1
        new_tail = u[tm - SUBLANES:, :]
        carry_ref[j] = new_tail
        u_ref[0] = new_tail
    u1 = jnp.where(first, pm1, pltpu.roll(u, 1, 0))
    u2 = jnp.where(first, pm2, jnp.where(second, pm1, pltpu.roll(u, 2, 0)))
    cw = cw_ref[...]
    y = cw[0:1, :] * u2 + cw[1:2, :] * u1 + cw[2:3, :] * u
    a_ref[...] = (gate_b * y).astype(BF16)

    q_ref[...] = (jnp.dot(xn, wq_ref[...], preferred_element_type=F32) * q_scale).astype(BF16)
    k = jnp.dot(xn, wk_ref[...], preferred_element_type=F32)
    kb_ref[...] = k.astype(BF16)
    if rows_span_sequences:
        ko_ref[...] = k
    else:
        ko_ref[0] = k.T
    v_ref[...] = jnp.dot(xn, wv_ref[...], preferred_element_type=F32)


def _in_proj(x2d, norm_g, w_in_bf, conv_w, prev, *, seq_len, tm, tn):
    n, d = x2d.shape
    d_conv = conv_w.shape[1]
    assert w_in_bf.shape == (d, 6 * d_conv), "conv and attention widths must match"
    assert n % tm == 0 and d_conv % tn == 0
    nj = d_conv // tn
    n_i = n // tm
    rows_span_sequences = tm > seq_len
    if rows_span_sequences:
        assert tm % seq_len == 0 and prev.shape == (2, n, d_conv)
        tiles_per_seq = 1
        prev_spec = pl.BlockSpec((2, tm, tn), lambda i, j: (0, i, j))
        u_shape = jax.ShapeDtypeStruct((n, d_conv), F32)
        u_spec = pl.BlockSpec((tm, tn), lambda i, j: (i, j))
        ko_shape = jax.ShapeDtypeStruct((n, d_conv), F32)
        ko_spec = pl.BlockSpec((tm, tn), lambda i, j: (i, j))
    else:
        assert seq_len % tm == 0 and prev.shape == (n // seq_len, CONV_WIDTH - 1, d_conv)
        tiles_per_seq = seq_len // tm
        prev_spec = pl.BlockSpec((1, CONV_WIDTH - 1, tn), lambda i, j: (i // tiles_per_seq, 0, j))
        u_shape = jax.ShapeDtypeStruct((n_i, SUBLANES, d_conv), F32)
        u_spec = pl.BlockSpec((1, SUBLANES, tn), lambda i, j: (i, 0, j))
        ko_shape = jax.ShapeDtypeStruct((n // seq_len, d_conv, seq_len), F32)
        ko_spec = pl.BlockSpec((1, tn, tm), lambda i, j: (i // tiles_per_seq, j, i % tiles_per_seq))

    def wspec(k):
        return pl.BlockSpec((d, tn), lambda i, j: (0, k * nj + j))

    tile = pl.BlockSpec((tm, tn), lambda i, j: (i, j))
    head_dim = d_conv // (2 * N_HEADS)
    kern = functools.partial(_in_proj_kernel, seq_len=seq_len, tiles_per_seq=tiles_per_seq,
                             rows_span_sequences=rows_span_sequences,
                             q_scale=head_dim ** -0.5 * math.log2(math.e))
    return pl.pallas_call(
        kern,
        grid=(n_i, nj),
        in_specs=[
            pl.BlockSpec((tm, d), lambda i, j: (i, 0)),
            pl.BlockSpec((1, d), lambda i, j: (0, 0)),
            wspec(0), wspec(1), wspec(2), wspec(3), wspec(4), wspec(5),
            pl.BlockSpec((CONV_WIDTH, tn), lambda i, j: (0, j)),
            prev_spec,
        ],
        out_specs=[tile, tile, tile, ko_spec, tile, u_spec],
        out_shape=[
            jax.ShapeDtypeStruct((n, d_conv), BF16),
            jax.ShapeDtypeStruct((n, d_conv), BF16),
            jax.ShapeDtypeStruct((n, d_conv), BF16),
            ko_shape,
            jax.ShapeDtypeStruct((n, d_conv), F32),
            u_shape,
        ],
        scratch_shapes=[pltpu.VMEM((tm, d), BF16), pltpu.VMEM((nj, SUBLANES, tn), F32)],
        compiler_params=_params(2),
        name="in_proj",
    )(x2d, norm_g, *([w_in_bf] * 6), conv_w, prev)


def _diff_lambda(lq1_ref, lk1_ref, lq2_ref, lk2_ref, lam_init):
    s1 = jnp.sum(lq1_ref[...] * lk1_ref[...], axis=1, keepdims=True)
    s2 = jnp.sum(lq2_ref[...] * lk2_ref[...], axis=1, keepdims=True)
    return jnp.exp(s1) - jnp.exp(s2) + lam_init


def _stacked_queries(q, head_dim):
    lane = lax.broadcasted_iota(jnp.int32, q.shape, 1)
    zero = jnp.zeros_like(q)
    return jnp.concatenate([jnp.where(lane < head_dim, q, zero),
                            jnp.where(lane >= head_dim, q, zero)], axis=0)


def _scores(qq, kblk):
    return lax.dot_general(qq, kblk, (((1,), (1,)), ((), ())), preferred_element_type=F32)


def _finish_heads(acc, l, lam, hn, lam_init, tq):
    o = acc[:tq] / l[:tq] - lam * (acc[tq:] / l[tq:])
    return (_rms(o, hn) * (1.0 - lam_init)).astype(BF16)


def _attn_prompt_kernel(lq1_ref, lk1_ref, lq2_ref, lk2_ref, hn_ref, q_ref, k_ref, v_ref, o_ref,
                        qt1_ref, qt2_ref, vt_ref, s0_ref, s1_ref, p0_ref, p1_ref,
                        *, lam_init, head_dim, tq):
    seq_len, hw = q_ref.shape
    lam = _diff_lambda(lq1_ref, lk1_ref, lq2_ref, lk2_ref, lam_init)

    qt = q_ref[...].astype(F32).T
    dim = lax.broadcasted_iota(jnp.int32, qt.shape, 0)
    qt1_ref[...] = jnp.where(dim < head_dim, qt, 0.0).astype(BF16)
    qt2_ref[...] = jnp.where(dim >= head_dim, qt, 0.0).astype(BF16)
    vt_ref[:hw, :] = v_ref[...].T.astype(BF16)
    vt_ref[hw:, :] = jnp.ones((vt_ref.shape[0] - hw, seq_len), BF16)

    key_chunk = lax.broadcasted_iota(jnp.int32, (tq, 2 * tq), 0) // CHUNK
    qry_chunk = (lax.broadcasted_iota(jnp.int32, (tq, 2 * tq), 1) % tq) // CHUNK
    visible = key_chunk <= qry_chunk

    bufs = ((s0_ref, p0_ref), (s1_ref, p1_ref))
    n_blocks = seq_len // tq

    def scores(qi):
        q0, n_keys, s_buf = qi * tq, (qi + 1) * tq, bufs[qi % 2][0]
        qqt = jnp.concatenate([qt1_ref[:, q0:n_keys], qt2_ref[:, q0:n_keys]], axis=1)
        s = jnp.dot(k_ref[q0:n_keys, :], qqt, preferred_element_type=F32)
        s = jnp.where(visible, s, NEG_INF)
        s_buf[q0:n_keys, :] = s
        m = jnp.max(s, axis=0, keepdims=True)
        if qi > 0:
            s = jnp.dot(k_ref[:q0, :], qqt, preferred_element_type=F32)
            s_buf[:q0, :] = s
            m = jnp.maximum(m, jnp.max(s, axis=0, keepdims=True))
        return m

    def finish(qi, m):
        q0, n_keys = qi * tq, (qi + 1) * tq
        s_buf, p_buf = bufs[qi % 2]
        p_buf[:n_keys, :] = jnp.exp2(s_buf[:n_keys, :] - m).astype(BF16)
        ot = jnp.dot(vt_ref[:, :n_keys], p_buf[:n_keys, :], preferred_element_type=F32)
        ot = ot[:hw] / ot[hw:hw + 1]
        o = (ot[:, :tq] - lam * ot[:, tq:]).T
        o_ref[q0:n_keys, :] = (_rms(o, hn_ref[...]) * (1.0 - lam_init)).astype(BF16)

    m = scores(0)
    for qi in range(n_blocks):
        m_next = scores(qi + 1) if qi + 1 < n_blocks else None
        finish(qi, m)
        m = m_next


def _attn_prompt(q, k, v, lams, head_norm, *, batch, seq_len, lam_init, tq):
    n, d_attn = q.shape
    hw = d_attn // N_HEADS
    assert seq_len % tq == 0 and tq % CHUNK == 0
    small = pl.BlockSpec((1, hw // 2), lambda b, h: (0, 0))
    head = pl.BlockSpec((seq_len, hw), lambda b, h: (b, h))
    kern = functools.partial(_attn_prompt_kernel, lam_init=lam_init, head_dim=hw // 2, tq=tq)
    return pl.pallas_call(
        kern,
        grid=(batch, N_HEADS),
        in_specs=[small, small, small, small, pl.BlockSpec((1, hw), lambda b, h: (0, 0)),
                  head, head, head],
        out_specs=head,
        out_shape=jax.ShapeDtypeStruct((n, d_attn), BF16),
        scratch_shapes=[pltpu.VMEM((hw, seq_len), BF16), pltpu.VMEM((hw, seq_len), BF16),
                        pltpu.VMEM((hw + BF16_SUBLANES, seq_len), BF16),
                        pltpu.VMEM((seq_len, 2 * tq), F32), pltpu.VMEM((seq_len, 2 * tq), F32),
                        pltpu.VMEM((seq_len, 2 * tq), BF16), pltpu.VMEM((seq_len, 2 * tq), BF16)],
        compiler_params=_params(2),
        name="attn_prompt",
    )(*lams, head_norm, q, k, v)


def _attn_sample_kernel(lq1_ref, lk1_ref, lq2_ref, lk2_ref, hn_ref, q_ref, kc_ref, vc_ref,
                        kn_ref, vn_ref, o_ref, *, lam_init, head_dim):
    tq = q_ref.shape[0]
    hw = 2 * head_dim
    lam = _diff_lambda(lq1_ref, lk1_ref, lq2_ref, lk2_ref, lam_init)
    for h in range(q_ref.shape[1] // hw):
        cols = slice(h * hw, (h + 1) * hw)
        qq = _stacked_queries(q_ref[:, cols], head_dim)
        s_past = jnp.dot(qq, kc_ref[cols, :].astype(BF16), preferred_element_type=F32)
        s_new = _scores(qq, kn_ref[:, cols])
        m = jnp.maximum(jnp.max(s_past, axis=1, keepdims=True), jnp.max(s_new, axis=1, keepdims=True))
        p_past = jnp.exp2(s_past - m)
        p_new = jnp.exp2(s_new - m)
        l = jnp.sum(p_past, axis=1, keepdims=True) + jnp.sum(p_new, axis=1, keepdims=True)
        acc = (jnp.dot(p_past.astype(BF16), vc_ref[:, cols].astype(BF16), preferred_element_type=F32)
               + jnp.dot(p_new.astype(BF16), vn_ref[:, cols].astype(BF16), preferred_element_type=F32))
        o_ref[:, cols] = _finish_heads(acc, l, lam, hn_ref[...], lam_init, tq)


def _attn_sample(q, k_new, v_new, k_past, v_past, lams, head_norm, *, batch, seq_len, past_len, lam_init):
    n, d_attn = q.shape
    hw = d_attn // N_HEADS
    small = pl.BlockSpec((1, hw // 2), lambda b: (0, 0))
    new_spec = pl.BlockSpec((seq_len, d_attn), lambda b: (b, 0))
    past_spec = pl.BlockSpec((past_len, d_attn), lambda b: (b, 0))
    past_t_spec = pl.BlockSpec((d_attn, past_len), lambda b: (b, 0))
    kern = functools.partial(_attn_sample_kernel, lam_init=lam_init, head_dim=hw // 2)
    return pl.pallas_call(
        kern,
        grid=(batch,),
        in_specs=[small, small, small, small, pl.BlockSpec((1, hw), lambda b: (0, 0)),
                  new_spec, past_t_spec, past_spec, new_spec, new_spec],
        out_specs=new_spec,
        out_shape=jax.ShapeDtypeStruct((n, d_attn), BF16),
        compiler_params=_params(1),
        name="attn_sample",
    )(*lams, head_norm, q, k_past, v_past, k_new, v_new)


R_E1, R_E2, R_W1, R_W2, R_RANK1, R_RANK2 = 0, 1, 2, 3, 4, 5
OUT_PROJ_CHUNKS = 4


def _pack_bf16_pairs(lo_bf, hi_bf):
    lo = lax.bitcast_convert_type(lo_bf.astype(F32), U32)
    hi = lax.bitcast_convert_type(hi_bf.astype(F32), U32)
    return (hi & jnp.uint32(0xFFFF0000)) | (lo >> 16)


def _unpack_bf16_pairs(w):
    lo = lax.bitcast_convert_type(w << 16, F32).astype(BF16)
    hi = lax.bitcast_convert_type(w & jnp.uint32(0xFFFF0000), F32).astype(BF16)
    return lo, hi


def _out_proj_kernel(*refs, n_prompt_tiles):
    ap_ref, op_ref, xp_ref, as_ref, os_ref, xs_ref = refs[:6]
    n_w = 2 * OUT_PROJ_CHUNKS
    w_refs = (refs[6:6 + OUT_PROJ_CHUNKS], refs[6 + OUT_PROJ_CHUNKS:6 + n_w])
    rest = refs[6 + n_w:]
    carry_ref = rest[-1]
    i = pl.program_id(0)

    @pl.when(i == 0)
    def _():
        carry_ref[...] = jnp.zeros_like(carry_ref)

    @pl.when(i < n_prompt_tiles)
    def _():
        _out_proj_rows(ap_ref, op_ref, xp_ref, w_refs, *rest)

    @pl.when(i >= n_prompt_tiles)
    def _():
        _out_proj_rows(as_ref, os_ref, xs_ref, w_refs, *rest)


def _out_proj_rows(a_ref, o_ref, x_ref, w_refs, g_ref, wr_ref, br_ref,
                   h_ref, hnp_ref, rinfo_ref, rinfo_t_ref, cnt_ref, carry_ref):
    tm, d = x_ref.shape
    cw = d // OUT_PROJ_CHUNKS
    a, o = a_ref[...], o_ref[...]

    ss = jnp.zeros((tm, 1), F32)
    for c in range(OUT_PROJ_CHUNKS):
        cols = slice(c * cw, (c + 1) * cw)
        h = (x_ref[:, cols] + jnp.dot(a, w_refs[0][c][...], preferred_element_type=F32)
             + jnp.dot(o, w_refs[1][c][...], preferred_element_type=F32))
        h_ref[:, cols] = h
        ss = ss + jnp.sum(h * h, axis=1, keepdims=True)
    inv = lax.rsqrt(ss / d + EPS)

    logits = br_ref[...]
    for c in range(OUT_PROJ_CHUNKS // 2):
        halves = []
        for cc in (c, c + OUT_PROJ_CHUNKS // 2):
            cols = slice(cc * cw, (cc + 1) * cw)
            hb = (h_ref[:, cols] * inv * g_ref[:, cols]).astype(BF16)
            logits = logits + jnp.dot(hb, wr_ref[cols, :], preferred_element_type=F32)
            halves.append(hb)
        hnp_ref[:, c * cw:(c + 1) * cw] = _pack_bf16_pairs(*halves)

    lane = lax.broadcasted_iota(jnp.int32, logits.shape, 1)
    lane_f = lane.astype(F32)
    lg = jnp.where(lane < N_GROUPS, logits, NEG_INF)
    mg = jnp.max(lg, axis=1, keepdims=True)
    grp = jnp.min(jnp.where(lg == mg, lane_f, float(LANES)), axis=1, keepdims=True)
    gate_g = 1.0 / jnp.sum(jnp.exp(lg - mg), axis=1, keepdims=True)
    ex = lane - N_GROUPS
    in_grp = (ex >= 0) & (ex < N_EXPERTS) & ((ex >> 3).astype(F32) == grp)
    le = jnp.where(in_grp, logits, NEG_INF)
    v1 = jnp.max(le, axis=1, keepdims=True)
    i1 = jnp.min(jnp.where(le == v1, lane_f, float(LANES)), axis=1, keepdims=True)
    le2 = jnp.where(lane_f == i1, NEG_INF, le)
    v2 = jnp.max(le2, axis=1, keepdims=True)
    i2 = jnp.min(jnp.where(le2 == v2, lane_f, float(LANES)), axis=1, keepdims=True)
    e21 = jnp.exp(v2 - v1)
    w1 = gate_g / (1.0 + e21)
    w2 = gate_g * e21 / (1.0 + e21)
    e1 = i1 - float(N_GROUPS)
    e2 = i2 - float(N_GROUPS)

    onehot = jnp.where((lane_f == e1) | (lane_f == e2), 1.0, 0.0)
    r = lax.broadcasted_iota(jnp.int32, (tm, tm), 0)
    c = lax.broadcasted_iota(jnp.int32, (tm, tm), 1)
    earlier = jnp.where(c < r, 1.0, 0.0).astype(BF16)
    before = jnp.dot(earlier, onehot.astype(BF16), preferred_element_type=F32) + carry_ref[0:1, :]
    rank1 = _lane_pick(before, lane_f, e1)
    rank2 = _lane_pick(before, lane_f, e2)
    new_carry = carry_ref[0:1, :] + jnp.sum(onehot, axis=0, keepdims=True)
    carry_ref[...] = jnp.broadcast_to(new_carry, carry_ref.shape)
    cnt_ref[...] = jnp.broadcast_to(new_carry, cnt_ref.shape)

    rec = jnp.zeros_like(logits)
    for idx, val in ((R_E1, e1), (R_E2, e2), (R_W1, w1), (R_W2, w2), (R_RANK1, rank1), (R_RANK2, rank2)):
        rec = jnp.where(lane == idx, val, rec)
    rinfo_ref[...] = rec
    rinfo_t_ref[...] = rec.T[:SUBLANES, :]


def _out_proj(a_p, o_p, x_p, a_s, o_s, x_s, w_out_bf, norm_g, w_router, b_router, *, tm):
    n_p, d = x_p.shape
    n_s = x_s.shape[0]
    dc = a_p.shape[1]
    assert n_p % tm == 0 and n_s == tm, "sample rows must fill exactly one tile"
    assert w_out_bf.shape == (2 * dc, d)
    npt = n_p // tm
    n_all = n_p + n_s
    cw = d // OUT_PROJ_CHUNKS
    p_idx = lambda i: (jnp.minimum(i, npt - 1), 0)
    zero = lambda i: (0, 0)

    def wspec(half, c):
        return pl.BlockSpec((dc, cw), lambda i: (half, c))

    kern = functools.partial(_out_proj_kernel, n_prompt_tiles=npt)
    return pl.pallas_call(
        kern,
        grid=(npt + 1,),
        in_specs=[
            pl.BlockSpec((tm, dc), p_idx), pl.BlockSpec((tm, dc), p_idx), pl.BlockSpec((tm, d), p_idx),
            pl.BlockSpec((tm, dc), zero), pl.BlockSpec((tm, dc), zero), pl.BlockSpec((tm, d), zero),
            *[wspec(half, c) for half in range(2) for c in range(OUT_PROJ_CHUNKS)],
            pl.BlockSpec((1, d), zero),
            pl.BlockSpec((d, LANES), zero), pl.BlockSpec((1, LANES), zero),
        ],
        out_specs=[
            pl.BlockSpec((tm, d), lambda i: (i, 0)),
            pl.BlockSpec((tm, d // 2), lambda i: (i, 0)),
            pl.BlockSpec((tm, LANES), lambda i: (i, 0)),
            pl.BlockSpec((SUBLANES, tm), lambda i: (0, i)),
            pl.BlockSpec((SUBLANES, LANES), zero),
        ],
        out_shape=[
            jax.ShapeDtypeStruct((n_all, d), F32),
            jax.ShapeDtypeStruct((n_all, d // 2), U32),
            jax.ShapeDtypeStruct((n_all, LANES), F32),
            jax.ShapeDtypeStruct((SUBLANES, n_all), F32),
            jax.ShapeDtypeStruct((SUBLANES, LANES), F32),
        ],
        scratch_shapes=[pltpu.VMEM((SUBLANES, LANES), F32)],
        compiler_params=_params(1),
        name="out_proj",
    )(a_p, o_p, x_p, a_s, o_s, x_s, *([w_out_bf] * (2 * OUT_PROJ_CHUNKS)), norm_g, w_router, b_router)


EXPERT_DMA_GROUPS = 16
EXPERT_SLOTS = 3


def _experts_kernel(tile_expert_ref, n_used_ref, src_hbm, hnp_hbm, wi_f32_ref, wo_f32_ref, ys_ref,
                    src_ref, xs_ref, gu_ref, wi_ref, wo_ref, src_sem, gather_sem):
    w = pl.program_id(0)
    n_used = n_used_ref[0]
    tm = xs_ref.shape[1]
    n_tiles = pl.num_programs(0) - 1
    group = tm // EXPERT_DMA_GROUPS

    def src_copy(tile, slot):
        return pltpu.make_async_copy(src_hbm.at[pl.ds(tile, 1), :], src_ref.at[pl.ds(slot, 1), :],
                                     src_sem.at[slot])

    def gather_row(slot, r):
        return pltpu.make_async_copy(hnp_hbm.at[pl.ds(src_ref[slot, r], 1), :],
                                     xs_ref.at[slot, pl.ds(r, 1), :], gather_sem.at[slot])

    def wait_gathers(slot):
        pltpu.make_async_copy(xs_ref.at[slot], xs_ref.at[slot], gather_sem.at[slot]).wait()

    @pl.when(w == 0)
    def _():
        for t in range(EXPERT_SLOTS):
            src_copy(t, t).start()
        for t in range(EXPERT_SLOTS - 1):
            src_copy(t, t).wait()
            for r in range(tm):
                gather_row(t, r).start(priority=r % 2)

    def compute_tile(cur):
        ahead = (cur + EXPERT_SLOTS - 1) % EXPERT_SLOTS
        wait_gathers(cur)
        src_copy(w + 2, ahead).wait()
        src_copy(w + 3, cur).start()

        def move_rows(g):
            for r in range(g * group, (g + 1) * group):
                gather_row(ahead, r).start(priority=r % 2)

        @pl.when((w == 0) | (tile_expert_ref[w] != tile_expert_ref[jnp.maximum(w - 1, 0)]))
        def _():
            wi_ref[...] = wi_f32_ref[0].astype(BF16)
            wo_ref[...] = wo_f32_ref[0].astype(BF16)

        lo, hi = _unpack_bf16_pairs(xs_ref[cur])
        half = lo.shape[1]
        n_chunks = EXPERT_DMA_GROUPS // 4
        f2 = wi_ref.shape[1]
        cw = f2 // n_chunks
        for c in range(n_chunks):
            cols = slice(c * cw, (c + 1) * cw)
            part = jnp.dot(lo, wi_ref[:half, cols], preferred_element_type=F32)
            move_rows(2 * c)
            gu_ref[:, cols] = part + jnp.dot(hi, wi_ref[half:, cols], preferred_element_type=F32)
            move_rows(2 * c + 1)
        gate, up = gu_ref[:, :f2 // 2], gu_ref[:, f2 // 2:]
        act = (gate * jax.nn.sigmoid(gate) * up).astype(BF16)
        fh = act.shape[1] // 2
        d = wo_ref.shape[1]
        ow = d // n_chunks
        for c in range(n_chunks):
            cols = slice(c * ow, (c + 1) * ow)
            part = jnp.dot(act[:, :fh], wo_ref[:fh, cols], preferred_element_type=F32)
            move_rows(2 * (n_chunks + c))
            ys_ref[:, cols] = part + jnp.dot(act[:, fh:], wo_ref[fh:, cols], preferred_element_type=F32)
            move_rows(2 * (n_chunks + c) + 1)

    def drain(cur):
        wait_gathers(cur)
        wait_gathers((cur + 1) % EXPERT_SLOTS)
        src_copy(w + 2, (cur + EXPERT_SLOTS - 1) % EXPERT_SLOTS).wait()

    for key in range(EXPERT_SLOTS):
        mine = (w % EXPERT_SLOTS) == key
        pl.when(mine & (w < n_used))(functools.partial(compute_tile, key))
        pl.when(mine & (w == n_used))(functools.partial(drain, key))

    @pl.when((w >= n_used) & (w < n_tiles))
    def _():
        ys_ref[...] = jnp.zeros_like(ys_ref)


def _experts(tile_expert, n_used, src_tab, hnp, w_ei, w_eo, *, tm):
    n_e, d, f2 = w_ei.shape
    dw = hnp.shape[1]
    n_tiles = src_tab.shape[0] - EXPERT_SLOTS
    n_steps = n_tiles + 1
    assert tm % EXPERT_DMA_GROUPS == 0 and src_tab.shape[1] == tm

    def expert_of(w, te, nu):
        return te[jnp.minimum(w, nu[0] - 1)]

    grid_spec = pltpu.PrefetchScalarGridSpec(
        num_scalar_prefetch=2,
        grid=(n_steps,),
        in_specs=[
            pl.BlockSpec(memory_space=pl.ANY),
            pl.BlockSpec(memory_space=pl.ANY),
            pl.BlockSpec((1, d, f2), lambda w, te, nu: (expert_of(w, te, nu), 0, 0)),
            pl.BlockSpec((1, f2 // 2, d), lambda w, te, nu: (expert_of(w, te, nu), 0, 0)),
        ],
        out_specs=pl.BlockSpec((tm, d), lambda w, te, nu: (jnp.minimum(w, n_tiles - 1), 0)),
        scratch_shapes=[pltpu.SMEM((EXPERT_SLOTS, tm), jnp.int32), pltpu.VMEM((EXPERT_SLOTS, tm, dw), U32),
                        pltpu.VMEM((tm, f2), F32),
                        pltpu.VMEM((d, f2), BF16), pltpu.VMEM((f2 // 2, d), BF16),
                        pltpu.SemaphoreType.DMA((EXPERT_SLOTS,)), pltpu.SemaphoreType.DMA((EXPERT_SLOTS,))],
    )
    return pl.pallas_call(
        _experts_kernel,
        grid_spec=grid_spec,
        out_shape=jax.ShapeDtypeStruct((n_tiles * tm, d), F32),
        compiler_params=_params(1),
        name="experts",
    )(tile_expert, n_used, src_tab, hnp, w_ei, w_eo)


def _expert_tiles(counts, e12, r12, n_all, tm):
    max_tiles = (2 * n_all) // tm + N_EXPERTS
    n_rows = max_tiles + EXPERT_SLOTS
    tiles_per_expert = (counts + tm - 1) // tm
    tile_end = jnp.cumsum(tiles_per_expert)
    n_used = tile_end[-1]
    row_start = (tile_end - tiles_per_expert) * tm
    t = jnp.minimum(jnp.arange(n_rows, dtype=jnp.int32), n_used - 1)
    tile_expert = jnp.sum(tile_end[None, :] <= t[:, None], axis=1).astype(jnp.int32)

    pos = row_start[e12] + r12
    token = jnp.broadcast_to(jnp.arange(n_all, dtype=jnp.int32), (2, n_all))
    src = jnp.zeros((n_rows * tm,), jnp.int32).at[pos.reshape(-1)].set(
        token.reshape(-1), unique_indices=True, mode="promise_in_bounds").reshape(n_rows, tm)
    pos_tiles = pos.reshape(2, n_all // tm, tm).transpose(1, 0, 2).reshape(n_all // tm, 2 * tm)
    return tile_expert, n_used.reshape(1).astype(jnp.int32), src, pos_tiles.astype(jnp.int32)


def _combine_kernel(pos_hbm, h_ref, rinfo_ref, g_ref, ys_hbm, yp_ref, ysm_ref,
                    idx_ref, buf_ref, idx_sem, row_sem, *, n_prompt_tiles):
    i = pl.program_id(0)
    nt = pl.num_programs(0)
    tm = h_ref.shape[0]

    def idx_copy(tile, slot):
        return pltpu.make_async_copy(pos_hbm.at[pl.ds(tile, 1), :], idx_ref.at[pl.ds(slot, 1), :],
                                     idx_sem.at[slot])

    def start_rows(slot):
        def body(r, _):
            for k in range(2):
                pltpu.make_async_copy(ys_hbm.at[pl.ds(idx_ref[slot, k * tm + r], 1), :],
                                      buf_ref.at[slot, k, pl.ds(r, 1), :], row_sem.at[slot]).start(priority=k)
            return 0
        lax.fori_loop(0, tm, body, 0, unroll=8)

    def wait_rows(slot):
        pltpu.make_async_copy(buf_ref.at[slot], buf_ref.at[slot], row_sem.at[slot]).wait()

    @pl.when(i == 0)
    def _():
        idx_copy(0, 0).start()
        idx_copy(0, 0).wait()
        start_rows(0)

        @pl.when(nt > 1)
        def _():
            idx_copy(1, 1).start()

    def tile(cur):
        nxt = 1 - cur

        @pl.when(i + 1 < nt)
        def _():
            idx_copy(i + 1, nxt).wait()
            start_rows(nxt)

        @pl.when(i + 2 < nt)
        def _():
            idx_copy(i + 2, cur).start()

        wait_rows(cur)
        rinfo = rinfo_ref[...]
        lane = lax.broadcasted_iota(jnp.int32, rinfo.shape, 1)
        w1 = _lane_pick(rinfo, lane, R_W1)
        w2 = _lane_pick(rinfo, lane, R_W2)
        hp = h_ref[...] + (w1 * buf_ref[cur, 0] + w2 * buf_ref[cur, 1])
        y = _rms(hp, g_ref[...])

        @pl.when(i < n_prompt_tiles)
        def _():
            yp_ref[...] = y

        @pl.when(i >= n_prompt_tiles)
        def _():
            ysm_ref[...] = y

    for parity in (0, 1):
        pl.when((i % 2) == parity)(functools.partial(tile, parity))


def _combine(pos_tiles, h, rinfo, norm_g, ys, *, n_prompt, tm):
    n_all, d = h.shape
    nt = n_all // tm
    npt = n_prompt // tm
    assert n_all - n_prompt == tm and pos_tiles.shape == (nt, 2 * tm)
    kern = functools.partial(_combine_kernel, n_prompt_tiles=npt)
    return pl.pallas_call(
        kern,
        grid=(nt,),
        in_specs=[
            pl.BlockSpec(memory_space=pl.ANY),
            pl.BlockSpec((tm, d), lambda i: (i, 0)),
            pl.BlockSpec((tm, LANES), lambda i: (i, 0)),
            pl.BlockSpec((1, d), lambda i: (0, 0)),
            pl.BlockSpec(memory_space=pl.ANY),
        ],
        out_specs=[
            pl.BlockSpec((tm, d), lambda i: (jnp.minimum(i, npt - 1), 0)),
            pl.BlockSpec((tm, d), lambda i: (0, 0)),
        ],
        out_shape=[jax.ShapeDtypeStruct((n_prompt, d), F32), jax.ShapeDtypeStruct((tm, d), F32)],
        scratch_shapes=[pltpu.SMEM((2, 2 * tm), jnp.int32), pltpu.VMEM((2, 2, tm, d), F32),
                        pltpu.SemaphoreType.DMA((2,)), pltpu.SemaphoreType.DMA((2,))],
        compiler_params=_params(1),
        name="combine",
    )(pos_tiles, h, rinfo, norm_g, ys)


ROW_TILE = 256
PROJ_ROWS = 512
PROJ_COLS = 256
ATTN_Q_ROWS = 256


def kernel(x_prompt, x_sample, cache_k, cache_v, state_conv, norm_mix, w_in, conv_w, lambda_q1,
           lambda_k1, lambda_q2, lambda_k2, head_norm, w_out, norm_ffn, w_router_group,
           b_router_group, w_router_expert, b_router_expert, w_expert_in, w_expert_out, norm_final):
    depth = w_in.shape[0]
    b, t, d = x_prompt.shape
    db, dt, _ = x_sample.shape
    past = cache_k.shape[2]
    d_conv = conv_w.shape[2]
    d_attn = d - d_conv
    n_p, n_s = b * t, db * dt
    n_all = n_p + n_s
    tm = ROW_TILE

    hp = x_prompt.reshape(n_p, d)
    hs = x_sample.reshape(n_s, d)
    outs = {name: [] for name in ("kp", "vp", "cp", "ks", "vs", "cs")}
    y_p = y_s = None
    for l in range(depth):
        lam_init = 0.8 - 0.6 * math.exp(-0.3 * l)
        w_in_bf = w_in[l].astype(BF16)
        w_out_bf = w_out[l].astype(BF16)
        g_mix = norm_mix[l].reshape(1, d)
        lams = tuple(p[l].reshape(1, -1) for p in (lambda_q1, lambda_k1, lambda_q2, lambda_k2))
        hnorm = head_norm[l].reshape(1, -1)
        pad = LANES - N_GROUPS - N_EXPERTS
        w_router = jnp.concatenate([w_router_group[l], w_router_expert[l], jnp.zeros((d, pad), F32)],
                                   axis=1).astype(BF16)
        b_router = jnp.concatenate([b_router_group[l], b_router_expert[l], jnp.zeros((pad,), F32)]).reshape(1, LANES)

        zero_prev = jnp.zeros((b, CONV_WIDTH - 1, d_conv), F32)
        a_p, q_p, kb_p, kt_p, v_p, tail_p = _in_proj(hp, g_mix, w_in_bf, conv_w[l], zero_prev,
                                                     seq_len=t, tm=min(PROJ_ROWS, t), tn=PROJ_COLS)
        o_p = _attn_prompt(q_p, kb_p, v_p, lams, hnorm, batch=b, seq_len=t, lam_init=lam_init,
                           tq=min(ATTN_Q_ROWS, t))
        prev_rows = jnp.stack([jnp.repeat(state_conv[l][:, 0], dt, axis=0),
                               jnp.repeat(state_conv[l][:, 1], dt, axis=0)])
        a_s, q_s, kb_s, k_s, v_s, u_s = _in_proj(hs, g_mix, w_in_bf, conv_w[l], prev_rows,
                                                 seq_len=dt, tm=n_s, tn=PROJ_COLS)
        kt_past = cache_k[l].transpose(0, 2, 3, 4, 1).reshape(db * d_attn, past)
        o_s = _attn_sample(q_s, kb_s, v_s, kt_past, cache_v[l].reshape(db * past, d_attn), lams, hnorm,
                           batch=db, seq_len=dt, past_len=past, lam_init=lam_init)

        h, hnp, rinfo, rinfo_t, cnt = _out_proj(a_p, o_p, hp, a_s, o_s, hs, w_out_bf,
                                       norm_ffn[l].reshape(1, d), w_router, b_router, tm=tm)
        counts = cnt[0, :N_EXPERTS].astype(jnp.int32)
        e12 = rinfo_t[R_E1:R_E2 + 1].astype(jnp.int32)
        r12 = rinfo_t[R_RANK1:R_RANK2 + 1].astype(jnp.int32)
        tile_expert, n_used, src_tab, pos_tiles = _expert_tiles(counts, e12, r12, n_all, tm)
        ys = _experts(tile_expert, n_used, src_tab, hnp, w_expert_in[l], w_expert_out[l], tm=tm)
        last = l == depth - 1
        g_out = norm_final.reshape(1, d) if last else jnp.ones((1, d), F32)
        y_p, y_s = _combine(pos_tiles, h, rinfo, g_out, ys, n_prompt=n_p, tm=tm)
        assert last, "multi-layer stacks need an un-normalised combine output"

        outs["kp"].append(kt_p.reshape(b, N_HEADS, 2, d_attn // (2 * N_HEADS), t).transpose(0, 4, 1, 2, 3))
        outs["vp"].append(v_p.reshape(b, t, N_HEADS, d_attn // N_HEADS))
        tails = tail_p.reshape(b, t // min(PROJ_ROWS, t), SUBLANES, d_conv)
        outs["cp"].append(tails[:, -1, SUBLANES - (CONV_WIDTH - 1):, :])
        outs["ks"].append(k_s.reshape(db, dt, N_HEADS, 2, d_attn // (2 * N_HEADS)))
        outs["vs"].append(v_s.reshape(db, dt, N_HEADS, d_attn // N_HEADS))
        outs["cs"].append(u_s.reshape(db, dt, d_conv)[:, dt - (CONV_WIDTH - 1):, :])

    return (y_p.reshape(b, t, d), y_s.reshape(db, dt, d),
            jnp.stack(outs["kp"]), jnp.stack(outs["vp"]), jnp.stack(outs["cp"]),
            jnp.stack(outs["ks"]), jnp.stack(outs["vs"]), jnp.stack(outs["cs"]))
```

```python
import functools
import math

import jax
import jax.numpy as jnp
from jax import lax
from jax.experimental import pallas as pl
from jax.experimental.pallas import tpu as pltpu

EPS = 1e-6
N_HEADS = 8
CHUNK = 64
CONV_WIDTH = 3
N_GROUPS = 4
EXPERTS_PER_GROUP = 8
N_EXPERTS = N_GROUPS * EXPERTS_PER_GROUP
LANES = 128
SUBLANES = 8
BF16_SUBLANES = 16
VMEM_LIMIT = 56 * 1024 * 1024

F32 = jnp.float32
BF16 = jnp.bfloat16
U32 = jnp.uint32
NEG_INF = float("-inf")


def _params(n_axes, vmem=VMEM_LIMIT, **kw):
    return pltpu.CompilerParams(dimension_semantics=("arbitrary",) * n_axes,
                                vmem_limit_bytes=vmem, **kw)


def _rms(x, g):
    return x * lax.rsqrt(jnp.mean(x * x, axis=-1, keepdims=True) + EPS) * g


def _lane_pick(x, lane, idx):
    return jnp.sum(jnp.where(lane == idx, x, 0.0), axis=1, keepdims=True)


def _in_proj_kernel(x_ref, g_ref, wgb_ref, wgc_ref, wx_ref, wq_ref, wk_ref, wv_ref, cw_ref, prev_ref,
                    a_ref, q_ref, kb_ref, ko_ref, v_ref, u_ref, xn_ref, carry_ref,
                    *, seq_len, tiles_per_seq, rows_span_sequences, q_scale):
    i = pl.program_id(0)
    j = pl.program_id(1)
    tm = x_ref.shape[0]

    @pl.when(j == 0)
    def _():
        xn_ref[...] = _rms(x_ref[...], g_ref[...]).astype(BF16)

    xn = xn_ref[...]

    gate_b = jnp.dot(xn, wgb_ref[...], preferred_element_type=F32)
    u = (jnp.dot(xn, wgc_ref[...], preferred_element_type=F32)
         * jnp.dot(xn, wx_ref[...], preferred_element_type=F32))
    row = lax.broadcasted_iota(jnp.int32, u.shape, 0)
    if rows_span_sequences:
        rmod = row % seq_len
        pm2, pm1 = prev_ref[0], prev_ref[1]
        first, second = rmod == 0, rmod == 1
        u_ref[...] = u
    else:
        seq_start = (i % tiles_per_seq) == 0
        tail = carry_ref[j]
        pm2 = jnp.where(seq_start, prev_ref[0, 0:1, :], tail[SUBLANES - 2:SUBLANES - 1, :])
        pm1 = jnp.where(seq_start, prev_ref[0, 1:2, :], tail[SUBLANES - 1:SUBLANES, :])
        first, second = row == 0, row == 1
        new_tail = u[tm - SUBLANES:, :]
        carry_ref[j] = new_tail
        u_ref[0] = new_tail
    u1 = jnp.where(first, pm1, pltpu.roll(u, 1, 0))
    u2 = jnp.where(first, pm2, jnp.where(second, pm1, pltpu.roll(u, 2, 0)))
    cw = cw_ref[...]
    y = cw[0:1, :] * u2 + cw[1:2, :] * u1 + cw[2:3, :] * u
    a_ref[...] = (gate_b * y).astype(BF16)

    q_ref[...] = (jnp.dot(xn, wq_ref[...], preferred_element_type=F32) * q_scale).astype(BF16)
    k = jnp.dot(xn, wk_ref[...], preferred_element_type=F32)
    kb_ref[...] = k.astype(BF16)
    if rows_span_sequences:
        ko_ref[...] = k
    else:
        ko_ref[0] = k.T
    v_ref[...] = jnp.dot(xn, wv_ref[...], preferred_element_type=F32)


def _in_proj(x2d, norm_g, w_in_bf, conv_w, prev, *, seq_len, tm, tn):
    n, d = x2d.shape
    d_conv = conv_w.shape[1]
    assert w_in_bf.shape == (d, 6 * d_conv), "conv and attention widths must match"
    assert n % tm == 0 and d_conv % tn == 0
    nj = d_conv // tn
    n_i = n // tm
    rows_span_sequences = tm > seq_len
    if rows_span_sequences:
        assert tm % seq_len == 0 and prev.shape == (2, n, d_conv)
        tiles_per_seq = 1
        prev_spec = pl.BlockSpec((2, tm, tn), lambda i, j: (0, i, j))
        u_shape = jax.ShapeDtypeStruct((n, d_conv), F32)
        u_spec = pl.BlockSpec((tm, tn), lambda i, j: (i, j))
        ko_shape = jax.ShapeDtypeStruct((n, d_conv), F32)
        ko_spec = pl.BlockSpec((tm, tn), lambda i, j: (i, j))
    else:
        assert seq_len % tm == 0 and prev.shape == (n // seq_len, CONV_WIDTH - 1, d_conv)
        tiles_per_seq = seq_len // tm
        prev_spec = pl.BlockSpec((1, CONV_WIDTH - 1, tn), lambda i, j: (i // tiles_per_seq, 0, j))
        u_shape = jax.ShapeDtypeStruct((n_i, SUBLANES, d_conv), F32)
        u_spec = pl.BlockSpec((1, SUBLANES, tn), lambda i, j: (i, 0, j))
        ko_shape = jax.ShapeDtypeStruct((n // seq_len, d_conv, seq_len), F32)
        ko_spec = pl.BlockSpec((1, tn, tm), lambda i, j: (i // tiles_per_seq, j, i % tiles_per_seq))

    def wspec(k):
        return pl.BlockSpec((d, tn), lambda i, j: (0, k * nj + j))

    tile = pl.BlockSpec((tm, tn), lambda i, j: (i, j))
    head_dim = d_conv // (2 * N_HEADS)
    kern = functools.partial(_in_proj_kernel, seq_len=seq_len, tiles_per_seq=tiles_per_seq,
                             rows_span_sequences=rows_span_sequences,
                             q_scale=head_dim ** -0.5 * math.log2(math.e))
    return pl.pallas_call(
        kern,
        grid=(n_i, nj),
        in_specs=[
            pl.BlockSpec((tm, d), lambda i, j: (i, 0)),
            pl.BlockSpec((1, d), lambda i, j: (0, 0)),
            wspec(0), wspec(1), wspec(2), wspec(3), wspec(4), wspec(5),
            pl.BlockSpec((CONV_WIDTH, tn), lambda i, j: (0, j)),
            prev_spec,
        ],
        out_specs=[tile, tile, tile, ko_spec, tile, u_spec],
        out_shape=[
            jax.ShapeDtypeStruct((n, d_conv), BF16),
            jax.ShapeDtypeStruct((n, d_conv), BF16),
            jax.ShapeDtypeStruct((n, d_conv), BF16),
            ko_shape,
            jax.ShapeDtypeStruct((n, d_conv), F32),
            u_shape,
        ],
        scratch_shapes=[pltpu.VMEM((tm, d), BF16), pltpu.VMEM((nj, SUBLANES, tn), F32)],
        compiler_params=_params(2),
        name="in_proj",
    )(x2d, norm_g, *([w_in_bf] * 6), conv_w, prev)


def _diff_lambda(lq1_ref, lk1_ref, lq2_ref, lk2_ref, lam_init):
    s1 = jnp.sum(lq1_ref[...] * lk1_ref[...], axis=1, keepdims=True)
    s2 = jnp.sum(lq2_ref[...] * lk2_ref[...], axis=1, keepdims=True)
    return jnp.exp(s1) - jnp.exp(s2) + lam_init


def _stacked_queries(q, head_dim):
    lane = lax.broadcasted_iota(jnp.int32, q.shape, 1)
    zero = jnp.zeros_like(q)
    return jnp.concatenate([jnp.where(lane < head_dim, q, zero),
                            jnp.where(lane >= head_dim, q, zero)], axis=0)


def _scores(qq, kblk):
    return lax.dot_general(qq, kblk, (((1,), (1,)), ((), ())), preferred_element_type=F32)


def _finish_heads(acc, l, lam, hn, lam_init, tq):
    o = acc[:tq] / l[:tq] - lam * (acc[tq:] / l[tq:])
    return (_rms(o, hn) * (1.0 - lam_init)).astype(BF16)


def _attn_prompt_kernel(lq1_ref, lk1_ref, lq2_ref, lk2_ref, hn_ref, q_ref, k_ref, v_ref, o_ref,
                        qt1_ref, qt2_ref, vt_ref, s0_ref, s1_ref, p0_ref, p1_ref,
                        *, lam_init, head_dim, tq):
    seq_len, hw = q_ref.shape
    lam = _diff_lambda(lq1_ref, lk1_ref, lq2_ref, lk2_ref, lam_init)

    qt = q_ref[...].astype(F32).T
    dim = lax.broadcasted_iota(jnp.int32, qt.shape, 0)
    qt1_ref[...] = jnp.where(dim < head_dim, qt, 0.0).astype(BF16)
    qt2_ref[...] = jnp.where(dim >= head_dim, qt, 0.0).astype(BF16)
    vt_ref[:hw, :] = v_ref[...].T.astype(BF16)
    vt_ref[hw:, :] = jnp.ones((vt_ref.shape[0] - hw, seq_len), BF16)

    key_chunk = lax.broadcasted_iota(jnp.int32, (tq, 2 * tq), 0) // CHUNK
    qry_chunk = (lax.broadcasted_iota(jnp.int32, (tq, 2 * tq), 1) % tq) // CHUNK
    visible = key_chunk <= qry_chunk

    bufs = ((s0_ref, p0_ref), (s1_ref, p1_ref))
    n_blocks = seq_len // tq

    def scores(qi):
        q0, n_keys, s_buf = qi * tq, (qi + 1) * tq, bufs[qi % 2][0]
        qqt = jnp.concatenate([qt1_ref[:, q0:n_keys], qt2_ref[:, q0:n_keys]], axis=1)
        s = jnp.dot(k_ref[q0:n_keys, :], qqt, preferred_element_type=F32)
        s = jnp.where(visible, s, NEG_INF)
        s_buf[q0:n_keys, :] = s
        m = jnp.max(s, axis=0, keepdims=True)
        if qi > 0:
            s = jnp.dot(k_ref[:q0, :], qqt, preferred_element_type=F32)
            s_buf[:q0, :] = s
            m = jnp.maximum(m, jnp.max(s, axis=0, keepdims=True))
        return m

    def finish(qi, m):
        q0, n_keys = qi * tq, (qi + 1) * tq
        s_buf, p_buf = bufs[qi % 2]
        p_buf[:n_keys, :] = jnp.exp2(s_buf[:n_keys, :] - m).astype(BF16)
        ot = jnp.dot(vt_ref[:, :n_keys], p_buf[:n_keys, :], preferred_element_type=F32)
        ot = ot[:hw] / ot[hw:hw + 1]
        o = (ot[:, :tq] - lam * ot[:, tq:]).T
        o_ref[q0:n_keys, :] = (_rms(o, hn_ref[...]) * (1.0 - lam_init)).astype(BF16)

    m = scores(0)
    for qi in range(n_blocks):
        m_next = scores(qi + 1) if qi + 1 < n_blocks else None
        finish(qi, m)
        m = m_next


def _attn_prompt(q, k, v, lams, head_norm, *, batch, seq_len, lam_init, tq):
    n, d_attn = q.shape
    hw = d_attn // N_HEADS
    assert seq_len % tq == 0 and tq % CHUNK == 0
    small = pl.BlockSpec((1, hw // 2), lambda b, h: (0, 0))
    head = pl.BlockSpec((seq_len, hw), lambda b, h: (b, h))
    kern = functools.partial(_attn_prompt_kernel, lam_init=lam_init, head_dim=hw // 2, tq=tq)
    return pl.pallas_call(
        kern,
        grid=(batch, N_HEADS),
        in_specs=[small, small, small, small, pl.BlockSpec((1, hw), lambda b, h: (0, 0)),
                  head, head, head],
        out_specs=head,
        out_shape=jax.ShapeDtypeStruct((n, d_attn), BF16),
        scratch_shapes=[pltpu.VMEM((hw, seq_len), BF16), pltpu.VMEM((hw, seq_len), BF16),
                        pltpu.VMEM((hw + BF16_SUBLANES, seq_len), BF16),
                        pltpu.VMEM((seq_len, 2 * tq), F32), pltpu.VMEM((seq_len, 2 * tq), F32),
                        pltpu.VMEM((seq_len, 2 * tq), BF16), pltpu.VMEM((seq_len, 2 * tq), BF16)],
        compiler_params=_params(2),
        name="attn_prompt",
    )(*lams, head_norm, q, k, v)


def _attn_sample_kernel(lq1_ref, lk1_ref, lq2_ref, lk2_ref, hn_ref, q_ref, kc_ref, vc_ref,
                        kn_ref, vn_ref, o_ref, *, lam_init, head_dim):
    tq = q_ref.shape[0]
    hw = 2 * head_dim
    lam = _diff_lambda(lq1_ref, lk1_ref, lq2_ref, lk2_ref, lam_init)
    for h in range(q_ref.shape[1] // hw):
        cols = slice(h * hw, (h + 1) * hw)
        qq = _stacked_queries(q_ref[:, cols], head_dim)
        s_past = jnp.dot(qq, kc_ref[cols, :].astype(BF16), preferred_element_type=F32)
        s_new = _scores(qq, kn_ref[:, cols])
        m = jnp.maximum(jnp.max(s_past, axis=1, keepdims=True), jnp.max(s_new, axis=1, keepdims=True))
        p_past = jnp.exp2(s_past - m)
        p_new = jnp.exp2(s_new - m)
        l = jnp.sum(p_past, axis=1, keepdims=True) + jnp.sum(p_new, axis=1, keepdims=True)
        acc = (jnp.dot(p_past.astype(BF16), vc_ref[:, cols].astype(BF16), preferred_element_type=F32)
               + jnp.dot(p_new.astype(BF16), vn_ref[:, cols].astype(BF16), preferred_element_type=F32))
        o_ref[:, cols] = _finish_heads(acc, l, lam, hn_ref[...], lam_init, tq)


def _attn_sample(q, k_new, v_new, k_past, v_past, lams, head_norm, *, batch, seq_len, past_len, lam_init):
    n, d_attn = q.shape
    hw = d_attn // N_HEADS
    small = pl.BlockSpec((1, hw // 2), lambda b: (0, 0))
    new_spec = pl.BlockSpec((seq_len, d_attn), lambda b: (b, 0))
    past_spec = pl.BlockSpec((past_len, d_attn), lambda b: (b, 0))
    past_t_spec = pl.BlockSpec((d_attn, past_len), lambda b: (b, 0))
    kern = functools.partial(_attn_sample_kernel, lam_init=lam_init, head_dim=hw // 2)
    return pl.pallas_call(
        kern,
        grid=(batch,),
        in_specs=[small, small, small, small, pl.BlockSpec((1, hw), lambda b: (0, 0)),
                  new_spec, past_t_spec, past_spec, new_spec, new_spec],
        out_specs=new_spec,
        out_shape=jax.ShapeDtypeStruct((n, d_attn), BF16),
        compiler_params=_params(1),
        name="attn_sample",
    )(*lams, head_norm, q, k_past, v_past, k_new, v_new)


R_E1, R_E2, R_W1, R_W2, R_RANK1, R_RANK2 = 0, 1, 2, 3, 4, 5
OUT_PROJ_CHUNKS = 4


def _pack_bf16_pairs(lo_bf, hi_bf):
    lo = lax.bitcast_convert_type(lo_bf.astype(F32), U32)
    hi = lax.bitcast_convert_type(hi_bf.astype(F32), U32)
    return (hi & jnp.uint32(0xFFFF0000)) | (lo >> 16)


def _unpack_bf16_pairs(w):
    lo = lax.bitcast_convert_type(w << 16, F32).astype(BF16)
    hi = lax.bitcast_convert_type(w & jnp.uint32(0xFFFF0000), F32).astype(BF16)
    return lo, hi


def _out_proj_kernel(*refs, n_prompt_tiles):
    ap_ref, op_ref, xp_ref, as_ref, os_ref, xs_ref = refs[:6]
    n_w = 2 * OUT_PROJ_CHUNKS
    w_refs = (refs[6:6 + OUT_PROJ_CHUNKS], refs[6 + OUT_PROJ_CHUNKS:6 + n_w])
    rest = refs[6 + n_w:]
    carry_ref = rest[-1]
    i = pl.program_id(0)

    @pl.when(i == 0)
    def _():
        carry_ref[...] = jnp.zeros_like(carry_ref)

    @pl.when(i < n_prompt_tiles)
    def _():
        _out_proj_rows(ap_ref, op_ref, xp_ref, w_refs, *rest)

    @pl.when(i >= n_prompt_tiles)
    def _():
        _out_proj_rows(as_ref, os_ref, xs_ref, w_refs, *rest)


def _out_proj_rows(a_ref, o_ref, x_ref, w_refs, g_ref, wr_ref, br_ref,
                   h_ref, hnp_ref, rinfo_ref, rinfo_t_ref, cnt_ref, carry_ref):
    tm, d = x_ref.shape
    cw = d // OUT_PROJ_CHUNKS
    a, o = a_ref[...], o_ref[...]

    ss = jnp.zeros((tm, 1), F32)
    for c in range(OUT_PROJ_CHUNKS):
        cols = slice(c * cw, (c + 1) * cw)
        h = (x_ref[:, cols] + jnp.dot(a, w_refs[0][c][...], preferred_element_type=F32)
             + jnp.dot(o, w_refs[1][c][...], preferred_element_type=F32))
        h_ref[:, cols] = h
        ss = ss + jnp.sum(h * h, axis=1, keepdims=True)
    inv = lax.rsqrt(ss / d + EPS)

    logits = br_ref[...]
    for c in range(OUT_PROJ_CHUNKS // 2):
        halves = []
        for cc in (c, c + OUT_PROJ_CHUNKS // 2):
            cols = slice(cc * cw, (cc + 1) * cw)
            hb = (h_ref[:, cols] * inv * g_ref[:, cols]).astype(BF16)
            logits = logits + jnp.dot(hb, wr_ref[cols, :], preferred_element_type=F32)
            halves.append(hb)
        hnp_ref[:, c * cw:(c + 1) * cw] = _pack_bf16_pairs(*halves)

    lane = lax.broadcasted_iota(jnp.int32, logits.shape, 1)
    lane_f = lane.astype(F32)
    lg = jnp.where(lane < N_GROUPS, logits, NEG_INF)
    mg = jnp.max(lg, axis=1, keepdims=True)
    grp = jnp.min(jnp.where(lg == mg, lane_f, float(LANES)), axis=1, keepdims=True)
    gate_g = 1.0 / jnp.sum(jnp.exp(lg - mg), axis=1, keepdims=True)
    ex = lane - N_GROUPS
    in_grp = (ex >= 0) & (ex < N_EXPERTS) & ((ex >> 3).astype(F32) == grp)
    le = jnp.where(in_grp, logits, NEG_INF)
    v1 = jnp.max(le, axis=1, keepdims=True)
    i1 = jnp.min(jnp.where(le == v1, lane_f, float(LANES)), axis=1, keepdims=True)
    le2 = jnp.where(lane_f == i1, NEG_INF, le)
    v2 = jnp.max(le2, axis=1, keepdims=True)
    i2 = jnp.min(jnp.where(le2 == v2, lane_f, float(LANES)), axis=1, keepdims=True)
    e21 = jnp.exp(v2 - v1)
    w1 = gate_g / (1.0 + e21)
    w2 = gate_g * e21 / (1.0 + e21)
    e1 = i1 - float(N_GROUPS)
    e2 = i2 - float(N_GROUPS)

    onehot = jnp.where((lane_f == e1) | (lane_f == e2), 1.0, 0.0)
    r = lax.broadcasted_iota(jnp.int32, (tm, tm), 0)
    c = lax.broadcasted_iota(jnp.int32, (tm, tm), 1)
    earlier = jnp.where(c < r, 1.0, 0.0).astype(BF16)
    before = jnp.dot(earlier, onehot.astype(BF16), preferred_element_type=F32) + carry_ref[0:1, :]
    rank1 = _lane_pick(before, lane_f, e1)
    rank2 = _lane_pick(before, lane_f, e2)
    new_carry = carry_ref[0:1, :] + jnp.sum(onehot, axis=0, keepdims=True)
    carry_ref[...] = jnp.broadcast_to(new_carry, carry_ref.shape)
    cnt_ref[...] = jnp.broadcast_to(new_carry, cnt_ref.shape)

    rec = jnp.zeros_like(logits)
    for idx, val in ((R_E1, e1), (R_E2, e2), (R_W1, w1), (R_W2, w2), (R_RANK1, rank1), (R_RANK2, rank2)):
        rec = jnp.where(lane == idx, val, rec)
    rinfo_ref[...] = rec
    rinfo_t_ref[...] = rec.T[:SUBLANES, :]


def _out_proj(a_p, o_p, x_p, a_s, o_s, x_s, w_out_bf, norm_g, w_router, b_router, *, tm):
    n_p, d = x_p.shape
    n_s = x_s.shape[0]
    dc = a_p.shape[1]
    assert n_p % tm == 0 and n_s == tm, "sample rows must fill exactly one tile"
    assert w_out_bf.shape == (2 * dc, d)
    npt = n_p // tm
    n_all = n_p + n_s
    cw = d // OUT_PROJ_CHUNKS
    p_idx = lambda i: (jnp.minimum(i, npt - 1), 0)
    zero = lambda i: (0, 0)

    def wspec(half, c):
        return pl.BlockSpec((dc, cw), lambda i: (half, c))

    kern = functools.partial(_out_proj_kernel, n_prompt_tiles=npt)
    return pl.pallas_call(
        kern,
        grid=(npt + 1,),
        in_specs=[
            pl.BlockSpec((tm, dc), p_idx), pl.BlockSpec((tm, dc), p_idx), pl.BlockSpec((tm, d), p_idx),
            pl.BlockSpec((tm, dc), zero), pl.BlockSpec((tm, dc), zero), pl.BlockSpec((tm, d), zero),
            *[wspec(half, c) for half in range(2) for c in range(OUT_PROJ_CHUNKS)],
            pl.BlockSpec((1, d), zero),
            pl.BlockSpec((d, LANES), zero), pl.BlockSpec((1, LANES), zero),
        ],
        out_specs=[
            pl.BlockSpec((tm, d), lambda i: (i, 0)),
            pl.BlockSpec((tm, d // 2), lambda i: (i, 0)),
            pl.BlockSpec((tm, LANES), lambda i: (i, 0)),
            pl.BlockSpec((SUBLANES, tm), lambda i: (0, i)),
            pl.BlockSpec((SUBLANES, LANES), zero),
        ],
        out_shape=[
            jax.ShapeDtypeStruct((n_all, d), F32),
            jax.ShapeDtypeStruct((n_all, d // 2), U32),
            jax.ShapeDtypeStruct((n_all, LANES), F32),
            jax.ShapeDtypeStruct((SUBLANES, n_all), F32),
            jax.ShapeDtypeStruct((SUBLANES, LANES), F32),
        ],
        scratch_shapes=[pltpu.VMEM((SUBLANES, LANES), F32)],
        compiler_params=_params(1),
        name="out_proj",
    )(a_p, o_p, x_p, a_s, o_s, x_s, *([w_out_bf] * (2 * OUT_PROJ_CHUNKS)), norm_g, w_router, b_router)


EXPERT_DMA_GROUPS = 16
EXPERT_SLOTS = 3


def _experts_kernel(tile_expert_ref, n_used_ref, src_hbm, hnp_hbm, wi_f32_ref, wo_f32_ref, ys_ref,
                    src_ref, xs_ref, gu_ref, wi_ref, wo_ref, src_sem, gather_sem):
    w = pl.program_id(0)
    n_used = n_used_ref[0]
    tm = xs_ref.shape[1]
    n_tiles = pl.num_programs(0) - 1
    group = tm // EXPERT_DMA_GROUPS

    def src_copy(tile, slot):
        return pltpu.make_async_copy(src_hbm.at[pl.ds(tile, 1), :], src_ref.at[pl.ds(slot, 1), :],
                                     src_sem.at[slot])

    def gather_row(slot, r):
        return pltpu.make_async_copy(hnp_hbm.at[pl.ds(src_ref[slot, r], 1), :],
                                     xs_ref.at[slot, pl.ds(r, 1), :], gather_sem.at[slot])

    def wait_gathers(slot):
        pltpu.make_async_copy(xs_ref.at[slot], xs_ref.at[slot], gather_sem.at[slot]).wait()

    @pl.when(w == 0)
    def _():
        for t in range(EXPERT_SLOTS):
            src_copy(t, t).start()
        for t in range(EXPERT_SLOTS - 1):
            src_copy(t, t).wait()
            for r in range(tm):
                gather_row(t, r).start(priority=r % 2)

    def compute_tile(cur):
        ahead = (cur + EXPERT_SLOTS - 1) % EXPERT_SLOTS
        wait_gathers(cur)
        src_copy(w + 2, ahead).wait()
        src_copy(w + 3, cur).start()

        def move_rows(g):
            for r in range(g * group, (g + 1) * group):
                gather_row(ahead, r).start(priority=r % 2)

        @pl.when((w == 0) | (tile_expert_ref[w] != tile_expert_ref[jnp.maximum(w - 1, 0)]))
        def _():
            wi_ref[...] = wi_f32_ref[0].astype(BF16)
            wo_ref[...] = wo_f32_ref[0].astype(BF16)

        lo, hi = _unpack_bf16_pairs(xs_ref[cur])
        half = lo.shape[1]
        n_chunks = EXPERT_DMA_GROUPS // 4
        f2 = wi_ref.shape[1]
        cw = f2 // n_chunks
        for c in range(n_chunks):
            cols = slice(c * cw, (c + 1) * cw)
            part = jnp.dot(lo, wi_ref[:half, cols], preferred_element_type=F32)
            move_rows(2 * c)
            gu_ref[:, cols] = part + jnp.dot(hi, wi_ref[half:, cols], preferred_element_type=F32)
            move_rows(2 * c + 1)
        gate, up = gu_ref[:, :f2 // 2], gu_ref[:, f2 // 2:]
        act = (gate * jax.nn.sigmoid(gate) * up).astype(BF16)
        fh = act.shape[1] // 2
        d = wo_ref.shape[1]
        ow = d // n_chunks
        for c in range(n_chunks):
            cols = slice(c * ow, (c + 1) * ow)
            part = jnp.dot(act[:, :fh], wo_ref[:fh, cols], preferred_element_type=F32)
            move_rows(2 * (n_chunks + c))
            ys_ref[:, cols] = part + jnp.dot(act[:, fh:], wo_ref[fh:, cols], preferred_element_type=F32)
            move_rows(2 * (n_chunks + c) + 1)

    def drain(cur):
        wait_gathers(cur)
        wait_gathers((cur + 1) % EXPERT_SLOTS)
        src_copy(w + 2, (cur + EXPERT_SLOTS - 1) % EXPERT_SLOTS).wait()

    for key in range(EXPERT_SLOTS):
        mine = (w % EXPERT_SLOTS) == key
        pl.when(mine & (w < n_used))(functools.partial(compute_tile, key))
        pl.when(mine & (w == n_used))(functools.partial(drain, key))

    @pl.when((w >= n_used) & (w < n_tiles))
    def _():
        ys_ref[...] = jnp.zeros_like(ys_ref)


def _experts(tile_expert, n_used, src_tab, hnp, w_ei, w_eo, *, tm):
    n_e, d, f2 = w_ei.shape
    dw = hnp.shape[1]
    n_tiles = src_tab.shape[0] - EXPERT_SLOTS
    n_steps = n_tiles + 1
    assert tm % EXPERT_DMA_GROUPS == 0 and src_tab.shape[1] == tm

    def expert_of(w, te, nu):
        return te[jnp.minimum(w, nu[0] - 1)]

    grid_spec = pltpu.PrefetchScalarGridSpec(
        num_scalar_prefetch=2,
        grid=(n_steps,),
        in_specs=[
            pl.BlockSpec(memory_space=pl.ANY),
            pl.BlockSpec(memory_space=pl.ANY),
            pl.BlockSpec((1, d, f2), lambda w, te, nu: (expert_of(w, te, nu), 0, 0)),
            pl.BlockSpec((1, f2 // 2, d), lambda w, te, nu: (expert_of(w, te, nu), 0, 0)),
        ],
        out_specs=pl.BlockSpec((tm, d), lambda w, te, nu: (jnp.minimum(w, n_tiles - 1), 0)),
        scratch_shapes=[pltpu.SMEM((EXPERT_SLOTS, tm), jnp.int32), pltpu.VMEM((EXPERT_SLOTS, tm, dw), U32),
                        pltpu.VMEM((tm, f2), F32),
                        pltpu.VMEM((d, f2), BF16), pltpu.VMEM((f2 // 2, d), BF16),
                        pltpu.SemaphoreType.DMA((EXPERT_SLOTS,)), pltpu.SemaphoreType.DMA((EXPERT_SLOTS,))],
    )
    return pl.pallas_call(
        _experts_kernel,
        grid_spec=grid_spec,
        out_shape=jax.ShapeDtypeStruct((n_tiles * tm, d), F32),
        compiler_params=_params(1),
        name="experts",
    )(tile_expert, n_used, src_tab, hnp, w_ei, w_eo)


def _expert_tiles(counts, e12, r12, n_all, tm):
    max_tiles = (2 * n_all) // tm + N_EXPERTS
    n_rows = max_tiles + EXPERT_SLOTS
    tiles_per_expert = (counts + tm - 1) // tm
    tile_end = jnp.cumsum(tiles_per_expert)
    n_used = tile_end[-1]
    row_start = (tile_end - tiles_per_expert) * tm
    t = jnp.minimum(jnp.arange(n_rows, dtype=jnp.int32), n_used - 1)
    tile_expert = jnp.sum(tile_end[None, :] <= t[:, None], axis=1).astype(jnp.int32)

    experts = jnp.arange(N_EXPERTS, dtype=jnp.int32)[:, None, None]
    pos = jnp.sum(jnp.where(e12[None] == experts, row_start[:, None, None], 0), axis=0) + r12
    token = jnp.broadcast_to(jnp.arange(n_all, dtype=jnp.int32), (2, n_all))
    src = jnp.zeros((n_rows * tm,), jnp.int32).at[pos.reshape(-1)].set(
        token.reshape(-1), unique_indices=True, mode="promise_in_bounds").reshape(n_rows, tm)
    pos_tiles = pos.reshape(2, n_all // tm, tm).transpose(1, 0, 2).reshape(n_all // tm, 2 * tm)
    return tile_expert, n_used.reshape(1).astype(jnp.int32), src, pos_tiles.astype(jnp.int32)


def _combine_kernel(pos_hbm, h_ref, rinfo_ref, g_ref, ys_hbm, yp_ref, ysm_ref,
                    idx_ref, buf_ref, idx_sem, row_sem, *, n_prompt_tiles):
    i = pl.program_id(0)
    nt = pl.num_programs(0)
    tm = h_ref.shape[0]

    def idx_copy(tile, slot):
        return pltpu.make_async_copy(pos_hbm.at[pl.ds(tile, 1), :], idx_ref.at[pl.ds(slot, 1), :],
                                     idx_sem.at[slot])

    def start_rows(slot):
        def body(r, _):
            for k in range(2):
                pltpu.make_async_copy(ys_hbm.at[pl.ds(idx_ref[slot, k * tm + r], 1), :],
                                      buf_ref.at[slot, k, pl.ds(r, 1), :], row_sem.at[slot]).start(priority=k)
            return 0
        lax.fori_loop(0, tm, body, 0, unroll=8)

    def wait_rows(slot):
        pltpu.make_async_copy(buf_ref.at[slot], buf_ref.at[slot], row_sem.at[slot]).wait()

    @pl.when(i == 0)
    def _():
        idx_copy(0, 0).start()
        idx_copy(0, 0).wait()
        start_rows(0)

        @pl.when(nt > 1)
        def _():
            idx_copy(1, 1).start()

    def tile(cur):
        nxt = 1 - cur

        @pl.when(i + 1 < nt)
        def _():
            idx_copy(i + 1, nxt).wait()
            start_rows(nxt)

        @pl.when(i + 2 < nt)
        def _():
            idx_copy(i + 2, cur).start()

        wait_rows(cur)
        rinfo = rinfo_ref[...]
        lane = lax.broadcasted_iota(jnp.int32, rinfo.shape, 1)
        w1 = _lane_pick(rinfo, lane, R_W1)
        w2 = _lane_pick(rinfo, lane, R_W2)
        hp = h_ref[...] + (w1 * buf_ref[cur, 0] + w2 * buf_ref[cur, 1])
        y = _rms(hp, g_ref[...])

        @pl.when(i < n_prompt_tiles)
        def _():
            yp_ref[...] = y

        @pl.when(i >= n_prompt_tiles)
        def _():
            ysm_ref[...] = y

    for parity in (0, 1):
        pl.when((i % 2) == parity)(functools.partial(tile, parity))


def _combine(pos_tiles, h, rinfo, norm_g, ys, *, n_prompt, tm):
    n_all, d = h.shape
    nt = n_all // tm
    npt = n_prompt // tm
    assert n_all - n_prompt == tm and pos_tiles.shape == (nt, 2 * tm)
    kern = functools.partial(_combine_kernel, n_prompt_tiles=npt)
    return pl.pallas_call(
        kern,
        grid=(nt,),
        in_specs=[
            pl.BlockSpec(memory_space=pl.ANY),
            pl.BlockSpec((tm, d), lambda i: (i, 0)),
            pl.BlockSpec((tm, LANES), lambda i: (i, 0)),
            pl.BlockSpec((1, d), lambda i: (0, 0)),
            pl.BlockSpec(memory_space=pl.ANY),
        ],
        out_specs=[
            pl.BlockSpec((tm, d), lambda i: (jnp.minimum(i, npt - 1), 0)),
            pl.BlockSpec((tm, d), lambda i: (0, 0)),
        ],
        out_shape=[jax.ShapeDtypeStruct((n_prompt, d), F32), jax.ShapeDtypeStruct((tm, d), F32)],
        scratch_shapes=[pltpu.SMEM((2, 2 * tm), jnp.int32), pltpu.VMEM((2, 2, tm, d), F32),
                        pltpu.SemaphoreType.DMA((2,)), pltpu.SemaphoreType.DMA((2,))],
        compiler_params=_params(1),
        name="combine",
    )(pos_tiles, h, rinfo, norm_g, ys)


ROW_TILE = 256
PROJ_ROWS = 1024
PROJ_COLS = 256
ATTN_Q_ROWS = 256


def kernel(x_prompt, x_sample, cache_k, cache_v, state_conv, norm_mix, w_in, conv_w, lambda_q1,
           lambda_k1, lambda_q2, lambda_k2, head_norm, w_out, norm_ffn, w_router_group,
           b_router_group, w_router_expert, b_router_expert, w_expert_in, w_expert_out, norm_final):
    depth = w_in.shape[0]
    b, t, d = x_prompt.shape
    db, dt, _ = x_sample.shape
    past = cache_k.shape[2]
    d_conv = conv_w.shape[2]
    d_attn = d - d_conv
    n_p, n_s = b * t, db * dt
    n_all = n_p + n_s
    tm = ROW_TILE

    hp = x_prompt.reshape(n_p, d)
    hs = x_sample.reshape(n_s, d)
    outs = {name: [] for name in ("kp", "vp", "cp", "ks", "vs", "cs")}
    y_p = y_s = None
    for l in range(depth):
        lam_init = 0.8 - 0.6 * math.exp(-0.3 * l)
        w_in_bf = w_in[l].astype(BF16)
        w_out_bf = w_out[l].astype(BF16)
        g_mix = norm_mix[l].reshape(1, d)
        lams = tuple(p[l].reshape(1, -1) for p in (lambda_q1, lambda_k1, lambda_q2, lambda_k2))
        hnorm = head_norm[l].reshape(1, -1)
        pad = LANES - N_GROUPS - N_EXPERTS
        w_router = jnp.concatenate([w_router_group[l], w_router_expert[l], jnp.zeros((d, pad), F32)],
                                   axis=1).astype(BF16)
        b_router = jnp.concatenate([b_router_group[l], b_router_expert[l], jnp.zeros((pad,), F32)]).reshape(1, LANES)

        zero_prev = jnp.zeros((b, CONV_WIDTH - 1, d_conv), F32)
        a_p, q_p, kb_p, kt_p, v_p, tail_p = _in_proj(hp, g_mix, w_in_bf, conv_w[l], zero_prev,
                                                     seq_len=t, tm=min(PROJ_ROWS, t), tn=PROJ_COLS)
        o_p = _attn_prompt(q_p, kb_p, v_p, lams, hnorm, batch=b, seq_len=t, lam_init=lam_init,
                           tq=min(ATTN_Q_ROWS, t))
        prev_rows = jnp.stack([jnp.repeat(state_conv[l][:, 0], dt, axis=0),
                               jnp.repeat(state_conv[l][:, 1], dt, axis=0)])
        a_s, q_s, kb_s, k_s, v_s, u_s = _in_proj(hs, g_mix, w_in_bf, conv_w[l], prev_rows,
                                                 seq_len=dt, tm=n_s, tn=PROJ_COLS)
        kt_past = cache_k[l].transpose(0, 2, 3, 4, 1).reshape(db * d_attn, past)
        o_s = _attn_sample(q_s, kb_s, v_s, kt_past, cache_v[l].reshape(db * past, d_attn), lams, hnorm,
                           batch=db, seq_len=dt, past_len=past, lam_init=lam_init)

        h, hnp, rinfo, rinfo_t, cnt = _out_proj(a_p, o_p, hp, a_s, o_s, hs, w_out_bf,
                                       norm_ffn[l].reshape(1, d), w_router, b_router, tm=tm)
        counts = cnt[0, :N_EXPERTS].astype(jnp.int32)
        e12 = rinfo_t[R_E1:R_E2 + 1].astype(jnp.int32)
        r12 = rinfo_t[R_RANK1:R_RANK2 + 1].astype(jnp.int32)
        tile_expert, n_used, src_tab, pos_tiles = _expert_tiles(counts, e12, r12, n_all, tm)
        ys = _experts(tile_expert, n_used, src_tab, hnp, w_expert_in[l], w_expert_out[l], tm=tm)
        last = l == depth - 1
        g_out = norm_final.reshape(1, d) if last else jnp.ones((1, d), F32)
        y_p, y_s = _combine(pos_tiles, h, rinfo, g_out, ys, n_prompt=n_p, tm=tm)
        assert last, "multi-layer stacks need an un-normalised combine output"

        outs["kp"].append(kt_p.reshape(b, N_HEADS, 2, d_attn // (2 * N_HEADS), t).transpose(0, 4, 1, 2, 3))
        outs["vp"].append(v_p.reshape(b, t, N_HEADS, d_attn // N_HEADS))
        tails = tail_p.reshape(b, t // min(PROJ_ROWS, t), SUBLANES, d_conv)
        outs["cp"].append(tails[:, -1, SUBLANES - (CONV_WIDTH - 1):, :])
        outs["ks"].append(k_s.reshape(db, dt, N_HEADS, 2, d_attn // (2 * N_HEADS)))
        outs["vs"].append(v_s.reshape(db, dt, N_HEADS, d_attn // N_HEADS))
        outs["cs"].append(u_s.reshape(db, dt, d_conv)[:, dt - (CONV_WIDTH - 1):, :])

    return (y_p.reshape(b, t, d), y_s.reshape(db, dt, d),
            jnp.stack(outs["kp"]), jnp.stack(outs["vp"]), jnp.stack(outs["cp"]),
            jnp.stack(outs["ks"]), jnp.stack(outs["vs"]), jnp.stack(outs["cs"]))
```

```python
import functools
import math

import jax
import jax.numpy as jnp
from jax import lax
from jax.experimental import pallas as pl
from jax.experimental.pallas import tpu as pltpu

EPS = 1e-6
N_HEADS = 8
CHUNK = 64
CONV_WIDTH = 3
N_GROUPS = 4
EXPERTS_PER_GROUP = 8
N_EXPERTS = N_GROUPS * EXPERTS_PER_GROUP
LANES = 128
SUBLANES = 8
BF16_SUBLANES = 16
VMEM_LIMIT = 56 * 1024 * 1024

F32 = jnp.float32
BF16 = jnp.bfloat16
U32 = jnp.uint32
NEG_INF = float("-inf")


def _params(n_axes, vmem=VMEM_LIMIT, **kw):
    return pltpu.CompilerParams(dimension_semantics=("arbitrary",) * n_axes,
                                vmem_limit_bytes=vmem, **kw)


def _rms(x, g):
    return x * lax.rsqrt(jnp.mean(x * x, axis=-1, keepdims=True) + EPS) * g


def _lane_pick(x, lane, idx):
    return jnp.sum(jnp.where(lane == idx, x, 0.0), axis=1, keepdims=True)


def _in_proj_kernel(x_ref, g_ref, wgb_ref, wgc_ref, wx_ref, wq_ref, wk_ref, wv_ref, cw_ref, prev_ref,
                    a_ref, q_ref, kb_ref, ko_ref, v_ref, u_ref, xn_ref, carry_ref,
                    *, seq_len, tiles_per_seq, rows_span_sequences, q_scale):
    i = pl.program_id(0)
    j = pl.program_id(1)
    tm = x_ref.shape[0]

    @pl.when(j == 0)
    def _():
        xn_ref[...] = _rms(x_ref[...], g_ref[...]).astype(BF16)

    xn = xn_ref[...]

    gate_b = jnp.dot(xn, wgb_ref[...], preferred_element_type=F32)
    u = (jnp.dot(xn, wgc_ref[...], preferred_element_type=F32)
         * jnp.dot(xn, wx_ref[...], preferred_element_type=F32))
    row = lax.broadcasted_iota(jnp.int32, u.shape, 0)
    if rows_span_sequences:
        rmod = row % seq_len
        pm2, pm1 = prev_ref[0], prev_ref[1]
        first, second = rmod == 0, rmod == 1
        u_ref[...] = u
    else:
        seq_start = (i % tiles_per_seq) == 0
        tail = carry_ref[j]
        pm2 = jnp.where(seq_start, prev_ref[0, 0:1, :], tail[SUBLANES - 2:SUBLANES - 1, :])
        pm1 = jnp.where(seq_start, prev_ref[0, 1:2, :], tail[SUBLANES - 1:SUBLANES, :])
        first, second = row == 0, row == 1
        new_tail = u[tm - SUBLANES:, :]
        carry_ref[j] = new_tail
        u_ref[0] = new_tail
    u1 = jnp.where(first, pm1, pltpu.roll(u, 1, 0))
    u2 = jnp.where(first, pm2, jnp.where(second, pm1, pltpu.roll(u, 2, 0)))
    cw = cw_ref[...]
    y = cw[0:1, :] * u2 + cw[1:2, :] * u1 + cw[2:3, :] * u
    a_ref[...] = (gate_b * y).astype(BF16)

    q_ref[...] = (jnp.dot(xn, wq_ref[...], preferred_element_type=F32) * q_scale).astype(BF16)
    k = jnp.dot(xn, wk_ref[...], preferred_element_type=F32)
    kb_ref[...] = k.astype(BF16)
    if rows_span_sequences:
        ko_ref[...] = k
    else:
        ko_ref[0] = k.T
    v_ref[...] = jnp.dot(xn, wv_ref[...], preferred_element_type=F32)


def _in_proj(x2d, norm_g, w_in_bf, conv_w, prev, *, seq_len, tm, tn):
    n, d = x2d.shape
    d_conv = conv_w.shape[1]
    assert w_in_bf.shape == (d, 6 * d_conv), "conv and attention widths must match"
    assert n % tm == 0 and d_conv % tn == 0
    nj = d_conv // tn
    n_i = n // tm
    rows_span_sequences = tm > seq_len
    if rows_span_sequences:
        assert tm % seq_len == 0 and prev.shape == (2, n, d_conv)
        tiles_per_seq = 1
        prev_spec = pl.BlockSpec((2, tm, tn), lambda i, j: (0, i, j))
        u_shape = jax.ShapeDtypeStruct((n, d_conv), F32)
        u_spec = pl.BlockSpec((tm, tn), lambda i, j: (i, j))
        ko_shape = jax.ShapeDtypeStruct((n, d_conv), F32)
        ko_spec = pl.BlockSpec((tm, tn), lambda i, j: (i, j))
    else:
        assert seq_len % tm == 0 and prev.shape == (n // seq_len, CONV_WIDTH - 1, d_conv)
        tiles_per_seq = seq_len // tm
        prev_spec = pl.BlockSpec((1, CONV_WIDTH - 1, tn), lambda i, j: (i // tiles_per_seq, 0, j))
        u_shape = jax.ShapeDtypeStruct((n_i, SUBLANES, d_conv), F32)
        u_spec = pl.BlockSpec((1, SUBLANES, tn), lambda i, j: (i, 0, j))
        ko_shape = jax.ShapeDtypeStruct((n // seq_len, d_conv, seq_len), F32)
        ko_spec = pl.BlockSpec((1, tn, tm), lambda i, j: (i // tiles_per_seq, j, i % tiles_per_seq))

    def wspec(k):
        return pl.BlockSpec((d, tn), lambda i, j: (0, k * nj + j))

    tile = pl.BlockSpec((tm, tn), lambda i, j: (i, j))
    head_dim = d_conv // (2 * N_HEADS)
    kern = functools.partial(_in_proj_kernel, seq_len=seq_len, tiles_per_seq=tiles_per_seq,
                             rows_span_sequences=rows_span_sequences,
                             q_scale=head_dim ** -0.5 * math.log2(math.e))
    return pl.pallas_call(
        kern,
        grid=(n_i, nj),
        in_specs=[
            pl.BlockSpec((tm, d), lambda i, j: (i, 0)),
            pl.BlockSpec((1, d), lambda i, j: (0, 0)),
            wspec(0), wspec(1), wspec(2), wspec(3), wspec(4), wspec(5),
            pl.BlockSpec((CONV_WIDTH, tn), lambda i, j: (0, j)),
            prev_spec,
        ],
        out_specs=[tile, tile, tile, ko_spec, tile, u_spec],
        out_shape=[
            jax.ShapeDtypeStruct((n, d_conv), BF16),
            jax.ShapeDtypeStruct((n, d_conv), BF16),
            jax.ShapeDtypeStruct((n, d_conv), BF16),
            ko_shape,
            jax.ShapeDtypeStruct((n, d_conv), F32),
            u_shape,
        ],
        scratch_shapes=[pltpu.VMEM((tm, d), BF16), pltpu.VMEM((nj, SUBLANES, tn), F32)],
        compiler_params=_params(2),
        name="in_proj",
    )(x2d, norm_g, *([w_in_bf] * 6), conv_w, prev)


def _diff_lambda(lq1_ref, lk1_ref, lq2_ref, lk2_ref, lam_init):
    s1 = jnp.sum(lq1_ref[...] * lk1_ref[...], axis=1, keepdims=True)
    s2 = jnp.sum(lq2_ref[...] * lk2_ref[...], axis=1, keepdims=True)
    return jnp.exp(s1) - jnp.exp(s2) + lam_init


def _stacked_queries(q, head_dim):
    lane = lax.broadcasted_iota(jnp.int32, q.shape, 1)
    zero = jnp.zeros_like(q)
    return jnp.concatenate([jnp.where(lane < head_dim, q, zero),
                            jnp.where(lane >= head_dim, q, zero)], axis=0)


def _scores(qq, kblk):
    return lax.dot_general(qq, kblk, (((1,), (1,)), ((), ())), preferred_element_type=F32)


def _finish_heads(acc, l, lam, hn, lam_init, tq):
    o = acc[:tq] / l[:tq] - lam * (acc[tq:] / l[tq:])
    return (_rms(o, hn) * (1.0 - lam_init)).astype(BF16)


def _attn_prompt_kernel(lq1_ref, lk1_ref, lq2_ref, lk2_ref, hn_ref, q_ref, k_ref, v_ref, o_ref,
                        qt1_ref, qt2_ref, vt_ref, s0_ref, s1_ref, p0_ref, p1_ref,
                        *, lam_init, head_dim, tq):
    seq_len = q_ref.shape[0]
    hw = 2 * head_dim
    lam = _diff_lambda(lq1_ref, lk1_ref, lq2_ref, lk2_ref, lam_init)

    def prepare(h):
        cols = slice(h * hw, (h + 1) * hw)
        qt = q_ref[:, cols].astype(F32).T
        dim = lax.broadcasted_iota(jnp.int32, qt.shape, 0)
        qt1_ref[h] = jnp.where(dim < head_dim, qt, 0.0).astype(BF16)
        qt2_ref[h] = jnp.where(dim >= head_dim, qt, 0.0).astype(BF16)
        vt_ref[h, :hw, :] = v_ref[:, cols].T.astype(BF16)
        vt_ref[h, hw:, :] = jnp.ones((vt_ref.shape[1] - hw, seq_len), BF16)

    key_chunk = lax.broadcasted_iota(jnp.int32, (tq, 2 * tq), 0) // CHUNK
    qry_chunk = (lax.broadcasted_iota(jnp.int32, (tq, 2 * tq), 1) % tq) // CHUNK
    visible = key_chunk <= qry_chunk

    bufs = ((s0_ref, p0_ref), (s1_ref, p1_ref))
    n_blocks = seq_len // tq
    blocks = [(h, qi) for h in range(q_ref.shape[1] // hw) for qi in range(n_blocks)]

    def scores(idx):
        h, qi = blocks[idx]
        cols = slice(h * hw, (h + 1) * hw)
        q0, n_keys, s_buf = qi * tq, (qi + 1) * tq, bufs[idx % 2][0]
        qqt = jnp.concatenate([qt1_ref[h, :, q0:n_keys], qt2_ref[h, :, q0:n_keys]], axis=1)
        s = jnp.dot(k_ref[q0:n_keys, cols], qqt, preferred_element_type=F32)
        s = jnp.where(visible, s, NEG_INF)
        s_buf[q0:n_keys, :] = s
        m = jnp.max(s, axis=0, keepdims=True)
        if qi > 0:
            s = jnp.dot(k_ref[:q0, cols], qqt, preferred_element_type=F32)
            s_buf[:q0, :] = s
            m = jnp.maximum(m, jnp.max(s, axis=0, keepdims=True))
        return m

    def finish(idx, m):
        h, qi = blocks[idx]
        q0, n_keys = qi * tq, (qi + 1) * tq
        s_buf, p_buf = bufs[idx % 2]
        p_buf[:n_keys, :] = jnp.exp2(s_buf[:n_keys, :] - m).astype(BF16)
        ot = jnp.dot(vt_ref[h, :, :n_keys], p_buf[:n_keys, :], preferred_element_type=F32)
        ot = ot[:hw] / ot[hw:hw + 1]
        o = (ot[:, :tq] - lam * ot[:, tq:]).T
        o_ref[q0:n_keys, h * hw:(h + 1) * hw] = (_rms(o, hn_ref[...]) * (1.0 - lam_init)).astype(BF16)

    prepare(0)
    m = scores(0)
    for idx in range(len(blocks)):
        m_next = None
        if idx + 1 < len(blocks):
            if blocks[idx + 1][1] == 0:
                prepare(blocks[idx + 1][0])
            m_next = scores(idx + 1)
        finish(idx, m)
        m = m_next


def _attn_prompt(q, k, v, lams, head_norm, *, batch, seq_len, lam_init, tq):
    n, d_attn = q.shape
    hw = d_attn // N_HEADS
    nh = ATTN_HEADS_PER_STEP
    assert seq_len % tq == 0 and tq % CHUNK == 0 and N_HEADS % nh == 0
    small = pl.BlockSpec((1, hw // 2), lambda b, h: (0, 0))
    head = pl.BlockSpec((seq_len, nh * hw), lambda b, h: (b, h))
    kern = functools.partial(_attn_prompt_kernel, lam_init=lam_init, head_dim=hw // 2, tq=tq)
    return pl.pallas_call(
        kern,
        grid=(batch, N_HEADS // nh),
        in_specs=[small, small, small, small, pl.BlockSpec((1, hw), lambda b, h: (0, 0)),
                  head, head, head],
        out_specs=head,
        out_shape=jax.ShapeDtypeStruct((n, d_attn), BF16),
        scratch_shapes=[pltpu.VMEM((nh, hw, seq_len), BF16), pltpu.VMEM((nh, hw, seq_len), BF16),
                        pltpu.VMEM((nh, hw + BF16_SUBLANES, seq_len), BF16),
                        pltpu.VMEM((seq_len, 2 * tq), F32), pltpu.VMEM((seq_len, 2 * tq), F32),
                        pltpu.VMEM((seq_len, 2 * tq), BF16), pltpu.VMEM((seq_len, 2 * tq), BF16)],
        compiler_params=_params(2),
        name="attn_prompt",
    )(*lams, head_norm, q, k, v)


def _attn_sample_kernel(lq1_ref, lk1_ref, lq2_ref, lk2_ref, hn_ref, q_ref, kc_ref, vc_ref,
                        kn_ref, vn_ref, o_ref, *, lam_init, head_dim):
    tq = q_ref.shape[0]
    hw = 2 * head_dim
    lam = _diff_lambda(lq1_ref, lk1_ref, lq2_ref, lk2_ref, lam_init)
    for h in range(q_ref.shape[1] // hw):
        cols = slice(h * hw, (h + 1) * hw)
        qq = _stacked_queries(q_ref[:, cols], head_dim)
        s_past = jnp.dot(qq, kc_ref[cols, :].astype(BF16), preferred_element_type=F32)
        s_new = _scores(qq, kn_ref[:, cols])
        m = jnp.maximum(jnp.max(s_past, axis=1, keepdims=True), jnp.max(s_new, axis=1, keepdims=True))
        p_past = jnp.exp2(s_past - m)
        p_new = jnp.exp2(s_new - m)
        l = jnp.sum(p_past, axis=1, keepdims=True) + jnp.sum(p_new, axis=1, keepdims=True)
        acc = (jnp.dot(p_past.astype(BF16), vc_ref[:, cols].astype(BF16), preferred_element_type=F32)
               + jnp.dot(p_new.astype(BF16), vn_ref[:, cols].astype(BF16), preferred_element_type=F32))
        o_ref[:, cols] = _finish_heads(acc, l, lam, hn_ref[...], lam_init, tq)


def _attn_sample(q, k_new, v_new, k_past, v_past, lams, head_norm, *, batch, seq_len, past_len, lam_init):
    n, d_attn = q.shape
    hw = d_attn // N_HEADS
    small = pl.BlockSpec((1, hw // 2), lambda b: (0, 0))
    new_spec = pl.BlockSpec((seq_len, d_attn), lambda b: (b, 0))
    past_spec = pl.BlockSpec((past_len, d_attn), lambda b: (b, 0))
    past_t_spec = pl.BlockSpec((d_attn, past_len), lambda b: (b, 0))
    kern = functools.partial(_attn_sample_kernel, lam_init=lam_init, head_dim=hw // 2)
    return pl.pallas_call(
        kern,
        grid=(batch,),
        in_specs=[small, small, small, small, pl.BlockSpec((1, hw), lambda b: (0, 0)),
                  new_spec, past_t_spec, past_spec, new_spec, new_spec],
        out_specs=new_spec,
        out_shape=jax.ShapeDtypeStruct((n, d_attn), BF16),
        compiler_params=_params(1),
        name="attn_sample",
    )(*lams, head_norm, q, k_past, v_past, k_new, v_new)


R_E1, R_E2, R_W1, R_W2, R_RANK1, R_RANK2 = 0, 1, 2, 3, 4, 5
OUT_PROJ_CHUNKS = 4


def _pack_bf16_pairs(lo_bf, hi_bf):
    lo = lax.bitcast_convert_type(lo_bf.astype(F32), U32)
    hi = lax.bitcast_convert_type(hi_bf.astype(F32), U32)
    return (hi & jnp.uint32(0xFFFF0000)) | (lo >> 16)


def _unpack_bf16_pairs(w):
    lo = lax.bitcast_convert_type(w << 16, F32).astype(BF16)
    hi = lax.bitcast_convert_type(w & jnp.uint32(0xFFFF0000), F32).astype(BF16)
    return lo, hi


def _out_proj_kernel(*refs, n_prompt_tiles):
    ap_ref, op_ref, xp_ref, as_ref, os_ref, xs_ref = refs[:6]
    n_w = 2 * OUT_PROJ_CHUNKS
    w_refs = (refs[6:6 + OUT_PROJ_CHUNKS], refs[6 + OUT_PROJ_CHUNKS:6 + n_w])
    rest = refs[6 + n_w:]
    carry_ref = rest[-1]
    i = pl.program_id(0)

    @pl.when(i == 0)
    def _():
        carry_ref[...] = jnp.zeros_like(carry_ref)

    @pl.when(i < n_prompt_tiles)
    def _():
        _out_proj_rows(ap_ref, op_ref, xp_ref, w_refs, *rest)

    @pl.when(i >= n_prompt_tiles)
    def _():
        _out_proj_rows(as_ref, os_ref, xs_ref, w_refs, *rest)


def _out_proj_rows(a_ref, o_ref, x_ref, w_refs, g_ref, wr_ref, br_ref,
                   h_ref, hnp_ref, rinfo_ref, rinfo_t_ref, cnt_ref, carry_ref):
    tm, d = x_ref.shape
    cw = d // OUT_PROJ_CHUNKS
    a, o = a_ref[...], o_ref[...]

    ss = jnp.zeros((tm, 1), F32)
    for c in range(OUT_PROJ_CHUNKS):
        cols = slice(c * cw, (c + 1) * cw)
        h = (x_ref[:, cols] + jnp.dot(a, w_refs[0][c][...], preferred_element_type=F32)
             + jnp.dot(o, w_refs[1][c][...], preferred_element_type=F32))
        h_ref[:, cols] = h
        ss = ss + jnp.sum(h * h, axis=1, keepdims=True)
    inv = lax.rsqrt(ss / d + EPS)

    logits = br_ref[...]
    for c in range(OUT_PROJ_CHUNKS // 2):
        halves = []
        for cc in (c, c + OUT_PROJ_CHUNKS // 2):
            cols = slice(cc * cw, (cc + 1) * cw)
            hb = (h_ref[:, cols] * inv * g_ref[:, cols]).astype(BF16)
            logits = logits + jnp.dot(hb, wr_ref[cols, :], preferred_element_type=F32)
            halves.append(hb)
        hnp_ref[:, c * cw:(c + 1) * cw] = _pack_bf16_pairs(*halves)

    lane = lax.broadcasted_iota(jnp.int32, logits.shape, 1)
    lane_f = lane.astype(F32)
    lg = jnp.where(lane < N_GROUPS, logits, NEG_INF)
    mg = jnp.max(lg, axis=1, keepdims=True)
    grp = jnp.min(jnp.where(lg == mg, lane_f, float(LANES)), axis=1, keepdims=True)
    gate_g = 1.0 / jnp.sum(jnp.exp(lg - mg), axis=1, keepdims=True)
    ex = lane - N_GROUPS
    in_grp = (ex >= 0) & (ex < N_EXPERTS) & ((ex >> 3).astype(F32) == grp)
    le = jnp.where(in_grp, logits, NEG_INF)
    v1 = jnp.max(le, axis=1, keepdims=True)
    i1 = jnp.min(jnp.where(le == v1, lane_f, float(LANES)), axis=1, keepdims=True)
    le2 = jnp.where(lane_f == i1, NEG_INF, le)
    v2 = jnp.max(le2, axis=1, keepdims=True)
    i2 = jnp.min(jnp.where(le2 == v2, lane_f, float(LANES)), axis=1, keepdims=True)
    e21 = jnp.exp(v2 - v1)
    w1 = gate_g / (1.0 + e21)
    w2 = gate_g * e21 / (1.0 + e21)
    e1 = i1 - float(N_GROUPS)
    e2 = i2 - float(N_GROUPS)

    onehot = jnp.where((lane_f == e1) | (lane_f == e2), 1.0, 0.0)
    r = lax.broadcasted_iota(jnp.int32, (tm, tm), 0)
    c = lax.broadcasted_iota(jnp.int32, (tm, tm), 1)
    earlier = jnp.where(c < r, 1.0, 0.0).astype(BF16)
    before = jnp.dot(earlier, onehot.astype(BF16), preferred_element_type=F32) + carry_ref[0:1, :]
    rank1 = _lane_pick(before, lane_f, e1)
    rank2 = _lane_pick(before, lane_f, e2)
    new_carry = carry_ref[0:1, :] + jnp.sum(onehot, axis=0, keepdims=True)
    carry_ref[...] = jnp.broadcast_to(new_carry, carry_ref.shape)
    cnt_ref[...] = jnp.broadcast_to(new_carry, cnt_ref.shape)

    rec = jnp.zeros_like(logits)
    for idx, val in ((R_E1, e1), (R_E2, e2), (R_W1, w1), (R_W2, w2), (R_RANK1, rank1), (R_RANK2, rank2)):
        rec = jnp.where(lane == idx, val, rec)
    rinfo_ref[...] = rec
    rinfo_t_ref[...] = rec.T[:SUBLANES, :]


def _out_proj(a_p, o_p, x_p, a_s, o_s, x_s, w_out_bf, norm_g, w_router, b_router, *, tm):
    n_p, d = x_p.shape
    n_s = x_s.shape[0]
    dc = a_p.shape[1]
    assert n_p % tm == 0 and n_s == tm, "sample rows must fill exactly one tile"
    assert w_out_bf.shape == (2 * dc, d)
    npt = n_p // tm
    n_all = n_p + n_s
    cw = d // OUT_PROJ_CHUNKS
    p_idx = lambda i: (jnp.minimum(i, npt - 1), 0)
    zero = lambda i: (0, 0)

    def wspec(half, c):
        return pl.BlockSpec((dc, cw), lambda i: (half, c))

    kern = functools.partial(_out_proj_kernel, n_prompt_tiles=npt)
    return pl.pallas_call(
        kern,
        grid=(npt + 1,),
        in_specs=[
            pl.BlockSpec((tm, dc), p_idx), pl.BlockSpec((tm, dc), p_idx), pl.BlockSpec((tm, d), p_idx),
            pl.BlockSpec((tm, dc), zero), pl.BlockSpec((tm, dc), zero), pl.BlockSpec((tm, d), zero),
            *[wspec(half, c) for half in range(2) for c in range(OUT_PROJ_CHUNKS)],
            pl.BlockSpec((1, d), zero),
            pl.BlockSpec((d, LANES), zero), pl.BlockSpec((1, LANES), zero),
        ],
        out_specs=[
            pl.BlockSpec((tm, d), lambda i: (i, 0)),
            pl.BlockSpec((tm, d // 2), lambda i: (i, 0)),
            pl.BlockSpec((tm, LANES), lambda i: (i, 0)),
            pl.BlockSpec((SUBLANES, tm), lambda i: (0, i)),
            pl.BlockSpec((SUBLANES, LANES), zero),
        ],
        out_shape=[
            jax.ShapeDtypeStruct((n_all, d), F32),
            jax.ShapeDtypeStruct((n_all, d // 2), U32),
            jax.ShapeDtypeStruct((n_all, LANES), F32),
            jax.ShapeDtypeStruct((SUBLANES, n_all), F32),
            jax.ShapeDtypeStruct((SUBLANES, LANES), F32),
        ],
        scratch_shapes=[pltpu.VMEM((SUBLANES, LANES), F32)],
        compiler_params=_params(1),
        name="out_proj",
    )(a_p, o_p, x_p, a_s, o_s, x_s, *([w_out_bf] * (2 * OUT_PROJ_CHUNKS)), norm_g, w_router, b_router)


EXPERT_DMA_GROUPS = 16
EXPERT_SLOTS = 3


def _experts_kernel(tile_expert_ref, n_used_ref, src_hbm, hnp_hbm, wi_f32_ref, wo_f32_ref, ys_ref,
                    src_ref, xs_ref, gu_ref, wi_ref, wo_ref, src_sem, gather_sem):
    w = pl.program_id(0)
    n_used = n_used_ref[0]
    tm = xs_ref.shape[1]
    n_tiles = pl.num_programs(0) - 1
    group = tm // EXPERT_DMA_GROUPS

    def src_copy(tile, slot):
        return pltpu.make_async_copy(src_hbm.at[pl.ds(tile, 1), :], src_ref.at[pl.ds(slot, 1), :],
                                     src_sem.at[slot])

    def gather_row(slot, r):
        return pltpu.make_async_copy(hnp_hbm.at[pl.ds(src_ref[slot, r], 1), :],
                                     xs_ref.at[slot, pl.ds(r, 1), :], gather_sem.at[slot])

    def wait_gathers(slot):
        pltpu.make_async_copy(xs_ref.at[slot], xs_ref.at[slot], gather_sem.at[slot]).wait()

    @pl.when(w == 0)
    def _():
        for t in range(EXPERT_SLOTS):
            src_copy(t, t).start()
        for t in range(EXPERT_SLOTS - 1):
            src_copy(t, t).wait()
            for r in range(tm):
                gather_row(t, r).start(priority=r % 2)

    def compute_tile(cur):
        ahead = (cur + EXPERT_SLOTS - 1) % EXPERT_SLOTS
        wait_gathers(cur)
        src_copy(w + 2, ahead).wait()
        src_copy(w + 3, cur).start()

        def move_rows(g):
            for r in range(g * group, (g + 1) * group):
                gather_row(ahead, r).start(priority=r % 2)

        @pl.when((w == 0) | (tile_expert_ref[w] != tile_expert_ref[jnp.maximum(w - 1, 0)]))
        def _():
            wi_ref[...] = wi_f32_ref[0].astype(BF16)
            wo_ref[...] = wo_f32_ref[0].astype(BF16)

        lo, hi = _unpack_bf16_pairs(xs_ref[cur])
        half = lo.shape[1]
        n_chunks = EXPERT_DMA_GROUPS // 4
        f2 = wi_ref.shape[1]
        cw = f2 // n_chunks
        for c in range(n_chunks):
            cols = slice(c * cw, (c + 1) * cw)
            part = jnp.dot(lo, wi_ref[:half, cols], preferred_element_type=F32)
            move_rows(2 * c)
            gu_ref[:, cols] = part + jnp.dot(hi, wi_ref[half:, cols], preferred_element_type=F32)
            move_rows(2 * c + 1)
        gate, up = gu_ref[:, :f2 // 2], gu_ref[:, f2 // 2:]
        act = (gate * jax.nn.sigmoid(gate) * up).astype(BF16)
        fh = act.shape[1] // 2
        d = wo_ref.shape[1]
        ow = d // n_chunks
        for c in range(n_chunks):
            cols = slice(c * ow, (c + 1) * ow)
            part = jnp.dot(act[:, :fh], wo_ref[:fh, cols], preferred_element_type=F32)
            move_rows(2 * (n_chunks + c))
            ys_ref[:, cols] = part + jnp.dot(act[:, fh:], wo_ref[fh:, cols], preferred_element_type=F32)
            move_rows(2 * (n_chunks + c) + 1)

    def drain(cur):
        wait_gathers(cur)
        wait_gathers((cur + 1) % EXPERT_SLOTS)
        src_copy(w + 2, (cur + EXPERT_SLOTS - 1) % EXPERT_SLOTS).wait()

    for key in range(EXPERT_SLOTS):
        mine = (w % EXPERT_SLOTS) == key
        pl.when(mine & (w < n_used))(functools.partial(compute_tile, key))
        pl.when(mine & (w == n_used))(functools.partial(drain, key))

    @pl.when((w >= n_used) & (w < n_tiles))
    def _():
        ys_ref[...] = jnp.zeros_like(ys_ref)


def _experts(tile_expert, n_used, src_tab, hnp, w_ei, w_eo, *, tm):
    n_e, d, f2 = w_ei.shape
    dw = hnp.shape[1]
    n_tiles = src_tab.shape[0] - EXPERT_SLOTS
    n_steps = n_tiles + 1
    assert tm % EXPERT_DMA_GROUPS == 0 and src_tab.shape[1] == tm

    def expert_of(w, te, nu):
        return te[jnp.minimum(w, nu[0] - 1)]

    grid_spec = pltpu.PrefetchScalarGridSpec(
        num_scalar_prefetch=2,
        grid=(n_steps,),
        in_specs=[
            pl.BlockSpec(memory_space=pl.ANY),
            pl.BlockSpec(memory_space=pl.ANY),
            pl.BlockSpec((1, d, f2), lambda w, te, nu: (expert_of(w, te, nu), 0, 0)),
            pl.BlockSpec((1, f2 // 2, d), lambda w, te, nu: (expert_of(w, te, nu), 0, 0)),
        ],
        out_specs=pl.BlockSpec((tm, d), lambda w, te, nu: (jnp.minimum(w, n_tiles - 1), 0)),
        scratch_shapes=[pltpu.SMEM((EXPERT_SLOTS, tm), jnp.int32), pltpu.VMEM((EXPERT_SLOTS, tm, dw), U32),
                        pltpu.VMEM((tm, f2), F32),
                        pltpu.VMEM((d, f2), BF16), pltpu.VMEM((f2 // 2, d), BF16),
                        pltpu.SemaphoreType.DMA((EXPERT_SLOTS,)), pltpu.SemaphoreType.DMA((EXPERT_SLOTS,))],
    )
    return pl.pallas_call(
        _experts_kernel,
        grid_spec=grid_spec,
        out_shape=jax.ShapeDtypeStruct((n_tiles * tm, d), F32),
        compiler_params=_params(1),
        name="experts",
    )(tile_expert, n_used, src_tab, hnp, w_ei, w_eo)


def _expert_tiles(counts, e12, r12, n_all, tm):
    max_tiles = (2 * n_all) // tm + N_EXPERTS
    n_rows = max_tiles + EXPERT_SLOTS
    tiles_per_expert = (counts + tm - 1) // tm
    tile_end = jnp.cumsum(tiles_per_expert)
    n_used = tile_end[-1]
    row_start = (tile_end - tiles_per_expert) * tm
    t = jnp.minimum(jnp.arange(n_rows, dtype=jnp.int32), n_used - 1)
    tile_expert = jnp.sum(tile_end[None, :] <= t[:, None], axis=1).astype(jnp.int32)

    experts = jnp.arange(N_EXPERTS, dtype=jnp.int32)[:, None, None]
    pos = jnp.sum(jnp.where(e12[None] == experts, row_start[:, None, None], 0), axis=0) + r12
    token = jnp.broadcast_to(jnp.arange(n_all, dtype=jnp.int32), (2, n_all))
    src = jnp.zeros((n_rows * tm,), jnp.int32).at[pos.reshape(-1)].set(
        token.reshape(-1), unique_indices=True, mode="promise_in_bounds").reshape(n_rows, tm)
    pos_tiles = pos.reshape(2, n_all // tm, tm).transpose(1, 0, 2).reshape(n_all // tm, 2 * tm)
    return tile_expert, n_used.reshape(1).astype(jnp.int32), src, pos_tiles.astype(jnp.int32)


def _combine_kernel(pos_hbm, h_ref, rinfo_ref, g_ref, ys_hbm, yp_ref, ysm_ref,
                    idx_ref, buf_ref, idx_sem, row_sem, *, n_prompt_tiles):
    i = pl.program_id(0)
    nt = pl.num_programs(0)
    tm = h_ref.shape[0]

    def idx_copy(tile, slot):
        return pltpu.make_async_copy(pos_hbm.at[pl.ds(tile, 1), :], idx_ref.at[pl.ds(slot, 1), :],
                                     idx_sem.at[slot])

    def start_rows(slot):
        def body(r, _):
            for k in range(2):
                pltpu.make_async_copy(ys_hbm.at[pl.ds(idx_ref[slot, k * tm + r], 1), :],
                                      buf_ref.at[slot, k, pl.ds(r, 1), :], row_sem.at[slot]).start(priority=k)
            return 0
        lax.fori_loop(0, tm, body, 0, unroll=8)

    def wait_rows(slot):
        pltpu.make_async_copy(buf_ref.at[slot], buf_ref.at[slot], row_sem.at[slot]).wait()

    @pl.when(i == 0)
    def _():
        idx_copy(0, 0).start()
        idx_copy(0, 0).wait()
        start_rows(0)

        @pl.when(nt > 1)
        def _():
            idx_copy(1, 1).start()

    def tile(cur):
        nxt = 1 - cur

        @pl.when(i + 1 < nt)
        def _():
            idx_copy(i + 1, nxt).wait()
            start_rows(nxt)

        @pl.when(i + 2 < nt)
        def _():
            idx_copy(i + 2, cur).start()

        wait_rows(cur)
        rinfo = rinfo_ref[...]
        lane = lax.broadcasted_iota(jnp.int32, rinfo.shape, 1)
        w1 = _lane_pick(rinfo, lane, R_W1)
        w2 = _lane_pick(rinfo, lane, R_W2)
        hp = h_ref[...] + (w1 * buf_ref[cur, 0] + w2 * buf_ref[cur, 1])
        y = _rms(hp, g_ref[...])

        @pl.when(i < n_prompt_tiles)
        def _():
            yp_ref[...] = y

        @pl.when(i >= n_prompt_tiles)
        def _():
            ysm_ref[...] = y

    for parity in (0, 1):
        pl.when((i % 2) == parity)(functools.partial(tile, parity))


def _combine(pos_tiles, h, rinfo, norm_g, ys, *, n_prompt, tm):
    n_all, d = h.shape
    nt = n_all // tm
    npt = n_prompt // tm
    assert n_all - n_prompt == tm and pos_tiles.shape == (nt, 2 * tm)
    kern = functools.partial(_combine_kernel, n_prompt_tiles=npt)
    return pl.pallas_call(
        kern,
        grid=(nt,),
        in_specs=[
            pl.BlockSpec(memory_space=pl.ANY),
            pl.BlockSpec((tm, d), lambda i: (i, 0)),
            pl.BlockSpec((tm, LANES), lambda i: (i, 0)),
            pl.BlockSpec((1, d), lambda i: (0, 0)),
            pl.BlockSpec(memory_space=pl.ANY),
        ],
        out_specs=[
            pl.BlockSpec((tm, d), lambda i: (jnp.minimum(i, npt - 1), 0)),
            pl.BlockSpec((tm, d), lambda i: (0, 0)),
        ],
        out_shape=[jax.ShapeDtypeStruct((n_prompt, d), F32), jax.ShapeDtypeStruct((tm, d), F32)],
        scratch_shapes=[pltpu.SMEM((2, 2 * tm), jnp.int32), pltpu.VMEM((2, 2, tm, d), F32),
                        pltpu.SemaphoreType.DMA((2,)), pltpu.SemaphoreType.DMA((2,))],
        compiler_params=_params(1),
        name="combine",
    )(pos_tiles, h, rinfo, norm_g, ys)


ROW_TILE = 256
PROJ_ROWS = 1024
PROJ_COLS = 256
ATTN_Q_ROWS = 256
ATTN_HEADS_PER_STEP = 2


def kernel(x_prompt, x_sample, cache_k, cache_v, state_conv, norm_mix, w_in, conv_w, lambda_q1,
           lambda_k1, lambda_q2, lambda_k2, head_norm, w_out, norm_ffn, w_router_group,
           b_router_group, w_router_expert, b_router_expert, w_expert_in, w_expert_out, norm_final):
    depth = w_in.shape[0]
    b, t, d = x_prompt.shape
    db, dt, _ = x_sample.shape
    past = cache_k.shape[2]
    d_conv = conv_w.shape[2]
    d_attn = d - d_conv
    n_p, n_s = b * t, db * dt
    n_all = n_p + n_s
    tm = ROW_TILE

    hp = x_prompt.reshape(n_p, d)
    hs = x_sample.reshape(n_s, d)
    outs = {name: [] for name in ("kp", "vp", "cp", "ks", "vs", "cs")}
    y_p = y_s = None
    for l in range(depth):
        lam_init = 0.8 - 0.6 * math.exp(-0.3 * l)
        w_in_bf = w_in[l].astype(BF16)
        w_out_bf = w_out[l].astype(BF16)
        g_mix = norm_mix[l].reshape(1, d)
        lams = tuple(p[l].reshape(1, -1) for p in (lambda_q1, lambda_k1, lambda_q2, lambda_k2))
        hnorm = head_norm[l].reshape(1, -1)
        pad = LANES - N_GROUPS - N_EXPERTS
        w_router = jnp.concatenate([w_router_group[l], w_router_expert[l], jnp.zeros((d, pad), F32)],
                                   axis=1).astype(BF16)
        b_router = jnp.concatenate([b_router_group[l], b_router_expert[l], jnp.zeros((pad,), F32)]).reshape(1, LANES)

        zero_prev = jnp.zeros((b, CONV_WIDTH - 1, d_conv), F32)
        a_p, q_p, kb_p, kt_p, v_p, tail_p = _in_proj(hp, g_mix, w_in_bf, conv_w[l], zero_prev,
                                                     seq_len=t, tm=min(PROJ_ROWS, t), tn=PROJ_COLS)
        o_p = _attn_prompt(q_p, kb_p, v_p, lams, hnorm, batch=b, seq_len=t, lam_init=lam_init,
                           tq=min(ATTN_Q_ROWS, t))
        prev_rows = jnp.stack([jnp.repeat(state_conv[l][:, 0], dt, axis=0),
                               jnp.repeat(state_conv[l][:, 1], dt, axis=0)])
        a_s, q_s, kb_s, k_s, v_s, u_s = _in_proj(hs, g_mix, w_in_bf, conv_w[l], prev_rows,
                                                 seq_len=dt, tm=n_s, tn=2 * PROJ_COLS)
        kt_past = cache_k[l].transpose(0, 2, 3, 4, 1).reshape(db * d_attn, past)
        o_s = _attn_sample(q_s, kb_s, v_s, kt_past, cache_v[l].reshape(db * past, d_attn), lams, hnorm,
                           batch=db, seq_len=dt, past_len=past, lam_init=lam_init)

        h, hnp, rinfo, rinfo_t, cnt = _out_proj(a_p, o_p, hp, a_s, o_s, hs, w_out_bf,
                                       norm_ffn[l].reshape(1, d), w_router, b_router, tm=tm)
        counts = cnt[0, :N_EXPERTS].astype(jnp.int32)
        e12 = rinfo_t[R_E1:R_E2 + 1].astype(jnp.int32)
        r12 = rinfo_t[R_RANK1:R_RANK2 + 1].astype(jnp.int32)
        tile_expert, n_used, src_tab, pos_tiles = _expert_tiles(counts, e12, r12, n_all, tm)
        ys = _experts(tile_expert, n_used, src_tab, hnp, w_expert_in[l], w_expert_out[l], tm=tm)
        last = l == depth - 1
        g_out = norm_final.reshape(1, d) if last else jnp.ones((1, d), F32)
        y_p, y_s = _combine(pos_tiles, h, rinfo, g_out, ys, n_prompt=n_p, tm=tm)
        assert last, "multi-layer stacks need an un-normalised combine output"

        outs["kp"].append(kt_p.reshape(b, N_HEADS, 2, d_attn // (2 * N_HEADS), t).transpose(0, 4, 1, 2, 3))
        outs["vp"].append(v_p.reshape(b, t, N_HEADS, d_attn // N_HEADS))
        tails = tail_p.reshape(b, t // min(PROJ_ROWS, t), SUBLANES, d_conv)
        outs["cp"].append(tails[:, -1, SUBLANES - (CONV_WIDTH - 1):, :])
        outs["ks"].append(k_s.reshape(db, dt, N_HEADS, 2, d_attn // (2 * N_HEADS)))
        outs["vs"].append(v_s.reshape(db, dt, N_HEADS, d_attn // N_HEADS))
        outs["cs"].append(u_s.reshape(db, dt, d_conv)[:, dt - (CONV_WIDTH - 1):, :])

    return (y_p.reshape(b, t, d), y_s.reshape(db, dt, d),
            jnp.stack(outs["kp"]), jnp.stack(outs["vp"]), jnp.stack(outs["cp"]),
            jnp.stack(outs["ks"]), jnp.stack(outs["vs"]), jnp.stack(outs["cs"]))
```

```python
import functools
import math

import jax
import jax.numpy as jnp
from jax import lax
from jax.experimental import pallas as pl
from jax.experimental.pallas import tpu as pltpu

EPS = 1e-6
N_HEADS = 8
CHUNK = 64
CONV_WIDTH = 3
N_GROUPS = 4
EXPERTS_PER_GROUP = 8
N_EXPERTS = N_GROUPS * EXPERTS_PER_GROUP
LANES = 128
SUBLANES = 8
BF16_SUBLANES = 16
VMEM_LIMIT = 56 * 1024 * 1024

F32 = jnp.float32
BF16 = jnp.bfloat16
U32 = jnp.uint32
NEG_INF = float("-inf")


def _params(n_axes, vmem=VMEM_LIMIT, **kw):
    return pltpu.CompilerParams(dimension_semantics=("arbitrary",) * n_axes,
                                vmem_limit_bytes=vmem, **kw)


def _rms(x, g):
    return x * lax.rsqrt(jnp.mean(x * x, axis=-1, keepdims=True) + EPS) * g


def _lane_pick(x, lane, idx):
    return jnp.sum(jnp.where(lane == idx, x, 0.0), axis=1, keepdims=True)


def _in_proj_kernel(x_ref, g_ref, wgb_ref, wgc_ref, wx_ref, wq_ref, wk_ref, wv_ref, cw_ref, prev_ref,
                    a_ref, q_ref, kb_ref, ko_ref, v_ref, u_ref, xn_ref, carry_ref,
                    *, seq_len, tiles_per_seq, rows_span_sequences, q_scale):
    i = pl.program_id(0)
    j = pl.program_id(1)
    tm = x_ref.shape[0]

    @pl.when(j == 0)
    def _():
        xn_ref[...] = _rms(x_ref[...], g_ref[...]).astype(BF16)

    xn = xn_ref[...]

    gate_b = jnp.dot(xn, wgb_ref[...], preferred_element_type=F32)
    u = (jnp.dot(xn, wgc_ref[...], preferred_element_type=F32)
         * jnp.dot(xn, wx_ref[...], preferred_element_type=F32))
    row = lax.broadcasted_iota(jnp.int32, u.shape, 0)
    if rows_span_sequences:
        rmod = row % seq_len
        pm2, pm1 = prev_ref[0], prev_ref[1]
        first, second = rmod == 0, rmod == 1
        u_ref[...] = u
    else:
        seq_start = (i % tiles_per_seq) == 0
        tail = carry_ref[j]
        pm2 = jnp.where(seq_start, prev_ref[0, 0:1, :], tail[SUBLANES - 2:SUBLANES - 1, :])
        pm1 = jnp.where(seq_start, prev_ref[0, 1:2, :], tail[SUBLANES - 1:SUBLANES, :])
        first, second = row == 0, row == 1
        new_tail = u[tm - SUBLANES:, :]
        carry_ref[j] = new_tail
        u_ref[0] = new_tail
    u1 = jnp.where(first, pm1, pltpu.roll(u, 1, 0))
    u2 = jnp.where(first, pm2, jnp.where(second, pm1, pltpu.roll(u, 2, 0)))
    cw = cw_ref[...]
    y = cw[0:1, :] * u2 + cw[1:2, :] * u1 + cw[2:3, :] * u
    a_ref[...] = (gate_b * y).astype(BF16)

    q_ref[...] = (jnp.dot(xn, wq_ref[...], preferred_element_type=F32) * q_scale).astype(BF16)
    k = jnp.dot(xn, wk_ref[...], preferred_element_type=F32)
    kb_ref[...] = k.astype(BF16)
    if rows_span_sequences:
        ko_ref[...] = k
    else:
        ko_ref[0] = k.T
    v_ref[...] = jnp.dot(xn, wv_ref[...], preferred_element_type=F32)


def _in_proj(x2d, norm_g, w_in_bf, conv_w, prev, *, seq_len, tm, tn):
    n, d = x2d.shape
    d_conv = conv_w.shape[1]
    assert w_in_bf.shape == (d, 6 * d_conv), "conv and attention widths must match"
    assert n % tm == 0 and d_conv % tn == 0
    nj = d_conv // tn
    n_i = n // tm
    rows_span_sequences = tm > seq_len
    if rows_span_sequences:
        assert tm % seq_len == 0 and prev.shape == (2, n, d_conv)
        tiles_per_seq = 1
        prev_spec = pl.BlockSpec((2, tm, tn), lambda i, j: (0, i, j))
        u_shape = jax.ShapeDtypeStruct((n, d_conv), F32)
        u_spec = pl.BlockSpec((tm, tn), lambda i, j: (i, j))
        ko_shape = jax.ShapeDtypeStruct((n, d_conv), F32)
        ko_spec = pl.BlockSpec((tm, tn), lambda i, j: (i, j))
    else:
        assert seq_len % tm == 0 and prev.shape == (n // seq_len, CONV_WIDTH - 1, d_conv)
        tiles_per_seq = seq_len // tm
        prev_spec = pl.BlockSpec((1, CONV_WIDTH - 1, tn), lambda i, j: (i // tiles_per_seq, 0, j))
        u_shape = jax.ShapeDtypeStruct((n_i, SUBLANES, d_conv), F32)
        u_spec = pl.BlockSpec((1, SUBLANES, tn), lambda i, j: (i, 0, j))
        ko_shape = jax.ShapeDtypeStruct((n // seq_len, d_conv, seq_len), F32)
        ko_spec = pl.BlockSpec((1, tn, tm), lambda i, j: (i // tiles_per_seq, j, i % tiles_per_seq))

    def wspec(k):
        return pl.BlockSpec((d, tn), lambda i, j: (0, k * nj + j))

    tile = pl.BlockSpec((tm, tn), lambda i, j: (i, j))
    head_dim = d_conv // (2 * N_HEADS)
    kern = functools.partial(_in_proj_kernel, seq_len=seq_len, tiles_per_seq=tiles_per_seq,
                             rows_span_sequences=rows_span_sequences,
                             q_scale=head_dim ** -0.5 * math.log2(math.e))
    return pl.pallas_call(
        kern,
        grid=(n_i, nj),
        in_specs=[
            pl.BlockSpec((tm, d), lambda i, j: (i, 0)),
            pl.BlockSpec((1, d), lambda i, j: (0, 0)),
            wspec(0), wspec(1), wspec(2), wspec(3), wspec(4), wspec(5),
            pl.BlockSpec((CONV_WIDTH, tn), lambda i, j: (0, j)),
            prev_spec,
        ],
        out_specs=[tile, tile, tile, ko_spec, tile, u_spec],
        out_shape=[
            jax.ShapeDtypeStruct((n, d_conv), BF16),
            jax.ShapeDtypeStruct((n, d_conv), BF16),
            jax.ShapeDtypeStruct((n, d_conv), BF16),
            ko_shape,
            jax.ShapeDtypeStruct((n, d_conv), F32),
            u_shape,
        ],
        scratch_shapes=[pltpu.VMEM((tm, d), BF16), pltpu.VMEM((nj, SUBLANES, tn), F32)],
        compiler_params=_params(2),
        name="in_proj",
    )(x2d, norm_g, *([w_in_bf] * 6), conv_w, prev)


def _diff_lambda(lq1_ref, lk1_ref, lq2_ref, lk2_ref, lam_init):
    s1 = jnp.sum(lq1_ref[...] * lk1_ref[...], axis=1, keepdims=True)
    s2 = jnp.sum(lq2_ref[...] * lk2_ref[...], axis=1, keepdims=True)
    return jnp.exp(s1) - jnp.exp(s2) + lam_init


def _stacked_queries(q, head_dim):
    lane = lax.broadcasted_iota(jnp.int32, q.shape, 1)
    zero = jnp.zeros_like(q)
    return jnp.concatenate([jnp.where(lane < head_dim, q, zero),
                            jnp.where(lane >= head_dim, q, zero)], axis=0)


def _scores(qq, kblk):
    return lax.dot_general(qq, kblk, (((1,), (1,)), ((), ())), preferred_element_type=F32)


def _finish_heads(acc, l, lam, hn, lam_init, tq):
    o = acc[:tq] / l[:tq] - lam * (acc[tq:] / l[tq:])
    return (_rms(o, hn) * (1.0 - lam_init)).astype(BF16)


def _attn_prompt_kernel(lq1_ref, lk1_ref, lq2_ref, lk2_ref, hn_ref, q_ref, k_ref, v_ref, o_ref,
                        qt1_ref, qt2_ref, vt_ref, s0_ref, s1_ref, p0_ref, p1_ref,
                        *, lam_init, head_dim, tq):
    seq_len = q_ref.shape[0]
    hw = 2 * head_dim
    lam = _diff_lambda(lq1_ref, lk1_ref, lq2_ref, lk2_ref, lam_init)

    def prepare(h):
        cols = slice(h * hw, (h + 1) * hw)
        qt = q_ref[:, cols].astype(F32).T
        dim = lax.broadcasted_iota(jnp.int32, qt.shape, 0)
        qt1_ref[h] = jnp.where(dim < head_dim, qt, 0.0).astype(BF16)
        qt2_ref[h] = jnp.where(dim >= head_dim, qt, 0.0).astype(BF16)
        vt_ref[h, :hw, :] = v_ref[:, cols].T.astype(BF16)
        vt_ref[h, hw:, :] = jnp.ones((vt_ref.shape[1] - hw, seq_len), BF16)

    key_chunk = lax.broadcasted_iota(jnp.int32, (tq, 2 * tq), 0) // CHUNK
    qry_chunk = (lax.broadcasted_iota(jnp.int32, (tq, 2 * tq), 1) % tq) // CHUNK
    visible = key_chunk <= qry_chunk

    bufs = ((s0_ref, p0_ref), (s1_ref, p1_ref))
    n_blocks = seq_len // tq
    blocks = [(h, qi) for h in range(q_ref.shape[1] // hw) for qi in range(n_blocks)]

    def scores(idx):
        h, qi = blocks[idx]
        cols = slice(h * hw, (h + 1) * hw)
        q0, n_keys, s_buf = qi * tq, (qi + 1) * tq, bufs[idx % 2][0]
        qqt = jnp.concatenate([qt1_ref[h, :, q0:n_keys], qt2_ref[h, :, q0:n_keys]], axis=1)
        s = jnp.dot(k_ref[q0:n_keys, cols], qqt, preferred_element_type=F32)
        s = jnp.where(visible, s, NEG_INF)
        s_buf[q0:n_keys, :] = s
        m = jnp.max(s, axis=0, keepdims=True)
        if qi > 0:
            s = jnp.dot(k_ref[:q0, cols], qqt, preferred_element_type=F32)
            s_buf[:q0, :] = s
            m = jnp.maximum(m, jnp.max(s, axis=0, keepdims=True))
        return m

    def finish(idx, m):
        h, qi = blocks[idx]
        q0, n_keys = qi * tq, (qi + 1) * tq
        s_buf, p_buf = bufs[idx % 2]
        p_buf[:n_keys, :] = jnp.exp2(s_buf[:n_keys, :] - m).astype(BF16)
        ot = jnp.dot(vt_ref[h, :, :n_keys], p_buf[:n_keys, :], preferred_element_type=F32)
        ot = ot[:hw] / ot[hw:hw + 1]
        o = (ot[:, :tq] - lam * ot[:, tq:]).T
        o_ref[q0:n_keys, h * hw:(h + 1) * hw] = (_rms(o, hn_ref[...]) * (1.0 - lam_init)).astype(BF16)

    prepare(0)
    m = scores(0)
    for idx in range(len(blocks)):
        m_next = None
        if idx + 1 < len(blocks):
            if blocks[idx + 1][1] == 0:
                prepare(blocks[idx + 1][0])
            m_next = scores(idx + 1)
        finish(idx, m)
        m = m_next


def _attn_prompt(q, k, v, lams, head_norm, *, batch, seq_len, lam_init, tq):
    n, d_attn = q.shape
    hw = d_attn // N_HEADS
    nh = ATTN_HEADS_PER_STEP
    assert seq_len % tq == 0 and tq % CHUNK == 0 and N_HEADS % nh == 0
    small = pl.BlockSpec((1, hw // 2), lambda b, h: (0, 0))
    head = pl.BlockSpec((seq_len, nh * hw), lambda b, h: (b, h))
    kern = functools.partial(_attn_prompt_kernel, lam_init=lam_init, head_dim=hw // 2, tq=tq)
    return pl.pallas_call(
        kern,
        grid=(batch, N_HEADS // nh),
        in_specs=[small, small, small, small, pl.BlockSpec((1, hw), lambda b, h: (0, 0)),
                  head, head, head],
        out_specs=head,
        out_shape=jax.ShapeDtypeStruct((n, d_attn), BF16),
        scratch_shapes=[pltpu.VMEM((nh, hw, seq_len), BF16), pltpu.VMEM((nh, hw, seq_len), BF16),
                        pltpu.VMEM((nh, hw + BF16_SUBLANES, seq_len), BF16),
                        pltpu.VMEM((seq_len, 2 * tq), F32), pltpu.VMEM((seq_len, 2 * tq), F32),
                        pltpu.VMEM((seq_len, 2 * tq), BF16), pltpu.VMEM((seq_len, 2 * tq), BF16)],
        compiler_params=_params(2),
        name="attn_prompt",
    )(*lams, head_norm, q, k, v)


def _attn_sample_kernel(lq1_ref, lk1_ref, lq2_ref, lk2_ref, hn_ref, q_ref, kc_ref, vc_ref,
                        kn_ref, vn_ref, o_ref, *, lam_init, head_dim):
    tq = q_ref.shape[0]
    hw = 2 * head_dim
    lam = _diff_lambda(lq1_ref, lk1_ref, lq2_ref, lk2_ref, lam_init)
    for h in range(q_ref.shape[1] // hw):
        cols = slice(h * hw, (h + 1) * hw)
        qq = _stacked_queries(q_ref[:, cols], head_dim)
        s_past = jnp.dot(qq, kc_ref[cols, :].astype(BF16), preferred_element_type=F32)
        s_new = _scores(qq, kn_ref[:, cols])
        m = jnp.maximum(jnp.max(s_past, axis=1, keepdims=True), jnp.max(s_new, axis=1, keepdims=True))
        p_past = jnp.exp2(s_past - m)
        p_new = jnp.exp2(s_new - m)
        l = jnp.sum(p_past, axis=1, keepdims=True) + jnp.sum(p_new, axis=1, keepdims=True)
        acc = (jnp.dot(p_past.astype(BF16), vc_ref[:, cols].astype(BF16), preferred_element_type=F32)
               + jnp.dot(p_new.astype(BF16), vn_ref[:, cols].astype(BF16), preferred_element_type=F32))
        o_ref[:, cols] = _finish_heads(acc, l, lam, hn_ref[...], lam_init, tq)


def _attn_sample(q, k_new, v_new, k_past, v_past, lams, head_norm, *, batch, seq_len, past_len, lam_init):
    n, d_attn = q.shape
    hw = d_attn // N_HEADS
    small = pl.BlockSpec((1, hw // 2), lambda b: (0, 0))
    new_spec = pl.BlockSpec((seq_len, d_attn), lambda b: (b, 0))
    past_spec = pl.BlockSpec((past_len, d_attn), lambda b: (b, 0))
    past_t_spec = pl.BlockSpec((d_attn, past_len), lambda b: (b, 0))
    kern = functools.partial(_attn_sample_kernel, lam_init=lam_init, head_dim=hw // 2)
    return pl.pallas_call(
        kern,
        grid=(batch,),
        in_specs=[small, small, small, small, pl.BlockSpec((1, hw), lambda b: (0, 0)),
                  new_spec, past_t_spec, past_spec, new_spec, new_spec],
        out_specs=new_spec,
        out_shape=jax.ShapeDtypeStruct((n, d_attn), BF16),
        compiler_params=_params(1),
        name="attn_sample",
    )(*lams, head_norm, q, k_past, v_past, k_new, v_new)


R_E1, R_E2, R_W1, R_W2, R_RANK1, R_RANK2 = 0, 1, 2, 3, 4, 5
OUT_PROJ_CHUNKS = 4


def _pack_bf16_pairs(lo_bf, hi_bf):
    lo = lax.bitcast_convert_type(lo_bf.astype(F32), U32)
    hi = lax.bitcast_convert_type(hi_bf.astype(F32), U32)
    return (hi & jnp.uint32(0xFFFF0000)) | (lo >> 16)


def _unpack_bf16_pairs(w):
    lo = lax.bitcast_convert_type(w << 16, F32).astype(BF16)
    hi = lax.bitcast_convert_type(w & jnp.uint32(0xFFFF0000), F32).astype(BF16)
    return lo, hi


def _out_proj_kernel(*refs, n_prompt_tiles):
    ap_ref, op_ref, xp_ref, as_ref, os_ref, xs_ref = refs[:6]
    n_w = 2 * OUT_PROJ_CHUNKS
    w_refs = (refs[6:6 + OUT_PROJ_CHUNKS], refs[6 + OUT_PROJ_CHUNKS:6 + n_w])
    g_ref, wr_ref, br_ref, h_ref, hnp_ref, rinfo_ref, rinfo_t_ref, cnt_ref = refs[6 + n_w:-5]
    hbufs, sss, carry_ref = refs[-5:-3], refs[-3:-1], refs[-1]
    i = pl.program_id(0)
    n_tiles = n_prompt_tiles + 1

    @pl.when(i == 0)
    def _():
        carry_ref[...] = jnp.zeros_like(carry_ref)

    def step(slot, inputs, route):
        streams = []
        if inputs is not None:
            streams.append(_out_proj_project(*inputs, w_refs, hbufs[slot], sss[slot]))
        if route:
            streams.append(_out_proj_route(hbufs[1 - slot], sss[1 - slot], g_ref, wr_ref, br_ref,
                                           h_ref, hnp_ref, rinfo_ref, rinfo_t_ref, cnt_ref, carry_ref))
        while streams:
            streams = [s for s in streams if next(s, None) is not None]

    prompt, sample = (ap_ref, op_ref, xp_ref), (as_ref, os_ref, xs_ref)
    pl.when(i == 0)(functools.partial(step, 0, prompt, False))
    for parity in (0, 1):
        pl.when((i >= 1) & (i < n_prompt_tiles) & (i % 2 == parity))(
            functools.partial(step, parity, prompt, True))
    pl.when(i == n_prompt_tiles)(functools.partial(step, n_prompt_tiles % 2, sample, True))
    pl.when(i == n_tiles)(functools.partial(step, n_tiles % 2, None, True))


def _out_proj_project(a_ref, o_ref, x_ref, w_refs, h_ref, ss_ref):
    tm, d = x_ref.shape
    cw = d // OUT_PROJ_CHUNKS
    a, o = a_ref[...], o_ref[...]
    ss = jnp.zeros((tm, 1), F32)
    for c in range(OUT_PROJ_CHUNKS):
        cols = slice(c * cw, (c + 1) * cw)
        h = x_ref[:, cols] + jnp.dot(a, w_refs[0][c][...], preferred_element_type=F32)
        yield True
        h = h + jnp.dot(o, w_refs[1][c][...], preferred_element_type=F32)
        h_ref[:, cols] = h
        ss = ss + jnp.sum(h * h, axis=1, keepdims=True)
        yield True
    ss_ref[...] = jnp.broadcast_to(ss, ss_ref.shape)


def _out_proj_route(hb_ref, ss_ref, g_ref, wr_ref, br_ref,
                    h_ref, hnp_ref, rinfo_ref, rinfo_t_ref, cnt_ref, carry_ref):
    tm, d = hb_ref.shape
    cw = d // OUT_PROJ_CHUNKS
    h_ref[...] = hb_ref[...]
    inv = lax.rsqrt(ss_ref[:, 0:1] / d + EPS)
    yield True

    logits = br_ref[...]
    for c in range(OUT_PROJ_CHUNKS // 2):
        halves = []
        for cc in (c, c + OUT_PROJ_CHUNKS // 2):
            cols = slice(cc * cw, (cc + 1) * cw)
            hb = (hb_ref[:, cols] * inv * g_ref[:, cols]).astype(BF16)
            logits = logits + jnp.dot(hb, wr_ref[cols, :], preferred_element_type=F32)
            halves.append(hb)
        hnp_ref[:, c * cw:(c + 1) * cw] = _pack_bf16_pairs(*halves)
        yield True

    lane = lax.broadcasted_iota(jnp.int32, logits.shape, 1)
    lane_f = lane.astype(F32)
    lg = jnp.where(lane < N_GROUPS, logits, NEG_INF)
    mg = jnp.max(lg, axis=1, keepdims=True)
    grp = jnp.min(jnp.where(lg == mg, lane_f, float(LANES)), axis=1, keepdims=True)
    gate_g = 1.0 / jnp.sum(jnp.exp(lg - mg), axis=1, keepdims=True)
    yield True
    ex = lane - N_GROUPS
    in_grp = (ex >= 0) & (ex < N_EXPERTS) & ((ex >> 3).astype(F32) == grp)
    le = jnp.where(in_grp, logits, NEG_INF)
    v1 = jnp.max(le, axis=1, keepdims=True)
    i1 = jnp.min(jnp.where(le == v1, lane_f, float(LANES)), axis=1, keepdims=True)
    yield True
    le2 = jnp.where(lane_f == i1, NEG_INF, le)
    v2 = jnp.max(le2, axis=1, keepdims=True)
    i2 = jnp.min(jnp.where(le2 == v2, lane_f, float(LANES)), axis=1, keepdims=True)
    e21 = jnp.exp(v2 - v1)
    w1 = gate_g / (1.0 + e21)
    w2 = gate_g * e21 / (1.0 + e21)
    e1 = i1 - float(N_GROUPS)
    e2 = i2 - float(N_GROUPS)
    yield True

    onehot = jnp.where((lane_f == e1) | (lane_f == e2), 1.0, 0.0)
    r = lax.broadcasted_iota(jnp.int32, (tm, tm), 0)
    c = lax.broadcasted_iota(jnp.int32, (tm, tm), 1)
    earlier = jnp.where(c < r, 1.0, 0.0).astype(BF16)
    before = jnp.dot(earlier, onehot.astype(BF16), preferred_element_type=F32) + carry_ref[0:1, :]
    rank1 = _lane_pick(before, lane_f, e1)
    rank2 = _lane_pick(before, lane_f, e2)
    new_carry = carry_ref[0:1, :] + jnp.sum(onehot, axis=0, keepdims=True)
    carry_ref[...] = jnp.broadcast_to(new_carry, carry_ref.shape)
    cnt_ref[...] = jnp.broadcast_to(new_carry, cnt_ref.shape)
    yield True

    rec = jnp.zeros_like(logits)
    for idx, val in ((R_E1, e1), (R_E2, e2), (R_W1, w1), (R_W2, w2), (R_RANK1, rank1), (R_RANK2, rank2)):
        rec = jnp.where(lane == idx, val, rec)
    rinfo_ref[...] = rec
    rinfo_t_ref[...] = rec.T[:SUBLANES, :]


def _out_proj(a_p, o_p, x_p, a_s, o_s, x_s, w_out_bf, norm_g, w_router, b_router, *, tm):
    n_p, d = x_p.shape
    n_s = x_s.shape[0]
    dc = a_p.shape[1]
    assert n_p % tm == 0 and n_s == tm, "sample rows must fill exactly one tile"
    assert w_out_bf.shape == (2 * dc, d)
    npt = n_p // tm
    n_all = n_p + n_s
    cw = d // OUT_PROJ_CHUNKS
    p_idx = lambda i: (jnp.minimum(i, npt - 1), 0)
    zero = lambda i: (0, 0)

    def wspec(half, c):
        return pl.BlockSpec((dc, cw), lambda i: (half, c))

    kern = functools.partial(_out_proj_kernel, n_prompt_tiles=npt)
    done = lambda i: (jnp.maximum(i - 1, 0), 0)
    return pl.pallas_call(
        kern,
        grid=(npt + 2,),
        in_specs=[
            pl.BlockSpec((tm, dc), p_idx), pl.BlockSpec((tm, dc), p_idx), pl.BlockSpec((tm, d), p_idx),
            pl.BlockSpec((tm, dc), zero), pl.BlockSpec((tm, dc), zero), pl.BlockSpec((tm, d), zero),
            *[wspec(half, c) for half in range(2) for c in range(OUT_PROJ_CHUNKS)],
            pl.BlockSpec((1, d), zero),
            pl.BlockSpec((d, LANES), zero), pl.BlockSpec((1, LANES), zero),
        ],
        out_specs=[
            pl.BlockSpec((tm, d), done),
            pl.BlockSpec((tm, d // 2), done),
            pl.BlockSpec((tm, LANES), done),
            pl.BlockSpec((SUBLANES, tm), lambda i: (0, jnp.maximum(i - 1, 0))),
            pl.BlockSpec((SUBLANES, LANES), zero),
        ],
        out_shape=[
            jax.ShapeDtypeStruct((n_all, d), F32),
            jax.ShapeDtypeStruct((n_all, d // 2), U32),
            jax.ShapeDtypeStruct((n_all, LANES), F32),
            jax.ShapeDtypeStruct((SUBLANES, n_all), F32),
            jax.ShapeDtypeStruct((SUBLANES, LANES), F32),
        ],
        scratch_shapes=[pltpu.VMEM((tm, d), F32), pltpu.VMEM((tm, d), F32),
                        pltpu.VMEM((tm, LANES), F32), pltpu.VMEM((tm, LANES), F32),
                        pltpu.VMEM((SUBLANES, LANES), F32)],
        compiler_params=_params(1),
        name="out_proj",
    )(a_p, o_p, x_p, a_s, o_s, x_s, *([w_out_bf] * (2 * OUT_PROJ_CHUNKS)), norm_g, w_router, b_router)


EXPERT_DMA_GROUPS = 16
EXPERT_SLOTS = 3


def _experts_kernel(tile_expert_ref, n_used_ref, src_hbm, hnp_hbm, wi_f32_ref, wo_f32_ref, ys_ref,
                    src_ref, xs_ref, gu_ref, wi_ref, wo_ref, src_sem, gather_sem):
    w = pl.program_id(0)
    n_used = n_used_ref[0]
    tm = xs_ref.shape[1]
    n_tiles = pl.num_programs(0) - 1
    group = tm // EXPERT_DMA_GROUPS

    def src_copy(tile, slot):
        return pltpu.make_async_copy(src_hbm.at[pl.ds(tile, 1), :], src_ref.at[pl.ds(slot, 1), :],
                                     src_sem.at[slot])

    def gather_row(slot, r):
        return pltpu.make_async_copy(hnp_hbm.at[pl.ds(src_ref[slot, r], 1), :],
                                     xs_ref.at[slot, pl.ds(r, 1), :], gather_sem.at[slot])

    def wait_gathers(slot):
        pltpu.make_async_copy(xs_ref.at[slot], xs_ref.at[slot], gather_sem.at[slot]).wait()

    @pl.when(w == 0)
    def _():
        for t in range(EXPERT_SLOTS):
            src_copy(t, t).start()
        for t in range(EXPERT_SLOTS - 1):
            src_copy(t, t).wait()
            for r in range(tm):
                gather_row(t, r).start(priority=r % 2)

    def compute_tile(cur):
        ahead = (cur + EXPERT_SLOTS - 1) % EXPERT_SLOTS
        wait_gathers(cur)
        src_copy(w + 2, ahead).wait()
        src_copy(w + 3, cur).start()

        def move_rows(g):
            for r in range(g * group, (g + 1) * group):
                gather_row(ahead, r).start(priority=r % 2)

        @pl.when((w == 0) | (tile_expert_ref[w] != tile_expert_ref[jnp.maximum(w - 1, 0)]))
        def _():
            wi_ref[...] = wi_f32_ref[0].astype(BF16)
            wo_ref[...] = wo_f32_ref[0].astype(BF16)

        lo, hi = _unpack_bf16_pairs(xs_ref[cur])
        half = lo.shape[1]
        n_chunks = EXPERT_DMA_GROUPS // 4
        f2 = wi_ref.shape[1]
        cw = f2 // n_chunks
        for c in range(n_chunks):
            cols = slice(c * cw, (c + 1) * cw)
            part = jnp.dot(lo, wi_ref[:half, cols], preferred_element_type=F32)
            move_rows(2 * c)
            gu_ref[:, cols] = part + jnp.dot(hi, wi_ref[half:, cols], preferred_element_type=F32)
            move_rows(2 * c + 1)
        gate, up = gu_ref[:, :f2 // 2], gu_ref[:, f2 // 2:]
        act = (gate * jax.nn.sigmoid(gate) * up).astype(BF16)
        fh = act.shape[1] // 2
        d = wo_ref.shape[1]
        ow = d // n_chunks
        for c in range(n_chunks):
            cols = slice(c * ow, (c + 1) * ow)
            part = jnp.dot(act[:, :fh], wo_ref[:fh, cols], preferred_element_type=F32)
            move_rows(2 * (n_chunks + c))
            ys_ref[:, cols] = part + jnp.dot(act[:, fh:], wo_ref[fh:, cols], preferred_element_type=F32)
            move_rows(2 * (n_chunks + c) + 1)

    def drain(cur):
        wait_gathers(cur)
        wait_gathers((cur + 1) % EXPERT_SLOTS)
        src_copy(w + 2, (cur + EXPERT_SLOTS - 1) % EXPERT_SLOTS).wait()

    for key in range(EXPERT_SLOTS):
        mine = (w % EXPERT_SLOTS) == key
        pl.when(mine & (w < n_used))(functools.partial(compute_tile, key))
        pl.when(mine & (w == n_used))(functools.partial(drain, key))

    @pl.when((w >= n_used) & (w < n_tiles))
    def _():
        ys_ref[...] = jnp.zeros_like(ys_ref)


def _experts(tile_expert, n_used, src_tab, hnp, w_ei, w_eo, *, tm):
    n_e, d, f2 = w_ei.shape
    dw = hnp.shape[1]
    n_tiles = src_tab.shape[0] - EXPERT_SLOTS
    n_steps = n_tiles + 1
    assert tm % EXPERT_DMA_GROUPS == 0 and src_tab.shape[1] == tm

    def expert_of(w, te, nu):
        return te[jnp.minimum(w, nu[0] - 1)]

    grid_spec = pltpu.PrefetchScalarGridSpec(
        num_scalar_prefetch=2,
        grid=(n_steps,),
        in_specs=[
            pl.BlockSpec(memory_space=pl.ANY),
            pl.BlockSpec(memory_space=pl.ANY),
            pl.BlockSpec((1, d, f2), lambda w, te, nu: (expert_of(w, te, nu), 0, 0)),
            pl.BlockSpec((1, f2 // 2, d), lambda w, te, nu: (expert_of(w, te, nu), 0, 0)),
        ],
        out_specs=pl.BlockSpec((tm, d), lambda w, te, nu: (jnp.minimum(w, n_tiles - 1), 0)),
        scratch_shapes=[pltpu.SMEM((EXPERT_SLOTS, tm), jnp.int32), pltpu.VMEM((EXPERT_SLOTS, tm, dw), U32),
                        pltpu.VMEM((tm, f2), F32),
                        pltpu.VMEM((d, f2), BF16), pltpu.VMEM((f2 // 2, d), BF16),
                        pltpu.SemaphoreType.DMA((EXPERT_SLOTS,)), pltpu.SemaphoreType.DMA((EXPERT_SLOTS,))],
    )
    return pl.pallas_call(
        _experts_kernel,
        grid_spec=grid_spec,
        out_shape=jax.ShapeDtypeStruct((n_tiles * tm, d), F32),
        compiler_params=_params(1),
        name="experts",
    )(tile_expert, n_used, src_tab, hnp, w_ei, w_eo)


def _expert_tiles(counts, e12, r12, n_all, tm):
    max_tiles = (2 * n_all) // tm + N_EXPERTS
    n_rows = max_tiles + EXPERT_SLOTS
    tiles_per_expert = (counts + tm - 1) // tm
    tile_end = jnp.cumsum(tiles_per_expert)
    n_used = tile_end[-1]
    row_start = (tile_end - tiles_per_expert) * tm
    t = jnp.minimum(jnp.arange(n_rows, dtype=jnp.int32), n_used - 1)
    tile_expert = jnp.sum(tile_end[None, :] <= t[:, None], axis=1).astype(jnp.int32)

    experts = jnp.arange(N_EXPERTS, dtype=jnp.int32)[:, None, None]
    pos = jnp.sum(jnp.where(e12[None] == experts, row_start[:, None, None], 0), axis=0) + r12
    token = jnp.broadcast_to(jnp.arange(n_all, dtype=jnp.int32), (2, n_all))
    src = jnp.zeros((n_rows * tm,), jnp.int32).at[pos.reshape(-1)].set(
        token.reshape(-1), unique_indices=True, mode="promise_in_bounds").reshape(n_rows, tm)
    pos_tiles = pos.reshape(2, n_all // tm, tm).transpose(1, 0, 2).reshape(n_all // tm, 2 * tm)
    return tile_expert, n_used.reshape(1).astype(jnp.int32), src, pos_tiles.astype(jnp.int32)


def _combine_kernel(pos_hbm, h_ref, rinfo_ref, g_ref, ys_hbm, yp_ref, ysm_ref,
                    idx_ref, buf_ref, idx_sem, row_sem, *, n_prompt_tiles):
    i = pl.program_id(0)
    nt = pl.num_programs(0)
    tm = h_ref.shape[0]

    def idx_copy(tile, slot):
        return pltpu.make_async_copy(pos_hbm.at[pl.ds(tile, 1), :], idx_ref.at[pl.ds(slot, 1), :],
                                     idx_sem.at[slot])

    def start_rows(slot):
        def body(r, _):
            for k in range(2):
                pltpu.make_async_copy(ys_hbm.at[pl.ds(idx_ref[slot, k * tm + r], 1), :],
                                      buf_ref.at[slot, k, pl.ds(r, 1), :], row_sem.at[slot]).start(priority=k)
            return 0
        lax.fori_loop(0, tm, body, 0, unroll=8)

    def wait_rows(slot):
        pltpu.make_async_copy(buf_ref.at[slot], buf_ref.at[slot], row_sem.at[slot]).wait()

    @pl.when(i == 0)
    def _():
        idx_copy(0, 0).start()
        idx_copy(0, 0).wait()
        start_rows(0)

        @pl.when(nt > 1)
        def _():
            idx_copy(1, 1).start()

    def tile(cur):
        nxt = 1 - cur

        @pl.when(i + 1 < nt)
        def _():
            idx_copy(i + 1, nxt).wait()
            start_rows(nxt)

        @pl.when(i + 2 < nt)
        def _():
            idx_copy(i + 2, cur).start()

        wait_rows(cur)
        rinfo = rinfo_ref[...]
        lane = lax.broadcasted_iota(jnp.int32, rinfo.shape, 1)
        w1 = _lane_pick(rinfo, lane, R_W1)
        w2 = _lane_pick(rinfo, lane, R_W2)
        hp = h_ref[...] + (w1 * buf_ref[cur, 0] + w2 * buf_ref[cur, 1])
        y = _rms(hp, g_ref[...])

        @pl.when(i < n_prompt_tiles)
        def _():
            yp_ref[...] = y

        @pl.when(i >= n_prompt_tiles)
        def _():
            ysm_ref[...] = y

    for parity in (0, 1):
        pl.when((i % 2) == parity)(functools.partial(tile, parity))


def _combine(pos_tiles, h, rinfo, norm_g, ys, *, n_prompt, tm):
    n_all, d = h.shape
    nt = n_all // tm
    npt = n_prompt // tm
    assert n_all - n_prompt == tm and pos_tiles.shape == (nt, 2 * tm)
    kern = functools.partial(_combine_kernel, n_prompt_tiles=npt)
    return pl.pallas_call(
        kern,
        grid=(nt,),
        in_specs=[
            pl.BlockSpec(memory_space=pl.ANY),
            pl.BlockSpec((tm, d), lambda i: (i, 0)),
            pl.BlockSpec((tm, LANES), lambda i: (i, 0)),
            pl.BlockSpec((1, d), lambda i: (0, 0)),
            pl.BlockSpec(memory_space=pl.ANY),
        ],
        out_specs=[
            pl.BlockSpec((tm, d), lambda i: (jnp.minimum(i, npt - 1), 0)),
            pl.BlockSpec((tm, d), lambda i: (0, 0)),
        ],
        out_shape=[jax.ShapeDtypeStruct((n_prompt, d), F32), jax.ShapeDtypeStruct((tm, d), F32)],
        scratch_shapes=[pltpu.SMEM((2, 2 * tm), jnp.int32), pltpu.VMEM((2, 2, tm, d), F32),
                        pltpu.SemaphoreType.DMA((2,)), pltpu.SemaphoreType.DMA((2,))],
        compiler_params=_params(1),
        name="combine",
    )(pos_tiles, h, rinfo, norm_g, ys)


ROW_TILE = 256
PROJ_ROWS = 1024
PROJ_COLS = 256
ATTN_Q_ROWS = 256
ATTN_HEADS_PER_STEP = 2


def kernel(x_prompt, x_sample, cache_k, cache_v, state_conv, norm_mix, w_in, conv_w, lambda_q1,
           lambda_k1, lambda_q2, lambda_k2, head_norm, w_out, norm_ffn, w_router_group,
           b_router_group, w_router_expert, b_router_expert, w_expert_in, w_expert_out, norm_final):
    depth = w_in.shape[0]
    b, t, d = x_prompt.shape
    db, dt, _ = x_sample.shape
    past = cache_k.shape[2]
    d_conv = conv_w.shape[2]
    d_attn = d - d_conv
    n_p, n_s = b * t, db * dt
    n_all = n_p + n_s
    tm = ROW_TILE

    hp = x_prompt.reshape(n_p, d)
    hs = x_sample.reshape(n_s, d)
    outs = {name: [] for name in ("kp", "vp", "cp", "ks", "vs", "cs")}
    y_p = y_s = None
    for l in range(depth):
        lam_init = 0.8 - 0.6 * math.exp(-0.3 * l)
        w_in_bf = w_in[l].astype(BF16)
        w_out_bf = w_out[l].astype(BF16)
        g_mix = norm_mix[l].reshape(1, d)
        lams = tuple(p[l].reshape(1, -1) for p in (lambda_q1, lambda_k1, lambda_q2, lambda_k2))
        hnorm = head_norm[l].reshape(1, -1)
        pad = LANES - N_GROUPS - N_EXPERTS
        w_router = jnp.concatenate([w_router_group[l], w_router_expert[l], jnp.zeros((d, pad), F32)],
                                   axis=1).astype(BF16)
        b_router = jnp.concatenate([b_router_group[l], b_router_expert[l], jnp.zeros((pad,), F32)]).reshape(1, LANES)

        zero_prev = jnp.zeros((b, CONV_WIDTH - 1, d_conv), F32)
        a_p, q_p, kb_p, kt_p, v_p, tail_p = _in_proj(hp, g_mix, w_in_bf, conv_w[l], zero_prev,
                                                     seq_len=t, tm=min(PROJ_ROWS, t), tn=PROJ_COLS)
        o_p = _attn_prompt(q_p, kb_p, v_p, lams, hnorm, batch=b, seq_len=t, lam_init=lam_init,
                           tq=min(ATTN_Q_ROWS, t))
        prev_rows = jnp.stack([jnp.repeat(state_conv[l][:, 0], dt, axis=0),
                               jnp.repeat(state_conv[l][:, 1], dt, axis=0)])
        a_s, q_s, kb_s, k_s, v_s, u_s = _in_proj(hs, g_mix, w_in_bf, conv_w[l], prev_rows,
                                                 seq_len=dt, tm=n_s, tn=2 * PROJ_COLS)
        kt_past = cache_k[l].transpose(0, 2, 3, 4, 1).reshape(db * d_attn, past)
        o_s = _attn_sample(q_s, kb_s, v_s, kt_past, cache_v[l].reshape(db * past, d_attn), lams, hnorm,
                           batch=db, seq_len=dt, past_len=past, lam_init=lam_init)

        h, hnp, rinfo, rinfo_t, cnt = _out_proj(a_p, o_p, hp, a_s, o_s, hs, w_out_bf,
                                       norm_ffn[l].reshape(1, d), w_router, b_router, tm=tm)
        counts = cnt[0, :N_EXPERTS].astype(jnp.int32)
        e12 = rinfo_t[R_E1:R_E2 + 1].astype(jnp.int32)
        r12 = rinfo_t[R_RANK1:R_RANK2 + 1].astype(jnp.int32)
        tile_expert, n_used, src_tab, pos_tiles = _expert_tiles(counts, e12, r12, n_all, tm)
        ys = _experts(tile_expert, n_used, src_tab, hnp, w_expert_in[l], w_expert_out[l], tm=tm)
        last = l == depth - 1
        g_out = norm_final.reshape(1, d) if last else jnp.ones((1, d), F32)
        y_p, y_s = _combine(pos_tiles, h, rinfo, g_out, ys, n_prompt=n_p, tm=tm)
        assert last, "multi-layer stacks need an un-normalised combine output"

        outs["kp"].append(kt_p.reshape(b, N_HEADS, 2, d_attn // (2 * N_HEADS), t).transpose(0, 4, 1, 2, 3))
        outs["vp"].append(v_p.reshape(b, t, N_HEADS, d_attn // N_HEADS))
        tails = tail_p.reshape(b, t // min(PROJ_ROWS, t), SUBLANES, d_conv)
        outs["cp"].append(tails[:, -1, SUBLANES - (CONV_WIDTH - 1):, :])
        outs["ks"].append(k_s.reshape(db, dt, N_HEADS, 2, d_attn // (2 * N_HEADS)))
        outs["vs"].append(v_s.reshape(db, dt, N_HEADS, d_attn // N_HEADS))
        outs["cs"].append(u_s.reshape(db, dt, d_conv)[:, dt - (CONV_WIDTH - 1):, :])

    return (y_p.reshape(b, t, d), y_s.reshape(db, dt, d),
            jnp.stack(outs["kp"]), jnp.stack(outs["vp"]), jnp.stack(outs["cp"]),
            jnp.stack(outs["ks"]), jnp.stack(outs["vs"]), jnp.stack(outs["cs"]))
```

```python
import functools
import math

import jax
import jax.numpy as jnp
from jax import lax
from jax.experimental import pallas as pl
from jax.experimental.pallas import tpu as pltpu

EPS = 1e-6
N_HEADS = 8
CHUNK = 64
CONV_WIDTH = 3
N_GROUPS = 4
EXPERTS_PER_GROUP = 8
N_EXPERTS = N_GROUPS * EXPERTS_PER_GROUP
LANES = 128
SUBLANES = 8
BF16_SUBLANES = 16
VMEM_LIMIT = 56 * 1024 * 1024

F32 = jnp.float32
BF16 = jnp.bfloat16
U32 = jnp.uint32
NEG_INF = float("-inf")


def _params(n_axes, vmem=VMEM_LIMIT, **kw):
    return pltpu.CompilerParams(dimension_semantics=("arbitrary",) * n_axes,
                                vmem_limit_bytes=vmem, **kw)


def _rms(x, g):
    return x * lax.rsqrt(jnp.mean(x * x, axis=-1, keepdims=True) + EPS) * g


def _lane_pick(x, lane, idx):
    return jnp.sum(jnp.where(lane == idx, x, 0.0), axis=1, keepdims=True)


def _in_proj_kernel(x_ref, g_ref, wgb_ref, wgc_ref, wx_ref, wq_ref, wk_ref, wv_ref, cw_ref, prev_ref,
                    a_ref, q_ref, kb_ref, ko_ref, v_ref, u_ref, xn_ref, carry_ref,
                    *, seq_len, tiles_per_seq, rows_span_sequences, q_scale):
    i = pl.program_id(0)
    j = pl.program_id(1)
    tm = x_ref.shape[0]

    @pl.when(j == 0)
    def _():
        xn_ref[...] = _rms(x_ref[...], g_ref[...]).astype(BF16)

    xn = xn_ref[...]

    gate_b = jnp.dot(xn, wgb_ref[...], preferred_element_type=F32)
    u = (jnp.dot(xn, wgc_ref[...], preferred_element_type=F32)
         * jnp.dot(xn, wx_ref[...], preferred_element_type=F32))
    row = lax.broadcasted_iota(jnp.int32, u.shape, 0)
    if rows_span_sequences:
        rmod = row % seq_len
        pm2, pm1 = prev_ref[0], prev_ref[1]
        first, second = rmod == 0, rmod == 1
        u_ref[...] = u
    else:
        seq_start = (i % tiles_per_seq) == 0
        tail = carry_ref[j]
        pm2 = jnp.where(seq_start, prev_ref[0, 0:1, :], tail[SUBLANES - 2:SUBLANES - 1, :])
        pm1 = jnp.where(seq_start, prev_ref[0, 1:2, :], tail[SUBLANES - 1:SUBLANES, :])
        first, second = row == 0, row == 1
        new_tail = u[tm - SUBLANES:, :]
        carry_ref[j] = new_tail
        u_ref[0] = new_tail
    u1 = jnp.where(first, pm1, pltpu.roll(u, 1, 0))
    u2 = jnp.where(first, pm2, jnp.where(second, pm1, pltpu.roll(u, 2, 0)))
    cw = cw_ref[...]
    y = cw[0:1, :] * u2 + cw[1:2, :] * u1 + cw[2:3, :] * u
    a_ref[...] = (gate_b * y).astype(BF16)

    q_ref[...] = (jnp.dot(xn, wq_ref[...], preferred_element_type=F32) * q_scale).astype(BF16)
    k = jnp.dot(xn, wk_ref[...], preferred_element_type=F32)
    kb_ref[...] = k.astype(BF16)
    if rows_span_sequences:
        ko_ref[...] = k
    else:
        ko_ref[0] = k.T
    v_ref[...] = jnp.dot(xn, wv_ref[...], preferred_element_type=F32)


def _in_proj(x2d, norm_g, w_in_bf, conv_w, prev, *, seq_len, tm, tn):
    n, d = x2d.shape
    d_conv = conv_w.shape[1]
    assert w_in_bf.shape == (d, 6 * d_conv), "conv and attention widths must match"
    assert n % tm == 0 and d_conv % tn == 0
    nj = d_conv // tn
    n_i = n // tm
    rows_span_sequences = tm > seq_len
    if rows_span_sequences:
        assert tm % seq_len == 0 and prev.shape == (2, n, d_conv)
        tiles_per_seq = 1
        prev_spec = pl.BlockSpec((2, tm, tn), lambda i, j: (0, i, j))
        u_shape = jax.ShapeDtypeStruct((n, d_conv), F32)
        u_spec = pl.BlockSpec((tm, tn), lambda i, j: (i, j))
        ko_shape = jax.ShapeDtypeStruct((n, d_conv), F32)
        ko_spec = pl.BlockSpec((tm, tn), lambda i, j: (i, j))
    else:
        assert seq_len % tm == 0 and prev.shape == (n // seq_len, CONV_WIDTH - 1, d_conv)
        tiles_per_seq = seq_len // tm
        prev_spec = pl.BlockSpec((1, CONV_WIDTH - 1, tn), lambda i, j: (i // tiles_per_seq, 0, j))
        u_shape = jax.ShapeDtypeStruct((n_i, SUBLANES, d_conv), F32)
        u_spec = pl.BlockSpec((1, SUBLANES, tn), lambda i, j: (i, 0, j))
        ko_shape = jax.ShapeDtypeStruct((n // seq_len, d_conv, seq_len), F32)
        ko_spec = pl.BlockSpec((1, tn, tm), lambda i, j: (i // tiles_per_seq, j, i % tiles_per_seq))

    def wspec(k):
        return pl.BlockSpec((d, tn), lambda i, j: (0, k * nj + j))

    tile = pl.BlockSpec((tm, tn), lambda i, j: (i, j))
    head_dim = d_conv // (2 * N_HEADS)
    kern = functools.partial(_in_proj_kernel, seq_len=seq_len, tiles_per_seq=tiles_per_seq,
                             rows_span_sequences=rows_span_sequences,
                             q_scale=head_dim ** -0.5 * math.log2(math.e))
    return pl.pallas_call(
        kern,
        grid=(n_i, nj),
        in_specs=[
            pl.BlockSpec((tm, d), lambda i, j: (i, 0)),
            pl.BlockSpec((1, d), lambda i, j: (0, 0)),
            wspec(0), wspec(1), wspec(2), wspec(3), wspec(4), wspec(5),
            pl.BlockSpec((CONV_WIDTH, tn), lambda i, j: (0, j)),
            prev_spec,
        ],
        out_specs=[tile, tile, tile, ko_spec, tile, u_spec],
        out_shape=[
            jax.ShapeDtypeStruct((n, d_conv), BF16),
            jax.ShapeDtypeStruct((n, d_conv), BF16),
            jax.ShapeDtypeStruct((n, d_conv), BF16),
            ko_shape,
            jax.ShapeDtypeStruct((n, d_conv), F32),
            u_shape,
        ],
        scratch_shapes=[pltpu.VMEM((tm, d), BF16), pltpu.VMEM((nj, SUBLANES, tn), F32)],
        compiler_params=_params(2),
        name="in_proj",
    )(x2d, norm_g, *([w_in_bf] * 6), conv_w, prev)


def _diff_lambda(lq1_ref, lk1_ref, lq2_ref, lk2_ref, lam_init):
    s1 = jnp.sum(lq1_ref[...] * lk1_ref[...], axis=1, keepdims=True)
    s2 = jnp.sum(lq2_ref[...] * lk2_ref[...], axis=1, keepdims=True)
    return jnp.exp(s1) - jnp.exp(s2) + lam_init


def _stacked_queries(q, head_dim):
    lane = lax.broadcasted_iota(jnp.int32, q.shape, 1)
    zero = jnp.zeros_like(q)
    return jnp.concatenate([jnp.where(lane < head_dim, q, zero),
                            jnp.where(lane >= head_dim, q, zero)], axis=0)


def _scores(qq, kblk):
    return lax.dot_general(qq, kblk, (((1,), (1,)), ((), ())), preferred_element_type=F32)


def _finish_heads(acc, l, lam, hn, lam_init, tq):
    o = acc[:tq] / l[:tq] - lam * (acc[tq:] / l[tq:])
    return (_rms(o, hn) * (1.0 - lam_init)).astype(BF16)


def _attn_prompt_kernel(lq1_ref, lk1_ref, lq2_ref, lk2_ref, hn_ref, q_ref, k_ref, v_ref, o_ref,
                        qt1_ref, qt2_ref, vt_ref, s0_ref, s1_ref, p0_ref, p1_ref,
                        *, lam_init, head_dim, tq):
    seq_len = q_ref.shape[0]
    hw = 2 * head_dim
    lam = _diff_lambda(lq1_ref, lk1_ref, lq2_ref, lk2_ref, lam_init)

    def prepare(h):
        cols = slice(h * hw, (h + 1) * hw)
        qt = q_ref[:, cols].astype(F32).T
        dim = lax.broadcasted_iota(jnp.int32, qt.shape, 0)
        qt1_ref[h] = jnp.where(dim < head_dim, qt, 0.0).astype(BF16)
        qt2_ref[h] = jnp.where(dim >= head_dim, qt, 0.0).astype(BF16)
        vt_ref[h, :hw, :] = v_ref[:, cols].T.astype(BF16)
        vt_ref[h, hw:, :] = jnp.ones((vt_ref.shape[1] - hw, seq_len), BF16)

    key_chunk = lax.broadcasted_iota(jnp.int32, (tq, 2 * tq), 0) // CHUNK
    qry_chunk = (lax.broadcasted_iota(jnp.int32, (tq, 2 * tq), 1) % tq) // CHUNK
    visible = key_chunk <= qry_chunk

    bufs = ((s0_ref, p0_ref), (s1_ref, p1_ref))
    n_blocks = seq_len // tq
    blocks = [(h, qi) for h in range(q_ref.shape[1] // hw) for qi in range(n_blocks)]

    col_max = {}
    piece = 2 * tq

    def scores(idx):
        h, qi = blocks[idx]
        cols = slice(h * hw, (h + 1) * hw)
        q0, n_keys, s_buf = qi * tq, (qi + 1) * tq, bufs[idx % 2][0]
        if qi == 0:
            prepare(h)
        qqt = jnp.concatenate([qt1_ref[h, :, q0:n_keys], qt2_ref[h, :, q0:n_keys]], axis=1)
        s = jnp.dot(k_ref[q0:n_keys, cols], qqt, preferred_element_type=F32)
        s = jnp.where(visible, s, NEG_INF)
        s_buf[q0:n_keys, :] = s
        m = jnp.max(s, axis=0, keepdims=True)
        yield True
        for r0 in range(0, q0, piece):
            r1 = min(r0 + piece, q0)
            s = jnp.dot(k_ref[r0:r1, cols], qqt, preferred_element_type=F32)
            s_buf[r0:r1, :] = s
            m = jnp.maximum(m, jnp.max(s, axis=0, keepdims=True))
            yield True
        col_max[idx] = m

    def finish(idx):
        h, qi = blocks[idx]
        q0, n_keys = qi * tq, (qi + 1) * tq
        s_buf, p_buf = bufs[idx % 2]
        m = col_max.pop(idx)
        for r0 in range(0, n_keys, piece):
            r1 = min(r0 + piece, n_keys)
            p_buf[r0:r1, :] = jnp.exp2(s_buf[r0:r1, :] - m).astype(BF16)
            yield True
        ot = jnp.dot(vt_ref[h, :, :n_keys], p_buf[:n_keys, :], preferred_element_type=F32)
        ot = ot[:hw] / ot[hw:hw + 1]
        o = (ot[:, :tq] - lam * ot[:, tq:]).T
        o_ref[q0:n_keys, h * hw:(h + 1) * hw] = (_rms(o, hn_ref[...]) * (1.0 - lam_init)).astype(BF16)

    for _ in scores(0):
        pass
    for idx in range(len(blocks)):
        streams = [finish(idx)]
        if idx + 1 < len(blocks):
            streams.insert(0, scores(idx + 1))
        while streams:
            streams = [s for s in streams if next(s, None) is not None]


def _attn_prompt(q, k, v, lams, head_norm, *, batch, seq_len, lam_init, tq):
    n, d_attn = q.shape
    hw = d_attn // N_HEADS
    nh = ATTN_HEADS_PER_STEP
    assert seq_len % tq == 0 and tq % CHUNK == 0 and N_HEADS % nh == 0
    small = pl.BlockSpec((1, hw // 2), lambda b, h: (0, 0))
    head = pl.BlockSpec((seq_len, nh * hw), lambda b, h: (b, h))
    kern = functools.partial(_attn_prompt_kernel, lam_init=lam_init, head_dim=hw // 2, tq=tq)
    return pl.pallas_call(
        kern,
        grid=(batch, N_HEADS // nh),
        in_specs=[small, small, small, small, pl.BlockSpec((1, hw), lambda b, h: (0, 0)),
                  head, head, head],
        out_specs=head,
        out_shape=jax.ShapeDtypeStruct((n, d_attn), BF16),
        scratch_shapes=[pltpu.VMEM((nh, hw, seq_len), BF16), pltpu.VMEM((nh, hw, seq_len), BF16),
                        pltpu.VMEM((nh, hw + BF16_SUBLANES, seq_len), BF16),
                        pltpu.VMEM((seq_len, 2 * tq), F32), pltpu.VMEM((seq_len, 2 * tq), F32),
                        pltpu.VMEM((seq_len, 2 * tq), BF16), pltpu.VMEM((seq_len, 2 * tq), BF16)],
        compiler_params=_params(2),
        name="attn_prompt",
    )(*lams, head_norm, q, k, v)


def _attn_sample_kernel(lq1_ref, lk1_ref, lq2_ref, lk2_ref, hn_ref, q_ref, kc_ref, vc_ref,
                        kn_ref, vn_ref, o_ref, *, lam_init, head_dim):
    tq = q_ref.shape[0]
    hw = 2 * head_dim
    lam = _diff_lambda(lq1_ref, lk1_ref, lq2_ref, lk2_ref, lam_init)
    for h in range(q_ref.shape[1] // hw):
        cols = slice(h * hw, (h + 1) * hw)
        qq = _stacked_queries(q_ref[:, cols], head_dim)
        s_past = jnp.dot(qq, kc_ref[cols, :].astype(BF16), preferred_element_type=F32)
        s_new = _scores(qq, kn_ref[:, cols])
        m = jnp.maximum(jnp.max(s_past, axis=1, keepdims=True), jnp.max(s_new, axis=1, keepdims=True))
        p_past = jnp.exp2(s_past - m)
        p_new = jnp.exp2(s_new - m)
        l = jnp.sum(p_past, axis=1, keepdims=True) + jnp.sum(p_new, axis=1, keepdims=True)
        acc = (jnp.dot(p_past.astype(BF16), vc_ref[:, cols].astype(BF16), preferred_element_type=F32)
               + jnp.dot(p_new.astype(BF16), vn_ref[:, cols].astype(BF16), preferred_element_type=F32))
        o_ref[:, cols] = _finish_heads(acc, l, lam, hn_ref[...], lam_init, tq)


def _attn_sample(q, k_new, v_new, k_past, v_past, lams, head_norm, *, batch, seq_len, past_len, lam_init):
    n, d_attn = q.shape
    hw = d_attn // N_HEADS
    small = pl.BlockSpec((1, hw // 2), lambda b: (0, 0))
    new_spec = pl.BlockSpec((seq_len, d_attn), lambda b: (b, 0))
    past_spec = pl.BlockSpec((past_len, d_attn), lambda b: (b, 0))
    past_t_spec = pl.BlockSpec((d_attn, past_len), lambda b: (b, 0))
    kern = functools.partial(_attn_sample_kernel, lam_init=lam_init, head_dim=hw // 2)
    return pl.pallas_call(
        kern,
        grid=(batch,),
        in_specs=[small, small, small, small, pl.BlockSpec((1, hw), lambda b: (0, 0)),
                  new_spec, past_t_spec, past_spec, new_spec, new_spec],
        out_specs=new_spec,
        out_shape=jax.ShapeDtypeStruct((n, d_attn), BF16),
        compiler_params=_params(1),
        name="attn_sample",
    )(*lams, head_norm, q, k_past, v_past, k_new, v_new)


R_E1, R_E2, R_W1, R_W2, R_RANK1, R_RANK2 = 0, 1, 2, 3, 4, 5
OUT_PROJ_CHUNKS = 4


def _pack_bf16_pairs(lo_bf, hi_bf):
    lo = lax.bitcast_convert_type(lo_bf.astype(F32), U32)
    hi = lax.bitcast_convert_type(hi_bf.astype(F32), U32)
    return (hi & jnp.uint32(0xFFFF0000)) | (lo >> 16)


def _unpack_bf16_pairs(w):
    lo = lax.bitcast_convert_type(w << 16, F32).astype(BF16)
    hi = lax.bitcast_convert_type(w & jnp.uint32(0xFFFF0000), F32).astype(BF16)
    return lo, hi


def _out_proj_kernel(*refs, n_prompt_tiles):
    ap_ref, op_ref, xp_ref, as_ref, os_ref, xs_ref = refs[:6]
    n_w = 2 * OUT_PROJ_CHUNKS
    w_refs = (refs[6:6 + OUT_PROJ_CHUNKS], refs[6 + OUT_PROJ_CHUNKS:6 + n_w])
    g_ref, wr_ref, br_ref, h_ref, hnp_ref, rinfo_ref, rinfo_t_ref, cnt_ref = refs[6 + n_w:-5]
    hbufs, sss, carry_ref = refs[-5:-3], refs[-3:-1], refs[-1]
    i = pl.program_id(0)
    n_tiles = n_prompt_tiles + 1

    @pl.when(i == 0)
    def _():
        carry_ref[...] = jnp.zeros_like(carry_ref)

    def step(slot, inputs, route):
        streams = []
        if inputs is not None:
            streams.append(_out_proj_project(*inputs, w_refs, hbufs[slot], sss[slot]))
        if route:
            streams.append(_out_proj_route(hbufs[1 - slot], sss[1 - slot], g_ref, wr_ref, br_ref,
                                           h_ref, hnp_ref, rinfo_ref, rinfo_t_ref, cnt_ref, carry_ref))
        while streams:
            streams = [s for s in streams if next(s, None) is not None]

    prompt, sample = (ap_ref, op_ref, xp_ref), (as_ref, os_ref, xs_ref)
    pl.when(i == 0)(functools.partial(step, 0, prompt, False))
    for parity in (0, 1):
        pl.when((i >= 1) & (i < n_prompt_tiles) & (i % 2 == parity))(
            functools.partial(step, parity, prompt, True))
    pl.when(i == n_prompt_tiles)(functools.partial(step, n_prompt_tiles % 2, sample, True))
    pl.when(i == n_tiles)(functools.partial(step, n_tiles % 2, None, True))


def _out_proj_project(a_ref, o_ref, x_ref, w_refs, h_ref, ss_ref):
    tm, d = x_ref.shape
    cw = d // OUT_PROJ_CHUNKS
    a, o = a_ref[...], o_ref[...]
    ss = jnp.zeros((tm, 1), F32)
    for c in range(OUT_PROJ_CHUNKS):
        cols = slice(c * cw, (c + 1) * cw)
        h = x_ref[:, cols] + jnp.dot(a, w_refs[0][c][...], preferred_element_type=F32)
        yield True
        h = h + jnp.dot(o, w_refs[1][c][...], preferred_element_type=F32)
        h_ref[:, cols] = h
        ss = ss + jnp.sum(h * h, axis=1, keepdims=True)
        yield True
    ss_ref[...] = jnp.broadcast_to(ss, ss_ref.shape)


def _out_proj_route(hb_ref, ss_ref, g_ref, wr_ref, br_ref,
                    h_ref, hnp_ref, rinfo_ref, rinfo_t_ref, cnt_ref, carry_ref):
    tm, d = hb_ref.shape
    cw = d // OUT_PROJ_CHUNKS
    h_ref[...] = hb_ref[...]
    inv = lax.rsqrt(ss_ref[:, 0:1] / d + EPS)
    yield True

    logits = br_ref[...]
    for c in range(OUT_PROJ_CHUNKS // 2):
        halves = []
        for cc in (c, c + OUT_PROJ_CHUNKS // 2):
            cols = slice(cc * cw, (cc + 1) * cw)
            hb = (hb_ref[:, cols] * inv * g_ref[:, cols]).astype(BF16)
            logits = logits + jnp.dot(hb, wr_ref[cols, :], preferred_element_type=F32)
            halves.append(hb)
        hnp_ref[:, c * cw:(c + 1) * cw] = _pack_bf16_pairs(*halves)
        yield True

    lane = lax.broadcasted_iota(jnp.int32, logits.shape, 1)
    lane_f = lane.astype(F32)
    lg = jnp.where(lane < N_GROUPS, logits, NEG_INF)
    mg = jnp.max(lg, axis=1, keepdims=True)
    grp = jnp.min(jnp.where(lg == mg, lane_f, float(LANES)), axis=1, keepdims=True)
    gate_g = 1.0 / jnp.sum(jnp.exp(lg - mg), axis=1, keepdims=True)
    yield True
    ex = lane - N_GROUPS
    in_grp = (ex >= 0) & (ex < N_EXPERTS) & ((ex >> 3).astype(F32) == grp)
    le = jnp.where(in_grp, logits, NEG_INF)
    v1 = jnp.max(le, axis=1, keepdims=True)
    i1 = jnp.min(jnp.where(le == v1, lane_f, float(LANES)), axis=1, keepdims=True)
    yield True
    le2 = jnp.where(lane_f == i1, NEG_INF, le)
    v2 = jnp.max(le2, axis=1, keepdims=True)
    i2 = jnp.min(jnp.where(le2 == v2, lane_f, float(LANES)), axis=1, keepdims=True)
    e21 = jnp.exp(v2 - v1)
    w1 = gate_g / (1.0 + e21)
    w2 = gate_g * e21 / (1.0 + e21)
    e1 = i1 - float(N_GROUPS)
    e2 = i2 - float(N_GROUPS)
    yield True

    onehot = jnp.where((lane_f == e1) | (lane_f == e2), 1.0, 0.0)
    r = lax.broadcasted_iota(jnp.int32, (tm, tm), 0)
    c = lax.broadcasted_iota(jnp.int32, (tm, tm), 1)
    earlier = jnp.where(c < r, 1.0, 0.0).astype(BF16)
    before = jnp.dot(earlier, onehot.astype(BF16), preferred_element_type=F32) + carry_ref[0:1, :]
    rank1 = _lane_pick(before, lane_f, e1)
    rank2 = _lane_pick(before, lane_f, e2)
    new_carry = carry_ref[0:1, :] + jnp.sum(onehot, axis=0, keepdims=True)
    carry_ref[...] = jnp.broadcast_to(new_carry, carry_ref.shape)
    cnt_ref[...] = jnp.broadcast_to(new_carry, cnt_ref.shape)
    yield True

    rec = jnp.zeros_like(logits)
    for idx, val in ((R_E1, e1), (R_E2, e2), (R_W1, w1), (R_W2, w2), (R_RANK1, rank1), (R_RANK2, rank2)):
        rec = jnp.where(lane == idx, val, rec)
    rinfo_ref[...] = rec
    rinfo_t_ref[...] = rec.T[:SUBLANES, :]


def _out_proj(a_p, o_p, x_p, a_s, o_s, x_s, w_out_bf, norm_g, w_router, b_router, *, tm):
    n_p, d = x_p.shape
    n_s = x_s.shape[0]
    dc = a_p.shape[1]
    assert n_p % tm == 0 and n_s == tm, "sample rows must fill exactly one tile"
    assert w_out_bf.shape == (2 * dc, d)
    npt = n_p // tm
    n_all = n_p + n_s
    cw = d // OUT_PROJ_CHUNKS
    p_idx = lambda i: (jnp.minimum(i, npt - 1), 0)
    zero = lambda i: (0, 0)

    def wspec(half, c):
        return pl.BlockSpec((dc, cw), lambda i: (half, c))

    kern = functools.partial(_out_proj_kernel, n_prompt_tiles=npt)
    done = lambda i: (jnp.maximum(i - 1, 0), 0)
    return pl.pallas_call(
        kern,
        grid=(npt + 2,),
        in_specs=[
            pl.BlockSpec((tm, dc), p_idx), pl.BlockSpec((tm, dc), p_idx), pl.BlockSpec((tm, d), p_idx),
            pl.BlockSpec((tm, dc), zero), pl.BlockSpec((tm, dc), zero), pl.BlockSpec((tm, d), zero),
            *[wspec(half, c) for half in range(2) for c in range(OUT_PROJ_CHUNKS)],
            pl.BlockSpec((1, d), zero),
            pl.BlockSpec((d, LANES), zero), pl.BlockSpec((1, LANES), zero),
        ],
        out_specs=[
            pl.BlockSpec((tm, d), done),
            pl.BlockSpec((tm, d // 2), done),
            pl.BlockSpec((tm, LANES), done),
            pl.BlockSpec((SUBLANES, tm), lambda i: (0, jnp.maximum(i - 1, 0))),
            pl.BlockSpec((SUBLANES, LANES), zero),
        ],
        out_shape=[
            jax.ShapeDtypeStruct((n_all, d), F32),
            jax.ShapeDtypeStruct((n_all, d // 2), U32),
            jax.ShapeDtypeStruct((n_all, LANES), F32),
            jax.ShapeDtypeStruct((SUBLANES, n_all), F32),
            jax.ShapeDtypeStruct((SUBLANES, LANES), F32),
        ],
        scratch_shapes=[pltpu.VMEM((tm, d), F32), pltpu.VMEM((tm, d), F32),
                        pltpu.VMEM((tm, LANES), F32), pltpu.VMEM((tm, LANES), F32),
                        pltpu.VMEM((SUBLANES, LANES), F32)],
        compiler_params=_params(1),
        name="out_proj",
    )(a_p, o_p, x_p, a_s, o_s, x_s, *([w_out_bf] * (2 * OUT_PROJ_CHUNKS)), norm_g, w_router, b_router)


EXPERT_DMA_GROUPS = 16
EXPERT_SLOTS = 3


def _experts_kernel(tile_expert_ref, n_used_ref, src_hbm, hnp_hbm, wi_f32_ref, wo_f32_ref, ys_ref,
                    src_ref, xs_ref, gu_ref, wi_ref, wo_ref, src_sem, gather_sem):
    w = pl.program_id(0)
    n_used = n_used_ref[0]
    tm = xs_ref.shape[1]
    n_tiles = pl.num_programs(0) - 1
    group = tm // EXPERT_DMA_GROUPS

    def src_copy(tile, slot):
        return pltpu.make_async_copy(src_hbm.at[pl.ds(tile, 1), :], src_ref.at[pl.ds(slot, 1), :],
                                     src_sem.at[slot])

    def gather_row(slot, r):
        return pltpu.make_async_copy(hnp_hbm.at[pl.ds(src_ref[slot, r], 1), :],
                                     xs_ref.at[slot, pl.ds(r, 1), :], gather_sem.at[slot])

    def wait_gathers(slot):
        pltpu.make_async_copy(xs_ref.at[slot], xs_ref.at[slot], gather_sem.at[slot]).wait()

    @pl.when(w == 0)
    def _():
        for t in range(EXPERT_SLOTS):
            src_copy(t, t).start()
        for t in range(EXPERT_SLOTS - 1):
            src_copy(t, t).wait()
            for r in range(tm):
                gather_row(t, r).start(priority=r % 2)

    def compute_tile(cur):
        ahead = (cur + EXPERT_SLOTS - 1) % EXPERT_SLOTS
        wait_gathers(cur)
        src_copy(w + 2, ahead).wait()
        src_copy(w + 3, cur).start()

        def move_rows(g):
            for r in range(g * group, (g + 1) * group):
                gather_row(ahead, r).start(priority=r % 2)

        @pl.when((w == 0) | (tile_expert_ref[w] != tile_expert_ref[jnp.maximum(w - 1, 0)]))
        def _():
            wi_ref[...] = wi_f32_ref[0].astype(BF16)
            wo_ref[...] = wo_f32_ref[0].astype(BF16)

        lo, hi = _unpack_bf16_pairs(xs_ref[cur])
        half = lo.shape[1]
        n_chunks = EXPERT_DMA_GROUPS // 4
        f2 = wi_ref.shape[1]
        cw = f2 // n_chunks
        fh = f2 // 4
        d = wo_ref.shape[1]
        ow = d // n_chunks
        groups = iter(range(EXPERT_DMA_GROUPS))

        def gate_up(c):
            cols = slice(c * cw, (c + 1) * cw)
            part = jnp.dot(lo, wi_ref[:half, cols], preferred_element_type=F32)
            move_rows(next(groups))
            gu_ref[:, cols] = part + jnp.dot(hi, wi_ref[half:, cols], preferred_element_type=F32)
            move_rows(next(groups))

        def hidden(part):
            gate = gu_ref[:, part * fh:(part + 1) * fh]
            up = gu_ref[:, f2 // 2 + part * fh:f2 // 2 + (part + 1) * fh]
            return (gate * jax.nn.sigmoid(gate) * up).astype(BF16)

        gate_up(0)
        gate_up(n_chunks // 2)
        act_lo = hidden(0)
        for c in range(1, n_chunks // 2):
            gate_up(c)
            gate_up(n_chunks // 2 + c)
        for c in range(n_chunks):
            cols = slice(c * ow, (c + 1) * ow)
            ys_ref[:, cols] = jnp.dot(act_lo, wo_ref[:fh, cols], preferred_element_type=F32)
            move_rows(next(groups))
        act_hi = hidden(1)
        for c in range(n_chunks):
            cols = slice(c * ow, (c + 1) * ow)
            ys_ref[:, cols] = ys_ref[:, cols] + jnp.dot(act_hi, wo_ref[fh:, cols], preferred_element_type=F32)
            move_rows(next(groups))

    def drain(cur):
        wait_gathers(cur)
        wait_gathers((cur + 1) % EXPERT_SLOTS)
        src_copy(w + 2, (cur + EXPERT_SLOTS - 1) % EXPERT_SLOTS).wait()

    for key in range(EXPERT_SLOTS):
        mine = (w % EXPERT_SLOTS) == key
        pl.when(mine & (w < n_used))(functools.partial(compute_tile, key))
        pl.when(mine & (w == n_used))(functools.partial(drain, key))

    @pl.when((w >= n_used) & (w < n_tiles))
    def _():
        ys_ref[...] = jnp.zeros_like(ys_ref)


def _experts(tile_expert, n_used, src_tab, hnp, w_ei, w_eo, *, tm):
    n_e, d, f2 = w_ei.shape
    dw = hnp.shape[1]
    n_tiles = src_tab.shape[0] - EXPERT_SLOTS
    n_steps = n_tiles + 1
    assert tm % EXPERT_DMA_GROUPS == 0 and src_tab.shape[1] == tm

    def expert_of(w, te, nu):
        return te[jnp.minimum(w, nu[0] - 1)]

    grid_spec = pltpu.PrefetchScalarGridSpec(
        num_scalar_prefetch=2,
        grid=(n_steps,),
        in_specs=[
            pl.BlockSpec(memory_space=pl.ANY),
            pl.BlockSpec(memory_space=pl.ANY),
            pl.BlockSpec((1, d, f2), lambda w, te, nu: (expert_of(w, te, nu), 0, 0)),
            pl.BlockSpec((1, f2 // 2, d), lambda w, te, nu: (expert_of(w, te, nu), 0, 0)),
        ],
        out_specs=pl.BlockSpec((tm, d), lambda w, te, nu: (jnp.minimum(w, n_tiles - 1), 0)),
        scratch_shapes=[pltpu.SMEM((EXPERT_SLOTS, tm), jnp.int32), pltpu.VMEM((EXPERT_SLOTS, tm, dw), U32),
                        pltpu.VMEM((tm, f2), F32),
                        pltpu.VMEM((d, f2), BF16), pltpu.VMEM((f2 // 2, d), BF16),
                        pltpu.SemaphoreType.DMA((EXPERT_SLOTS,)), pltpu.SemaphoreType.DMA((EXPERT_SLOTS,))],
    )
    return pl.pallas_call(
        _experts_kernel,
        grid_spec=grid_spec,
        out_shape=jax.ShapeDtypeStruct((n_tiles * tm, d), F32),
        compiler_params=_params(1),
        name="experts",
    )(tile_expert, n_used, src_tab, hnp, w_ei, w_eo)


def _expert_tiles(counts, e12, r12, n_all, tm):
    max_tiles = (2 * n_all) // tm + N_EXPERTS
    n_rows = max_tiles + EXPERT_SLOTS
    tiles_per_expert = (counts + tm - 1) // tm
    tile_end = jnp.cumsum(tiles_per_expert)
    n_used = tile_end[-1]
    row_start = (tile_end - tiles_per_expert) * tm
    t = jnp.minimum(jnp.arange(n_rows, dtype=jnp.int32), n_used - 1)
    tile_expert = jnp.sum(tile_end[None, :] <= t[:, None], axis=1).astype(jnp.int32)

    experts = jnp.arange(N_EXPERTS, dtype=jnp.int32)[:, None, None]
    pos = jnp.sum(jnp.where(e12[None] == experts, row_start[:, None, None], 0), axis=0) + r12
    token = jnp.broadcast_to(jnp.arange(n_all, dtype=jnp.int32), (2, n_all))
    src = jnp.zeros((n_rows * tm,), jnp.int32).at[pos.reshape(-1)].set(
        token.reshape(-1), unique_indices=True, mode="promise_in_bounds").reshape(n_rows, tm)
    pos_tiles = pos.reshape(2, n_all // tm, tm).transpose(1, 0, 2).reshape(n_all // tm, 2 * tm)
    return tile_expert, n_used.reshape(1).astype(jnp.int32), src, pos_tiles.astype(jnp.int32)


def _combine_kernel(pos_hbm, h_ref, rinfo_ref, g_ref, ys_hbm, yp_ref, ysm_ref,
                    idx_ref, buf_ref, idx_sem, row_sem, *, n_prompt_tiles):
    i = pl.program_id(0)
    nt = pl.num_programs(0)
    tm = h_ref.shape[0]

    def idx_copy(tile, slot):
        return pltpu.make_async_copy(pos_hbm.at[pl.ds(tile, 1), :], idx_ref.at[pl.ds(slot, 1), :],
                                     idx_sem.at[slot])

    def start_rows(slot):
        def body(r, _):
            for k in range(2):
                pltpu.make_async_copy(ys_hbm.at[pl.ds(idx_ref[slot, k * tm + r], 1), :],
                                      buf_ref.at[slot, k, pl.ds(r, 1), :], row_sem.at[slot]).start(priority=k)
            return 0
        lax.fori_loop(0, tm, body, 0, unroll=8)

    def wait_rows(slot):
        pltpu.make_async_copy(buf_ref.at[slot], buf_ref.at[slot], row_sem.at[slot]).wait()

    @pl.when(i == 0)
    def _():
        idx_copy(0, 0).start()
        idx_copy(0, 0).wait()
        start_rows(0)

        @pl.when(nt > 1)
        def _():
            idx_copy(1, 1).start()

    def tile(cur):
        nxt = 1 - cur

        @pl.when(i + 1 < nt)
        def _():
            idx_copy(i + 1, nxt).wait()
            start_rows(nxt)

        @pl.when(i + 2 < nt)
        def _():
            idx_copy(i + 2, cur).start()

        wait_rows(cur)
        rinfo = rinfo_ref[...]
        lane = lax.broadcasted_iota(jnp.int32, rinfo.shape, 1)
        w1 = _lane_pick(rinfo, lane, R_W1)
        w2 = _lane_pick(rinfo, lane, R_W2)
        hp = h_ref[...] + (w1 * buf_ref[cur, 0] + w2 * buf_ref[cur, 1])
        y = _rms(hp, g_ref[...])

        @pl.when(i < n_prompt_tiles)
        def _():
            yp_ref[...] = y

        @pl.when(i >= n_prompt_tiles)
        def _():
            ysm_ref[...] = y

    for parity in (0, 1):
        pl.when((i % 2) == parity)(functools.partial(tile, parity))


def _combine(pos_tiles, h, rinfo, norm_g, ys, *, n_prompt, tm):
    n_all, d = h.shape
    nt = n_all // tm
    npt = n_prompt // tm
    assert n_all - n_prompt == tm and pos_tiles.shape == (nt, 2 * tm)
    kern = functools.partial(_combine_kernel, n_prompt_tiles=npt)
    return pl.pallas_call(
        kern,
        grid=(nt,),
        in_specs=[
            pl.BlockSpec(memory_space=pl.ANY),
            pl.BlockSpec((tm, d), lambda i: (i, 0)),
            pl.BlockSpec((tm, LANES), lambda i: (i, 0)),
            pl.BlockSpec((1, d), lambda i: (0, 0)),
            pl.BlockSpec(memory_space=pl.ANY),
        ],
        out_specs=[
            pl.BlockSpec((tm, d), lambda i: (jnp.minimum(i, npt - 1), 0)),
            pl.BlockSpec((tm, d), lambda i: (0, 0)),
        ],
        out_shape=[jax.ShapeDtypeStruct((n_prompt, d), F32), jax.ShapeDtypeStruct((tm, d), F32)],
        scratch_shapes=[pltpu.SMEM((2, 2 * tm), jnp.int32), pltpu.VMEM((2, 2, tm, d), F32),
                        pltpu.SemaphoreType.DMA((2,)), pltpu.SemaphoreType.DMA((2,))],
        compiler_params=_params(1),
        name="combine",
    )(pos_tiles, h, rinfo, norm_g, ys)


ROW_TILE = 256
PROJ_ROWS = 1024
PROJ_COLS = 256
ATTN_Q_ROWS = 256
ATTN_HEADS_PER_STEP = 2


def kernel(x_prompt, x_sample, cache_k, cache_v, state_conv, norm_mix, w_in, conv_w, lambda_q1,
           lambda_k1, lambda_q2, lambda_k2, head_norm, w_out, norm_ffn, w_router_group,
           b_router_group, w_router_expert, b_router_expert, w_expert_in, w_expert_out, norm_final):
    depth = w_in.shape[0]
    b, t, d = x_prompt.shape
    db, dt, _ = x_sample.shape
    past = cache_k.shape[2]
    d_conv = conv_w.shape[2]
    d_attn = d - d_conv
    n_p, n_s = b * t, db * dt
    n_all = n_p + n_s
    tm = ROW_TILE

    hp = x_prompt.reshape(n_p, d)
    hs = x_sample.reshape(n_s, d)
    outs = {name: [] for name in ("kp", "vp", "cp", "ks", "vs", "cs")}
    y_p = y_s = None
    for l in range(depth):
        lam_init = 0.8 - 0.6 * math.exp(-0.3 * l)
        w_in_bf = w_in[l].astype(BF16)
        w_out_bf = w_out[l].astype(BF16)
        g_mix = norm_mix[l].reshape(1, d)
        lams = tuple(p[l].reshape(1, -1) for p in (lambda_q1, lambda_k1, lambda_q2, lambda_k2))
        hnorm = head_norm[l].reshape(1, -1)
        pad = LANES - N_GROUPS - N_EXPERTS
        w_router = jnp.concatenate([w_router_group[l], w_router_expert[l], jnp.zeros((d, pad), F32)],
                                   axis=1).astype(BF16)
        b_router = jnp.concatenate([b_router_group[l], b_router_expert[l], jnp.zeros((pad,), F32)]).reshape(1, LANES)

        zero_prev = jnp.zeros((b, CONV_WIDTH - 1, d_conv), F32)
        a_p, q_p, kb_p, kt_p, v_p, tail_p = _in_proj(hp, g_mix, w_in_bf, conv_w[l], zero_prev,
                                                     seq_len=t, tm=min(PROJ_ROWS, t), tn=PROJ_COLS)
        o_p = _attn_prompt(q_p, kb_p, v_p, lams, hnorm, batch=b, seq_len=t, lam_init=lam_init,
                           tq=min(ATTN_Q_ROWS, t))
        prev_rows = jnp.stack([jnp.repeat(state_conv[l][:, 0], dt, axis=0),
                               jnp.repeat(state_conv[l][:, 1], dt, axis=0)])
        a_s, q_s, kb_s, k_s, v_s, u_s = _in_proj(hs, g_mix, w_in_bf, conv_w[l], prev_rows,
                                                 seq_len=dt, tm=n_s, tn=2 * PROJ_COLS)
        kt_past = cache_k[l].transpose(0, 2, 3, 4, 1).reshape(db * d_attn, past)
        o_s = _attn_sample(q_s, kb_s, v_s, kt_past, cache_v[l].reshape(db * past, d_attn), lams, hnorm,
                           batch=db, seq_len=dt, past_len=past, lam_init=lam_init)

        h, hnp, rinfo, rinfo_t, cnt = _out_proj(a_p, o_p, hp, a_s, o_s, hs, w_out_bf,
                                       norm_ffn[l].reshape(1, d), w_router, b_router, tm=tm)
        counts = cnt[0, :N_EXPERTS].astype(jnp.int32)
        e12 = rinfo_t[R_E1:R_E2 + 1].astype(jnp.int32)
        r12 = rinfo_t[R_RANK1:R_RANK2 + 1].astype(jnp.int32)
        tile_expert, n_used, src_tab, pos_tiles = _expert_tiles(counts, e12, r12, n_all, tm)
        ys = _experts(tile_expert, n_used, src_tab, hnp, w_expert_in[l], w_expert_out[l], tm=tm)
        last = l == depth - 1
        g_out = norm_final.reshape(1, d) if last else jnp.ones((1, d), F32)
        y_p, y_s = _combine(pos_tiles, h, rinfo, g_out, ys, n_prompt=n_p, tm=tm)
        assert last, "multi-layer stacks need an un-normalised combine output"

        outs["kp"].append(kt_p.reshape(b, N_HEADS, 2, d_attn // (2 * N_HEADS), t).transpose(0, 4, 1, 2, 3))
        outs["vp"].append(v_p.reshape(b, t, N_HEADS, d_attn // N_HEADS))
        tails = tail_p.reshape(b, t // min(PROJ_ROWS, t), SUBLANES, d_conv)
        outs["cp"].append(tails[:, -1, SUBLANES - (CONV_WIDTH - 1):, :])
        outs["ks"].append(k_s.reshape(db, dt, N_HEADS, 2, d_attn // (2 * N_HEADS)))
        outs["vs"].append(v_s.reshape(db, dt, N_HEADS, d_attn // N_HEADS))
        outs["cs"].append(u_s.reshape(db, dt, d_conv)[:, dt - (CONV_WIDTH - 1):, :])

    return (y_p.reshape(b, t, d), y_s.reshape(db, dt, d),
            jnp.stack(outs["kp"]), jnp.stack(outs["vp"]), jnp.stack(outs["cp"]),
            jnp.stack(outs["ks"]), jnp.stack(outs["vs"]), jnp.stack(outs["cs"]))
```

```python
import functools
import math

import jax
import jax.numpy as jnp
from jax import lax
from jax.experimental import pallas as pl
from jax.experimental.pallas import tpu as pltpu

EPS = 1e-6
N_HEADS = 8
CHUNK = 64
CONV_WIDTH = 3
N_GROUPS = 4
EXPERTS_PER_GROUP = 8
N_EXPERTS = N_GROUPS * EXPERTS_PER_GROUP
assert EXPERTS_PER_GROUP & (EXPERTS_PER_GROUP - 1) == 0, "expert -> group uses a shift"
LANES = 128
SUBLANES = 8
BF16_SUBLANES = 16
VMEM_LIMIT = 56 * 1024 * 1024

F32 = jnp.float32
BF16 = jnp.bfloat16
U32 = jnp.uint32
NEG_INF = float("-inf")


def _params(n_axes):
    return pltpu.CompilerParams(dimension_semantics=("arbitrary",) * n_axes,
                                vmem_limit_bytes=VMEM_LIMIT)


def _rms(x, g):
    return x * lax.rsqrt(jnp.mean(x * x, axis=-1, keepdims=True) + EPS) * g


def _lane_pick(x, lane, idx):
    return jnp.sum(jnp.where(lane == idx, x, 0.0), axis=1, keepdims=True)


def _in_proj_kernel(x_ref, g_ref, wgb_ref, wgc_ref, wx_ref, wq_ref, wk_ref, wv_ref, cw_ref, prev_ref,
                    a_ref, q_ref, kb_ref, ko_ref, v_ref, u_ref, xn_ref, carry_ref,
                    *, seq_len, tiles_per_seq, rows_span_sequences, q_scale):
    i = pl.program_id(0)
    j = pl.program_id(1)
    tm = x_ref.shape[0]

    @pl.when(j == 0)
    def _():
        xn_ref[...] = _rms(x_ref[...], g_ref[...]).astype(BF16)

    xn = xn_ref[...]

    gate_b = jnp.dot(xn, wgb_ref[...], preferred_element_type=F32)
    u = (jnp.dot(xn, wgc_ref[...], preferred_element_type=F32)
         * jnp.dot(xn, wx_ref[...], preferred_element_type=F32))
    row = lax.broadcasted_iota(jnp.int32, u.shape, 0)
    if rows_span_sequences:
        rmod = row % seq_len
        pm2, pm1 = prev_ref[0], prev_ref[1]
        first, second = rmod == 0, rmod == 1
        u_ref[...] = u
    else:
        seq_start = (i % tiles_per_seq) == 0
        tail = carry_ref[j]
        pm2 = jnp.where(seq_start, prev_ref[0, 0:1, :], tail[SUBLANES - 2:SUBLANES - 1, :])
        pm1 = jnp.where(seq_start, prev_ref[0, 1:2, :], tail[SUBLANES - 1:SUBLANES, :])
        first, second = row == 0, row == 1
        new_tail = u[tm - SUBLANES:, :]
        carry_ref[j] = new_tail
        u_ref[0] = new_tail
    u1 = jnp.where(first, pm1, pltpu.roll(u, 1, 0))
    u2 = jnp.where(first, pm2, jnp.where(second, pm1, pltpu.roll(u, 2, 0)))
    cw = cw_ref[...]
    y = cw[0:1, :] * u2 + cw[1:2, :] * u1 + cw[2:3, :] * u
    a_ref[...] = (gate_b * y).astype(BF16)

    q_ref[...] = (jnp.dot(xn, wq_ref[...], preferred_element_type=F32) * q_scale).astype(BF16)
    k = jnp.dot(xn, wk_ref[...], preferred_element_type=F32)
    kb_ref[...] = k.astype(BF16)
    if rows_span_sequences:
        ko_ref[...] = k
    else:
        ko_ref[0] = k.T
    v_ref[...] = jnp.dot(xn, wv_ref[...], preferred_element_type=F32)


def _in_proj(x2d, norm_g, w_in_bf, conv_w, prev, *, seq_len, tm, tn):
    n, d = x2d.shape
    d_conv = conv_w.shape[1]
    assert w_in_bf.shape == (d, 6 * d_conv), "conv and attention widths must match"
    assert n % tm == 0 and d_conv % tn == 0
    nj = d_conv // tn
    n_i = n // tm
    rows_span_sequences = tm > seq_len
    if rows_span_sequences:
        assert tm % seq_len == 0 and prev.shape == (2, n, d_conv)
        tiles_per_seq = 1
        prev_spec = pl.BlockSpec((2, tm, tn), lambda i, j: (0, i, j))
        u_shape = jax.ShapeDtypeStruct((n, d_conv), F32)
        u_spec = pl.BlockSpec((tm, tn), lambda i, j: (i, j))
        ko_shape = jax.ShapeDtypeStruct((n, d_conv), F32)
        ko_spec = pl.BlockSpec((tm, tn), lambda i, j: (i, j))
    else:
        assert seq_len % tm == 0 and prev.shape == (n // seq_len, CONV_WIDTH - 1, d_conv)
        tiles_per_seq = seq_len // tm
        prev_spec = pl.BlockSpec((1, CONV_WIDTH - 1, tn), lambda i, j: (i // tiles_per_seq, 0, j))
        u_shape = jax.ShapeDtypeStruct((n_i, SUBLANES, d_conv), F32)
        u_spec = pl.BlockSpec((1, SUBLANES, tn), lambda i, j: (i, 0, j))
        ko_shape = jax.ShapeDtypeStruct((n // seq_len, d_conv, seq_len), F32)
        ko_spec = pl.BlockSpec((1, tn, tm), lambda i, j: (i // tiles_per_seq, j, i % tiles_per_seq))

    def wspec(k):
        return pl.BlockSpec((d, tn), lambda i, j: (0, k * nj + j))

    tile = pl.BlockSpec((tm, tn), lambda i, j: (i, j))
    head_dim = d_conv // (2 * N_HEADS)
    kern = functools.partial(_in_proj_kernel, seq_len=seq_len, tiles_per_seq=tiles_per_seq,
                             rows_span_sequences=rows_span_sequences,
                             q_scale=head_dim ** -0.5 * math.log2(math.e))
    return pl.pallas_call(
        kern,
        grid=(n_i, nj),
        in_specs=[
            pl.BlockSpec((tm, d), lambda i, j: (i, 0)),
            pl.BlockSpec((1, d), lambda i, j: (0, 0)),
            wspec(0), wspec(1), wspec(2), wspec(3), wspec(4), wspec(5),
            pl.BlockSpec((CONV_WIDTH, tn), lambda i, j: (0, j)),
            prev_spec,
        ],
        out_specs=[tile, tile, tile, ko_spec, tile, u_spec],
        out_shape=[
            jax.ShapeDtypeStruct((n, d_conv), BF16),
            jax.ShapeDtypeStruct((n, d_conv), BF16),
            jax.ShapeDtypeStruct((n, d_conv), BF16),
            ko_shape,
            jax.ShapeDtypeStruct((n, d_conv), F32),
            u_shape,
        ],
        scratch_shapes=[pltpu.VMEM((tm, d), BF16), pltpu.VMEM((nj, SUBLANES, tn), F32)],
        compiler_params=_params(2),
        name="in_proj",
    )(x2d, norm_g, *([w_in_bf] * 6), conv_w, prev)


def _diff_lambda(lq1_ref, lk1_ref, lq2_ref, lk2_ref, lam_init):
    s1 = jnp.sum(lq1_ref[...] * lk1_ref[...], axis=1, keepdims=True)
    s2 = jnp.sum(lq2_ref[...] * lk2_ref[...], axis=1, keepdims=True)
    return jnp.exp(s1) - jnp.exp(s2) + lam_init


def _stacked_queries(q, head_dim):
    lane = lax.broadcasted_iota(jnp.int32, q.shape, 1)
    zero = jnp.zeros_like(q)
    return jnp.concatenate([jnp.where(lane < head_dim, q, zero),
                            jnp.where(lane >= head_dim, q, zero)], axis=0)


def _scores(qq, kblk):
    return lax.dot_general(qq, kblk, (((1,), (1,)), ((), ())), preferred_element_type=F32)


def _finish_heads(acc, l, lam, hn, lam_init, tq):
    o = acc[:tq] / l[:tq] - lam * (acc[tq:] / l[tq:])
    return (_rms(o, hn) * (1.0 - lam_init)).astype(BF16)


def _attn_prompt_kernel(lq1_ref, lk1_ref, lq2_ref, lk2_ref, hn_ref, q_ref, k_ref, v_ref, o_ref,
                        qt1_ref, qt2_ref, vt_ref, s0_ref, s1_ref, p0_ref, p1_ref,
                        *, lam_init, head_dim, tq):
    seq_len = q_ref.shape[0]
    hw = 2 * head_dim
    lam = _diff_lambda(lq1_ref, lk1_ref, lq2_ref, lk2_ref, lam_init)

    def prepare(h):
        cols = slice(h * hw, (h + 1) * hw)
        qt = q_ref[:, cols].astype(F32).T
        dim = lax.broadcasted_iota(jnp.int32, qt.shape, 0)
        qt1_ref[h] = jnp.where(dim < head_dim, qt, 0.0).astype(BF16)
        qt2_ref[h] = jnp.where(dim >= head_dim, qt, 0.0).astype(BF16)
        vt_ref[h, :hw, :] = v_ref[:, cols].T.astype(BF16)
        vt_ref[h, hw:, :] = jnp.ones((vt_ref.shape[1] - hw, seq_len), BF16)

    key_chunk = lax.broadcasted_iota(jnp.int32, (tq, 2 * tq), 0) // CHUNK
    qry_chunk = (lax.broadcasted_iota(jnp.int32, (tq, 2 * tq), 1) % tq) // CHUNK
    visible = key_chunk <= qry_chunk

    bufs = ((s0_ref, p0_ref), (s1_ref, p1_ref))
    n_blocks = seq_len // tq
    blocks = [(h, qi) for h in range(q_ref.shape[1] // hw) for qi in range(n_blocks)]

    col_max = {}
    piece = 2 * tq

    def scores(idx):
        h, qi = blocks[idx]
        cols = slice(h * hw, (h + 1) * hw)
        q0, n_keys, s_buf = qi * tq, (qi + 1) * tq, bufs[idx % 2][0]
        if qi == 0:
            prepare(h)
        qqt = jnp.concatenate([qt1_ref[h, :, q0:n_keys], qt2_ref[h, :, q0:n_keys]], axis=1)
        s = jnp.dot(k_ref[q0:n_keys, cols], qqt, preferred_element_type=F32)
        s = jnp.where(visible, s, NEG_INF)
        s_buf[q0:n_keys, :] = s
        m = jnp.max(s, axis=0, keepdims=True)
        yield True
        for r0 in range(0, q0, piece):
            r1 = min(r0 + piece, q0)
            s = jnp.dot(k_ref[r0:r1, cols], qqt, preferred_element_type=F32)
            s_buf[r0:r1, :] = s
            m = jnp.maximum(m, jnp.max(s, axis=0, keepdims=True))
            yield True
        col_max[idx] = m

    def finish(idx):
        h, qi = blocks[idx]
        q0, n_keys = qi * tq, (qi + 1) * tq
        s_buf, p_buf = bufs[idx % 2]
        m = col_max.pop(idx)
        for r0 in range(0, n_keys, piece):
            r1 = min(r0 + piece, n_keys)
            p_buf[r0:r1, :] = jnp.exp2(s_buf[r0:r1, :] - m).astype(BF16)
            yield True
        ot = jnp.dot(vt_ref[h, :, :n_keys], p_buf[:n_keys, :], preferred_element_type=F32)
        ot = ot[:hw] / ot[hw:hw + 1]
        o = (ot[:, :tq] - lam * ot[:, tq:]).T
        o_ref[q0:n_keys, h * hw:(h + 1) * hw] = (_rms(o, hn_ref[...]) * (1.0 - lam_init)).astype(BF16)

    for _ in scores(0):
        pass
    for idx in range(len(blocks)):
        streams = [finish(idx)]
        if idx + 1 < len(blocks):
            streams.insert(0, scores(idx + 1))
        while streams:
            streams = [s for s in streams if next(s, None) is not None]


def _attn_prompt(q, k, v, lams, head_norm, *, batch, seq_len, lam_init, tq):
    n, d_attn = q.shape
    hw = d_attn // N_HEADS
    nh = ATTN_HEADS_PER_STEP
    assert seq_len % tq == 0 and tq % CHUNK == 0 and N_HEADS % nh == 0
    small = pl.BlockSpec((1, hw // 2), lambda b, h: (0, 0))
    head = pl.BlockSpec((seq_len, nh * hw), lambda b, h: (b, h))
    kern = functools.partial(_attn_prompt_kernel, lam_init=lam_init, head_dim=hw // 2, tq=tq)
    return pl.pallas_call(
        kern,
        grid=(batch, N_HEADS // nh),
        in_specs=[small, small, small, small, pl.BlockSpec((1, hw), lambda b, h: (0, 0)),
                  head, head, head],
        out_specs=head,
        out_shape=jax.ShapeDtypeStruct((n, d_attn), BF16),
        scratch_shapes=[pltpu.VMEM((nh, hw, seq_len), BF16), pltpu.VMEM((nh, hw, seq_len), BF16),
                        pltpu.VMEM((nh, hw + BF16_SUBLANES, seq_len), BF16),
                        pltpu.VMEM((seq_len, 2 * tq), F32), pltpu.VMEM((seq_len, 2 * tq), F32),
                        pltpu.VMEM((seq_len, 2 * tq), BF16), pltpu.VMEM((seq_len, 2 * tq), BF16)],
        compiler_params=_params(2),
        name="attn_prompt",
    )(*lams, head_norm, q, k, v)


def _attn_sample_kernel(lq1_ref, lk1_ref, lq2_ref, lk2_ref, hn_ref, q_ref, kc_ref, vc_ref,
                        kn_ref, vn_ref, o_ref, *, lam_init, head_dim):
    tq = q_ref.shape[0]
    hw = 2 * head_dim
    lam = _diff_lambda(lq1_ref, lk1_ref, lq2_ref, lk2_ref, lam_init)
    for h in range(q_ref.shape[1] // hw):
        cols = slice(h * hw, (h + 1) * hw)
        qq = _stacked_queries(q_ref[:, cols], head_dim)
        s_past = jnp.dot(qq, kc_ref[cols, :].astype(BF16), preferred_element_type=F32)
        s_new = _scores(qq, kn_ref[:, cols])
        m = jnp.maximum(jnp.max(s_past, axis=1, keepdims=True), jnp.max(s_new, axis=1, keepdims=True))
        p_past = jnp.exp2(s_past - m)
        p_new = jnp.exp2(s_new - m)
        l = jnp.sum(p_past, axis=1, keepdims=True) + jnp.sum(p_new, axis=1, keepdims=True)
        acc = (jnp.dot(p_past.astype(BF16), vc_ref[:, cols].astype(BF16), preferred_element_type=F32)
               + jnp.dot(p_new.astype(BF16), vn_ref[:, cols].astype(BF16), preferred_element_type=F32))
        o_ref[:, cols] = _finish_heads(acc, l, lam, hn_ref[...], lam_init, tq)


def _attn_sample(q, k_new, v_new, k_past, v_past, lams, head_norm, *, batch, seq_len, past_len, lam_init):
    n, d_attn = q.shape
    hw = d_attn // N_HEADS
    small = pl.BlockSpec((1, hw // 2), lambda b: (0, 0))
    new_spec = pl.BlockSpec((seq_len, d_attn), lambda b: (b, 0))
    past_spec = pl.BlockSpec((past_len, d_attn), lambda b: (b, 0))
    past_t_spec = pl.BlockSpec((d_attn, past_len), lambda b: (b, 0))
    kern = functools.partial(_attn_sample_kernel, lam_init=lam_init, head_dim=hw // 2)
    return pl.pallas_call(
        kern,
        grid=(batch,),
        in_specs=[small, small, small, small, pl.BlockSpec((1, hw), lambda b: (0, 0)),
                  new_spec, past_t_spec, past_spec, new_spec, new_spec],
        out_specs=new_spec,
        out_shape=jax.ShapeDtypeStruct((n, d_attn), BF16),
        compiler_params=_params(1),
        name="attn_sample",
    )(*lams, head_norm, q, k_past, v_past, k_new, v_new)


R_E1, R_E2, R_W1, R_W2, R_RANK1, R_RANK2 = 0, 1, 2, 3, 4, 5
OUT_PROJ_CHUNKS = 4


def _pack_bf16_pairs(lo_bf, hi_bf):
    lo = lax.bitcast_convert_type(lo_bf.astype(F32), U32)
    hi = lax.bitcast_convert_type(hi_bf.astype(F32), U32)
    return (hi & jnp.uint32(0xFFFF0000)) | (lo >> 16)


def _unpack_bf16_pairs(w):
    lo = lax.bitcast_convert_type(w << 16, F32).astype(BF16)
    hi = lax.bitcast_convert_type(w & jnp.uint32(0xFFFF0000), F32).astype(BF16)
    return lo, hi


def _out_proj_kernel(*refs, n_prompt_tiles):
    ap_ref, op_ref, xp_ref, as_ref, os_ref, xs_ref = refs[:6]
    n_w = 2 * OUT_PROJ_CHUNKS
    w_refs = (refs[6:6 + OUT_PROJ_CHUNKS], refs[6 + OUT_PROJ_CHUNKS:6 + n_w])
    g_ref, wr_ref, br_ref, h_ref, hnp_ref, rinfo_ref, rinfo_t_ref, cnt_ref = refs[6 + n_w:-5]
    hbufs, sss, carry_ref = refs[-5:-3], refs[-3:-1], refs[-1]
    i = pl.program_id(0)
    n_tiles = n_prompt_tiles + 1

    @pl.when(i == 0)
    def _():
        carry_ref[...] = jnp.zeros_like(carry_ref)

    def step(slot, inputs, route):
        streams = []
        if inputs is not None:
            streams.append(_out_proj_project(*inputs, w_refs, hbufs[slot], sss[slot]))
        if route:
            streams.append(_out_proj_route(hbufs[1 - slot], sss[1 - slot], g_ref, wr_ref, br_ref,
                                           h_ref, hnp_ref, rinfo_ref, rinfo_t_ref, cnt_ref, carry_ref))
        while streams:
            streams = [s for s in streams if next(s, None) is not None]

    prompt, sample = (ap_ref, op_ref, xp_ref), (as_ref, os_ref, xs_ref)
    pl.when(i == 0)(functools.partial(step, 0, prompt, False))
    for parity in (0, 1):
        pl.when((i >= 1) & (i < n_prompt_tiles) & (i % 2 == parity))(
            functools.partial(step, parity, prompt, True))
    pl.when(i == n_prompt_tiles)(functools.partial(step, n_prompt_tiles % 2, sample, True))
    pl.when(i == n_tiles)(functools.partial(step, n_tiles % 2, None, True))


def _out_proj_project(a_ref, o_ref, x_ref, w_refs, h_ref, ss_ref):
    tm, d = x_ref.shape
    cw = d // OUT_PROJ_CHUNKS
    a, o = a_ref[...], o_ref[...]
    ss = jnp.zeros((tm, 1), F32)
    for c in range(OUT_PROJ_CHUNKS):
        cols = slice(c * cw, (c + 1) * cw)
        h = x_ref[:, cols] + jnp.dot(a, w_refs[0][c][...], preferred_element_type=F32)
        yield True
        h = h + jnp.dot(o, w_refs[1][c][...], preferred_element_type=F32)
        h_ref[:, cols] = h
        ss = ss + jnp.sum(h * h, axis=1, keepdims=True)
        yield True
    ss_ref[...] = jnp.broadcast_to(ss, ss_ref.shape)


def _out_proj_route(hb_ref, ss_ref, g_ref, wr_ref, br_ref,
                    h_ref, hnp_ref, rinfo_ref, rinfo_t_ref, cnt_ref, carry_ref):
    tm, d = hb_ref.shape
    cw = d // OUT_PROJ_CHUNKS
    h_ref[...] = hb_ref[...]
    inv = lax.rsqrt(ss_ref[:, 0:1] / d + EPS)
    yield True

    logits = br_ref[...]
    for c in range(OUT_PROJ_CHUNKS // 2):
        halves = []
        for cc in (c, c + OUT_PROJ_CHUNKS // 2):
            cols = slice(cc * cw, (cc + 1) * cw)
            hb = (hb_ref[:, cols] * inv * g_ref[:, cols]).astype(BF16)
            logits = logits + jnp.dot(hb, wr_ref[cols, :], preferred_element_type=F32)
            halves.append(hb)
        hnp_ref[:, c * cw:(c + 1) * cw] = _pack_bf16_pairs(*halves)
        yield True

    lane = lax.broadcasted_iota(jnp.int32, logits.shape, 1)
    lane_f = lane.astype(F32)
    lg = jnp.where(lane < N_GROUPS, logits, NEG_INF)
    mg = jnp.max(lg, axis=1, keepdims=True)
    grp = jnp.min(jnp.where(lg == mg, lane_f, float(LANES)), axis=1, keepdims=True)
    gate_g = 1.0 / jnp.sum(jnp.exp(lg - mg), axis=1, keepdims=True)
    yield True
    ex = lane - N_GROUPS
    group_of = ex >> (EXPERTS_PER_GROUP.bit_length() - 1)
    in_grp = (ex >= 0) & (ex < N_EXPERTS) & (group_of.astype(F32) == grp)
    le = jnp.where(in_grp, logits, NEG_INF)
    v1 = jnp.max(le, axis=1, keepdims=True)
    i1 = jnp.min(jnp.where(le == v1, lane_f, float(LANES)), axis=1, keepdims=True)
    yield True
    le2 = jnp.where(lane_f == i1, NEG_INF, le)
    v2 = jnp.max(le2, axis=1, keepdims=True)
    i2 = jnp.min(jnp.where(le2 == v2, lane_f, float(LANES)), axis=1, keepdims=True)
    e21 = jnp.exp(v2 - v1)
    w1 = gate_g / (1.0 + e21)
    w2 = gate_g * e21 / (1.0 + e21)
    e1 = i1 - float(N_GROUPS)
    e2 = i2 - float(N_GROUPS)
    yield True

    onehot = jnp.where((lane_f == e1) | (lane_f == e2), 1.0, 0.0)
    r = lax.broadcasted_iota(jnp.int32, (tm, tm), 0)
    c = lax.broadcasted_iota(jnp.int32, (tm, tm), 1)
    earlier = jnp.where(c < r, 1.0, 0.0).astype(BF16)
    before = jnp.dot(earlier, onehot.astype(BF16), preferred_element_type=F32) + carry_ref[0:1, :]
    rank1 = _lane_pick(before, lane_f, e1)
    rank2 = _lane_pick(before, lane_f, e2)
    new_carry = carry_ref[0:1, :] + jnp.sum(onehot, axis=0, keepdims=True)
    carry_ref[...] = jnp.broadcast_to(new_carry, carry_ref.shape)
    cnt_ref[...] = jnp.broadcast_to(new_carry, cnt_ref.shape)
    yield True

    rec = jnp.zeros_like(logits)
    for idx, val in ((R_E1, e1), (R_E2, e2), (R_W1, w1), (R_W2, w2), (R_RANK1, rank1), (R_RANK2, rank2)):
        rec = jnp.where(lane == idx, val, rec)
    rinfo_ref[...] = rec
    rinfo_t_ref[...] = rec.T[:SUBLANES, :]


def _out_proj(a_p, o_p, x_p, a_s, o_s, x_s, w_out_bf, norm_g, w_router, b_router, *, tm):
    n_p, d = x_p.shape
    n_s = x_s.shape[0]
    dc = a_p.shape[1]
    assert n_p % tm == 0 and n_s == tm, "sample rows must fill exactly one tile"
    assert w_out_bf.shape == (2 * dc, d)
    npt = n_p // tm
    n_all = n_p + n_s
    cw = d // OUT_PROJ_CHUNKS
    p_idx = lambda i: (jnp.minimum(i, npt - 1), 0)
    zero = lambda i: (0, 0)

    def wspec(half, c):
        return pl.BlockSpec((dc, cw), lambda i: (half, c))

    kern = functools.partial(_out_proj_kernel, n_prompt_tiles=npt)
    done = lambda i: (jnp.maximum(i - 1, 0), 0)
    return pl.pallas_call(
        kern,
        grid=(npt + 2,),
        in_specs=[
            pl.BlockSpec((tm, dc), p_idx), pl.BlockSpec((tm, dc), p_idx), pl.BlockSpec((tm, d), p_idx),
            pl.BlockSpec((tm, dc), zero), pl.BlockSpec((tm, dc), zero), pl.BlockSpec((tm, d), zero),
            *[wspec(half, c) for half in range(2) for c in range(OUT_PROJ_CHUNKS)],
            pl.BlockSpec((1, d), zero),
            pl.BlockSpec((d, LANES), zero), pl.BlockSpec((1, LANES), zero),
        ],
        out_specs=[
            pl.BlockSpec((tm, d), done),
            pl.BlockSpec((tm, d // 2), done),
            pl.BlockSpec((tm, LANES), done),
            pl.BlockSpec((SUBLANES, tm), lambda i: (0, jnp.maximum(i - 1, 0))),
            pl.BlockSpec((SUBLANES, LANES), zero),
        ],
        out_shape=[
            jax.ShapeDtypeStruct((n_all, d), F32),
            jax.ShapeDtypeStruct((n_all, d // 2), U32),
            jax.ShapeDtypeStruct((n_all, LANES), F32),
            jax.ShapeDtypeStruct((SUBLANES, n_all), F32),
            jax.ShapeDtypeStruct((SUBLANES, LANES), F32),
        ],
        scratch_shapes=[pltpu.VMEM((tm, d), F32), pltpu.VMEM((tm, d), F32),
                        pltpu.VMEM((tm, LANES), F32), pltpu.VMEM((tm, LANES), F32),
                        pltpu.VMEM((SUBLANES, LANES), F32)],
        compiler_params=_params(1),
        name="out_proj",
    )(a_p, o_p, x_p, a_s, o_s, x_s, *([w_out_bf] * (2 * OUT_PROJ_CHUNKS)), norm_g, w_router, b_router)


EXPERT_DMA_GROUPS = 16
EXPERT_SLOTS = 3


def _experts_kernel(tile_expert_ref, n_used_ref, src_hbm, hnp_hbm, wi_f32_ref, wo_f32_ref, ys_ref,
                    src_ref, xs_ref, gu_ref, wi_ref, wo_ref, src_sem, gather_sem):
    w = pl.program_id(0)
    n_used = n_used_ref[0]
    tm = xs_ref.shape[1]
    n_tiles = pl.num_programs(0) - 1
    group = tm // EXPERT_DMA_GROUPS

    def src_copy(tile, slot):
        return pltpu.make_async_copy(src_hbm.at[pl.ds(tile, 1), :], src_ref.at[pl.ds(slot, 1), :],
                                     src_sem.at[slot])

    def gather_row(slot, r):
        return pltpu.make_async_copy(hnp_hbm.at[pl.ds(src_ref[slot, r], 1), :],
                                     xs_ref.at[slot, pl.ds(r, 1), :], gather_sem.at[slot])

    def wait_gathers(slot):
        pltpu.make_async_copy(xs_ref.at[slot], xs_ref.at[slot], gather_sem.at[slot]).wait()

    @pl.when(w == 0)
    def _():
        for t in range(EXPERT_SLOTS):
            src_copy(t, t).start()
        for t in range(EXPERT_SLOTS - 1):
            src_copy(t, t).wait()
            for r in range(tm):
                gather_row(t, r).start(priority=r % 2)

    def compute_tile(cur):
        ahead = (cur + EXPERT_SLOTS - 1) % EXPERT_SLOTS
        wait_gathers(cur)
        src_copy(w + 2, ahead).wait()
        src_copy(w + 3, cur).start()

        def move_rows(g):
            for r in range(g * group, (g + 1) * group):
                gather_row(ahead, r).start(priority=r % 2)

        @pl.when((w == 0) | (tile_expert_ref[w] != tile_expert_ref[jnp.maximum(w - 1, 0)]))
        def _():
            wi_ref[...] = wi_f32_ref[0].astype(BF16)
            wo_ref[...] = wo_f32_ref[0].astype(BF16)

        lo, hi = _unpack_bf16_pairs(xs_ref[cur])
        half = lo.shape[1]
        n_chunks = EXPERT_DMA_GROUPS // 4
        f2 = wi_ref.shape[1]
        cw = f2 // n_chunks
        fh = f2 // 4
        d = wo_ref.shape[1]
        ow = d // n_chunks
        groups = iter(range(EXPERT_DMA_GROUPS))

        def gate_up(c):
            cols = slice(c * cw, (c + 1) * cw)
            part = jnp.dot(lo, wi_ref[:half, cols], preferred_element_type=F32)
            move_rows(next(groups))
            gu_ref[:, cols] = part + jnp.dot(hi, wi_ref[half:, cols], preferred_element_type=F32)
            move_rows(next(groups))

        def hidden(part):
            gate = gu_ref[:, part * fh:(part + 1) * fh]
            up = gu_ref[:, f2 // 2 + part * fh:f2 // 2 + (part + 1) * fh]
            return (gate * jax.nn.sigmoid(gate) * up).astype(BF16)

        gate_up(0)
        gate_up(n_chunks // 2)
        act_lo = hidden(0)
        for c in range(1, n_chunks // 2):
            gate_up(c)
            gate_up(n_chunks // 2 + c)
        for c in range(n_chunks):
            cols = slice(c * ow, (c + 1) * ow)
            ys_ref[:, cols] = jnp.dot(act_lo, wo_ref[:fh, cols], preferred_element_type=F32)
            move_rows(next(groups))
        act_hi = hidden(1)
        for c in range(n_chunks):
            cols = slice(c * ow, (c + 1) * ow)
            ys_ref[:, cols] = ys_ref[:, cols] + jnp.dot(act_hi, wo_ref[fh:, cols], preferred_element_type=F32)
            move_rows(next(groups))

    def drain(cur):
        wait_gathers(cur)
        wait_gathers((cur + 1) % EXPERT_SLOTS)
        src_copy(w + 2, (cur + EXPERT_SLOTS - 1) % EXPERT_SLOTS).wait()

    for key in range(EXPERT_SLOTS):
        mine = (w % EXPERT_SLOTS) == key
        pl.when(mine & (w < n_used))(functools.partial(compute_tile, key))
        pl.when(mine & (w == n_used))(functools.partial(drain, key))

    @pl.when((w >= n_used) & (w < n_tiles))
    def _():
        ys_ref[...] = jnp.zeros_like(ys_ref)


def _experts(tile_expert, n_used, src_tab, hnp, w_ei, w_eo, *, tm):
    n_e, d, f2 = w_ei.shape
    dw = hnp.shape[1]
    n_tiles = src_tab.shape[0] - EXPERT_SLOTS
    n_steps = n_tiles + 1
    assert tm % EXPERT_DMA_GROUPS == 0 and src_tab.shape[1] == tm

    def expert_of(w, te, nu):
        return te[jnp.minimum(w, nu[0] - 1)]

    grid_spec = pltpu.PrefetchScalarGridSpec(
        num_scalar_prefetch=2,
        grid=(n_steps,),
        in_specs=[
            pl.BlockSpec(memory_space=pl.ANY),
            pl.BlockSpec(memory_space=pl.ANY),
            pl.BlockSpec((1, d, f2), lambda w, te, nu: (expert_of(w, te, nu), 0, 0)),
            pl.BlockSpec((1, f2 // 2, d), lambda w, te, nu: (expert_of(w, te, nu), 0, 0)),
        ],
        out_specs=pl.BlockSpec((tm, d), lambda w, te, nu: (jnp.minimum(w, n_tiles - 1), 0)),
        scratch_shapes=[pltpu.SMEM((EXPERT_SLOTS, tm), jnp.int32), pltpu.VMEM((EXPERT_SLOTS, tm, dw), U32),
                        pltpu.VMEM((tm, f2), F32),
                        pltpu.VMEM((d, f2), BF16), pltpu.VMEM((f2 // 2, d), BF16),
                        pltpu.SemaphoreType.DMA((EXPERT_SLOTS,)), pltpu.SemaphoreType.DMA((EXPERT_SLOTS,))],
    )
    return pl.pallas_call(
        _experts_kernel,
        grid_spec=grid_spec,
        out_shape=jax.ShapeDtypeStruct((n_tiles * tm, d), F32),
        compiler_params=_params(1),
        name="experts",
    )(tile_expert, n_used, src_tab, hnp, w_ei, w_eo)


def _expert_tiles(counts, e12, r12, n_all, tm):
    max_tiles = (2 * n_all) // tm + N_EXPERTS
    n_rows = max_tiles + EXPERT_SLOTS
    tiles_per_expert = (counts + tm - 1) // tm
    tile_end = jnp.cumsum(tiles_per_expert)
    n_used = tile_end[-1]
    row_start = (tile_end - tiles_per_expert) * tm
    t = jnp.minimum(jnp.arange(n_rows, dtype=jnp.int32), n_used - 1)
    tile_expert = jnp.sum(tile_end[None, :] <= t[:, None], axis=1).astype(jnp.int32)

    experts = jnp.arange(N_EXPERTS, dtype=jnp.int32)[:, None, None]
    pos = jnp.sum(jnp.where(e12[None] == experts, row_start[:, None, None], 0), axis=0) + r12
    token = jnp.broadcast_to(jnp.arange(n_all, dtype=jnp.int32), (2, n_all))
    src = jnp.zeros((n_rows * tm,), jnp.int32).at[pos.reshape(-1)].set(
        token.reshape(-1), unique_indices=True, mode="promise_in_bounds").reshape(n_rows, tm)
    pos_tiles = pos.reshape(2, n_all // tm, tm).transpose(1, 0, 2).reshape(n_all // tm, 2 * tm)
    return tile_expert, n_used.reshape(1).astype(jnp.int32), src, pos_tiles.astype(jnp.int32)


def _combine_kernel(pos_hbm, h_ref, rinfo_ref, g_ref, ys_hbm, yp_ref, ysm_ref,
                    idx_ref, buf_ref, idx_sem, row_sem, *, n_prompt_tiles):
    i = pl.program_id(0)
    nt = pl.num_programs(0)
    tm = h_ref.shape[0]

    def idx_copy(tile, slot):
        return pltpu.make_async_copy(pos_hbm.at[pl.ds(tile, 1), :], idx_ref.at[pl.ds(slot, 1), :],
                                     idx_sem.at[slot])

    def start_rows(slot):
        def body(r, _):
            for k in range(2):
                pltpu.make_async_copy(ys_hbm.at[pl.ds(idx_ref[slot, k * tm + r], 1), :],
                                      buf_ref.at[slot, k, pl.ds(r, 1), :], row_sem.at[slot]).start(priority=k)
            return 0
        lax.fori_loop(0, tm, body, 0, unroll=8)

    def wait_rows(slot):
        pltpu.make_async_copy(buf_ref.at[slot], buf_ref.at[slot], row_sem.at[slot]).wait()

    @pl.when(i == 0)
    def _():
        idx_copy(0, 0).start()
        idx_copy(0, 0).wait()
        start_rows(0)

        @pl.when(nt > 1)
        def _():
            idx_copy(1, 1).start()

    def tile(cur):
        nxt = 1 - cur

        @pl.when(i + 1 < nt)
        def _():
            idx_copy(i + 1, nxt).wait()
            start_rows(nxt)

        @pl.when(i + 2 < nt)
        def _():
            idx_copy(i + 2, cur).start()

        wait_rows(cur)
        rinfo = rinfo_ref[...]
        lane = lax.broadcasted_iota(jnp.int32, rinfo.shape, 1)
        w1 = _lane_pick(rinfo, lane, R_W1)
        w2 = _lane_pick(rinfo, lane, R_W2)
        hp = h_ref[...] + (w1 * buf_ref[cur, 0] + w2 * buf_ref[cur, 1])
        y = _rms(hp, g_ref[...])

        @pl.when(i < n_prompt_tiles)
        def _():
            yp_ref[...] = y

        @pl.when(i >= n_prompt_tiles)
        def _():
            ysm_ref[...] = y

    for parity in (0, 1):
        pl.when((i % 2) == parity)(functools.partial(tile, parity))


def _combine(pos_tiles, h, rinfo, norm_g, ys, *, n_prompt, tm):
    n_all, d = h.shape
    nt = n_all // tm
    npt = n_prompt // tm
    assert n_all - n_prompt == tm and pos_tiles.shape == (nt, 2 * tm)
    kern = functools.partial(_combine_kernel, n_prompt_tiles=npt)
    return pl.pallas_call(
        kern,
        grid=(nt,),
        in_specs=[
            pl.BlockSpec(memory_space=pl.ANY),
            pl.BlockSpec((tm, d), lambda i: (i, 0)),
            pl.BlockSpec((tm, LANES), lambda i: (i, 0)),
            pl.BlockSpec((1, d), lambda i: (0, 0)),
            pl.BlockSpec(memory_space=pl.ANY),
        ],
        out_specs=[
            pl.BlockSpec((tm, d), lambda i: (jnp.minimum(i, npt - 1), 0)),
            pl.BlockSpec((tm, d), lambda i: (0, 0)),
        ],
        out_shape=[jax.ShapeDtypeStruct((n_prompt, d), F32), jax.ShapeDtypeStruct((tm, d), F32)],
        scratch_shapes=[pltpu.SMEM((2, 2 * tm), jnp.int32), pltpu.VMEM((2, 2, tm, d), F32),
                        pltpu.SemaphoreType.DMA((2,)), pltpu.SemaphoreType.DMA((2,))],
        compiler_params=_params(1),
        name="combine",
    )(pos_tiles, h, rinfo, norm_g, ys)


ROW_TILE = 256
PROJ_ROWS = 1024
PROJ_COLS = 256
ATTN_Q_ROWS = 256
ATTN_HEADS_PER_STEP = 2


def kernel(x_prompt, x_sample, cache_k, cache_v, state_conv, norm_mix, w_in, conv_w, lambda_q1,
           lambda_k1, lambda_q2, lambda_k2, head_norm, w_out, norm_ffn, w_router_group,
           b_router_group, w_router_expert, b_router_expert, w_expert_in, w_expert_out, norm_final):
    depth = w_in.shape[0]
    b, t, d = x_prompt.shape
    db, dt, _ = x_sample.shape
    past = cache_k.shape[2]
    d_conv = conv_w.shape[2]
    d_attn = d - d_conv
    n_p, n_s = b * t, db * dt
    n_all = n_p + n_s
    tm = ROW_TILE

    hp = x_prompt.reshape(n_p, d)
    hs = x_sample.reshape(n_s, d)
    outs = {name: [] for name in ("kp", "vp", "cp", "ks", "vs", "cs")}
    y_p = y_s = None
    for l in range(depth):
        lam_init = 0.8 - 0.6 * math.exp(-0.3 * l)
        w_in_bf = w_in[l].astype(BF16)
        w_out_bf = w_out[l].astype(BF16)
        g_mix = norm_mix[l].reshape(1, d)
        lams = tuple(p[l].reshape(1, -1) for p in (lambda_q1, lambda_k1, lambda_q2, lambda_k2))
        hnorm = head_norm[l].reshape(1, -1)
        pad = LANES - N_GROUPS - N_EXPERTS
        w_router = jnp.concatenate([w_router_group[l], w_router_expert[l], jnp.zeros((d, pad), F32)],
                                   axis=1).astype(BF16)
        b_router = jnp.concatenate([b_router_group[l], b_router_expert[l], jnp.zeros((pad,), F32)]).reshape(1, LANES)

        zero_prev = jnp.zeros((b, CONV_WIDTH - 1, d_conv), F32)
        a_p, q_p, kb_p, kt_p, v_p, tail_p = _in_proj(hp, g_mix, w_in_bf, conv_w[l], zero_prev,
                                                     seq_len=t, tm=min(PROJ_ROWS, t), tn=PROJ_COLS)
        o_p = _attn_prompt(q_p, kb_p, v_p, lams, hnorm, batch=b, seq_len=t, lam_init=lam_init,
                           tq=min(ATTN_Q_ROWS, t))
        prev_rows = jnp.stack([jnp.repeat(state_conv[l][:, 0], dt, axis=0),
                               jnp.repeat(state_conv[l][:, 1], dt, axis=0)])
        a_s, q_s, kb_s, k_s, v_s, u_s = _in_proj(hs, g_mix, w_in_bf, conv_w[l], prev_rows,
                                                 seq_len=dt, tm=n_s, tn=2 * PROJ_COLS)
        kt_past = cache_k[l].transpose(0, 2, 3, 4, 1).reshape(db * d_attn, past)
        o_s = _attn_sample(q_s, kb_s, v_s, kt_past, cache_v[l].reshape(db * past, d_attn), lams, hnorm,
                           batch=db, seq_len=dt, past_len=past, lam_init=lam_init)

        h, hnp, rinfo, rinfo_t, cnt = _out_proj(a_p, o_p, hp, a_s, o_s, hs, w_out_bf,
                                       norm_ffn[l].reshape(1, d), w_router, b_router, tm=tm)
        counts = cnt[0, :N_EXPERTS].astype(jnp.int32)
        e12 = rinfo_t[R_E1:R_E2 + 1].astype(jnp.int32)
        r12 = rinfo_t[R_RANK1:R_RANK2 + 1].astype(jnp.int32)
        tile_expert, n_used, src_tab, pos_tiles = _expert_tiles(counts, e12, r12, n_all, tm)
        ys = _experts(tile_expert, n_used, src_tab, hnp, w_expert_in[l], w_expert_out[l], tm=tm)
        last = l == depth - 1
        g_out = norm_final.reshape(1, d) if last else jnp.ones((1, d), F32)
        y_p, y_s = _combine(pos_tiles, h, rinfo, g_out, ys, n_prompt=n_p, tm=tm)
        assert last, "multi-layer stacks need an un-normalised combine output"

        outs["kp"].append(kt_p.reshape(b, N_HEADS, 2, d_attn // (2 * N_HEADS), t).transpose(0, 4, 1, 2, 3))
        outs["vp"].append(v_p.reshape(b, t, N_HEADS, d_attn // N_HEADS))
        tails = tail_p.reshape(b, t // min(PROJ_ROWS, t), SUBLANES, d_conv)
        outs["cp"].append(tails[:, -1, SUBLANES - (CONV_WIDTH - 1):, :])
        outs["ks"].append(k_s.reshape(db, dt, N_HEADS, 2, d_attn // (2 * N_HEADS)))
        outs["vs"].append(v_s.reshape(db, dt, N_HEADS, d_attn // N_HEADS))
        outs["cs"].append(u_s.reshape(db, dt, d_conv)[:, dt - (CONV_WIDTH - 1):, :])

    return (y_p.reshape(b, t, d), y_s.reshape(db, dt, d),
            jnp.stack(outs["kp"]), jnp.stack(outs["vp"]), jnp.stack(outs["cp"]),
            jnp.stack(outs["ks"]), jnp.stack(outs["vs"]), jnp.stack(outs["cs"]))
```

```python
import functools
import math

import jax
import jax.numpy as jnp
from jax import lax
from jax.experimental import pallas as pl
from jax.experimental.pallas import tpu as pltpu

EPS = 1e-6
N_HEADS = 8
CHUNK = 64
CONV_WIDTH = 3
N_GROUPS = 4
EXPERTS_PER_GROUP = 8
N_EXPERTS = N_GROUPS * EXPERTS_PER_GROUP
assert EXPERTS_PER_GROUP & (EXPERTS_PER_GROUP - 1) == 0, "expert -> group uses a shift"
LANES = 128
SUBLANES = 8
BF16_SUBLANES = 16
VMEM_LIMIT = 56 * 1024 * 1024

F32 = jnp.float32
BF16 = jnp.bfloat16
U32 = jnp.uint32
NEG_INF = float("-inf")


def _params(n_axes):
    return pltpu.CompilerParams(dimension_semantics=("arbitrary",) * n_axes,
                                vmem_limit_bytes=VMEM_LIMIT)


def _rms(x, g):
    return x * lax.rsqrt(jnp.mean(x * x, axis=-1, keepdims=True) + EPS) * g


def _lane_pick(x, lane, idx):
    return jnp.sum(jnp.where(lane == idx, x, 0.0), axis=1, keepdims=True)


def _in_proj_kernel(x_ref, g_ref, wgb_ref, wgc_ref, wx_ref, wq_ref, wk_ref, wv_ref, cw_ref, prev_ref,
                    a_ref, q_ref, kb_ref, ko_ref, v_ref, u_ref, xn_ref, carry_ref,
                    *, seq_len, tiles_per_seq, rows_span_sequences, q_scale):
    i = pl.program_id(0)
    j = pl.program_id(1)
    tm = x_ref.shape[0]

    @pl.when(j == 0)
    def _():
        xn_ref[...] = _rms(x_ref[...], g_ref[...]).astype(BF16)

    xn = xn_ref[...]

    gate_b = jnp.dot(xn, wgb_ref[...], preferred_element_type=F32)
    u = (jnp.dot(xn, wgc_ref[...], preferred_element_type=F32)
         * jnp.dot(xn, wx_ref[...], preferred_element_type=F32))
    row = lax.broadcasted_iota(jnp.int32, u.shape, 0)
    if rows_span_sequences:
        rmod = row % seq_len
        pm2, pm1 = prev_ref[0], prev_ref[1]
        first, second = rmod == 0, rmod == 1
        u_ref[...] = u
    else:
        seq_start = (i % tiles_per_seq) == 0
        tail = carry_ref[j]
        pm2 = jnp.where(seq_start, prev_ref[0, 0:1, :], tail[SUBLANES - 2:SUBLANES - 1, :])
        pm1 = jnp.where(seq_start, prev_ref[0, 1:2, :], tail[SUBLANES - 1:SUBLANES, :])
        first, second = row == 0, row == 1
        new_tail = u[tm - SUBLANES:, :]
        carry_ref[j] = new_tail
        u_ref[0] = new_tail
    u1 = jnp.where(first, pm1, pltpu.roll(u, 1, 0))
    u2 = jnp.where(first, pm2, jnp.where(second, pm1, pltpu.roll(u, 2, 0)))
    cw = cw_ref[...]
    y = cw[0:1, :] * u2 + cw[1:2, :] * u1 + cw[2:3, :] * u
    a_ref[...] = (gate_b * y).astype(BF16)

    q_ref[...] = (jnp.dot(xn, wq_ref[...], preferred_element_type=F32) * q_scale).astype(BF16)
    k = jnp.dot(xn, wk_ref[...], preferred_element_type=F32)
    kb_ref[...] = k.astype(BF16)
    if rows_span_sequences:
        ko_ref[...] = k
    else:
        ko_ref[0] = k.T
    v_ref[...] = jnp.dot(xn, wv_ref[...], preferred_element_type=F32)


def _in_proj(x2d, norm_g, w_in_bf, conv_w, prev, *, seq_len, tm, tn):
    n, d = x2d.shape
    d_conv = conv_w.shape[1]
    assert w_in_bf.shape == (d, 6 * d_conv), "conv and attention widths must match"
    assert n % tm == 0 and d_conv % tn == 0
    nj = d_conv // tn
    n_i = n // tm
    rows_span_sequences = tm > seq_len
    if rows_span_sequences:
        assert tm % seq_len == 0 and prev.shape == (2, n, d_conv)
        tiles_per_seq = 1
        prev_spec = pl.BlockSpec((2, tm, tn), lambda i, j: (0, i, j))
        u_shape = jax.ShapeDtypeStruct((n, d_conv), F32)
        u_spec = pl.BlockSpec((tm, tn), lambda i, j: (i, j))
        ko_shape = jax.ShapeDtypeStruct((n, d_conv), F32)
        ko_spec = pl.BlockSpec((tm, tn), lambda i, j: (i, j))
    else:
        assert seq_len % tm == 0 and prev.shape == (n // seq_len, CONV_WIDTH - 1, d_conv)
        tiles_per_seq = seq_len // tm
        prev_spec = pl.BlockSpec((1, CONV_WIDTH - 1, tn), lambda i, j: (i // tiles_per_seq, 0, j))
        u_shape = jax.ShapeDtypeStruct((n_i, SUBLANES, d_conv), F32)
        u_spec = pl.BlockSpec((1, SUBLANES, tn), lambda i, j: (i, 0, j))
        ko_shape = jax.ShapeDtypeStruct((n // seq_len, d_conv, seq_len), F32)
        ko_spec = pl.BlockSpec((1, tn, tm), lambda i, j: (i // tiles_per_seq, j, i % tiles_per_seq))

    def wspec(k):
        return pl.BlockSpec((d, tn), lambda i, j: (0, k * nj + j))

    tile = pl.BlockSpec((tm, tn), lambda i, j: (i, j))
    head_dim = d_conv // (2 * N_HEADS)
    kern = functools.partial(_in_proj_kernel, seq_len=seq_len, tiles_per_seq=tiles_per_seq,
                             rows_span_sequences=rows_span_sequences,
                             q_scale=head_dim ** -0.5 * math.log2(math.e))
    return pl.pallas_call(
        kern,
        grid=(n_i, nj),
        in_specs=[
            pl.BlockSpec((tm, d), lambda i, j: (i, 0)),
            pl.BlockSpec((1, d), lambda i, j: (0, 0)),
            wspec(0), wspec(1), wspec(2), wspec(3), wspec(4), wspec(5),
            pl.BlockSpec((CONV_WIDTH, tn), lambda i, j: (0, j)),
            prev_spec,
        ],
        out_specs=[tile, tile, tile, ko_spec, tile, u_spec],
        out_shape=[
            jax.ShapeDtypeStruct((n, d_conv), BF16),
            jax.ShapeDtypeStruct((n, d_conv), BF16),
            jax.ShapeDtypeStruct((n, d_conv), BF16),
            ko_shape,
            jax.ShapeDtypeStruct((n, d_conv), F32),
            u_shape,
        ],
        scratch_shapes=[pltpu.VMEM((tm, d), BF16), pltpu.VMEM((nj, SUBLANES, tn), F32)],
        compiler_params=_params(2),
        name="in_proj",
    )(x2d, norm_g, *([w_in_bf] * 6), conv_w, prev)


def _diff_lambda(lq1_ref, lk1_ref, lq2_ref, lk2_ref, lam_init):
    s1 = jnp.sum(lq1_ref[...] * lk1_ref[...], axis=1, keepdims=True)
    s2 = jnp.sum(lq2_ref[...] * lk2_ref[...], axis=1, keepdims=True)
    return jnp.exp(s1) - jnp.exp(s2) + lam_init


def _stacked_queries(q, head_dim):
    lane = lax.broadcasted_iota(jnp.int32, q.shape, 1)
    zero = jnp.zeros_like(q)
    return jnp.concatenate([jnp.where(lane < head_dim, q, zero),
                            jnp.where(lane >= head_dim, q, zero)], axis=0)


def _scores(qq, kblk):
    return lax.dot_general(qq, kblk, (((1,), (1,)), ((), ())), preferred_element_type=F32)


def _finish_heads(acc, l, lam, hn, lam_init, tq):
    o = acc[:tq] / l[:tq] - lam * (acc[tq:] / l[tq:])
    return (_rms(o, hn) * (1.0 - lam_init)).astype(BF16)


def _attn_prompt_kernel(lq1_ref, lk1_ref, lq2_ref, lk2_ref, hn_ref, q_ref, k_ref, v_ref, o_ref,
                        qt1_ref, qt2_ref, vt_ref, s0_ref, s1_ref, p0_ref, p1_ref,
                        *, lam_init, head_dim, tq):
    seq_len = q_ref.shape[0]
    hw = 2 * head_dim
    lam = _diff_lambda(lq1_ref, lk1_ref, lq2_ref, lk2_ref, lam_init)

    def prepare(h):
        cols = slice(h * hw, (h + 1) * hw)
        qt = q_ref[:, cols].astype(F32).T
        dim = lax.broadcasted_iota(jnp.int32, qt.shape, 0)
        qt1_ref[h] = jnp.where(dim < head_dim, qt, 0.0).astype(BF16)
        qt2_ref[h] = jnp.where(dim >= head_dim, qt, 0.0).astype(BF16)
        vt_ref[h, :hw, :] = v_ref[:, cols].T.astype(BF16)
        vt_ref[h, hw:, :] = jnp.ones((vt_ref.shape[1] - hw, seq_len), BF16)

    key_chunk = lax.broadcasted_iota(jnp.int32, (tq, 2 * tq), 0) // CHUNK
    qry_chunk = (lax.broadcasted_iota(jnp.int32, (tq, 2 * tq), 1) % tq) // CHUNK
    visible = key_chunk <= qry_chunk

    bufs = ((s0_ref, p0_ref), (s1_ref, p1_ref))
    n_blocks = seq_len // tq
    blocks = [(h, qi) for h in range(q_ref.shape[1] // hw) for qi in range(n_blocks)]

    col_max = {}
    piece = 2 * tq

    def scores(idx):
        h, qi = blocks[idx]
        cols = slice(h * hw, (h + 1) * hw)
        q0, n_keys, s_buf = qi * tq, (qi + 1) * tq, bufs[idx % 2][0]
        if qi == 0:
            prepare(h)
        qqt = jnp.concatenate([qt1_ref[h, :, q0:n_keys], qt2_ref[h, :, q0:n_keys]], axis=1)
        s = jnp.dot(k_ref[q0:n_keys, cols], qqt, preferred_element_type=F32)
        s = jnp.where(visible, s, NEG_INF)
        s_buf[q0:n_keys, :] = s
        m = jnp.max(s, axis=0, keepdims=True)
        yield True
        for r0 in range(0, q0, piece):
            r1 = min(r0 + piece, q0)
            s = jnp.dot(k_ref[r0:r1, cols], qqt, preferred_element_type=F32)
            s_buf[r0:r1, :] = s
            m = jnp.maximum(m, jnp.max(s, axis=0, keepdims=True))
            yield True
        col_max[idx] = m

    def finish(idx):
        h, qi = blocks[idx]
        q0, n_keys = qi * tq, (qi + 1) * tq
        s_buf, p_buf = bufs[idx % 2]
        m = col_max.pop(idx)
        for r0 in range(0, n_keys, piece):
            r1 = min(r0 + piece, n_keys)
            p_buf[r0:r1, :] = jnp.exp2(s_buf[r0:r1, :] - m).astype(BF16)
            yield True
        ot = jnp.dot(vt_ref[h, :, :n_keys], p_buf[:n_keys, :], preferred_element_type=F32)
        ot = ot[:hw] / ot[hw:hw + 1]
        o = (ot[:, :tq] - lam * ot[:, tq:]).T
        o_ref[q0:n_keys, h * hw:(h + 1) * hw] = (_rms(o, hn_ref[...]) * (1.0 - lam_init)).astype(BF16)

    for _ in scores(0):
        pass
    for idx in range(len(blocks)):
        streams = [finish(idx)]
        if idx + 1 < len(blocks):
            streams.insert(0, scores(idx + 1))
        while streams:
            streams = [s for s in streams if next(s, None) is not None]


def _attn_prompt(q, k, v, lams, head_norm, *, batch, seq_len, lam_init, tq):
    n, d_attn = q.shape
    hw = d_attn // N_HEADS
    nh = ATTN_HEADS_PER_STEP
    assert seq_len % tq == 0 and tq % CHUNK == 0 and N_HEADS % nh == 0
    small = pl.BlockSpec((1, hw // 2), lambda b, h: (0, 0))
    head = pl.BlockSpec((seq_len, nh * hw), lambda b, h: (b, h))
    kern = functools.partial(_attn_prompt_kernel, lam_init=lam_init, head_dim=hw // 2, tq=tq)
    return pl.pallas_call(
        kern,
        grid=(batch, N_HEADS // nh),
        in_specs=[small, small, small, small, pl.BlockSpec((1, hw), lambda b, h: (0, 0)),
                  head, head, head],
        out_specs=head,
        out_shape=jax.ShapeDtypeStruct((n, d_attn), BF16),
        scratch_shapes=[pltpu.VMEM((nh, hw, seq_len), BF16), pltpu.VMEM((nh, hw, seq_len), BF16),
                        pltpu.VMEM((nh, hw + BF16_SUBLANES, seq_len), BF16),
                        pltpu.VMEM((seq_len, 2 * tq), F32), pltpu.VMEM((seq_len, 2 * tq), F32),
                        pltpu.VMEM((seq_len, 2 * tq), BF16), pltpu.VMEM((seq_len, 2 * tq), BF16)],
        compiler_params=_params(2),
        name="attn_prompt",
    )(*lams, head_norm, q, k, v)


def _attn_sample_kernel(lq1_ref, lk1_ref, lq2_ref, lk2_ref, hn_ref, q_ref, kc_ref, vc_ref,
                        kn_ref, vn_ref, o_ref, *, lam_init, head_dim):
    tq = q_ref.shape[0]
    hw = 2 * head_dim
    lam = _diff_lambda(lq1_ref, lk1_ref, lq2_ref, lk2_ref, lam_init)
    for h in range(q_ref.shape[1] // hw):
        cols = slice(h * hw, (h + 1) * hw)
        qq = _stacked_queries(q_ref[:, cols], head_dim)
        s_past = jnp.dot(qq, kc_ref[cols, :].astype(BF16), preferred_element_type=F32)
        s_new = _scores(qq, kn_ref[:, cols])
        m = jnp.maximum(jnp.max(s_past, axis=1, keepdims=True), jnp.max(s_new, axis=1, keepdims=True))
        p_past = jnp.exp2(s_past - m)
        p_new = jnp.exp2(s_new - m)
        l = jnp.sum(p_past, axis=1, keepdims=True) + jnp.sum(p_new, axis=1, keepdims=True)
        acc = (jnp.dot(p_past.astype(BF16), vc_ref[:, cols].astype(BF16), preferred_element_type=F32)
               + jnp.dot(p_new.astype(BF16), vn_ref[:, cols].astype(BF16), preferred_element_type=F32))
        o_ref[:, cols] = _finish_heads(acc, l, lam, hn_ref[...], lam_init, tq)


def _attn_sample(q, k_new, v_new, k_past, v_past, lams, head_norm, *, batch, seq_len, past_len, lam_init):
    n, d_attn = q.shape
    hw = d_attn // N_HEADS
    small = pl.BlockSpec((1, hw // 2), lambda b: (0, 0))
    new_spec = pl.BlockSpec((seq_len, d_attn), lambda b: (b, 0))
    past_spec = pl.BlockSpec((past_len, d_attn), lambda b: (b, 0))
    past_t_spec = pl.BlockSpec((d_attn, past_len), lambda b: (b, 0))
    kern = functools.partial(_attn_sample_kernel, lam_init=lam_init, head_dim=hw // 2)
    return pl.pallas_call(
        kern,
        grid=(batch,),
        in_specs=[small, small, small, small, pl.BlockSpec((1, hw), lambda b: (0, 0)),
                  new_spec, past_t_spec, past_spec, new_spec, new_spec],
        out_specs=new_spec,
        out_shape=jax.ShapeDtypeStruct((n, d_attn), BF16),
        compiler_params=_params(1),
        name="attn_sample",
    )(*lams, head_norm, q, k_past, v_past, k_new, v_new)


R_E1, R_E2, R_W1, R_W2, R_RANK1, R_RANK2 = 0, 1, 2, 3, 4, 5
OUT_PROJ_CHUNKS = 4


def _pack_bf16_pairs(lo_bf, hi_bf):
    lo = lax.bitcast_convert_type(lo_bf.astype(F32), U32)
    hi = lax.bitcast_convert_type(hi_bf.astype(F32), U32)
    return (hi & jnp.uint32(0xFFFF0000)) | (lo >> 16)


def _unpack_bf16_pairs(w):
    lo = lax.bitcast_convert_type(w << 16, F32).astype(BF16)
    hi = lax.bitcast_convert_type(w & jnp.uint32(0xFFFF0000), F32).astype(BF16)
    return lo, hi


def _out_proj_kernel(*refs, n_prompt_tiles):
    ap_ref, op_ref, xp_ref, as_ref, os_ref, xs_ref = refs[:6]
    n_w = 2 * OUT_PROJ_CHUNKS
    w_refs = (refs[6:6 + OUT_PROJ_CHUNKS], refs[6 + OUT_PROJ_CHUNKS:6 + n_w])
    g_ref, wr_ref, br_ref, h_ref, hnp_ref, rinfo_ref, rinfo_t_ref, cnt_ref = refs[6 + n_w:-5]
    hbufs, sss, carry_ref = refs[-5:-3], refs[-3:-1], refs[-1]
    i = pl.program_id(0)
    n_tiles = n_prompt_tiles + 1

    @pl.when(i == 0)
    def _():
        carry_ref[...] = jnp.zeros_like(carry_ref)

    def step(slot, inputs, route):
        streams = []
        if inputs is not None:
            streams.append(_out_proj_project(*inputs, w_refs, hbufs[slot], sss[slot]))
        if route:
            streams.append(_out_proj_route(hbufs[1 - slot], sss[1 - slot], g_ref, wr_ref, br_ref,
                                           h_ref, hnp_ref, rinfo_ref, rinfo_t_ref, cnt_ref, carry_ref))
        while streams:
            streams = [s for s in streams if next(s, None) is not None]

    prompt, sample = (ap_ref, op_ref, xp_ref), (as_ref, os_ref, xs_ref)
    pl.when(i == 0)(functools.partial(step, 0, prompt, False))
    for parity in (0, 1):
        pl.when((i >= 1) & (i < n_prompt_tiles) & (i % 2 == parity))(
            functools.partial(step, parity, prompt, True))
    pl.when(i == n_prompt_tiles)(functools.partial(step, n_prompt_tiles % 2, sample, True))
    pl.when(i == n_tiles)(functools.partial(step, n_tiles % 2, None, True))


def _out_proj_project(a_ref, o_ref, x_ref, w_refs, h_ref, ss_ref):
    tm, d = x_ref.shape
    cw = d // OUT_PROJ_CHUNKS
    a, o = a_ref[...], o_ref[...]
    ss = jnp.zeros((tm, 1), F32)
    for c in range(OUT_PROJ_CHUNKS):
        cols = slice(c * cw, (c + 1) * cw)
        h = x_ref[:, cols] + jnp.dot(a, w_refs[0][c][...], preferred_element_type=F32)
        yield True
        h = h + jnp.dot(o, w_refs[1][c][...], preferred_element_type=F32)
        h_ref[:, cols] = h
        ss = ss + jnp.sum(h * h, axis=1, keepdims=True)
        yield True
    ss_ref[...] = jnp.broadcast_to(ss, ss_ref.shape)


def _out_proj_route(hb_ref, ss_ref, g_ref, wr_ref, br_ref,
                    h_ref, hnp_ref, rinfo_ref, rinfo_t_ref, cnt_ref, carry_ref):
    tm, d = hb_ref.shape
    cw = d // OUT_PROJ_CHUNKS
    h_ref[...] = hb_ref[...]
    inv = lax.rsqrt(ss_ref[:, 0:1] / d + EPS)
    yield True

    logits = br_ref[...]
    for c in range(OUT_PROJ_CHUNKS // 2):
        halves = []
        for cc in (c, c + OUT_PROJ_CHUNKS // 2):
            cols = slice(cc * cw, (cc + 1) * cw)
            hb = (hb_ref[:, cols] * inv * g_ref[:, cols]).astype(BF16)
            logits = logits + jnp.dot(hb, wr_ref[cols, :], preferred_element_type=F32)
            halves.append(hb)
        hnp_ref[:, c * cw:(c + 1) * cw] = _pack_bf16_pairs(*halves)
        yield True

    lane = lax.broadcasted_iota(jnp.int32, logits.shape, 1)
    lane_f = lane.astype(F32)
    lg = jnp.where(lane < N_GROUPS, logits, NEG_INF)
    mg = jnp.max(lg, axis=1, keepdims=True)
    grp = jnp.min(jnp.where(lg == mg, lane_f, float(LANES)), axis=1, keepdims=True)
    gate_g = 1.0 / jnp.sum(jnp.exp(lg - mg), axis=1, keepdims=True)
    yield True
    ex = lane - N_GROUPS
    group_of = ex >> (EXPERTS_PER_GROUP.bit_length() - 1)
    in_grp = (ex >= 0) & (ex < N_EXPERTS) & (group_of.astype(F32) == grp)
    le = jnp.where(in_grp, logits, NEG_INF)
    v1 = jnp.max(le, axis=1, keepdims=True)
    i1 = jnp.min(jnp.where(le == v1, lane_f, float(LANES)), axis=1, keepdims=True)
    yield True
    le2 = jnp.where(lane_f == i1, NEG_INF, le)
    v2 = jnp.max(le2, axis=1, keepdims=True)
    i2 = jnp.min(jnp.where(le2 == v2, lane_f, float(LANES)), axis=1, keepdims=True)
    e21 = jnp.exp(v2 - v1)
    w1 = gate_g / (1.0 + e21)
    w2 = gate_g * e21 / (1.0 + e21)
    e1 = i1 - float(N_GROUPS)
    e2 = i2 - float(N_GROUPS)
    yield True

    onehot = jnp.where((lane_f == e1) | (lane_f == e2), 1.0, 0.0)
    r = lax.broadcasted_iota(jnp.int32, (tm, tm), 0)
    c = lax.broadcasted_iota(jnp.int32, (tm, tm), 1)
    earlier = jnp.where(c < r, 1.0, 0.0).astype(BF16)
    before = jnp.dot(earlier, onehot.astype(BF16), preferred_element_type=F32) + carry_ref[0:1, :]
    rank1 = _lane_pick(before, lane_f, e1)
    rank2 = _lane_pick(before, lane_f, e2)
    new_carry = carry_ref[0:1, :] + jnp.sum(onehot, axis=0, keepdims=True)
    carry_ref[...] = jnp.broadcast_to(new_carry, carry_ref.shape)
    cnt_ref[...] = jnp.broadcast_to(new_carry, cnt_ref.shape)
    yield True

    rec = jnp.zeros_like(logits)
    for idx, val in ((R_E1, e1), (R_E2, e2), (R_W1, w1), (R_W2, w2), (R_RANK1, rank1), (R_RANK2, rank2)):
        rec = jnp.where(lane == idx, val, rec)
    rinfo_ref[...] = rec
    rinfo_t_ref[...] = rec.T[:SUBLANES, :]


def _out_proj(a_p, o_p, x_p, a_s, o_s, x_s, w_out_bf, norm_g, w_router, b_router, *, tm):
    n_p, d = x_p.shape
    n_s = x_s.shape[0]
    dc = a_p.shape[1]
    assert n_p % tm == 0 and n_s == tm, "sample rows must fill exactly one tile"
    assert w_out_bf.shape == (2 * dc, d)
    npt = n_p // tm
    n_all = n_p + n_s
    cw = d // OUT_PROJ_CHUNKS
    p_idx = lambda i: (jnp.minimum(i, npt - 1), 0)
    zero = lambda i: (0, 0)

    def wspec(half, c):
        return pl.BlockSpec((dc, cw), lambda i: (half, c))

    kern = functools.partial(_out_proj_kernel, n_prompt_tiles=npt)
    done = lambda i: (jnp.maximum(i - 1, 0), 0)
    return pl.pallas_call(
        kern,
        grid=(npt + 2,),
        in_specs=[
            pl.BlockSpec((tm, dc), p_idx), pl.BlockSpec((tm, dc), p_idx), pl.BlockSpec((tm, d), p_idx),
            pl.BlockSpec((tm, dc), zero), pl.BlockSpec((tm, dc), zero), pl.BlockSpec((tm, d), zero),
            *[wspec(half, c) for half in range(2) for c in range(OUT_PROJ_CHUNKS)],
            pl.BlockSpec((1, d), zero),
            pl.BlockSpec((d, LANES), zero), pl.BlockSpec((1, LANES), zero),
        ],
        out_specs=[
            pl.BlockSpec((tm, d), done),
            pl.BlockSpec((tm, d // 2), done),
            pl.BlockSpec((tm, LANES), done),
            pl.BlockSpec((SUBLANES, tm), lambda i: (0, jnp.maximum(i - 1, 0))),
            pl.BlockSpec((SUBLANES, LANES), zero),
        ],
        out_shape=[
            jax.ShapeDtypeStruct((n_all, d), F32),
            jax.ShapeDtypeStruct((n_all, d // 2), U32),
            jax.ShapeDtypeStruct((n_all, LANES), F32),
            jax.ShapeDtypeStruct((SUBLANES, n_all), F32),
            jax.ShapeDtypeStruct((SUBLANES, LANES), F32),
        ],
        scratch_shapes=[pltpu.VMEM((tm, d), F32), pltpu.VMEM((tm, d), F32),
                        pltpu.VMEM((tm, LANES), F32), pltpu.VMEM((tm, LANES), F32),
                        pltpu.VMEM((SUBLANES, LANES), F32)],
        compiler_params=_params(1),
        name="out_proj",
    )(a_p, o_p, x_p, a_s, o_s, x_s, *([w_out_bf] * (2 * OUT_PROJ_CHUNKS)), norm_g, w_router, b_router)


EXPERT_DMA_GROUPS = 16
EXPERT_SLOTS = 3


def _experts_kernel(tile_expert_ref, n_used_ref, src_hbm, hnp_hbm, wi_f32_ref, wo_f32_ref, ys_ref,
                    src_ref, xs_ref, gu_ref, wi_ref, wo_ref, src_sem, gather_sem):
    w = pl.program_id(0)
    n_used = n_used_ref[0]
    tm = xs_ref.shape[1]
    n_tiles = pl.num_programs(0) - 1
    group = tm // EXPERT_DMA_GROUPS

    def src_copy(tile, slot):
        return pltpu.make_async_copy(src_hbm.at[pl.ds(tile, 1), :], src_ref.at[pl.ds(slot, 1), :],
                                     src_sem.at[slot])

    def gather_row(slot, r):
        return pltpu.make_async_copy(hnp_hbm.at[pl.ds(src_ref[slot, r], 1), :],
                                     xs_ref.at[slot, pl.ds(r, 1), :], gather_sem.at[slot])

    def wait_gathers(slot):
        pltpu.make_async_copy(xs_ref.at[slot], xs_ref.at[slot], gather_sem.at[slot]).wait()

    @pl.when(w == 0)
    def _():
        for t in range(EXPERT_SLOTS):
            src_copy(t, t).start()
        for t in range(EXPERT_SLOTS - 1):
            src_copy(t, t).wait()
            for r in range(tm):
                gather_row(t, r).start(priority=1)

    def compute_tile(cur):
        ahead = (cur + EXPERT_SLOTS - 1) % EXPERT_SLOTS
        wait_gathers(cur)
        src_copy(w + 2, ahead).wait()
        src_copy(w + 3, cur).start()

        def move_rows(g):
            for r in range(g * group, (g + 1) * group):
                gather_row(ahead, r).start(priority=1)

        @pl.when((w == 0) | (tile_expert_ref[w] != tile_expert_ref[jnp.maximum(w - 1, 0)]))
        def _():
            wi_ref[...] = wi_f32_ref[0].astype(BF16)
            wo_ref[...] = wo_f32_ref[0].astype(BF16)

        lo, hi = _unpack_bf16_pairs(xs_ref[cur])
        half = lo.shape[1]
        n_chunks = EXPERT_DMA_GROUPS // 4
        f2 = wi_ref.shape[1]
        cw = f2 // n_chunks
        fh = f2 // 4
        d = wo_ref.shape[1]
        ow = d // n_chunks
        groups = iter(range(EXPERT_DMA_GROUPS))

        def gate_up(c):
            cols = slice(c * cw, (c + 1) * cw)
            part = jnp.dot(lo, wi_ref[:half, cols], preferred_element_type=F32)
            move_rows(next(groups))
            gu_ref[:, cols] = part + jnp.dot(hi, wi_ref[half:, cols], preferred_element_type=F32)
            move_rows(next(groups))

        def hidden(part):
            gate = gu_ref[:, part * fh:(part + 1) * fh]
            up = gu_ref[:, f2 // 2 + part * fh:f2 // 2 + (part + 1) * fh]
            return (gate * jax.nn.sigmoid(gate) * up).astype(BF16)

        gate_up(0)
        gate_up(n_chunks // 2)
        act_lo = hidden(0)
        for c in range(1, n_chunks // 2):
            gate_up(c)
            gate_up(n_chunks // 2 + c)
        for c in range(n_chunks):
            cols = slice(c * ow, (c + 1) * ow)
            ys_ref[:, cols] = jnp.dot(act_lo, wo_ref[:fh, cols], preferred_element_type=F32)
            move_rows(next(groups))
        act_hi = hidden(1)
        for c in range(n_chunks):
            cols = slice(c * ow, (c + 1) * ow)
            ys_ref[:, cols] = ys_ref[:, cols] + jnp.dot(act_hi, wo_ref[fh:, cols], preferred_element_type=F32)
            move_rows(next(groups))

    def drain(cur):
        wait_gathers(cur)
        wait_gathers((cur + 1) % EXPERT_SLOTS)
        src_copy(w + 2, (cur + EXPERT_SLOTS - 1) % EXPERT_SLOTS).wait()

    for key in range(EXPERT_SLOTS):
        mine = (w % EXPERT_SLOTS) == key
        pl.when(mine & (w < n_used))(functools.partial(compute_tile, key))
        pl.when(mine & (w == n_used))(functools.partial(drain, key))

    @pl.when((w >= n_used) & (w < n_tiles))
    def _():
        ys_ref[...] = jnp.zeros_like(ys_ref)


def _experts(tile_expert, n_used, src_tab, hnp, w_ei, w_eo, *, tm):
    n_e, d, f2 = w_ei.shape
    dw = hnp.shape[1]
    n_tiles = src_tab.shape[0] - EXPERT_SLOTS
    n_steps = n_tiles + 1
    assert tm % EXPERT_DMA_GROUPS == 0 and src_tab.shape[1] == tm

    def expert_of(w, te, nu):
        return te[jnp.minimum(w, nu[0] - 1)]

    grid_spec = pltpu.PrefetchScalarGridSpec(
        num_scalar_prefetch=2,
        grid=(n_steps,),
        in_specs=[
            pl.BlockSpec(memory_space=pl.ANY),
            pl.BlockSpec(memory_space=pl.ANY),
            pl.BlockSpec((1, d, f2), lambda w, te, nu: (expert_of(w, te, nu), 0, 0)),
            pl.BlockSpec((1, f2 // 2, d), lambda w, te, nu: (expert_of(w, te, nu), 0, 0)),
        ],
        out_specs=pl.BlockSpec((tm, d), lambda w, te, nu: (jnp.minimum(w, n_tiles - 1), 0)),
        scratch_shapes=[pltpu.SMEM((EXPERT_SLOTS, tm), jnp.int32), pltpu.VMEM((EXPERT_SLOTS, tm, dw), U32),
                        pltpu.VMEM((tm, f2), F32),
                        pltpu.VMEM((d, f2), BF16), pltpu.VMEM((f2 // 2, d), BF16),
                        pltpu.SemaphoreType.DMA((EXPERT_SLOTS,)), pltpu.SemaphoreType.DMA((EXPERT_SLOTS,))],
    )
    return pl.pallas_call(
        _experts_kernel,
        grid_spec=grid_spec,
        out_shape=jax.ShapeDtypeStruct((n_tiles * tm, d), F32),
        compiler_params=_params(1),
        name="experts",
    )(tile_expert, n_used, src_tab, hnp, w_ei, w_eo)


def _expert_tiles(counts, e12, r12, n_all, tm):
    max_tiles = (2 * n_all) // tm + N_EXPERTS
    n_rows = max_tiles + EXPERT_SLOTS
    tiles_per_expert = (counts + tm - 1) // tm
    tile_end = jnp.cumsum(tiles_per_expert)
    n_used = tile_end[-1]
    row_start = (tile_end - tiles_per_expert) * tm
    t = jnp.minimum(jnp.arange(n_rows, dtype=jnp.int32), n_used - 1)
    tile_expert = jnp.sum(tile_end[None, :] <= t[:, None], axis=1).astype(jnp.int32)

    experts = jnp.arange(N_EXPERTS, dtype=jnp.int32)[:, None, None]
    pos = jnp.sum(jnp.where(e12[None] == experts, row_start[:, None, None], 0), axis=0) + r12
    token = jnp.broadcast_to(jnp.arange(n_all, dtype=jnp.int32), (2, n_all))
    src = jnp.zeros((n_rows * tm,), jnp.int32).at[pos.reshape(-1)].set(
        token.reshape(-1), unique_indices=True, mode="promise_in_bounds").reshape(n_rows, tm)
    pos_tiles = pos.reshape(2, n_all // tm, tm).transpose(1, 0, 2).reshape(n_all // tm, 2 * tm)
    return tile_expert, n_used.reshape(1).astype(jnp.int32), src, pos_tiles.astype(jnp.int32)


def _combine_kernel(pos_hbm, h_ref, rinfo_ref, g_ref, ys_hbm, yp_ref, ysm_ref,
                    idx_ref, buf_ref, idx_sem, row_sem, *, n_prompt_tiles):
    i = pl.program_id(0)
    nt = pl.num_programs(0)
    tm = h_ref.shape[0]

    def idx_copy(tile, slot):
        return pltpu.make_async_copy(pos_hbm.at[pl.ds(tile, 1), :], idx_ref.at[pl.ds(slot, 1), :],
                                     idx_sem.at[slot])

    def start_rows(slot):
        def body(r, _):
            for k in range(2):
                pltpu.make_async_copy(ys_hbm.at[pl.ds(idx_ref[slot, k * tm + r], 1), :],
                                      buf_ref.at[slot, k, pl.ds(r, 1), :], row_sem.at[slot]).start(priority=1)
            return 0
        lax.fori_loop(0, tm, body, 0, unroll=8)

    def wait_rows(slot):
        pltpu.make_async_copy(buf_ref.at[slot], buf_ref.at[slot], row_sem.at[slot]).wait()

    @pl.when(i == 0)
    def _():
        idx_copy(0, 0).start()
        idx_copy(0, 0).wait()
        start_rows(0)

        @pl.when(nt > 1)
        def _():
            idx_copy(1, 1).start()

    def tile(cur):
        nxt = 1 - cur

        @pl.when(i + 1 < nt)
        def _():
            idx_copy(i + 1, nxt).wait()
            start_rows(nxt)

        @pl.when(i + 2 < nt)
        def _():
            idx_copy(i + 2, cur).start()

        wait_rows(cur)
        rinfo = rinfo_ref[...]
        lane = lax.broadcasted_iota(jnp.int32, rinfo.shape, 1)
        w1 = _lane_pick(rinfo, lane, R_W1)
        w2 = _lane_pick(rinfo, lane, R_W2)
        hp = h_ref[...] + (w1 * buf_ref[cur, 0] + w2 * buf_ref[cur, 1])
        y = _rms(hp, g_ref[...])

        @pl.when(i < n_prompt_tiles)
        def _():
            yp_ref[...] = y

        @pl.when(i >= n_prompt_tiles)
        def _():
            ysm_ref[...] = y

    for parity in (0, 1):
        pl.when((i % 2) == parity)(functools.partial(tile, parity))


def _combine(pos_tiles, h, rinfo, norm_g, ys, *, n_prompt, tm):
    n_all, d = h.shape
    nt = n_all // tm
    npt = n_prompt // tm
    assert n_all - n_prompt == tm and pos_tiles.shape == (nt, 2 * tm)
    kern = functools.partial(_combine_kernel, n_prompt_tiles=npt)
    return pl.pallas_call(
        kern,
        grid=(nt,),
        in_specs=[
            pl.BlockSpec(memory_space=pl.ANY),
            pl.BlockSpec((tm, d), lambda i: (i, 0)),
            pl.BlockSpec((tm, LANES), lambda i: (i, 0)),
            pl.BlockSpec((1, d), lambda i: (0, 0)),
            pl.BlockSpec(memory_space=pl.ANY),
        ],
        out_specs=[
            pl.BlockSpec((tm, d), lambda i: (jnp.minimum(i, npt - 1), 0)),
            pl.BlockSpec((tm, d), lambda i: (0, 0)),
        ],
        out_shape=[jax.ShapeDtypeStruct((n_prompt, d), F32), jax.ShapeDtypeStruct((tm, d), F32)],
        scratch_shapes=[pltpu.SMEM((2, 2 * tm), jnp.int32), pltpu.VMEM((2, 2, tm, d), F32),
                        pltpu.SemaphoreType.DMA((2,)), pltpu.SemaphoreType.DMA((2,))],
        compiler_params=_params(1),
        name="combine",
    )(pos_tiles, h, rinfo, norm_g, ys)


ROW_TILE = 256
PROJ_ROWS = 1024
PROJ_COLS = 256
ATTN_Q_ROWS = 256
ATTN_HEADS_PER_STEP = 2


def kernel(x_prompt, x_sample, cache_k, cache_v, state_conv, norm_mix, w_in, conv_w, lambda_q1,
           lambda_k1, lambda_q2, lambda_k2, head_norm, w_out, norm_ffn, w_router_group,
           b_router_group, w_router_expert, b_router_expert, w_expert_in, w_expert_out, norm_final):
    depth = w_in.shape[0]
    b, t, d = x_prompt.shape
    db, dt, _ = x_sample.shape
    past = cache_k.shape[2]
    d_conv = conv_w.shape[2]
    d_attn = d - d_conv
    n_p, n_s = b * t, db * dt
    n_all = n_p + n_s
    tm = ROW_TILE

    hp = x_prompt.reshape(n_p, d)
    hs = x_sample.reshape(n_s, d)
    outs = {name: [] for name in ("kp", "vp", "cp", "ks", "vs", "cs")}
    y_p = y_s = None
    for l in range(depth):
        lam_init = 0.8 - 0.6 * math.exp(-0.3 * l)
        w_in_bf = w_in[l].astype(BF16)
        w_out_bf = w_out[l].astype(BF16)
        g_mix = norm_mix[l].reshape(1, d)
        lams = tuple(p[l].reshape(1, -1) for p in (lambda_q1, lambda_k1, lambda_q2, lambda_k2))
        hnorm = head_norm[l].reshape(1, -1)
        pad = LANES - N_GROUPS - N_EXPERTS
        w_router = jnp.concatenate([w_router_group[l], w_router_expert[l], jnp.zeros((d, pad), F32)],
                                   axis=1).astype(BF16)
        b_router = jnp.concatenate([b_router_group[l], b_router_expert[l], jnp.zeros((pad,), F32)]).reshape(1, LANES)

        zero_prev = jnp.zeros((b, CONV_WIDTH - 1, d_conv), F32)
        a_p, q_p, kb_p, kt_p, v_p, tail_p = _in_proj(hp, g_mix, w_in_bf, conv_w[l], zero_prev,
                                                     seq_len=t, tm=min(PROJ_ROWS, t), tn=PROJ_COLS)
        o_p = _attn_prompt(q_p, kb_p, v_p, lams, hnorm, batch=b, seq_len=t, lam_init=lam_init,
                           tq=min(ATTN_Q_ROWS, t))
        prev_rows = jnp.stack([jnp.repeat(state_conv[l][:, 0], dt, axis=0),
                               jnp.repeat(state_conv[l][:, 1], dt, axis=0)])
        a_s, q_s, kb_s, k_s, v_s, u_s = _in_proj(hs, g_mix, w_in_bf, conv_w[l], prev_rows,
                                                 seq_len=dt, tm=n_s, tn=2 * PROJ_COLS)
        kt_past = cache_k[l].transpose(0, 2, 3, 4, 1).reshape(db * d_attn, past)
        o_s = _attn_sample(q_s, kb_s, v_s, kt_past, cache_v[l].reshape(db * past, d_attn), lams, hnorm,
                           batch=db, seq_len=dt, past_len=past, lam_init=lam_init)

        h, hnp, rinfo, rinfo_t, cnt = _out_proj(a_p, o_p, hp, a_s, o_s, hs, w_out_bf,
                                       norm_ffn[l].reshape(1, d), w_router, b_router, tm=tm)
        counts = cnt[0, :N_EXPERTS].astype(jnp.int32)
        e12 = rinfo_t[R_E1:R_E2 + 1].astype(jnp.int32)
        r12 = rinfo_t[R_RANK1:R_RANK2 + 1].astype(jnp.int32)
        tile_expert, n_used, src_tab, pos_tiles = _expert_tiles(counts, e12, r12, n_all, tm)
        ys = _experts(tile_expert, n_used, src_tab, hnp, w_expert_in[l], w_expert_out[l], tm=tm)
        last = l == depth - 1
        g_out = norm_final.reshape(1, d) if last else jnp.ones((1, d), F32)
        y_p, y_s = _combine(pos_tiles, h, rinfo, g_out, ys, n_prompt=n_p, tm=tm)
        assert last, "multi-layer stacks need an un-normalised combine output"

        outs["kp"].append(kt_p.reshape(b, N_HEADS, 2, d_attn // (2 * N_HEADS), t).transpose(0, 4, 1, 2, 3))
        outs["vp"].append(v_p.reshape(b, t, N_HEADS, d_attn // N_HEADS))
        tails = tail_p.reshape(b, t // min(PROJ_ROWS, t), SUBLANES, d_conv)
        outs["cp"].append(tails[:, -1, SUBLANES - (CONV_WIDTH - 1):, :])
        outs["ks"].append(k_s.reshape(db, dt, N_HEADS, 2, d_attn // (2 * N_HEADS)))
        outs["vs"].append(v_s.reshape(db, dt, N_HEADS, d_attn // N_HEADS))
        outs["cs"].append(u_s.reshape(db, dt, d_conv)[:, dt - (CONV_WIDTH - 1):, :])

    return (y_p.reshape(b, t, d), y_s.reshape(db, dt, d),
            jnp.stack(outs["kp"]), jnp.stack(outs["vp"]), jnp.stack(outs["cp"]),
            jnp.stack(outs["ks"]), jnp.stack(outs["vs"]), jnp.stack(outs["cs"]))
```

```python
import functools
import math

import jax
import jax.numpy as jnp
from jax import lax
from jax.experimental import pallas as pl
from jax.experimental.pallas import tpu as pltpu
from jax.experimental.pallas import tpu_sc as plsc

EPS = 1e-6
N_HEADS = 8
CHUNK = 64
CONV_WIDTH = 3
N_GROUPS = 4
EXPERTS_PER_GROUP = 8
N_EXPERTS = N_GROUPS * EXPERTS_PER_GROUP
assert EXPERTS_PER_GROUP & (EXPERTS_PER_GROUP - 1) == 0, "expert -> group uses a shift"
LANES = 128
SUBLANES = 8
BF16_SUBLANES = 16
VMEM_LIMIT = 56 * 1024 * 1024

F32 = jnp.float32
BF16 = jnp.bfloat16
U32 = jnp.uint32
NEG_INF = float("-inf")


def _params(n_axes):
    return pltpu.CompilerParams(dimension_semantics=("arbitrary",) * n_axes,
                                vmem_limit_bytes=VMEM_LIMIT)


def _rms(x, g):
    return x * lax.rsqrt(jnp.mean(x * x, axis=-1, keepdims=True) + EPS) * g


def _lane_pick(x, lane, idx):
    return jnp.sum(jnp.where(lane == idx, x, 0.0), axis=1, keepdims=True)


def _in_proj_kernel(x_ref, g_ref, wgb_ref, wgc_ref, wx_ref, wq_ref, wk_ref, wv_ref, cw_ref, prev_ref,
                    a_ref, q_ref, kb_ref, ko_ref, v_ref, u_ref, xn_ref, carry_ref,
                    *, seq_len, tiles_per_seq, rows_span_sequences, q_scale):
    i = pl.program_id(0)
    j = pl.program_id(1)
    tm = x_ref.shape[0]

    @pl.when(j == 0)
    def _():
        xn_ref[...] = _rms(x_ref[...], g_ref[...]).astype(BF16)

    xn = xn_ref[...]

    gate_b = jnp.dot(xn, wgb_ref[...], preferred_element_type=F32)
    u = (jnp.dot(xn, wgc_ref[...], preferred_element_type=F32)
         * jnp.dot(xn, wx_ref[...], preferred_element_type=F32))
    row = lax.broadcasted_iota(jnp.int32, u.shape, 0)
    if rows_span_sequences:
        rmod = row % seq_len
        pm2, pm1 = prev_ref[0], prev_ref[1]
        first, second = rmod == 0, rmod == 1
        u_ref[...] = u
    else:
        seq_start = (i % tiles_per_seq) == 0
        tail = carry_ref[j]
        pm2 = jnp.where(seq_start, prev_ref[0, 0:1, :], tail[SUBLANES - 2:SUBLANES - 1, :])
        pm1 = jnp.where(seq_start, prev_ref[0, 1:2, :], tail[SUBLANES - 1:SUBLANES, :])
        first, second = row == 0, row == 1
        new_tail = u[tm - SUBLANES:, :]
        carry_ref[j] = new_tail
        u_ref[0] = new_tail
    u1 = jnp.where(first, pm1, pltpu.roll(u, 1, 0))
    u2 = jnp.where(first, pm2, jnp.where(second, pm1, pltpu.roll(u, 2, 0)))
    cw = cw_ref[...]
    y = cw[0:1, :] * u2 + cw[1:2, :] * u1 + cw[2:3, :] * u
    a_ref[...] = (gate_b * y).astype(BF16)

    q_ref[...] = (jnp.dot(xn, wq_ref[...], preferred_element_type=F32) * q_scale).astype(BF16)
    k = jnp.dot(xn, wk_ref[...], preferred_element_type=F32)
    kb_ref[...] = k.astype(BF16)
    if rows_span_sequences:
        ko_ref[...] = k
    else:
        ko_ref[0] = k.T
    v_ref[...] = jnp.dot(xn, wv_ref[...], preferred_element_type=F32)


def _in_proj(x2d, norm_g, w_in_bf, conv_w, prev, *, seq_len, tm, tn):
    n, d = x2d.shape
    d_conv = conv_w.shape[1]
    assert w_in_bf.shape == (d, 6 * d_conv), "conv and attention widths must match"
    assert n % tm == 0 and d_conv % tn == 0
    nj = d_conv // tn
    n_i = n // tm
    rows_span_sequences = tm > seq_len
    if rows_span_sequences:
        assert tm % seq_len == 0 and prev.shape == (2, n, d_conv)
        tiles_per_seq = 1
        prev_spec = pl.BlockSpec((2, tm, tn), lambda i, j: (0, i, j))
        u_shape = jax.ShapeDtypeStruct((n, d_conv), F32)
        u_spec = pl.BlockSpec((tm, tn), lambda i, j: (i, j))
        ko_shape = jax.ShapeDtypeStruct((n, d_conv), F32)
        ko_spec = pl.BlockSpec((tm, tn), lambda i, j: (i, j))
    else:
        assert seq_len % tm == 0 and prev.shape == (n // seq_len, CONV_WIDTH - 1, d_conv)
        tiles_per_seq = seq_len // tm
        prev_spec = pl.BlockSpec((1, CONV_WIDTH - 1, tn), lambda i, j: (i // tiles_per_seq, 0, j))
        u_shape = jax.ShapeDtypeStruct((n_i, SUBLANES, d_conv), F32)
        u_spec = pl.BlockSpec((1, SUBLANES, tn), lambda i, j: (i, 0, j))
        ko_shape = jax.ShapeDtypeStruct((n // seq_len, d_conv, seq_len), F32)
        ko_spec = pl.BlockSpec((1, tn, tm), lambda i, j: (i // tiles_per_seq, j, i % tiles_per_seq))

    def wspec(k):
        return pl.BlockSpec((d, tn), lambda i, j: (0, k * nj + j))

    tile = pl.BlockSpec((tm, tn), lambda i, j: (i, j))
    head_dim = d_conv // (2 * N_HEADS)
    kern = functools.partial(_in_proj_kernel, seq_len=seq_len, tiles_per_seq=tiles_per_seq,
                             rows_span_sequences=rows_span_sequences,
                             q_scale=head_dim ** -0.5 * math.log2(math.e))
    return pl.pallas_call(
        kern,
        grid=(n_i, nj),
        in_specs=[
            pl.BlockSpec((tm, d), lambda i, j: (i, 0)),
            pl.BlockSpec((1, d), lambda i, j: (0, 0)),
            wspec(0), wspec(1), wspec(2), wspec(3), wspec(4), wspec(5),
            pl.BlockSpec((CONV_WIDTH, tn), lambda i, j: (0, j)),
            prev_spec,
        ],
        out_specs=[tile, tile, tile, ko_spec, tile, u_spec],
        out_shape=[
            jax.ShapeDtypeStruct((n, d_conv), BF16),
            jax.ShapeDtypeStruct((n, d_conv), BF16),
            jax.ShapeDtypeStruct((n, d_conv), BF16),
            ko_shape,
            jax.ShapeDtypeStruct((n, d_conv), F32),
            u_shape,
        ],
        scratch_shapes=[pltpu.VMEM((tm, d), BF16), pltpu.VMEM((nj, SUBLANES, tn), F32)],
        compiler_params=_params(2),
        name="in_proj",
    )(x2d, norm_g, *([w_in_bf] * 6), conv_w, prev)


def _diff_lambda(lq1_ref, lk1_ref, lq2_ref, lk2_ref, lam_init):
    s1 = jnp.sum(lq1_ref[...] * lk1_ref[...], axis=1, keepdims=True)
    s2 = jnp.sum(lq2_ref[...] * lk2_ref[...], axis=1, keepdims=True)
    return jnp.exp(s1) - jnp.exp(s2) + lam_init


def _stacked_queries(q, head_dim):
    lane = lax.broadcasted_iota(jnp.int32, q.shape, 1)
    zero = jnp.zeros_like(q)
    return jnp.concatenate([jnp.where(lane < head_dim, q, zero),
                            jnp.where(lane >= head_dim, q, zero)], axis=0)


def _scores(qq, kblk):
    return lax.dot_general(qq, kblk, (((1,), (1,)), ((), ())), preferred_element_type=F32)


def _finish_heads(acc, l, lam, hn, lam_init, tq):
    o = acc[:tq] / l[:tq] - lam * (acc[tq:] / l[tq:])
    return (_rms(o, hn) * (1.0 - lam_init)).astype(BF16)


def _attn_prompt_kernel(lq1_ref, lk1_ref, lq2_ref, lk2_ref, hn_ref, q_ref, k_ref, v_ref, o_ref,
                        qt1_ref, qt2_ref, vt_ref, s0_ref, s1_ref, p0_ref, p1_ref,
                        *, lam_init, head_dim, tq):
    seq_len = q_ref.shape[0]
    hw = 2 * head_dim
    lam = _diff_lambda(lq1_ref, lk1_ref, lq2_ref, lk2_ref, lam_init)

    def prepare(h):
        cols = slice(h * hw, (h + 1) * hw)
        qt = q_ref[:, cols].astype(F32).T
        dim = lax.broadcasted_iota(jnp.int32, qt.shape, 0)
        qt1_ref[h] = jnp.where(dim < head_dim, qt, 0.0).astype(BF16)
        qt2_ref[h] = jnp.where(dim >= head_dim, qt, 0.0).astype(BF16)
        vt_ref[h, :hw, :] = v_ref[:, cols].T.astype(BF16)
        vt_ref[h, hw:, :] = jnp.ones((vt_ref.shape[1] - hw, seq_len), BF16)

    key_chunk = lax.broadcasted_iota(jnp.int32, (tq, 2 * tq), 0) // CHUNK
    qry_chunk = (lax.broadcasted_iota(jnp.int32, (tq, 2 * tq), 1) % tq) // CHUNK
    visible = key_chunk <= qry_chunk

    bufs = ((s0_ref, p0_ref), (s1_ref, p1_ref))
    n_blocks = seq_len // tq
    blocks = [(h, qi) for h in range(q_ref.shape[1] // hw) for qi in range(n_blocks)]

    col_max = {}
    piece = 2 * tq

    def scores(idx):
        h, qi = blocks[idx]
        cols = slice(h * hw, (h + 1) * hw)
        q0, n_keys, s_buf = qi * tq, (qi + 1) * tq, bufs[idx % 2][0]
        if qi == 0:
            prepare(h)
        qqt = jnp.concatenate([qt1_ref[h, :, q0:n_keys], qt2_ref[h, :, q0:n_keys]], axis=1)
        s = jnp.dot(k_ref[q0:n_keys, cols], qqt, preferred_element_type=F32)
        s = jnp.where(visible, s, NEG_INF)
        s_buf[q0:n_keys, :] = s
        m = jnp.max(s, axis=0, keepdims=True)
        yield True
        for r0 in range(0, q0, piece):
            r1 = min(r0 + piece, q0)
            s = jnp.dot(k_ref[r0:r1, cols], qqt, preferred_element_type=F32)
            s_buf[r0:r1, :] = s
            m = jnp.maximum(m, jnp.max(s, axis=0, keepdims=True))
            yield True
        col_max[idx] = m

    def finish(idx):
        h, qi = blocks[idx]
        q0, n_keys = qi * tq, (qi + 1) * tq
        s_buf, p_buf = bufs[idx % 2]
        m = col_max.pop(idx)
        for r0 in range(0, n_keys, piece):
            r1 = min(r0 + piece, n_keys)
            p_buf[r0:r1, :] = jnp.exp2(s_buf[r0:r1, :] - m).astype(BF16)
            yield True
        ot = jnp.dot(vt_ref[h, :, :n_keys], p_buf[:n_keys, :], preferred_element_type=F32)
        ot = ot[:hw] / ot[hw:hw + 1]
        o = (ot[:, :tq] - lam * ot[:, tq:]).T
        o_ref[q0:n_keys, h * hw:(h + 1) * hw] = (_rms(o, hn_ref[...]) * (1.0 - lam_init)).astype(BF16)

    for _ in scores(0):
        pass
    for idx in range(len(blocks)):
        streams = [finish(idx)]
        if idx + 1 < len(blocks):
            streams.insert(0, scores(idx + 1))
        while streams:
            streams = [s for s in streams if next(s, None) is not None]


def _attn_prompt(q, k, v, lams, head_norm, *, batch, seq_len, lam_init, tq):
    n, d_attn = q.shape
    hw = d_attn // N_HEADS
    nh = ATTN_HEADS_PER_STEP
    assert seq_len % tq == 0 and tq % CHUNK == 0 and N_HEADS % nh == 0
    small = pl.BlockSpec((1, hw // 2), lambda b, h: (0, 0))
    head = pl.BlockSpec((seq_len, nh * hw), lambda b, h: (b, h))
    kern = functools.partial(_attn_prompt_kernel, lam_init=lam_init, head_dim=hw // 2, tq=tq)
    return pl.pallas_call(
        kern,
        grid=(batch, N_HEADS // nh),
        in_specs=[small, small, small, small, pl.BlockSpec((1, hw), lambda b, h: (0, 0)),
                  head, head, head],
        out_specs=head,
        out_shape=jax.ShapeDtypeStruct((n, d_attn), BF16),
        scratch_shapes=[pltpu.VMEM((nh, hw, seq_len), BF16), pltpu.VMEM((nh, hw, seq_len), BF16),
                        pltpu.VMEM((nh, hw + BF16_SUBLANES, seq_len), BF16),
                        pltpu.VMEM((seq_len, 2 * tq), F32), pltpu.VMEM((seq_len, 2 * tq), F32),
                        pltpu.VMEM((seq_len, 2 * tq), BF16), pltpu.VMEM((seq_len, 2 * tq), BF16)],
        compiler_params=_params(2),
        name="attn_prompt",
    )(*lams, head_norm, q, k, v)


def _attn_sample_kernel(lq1_ref, lk1_ref, lq2_ref, lk2_ref, hn_ref, q_ref, kc_ref, vc_ref,
                        kn_ref, vn_ref, o_ref, *, lam_init, head_dim):
    tq = q_ref.shape[0]
    hw = 2 * head_dim
    lam = _diff_lambda(lq1_ref, lk1_ref, lq2_ref, lk2_ref, lam_init)
    for h in range(q_ref.shape[1] // hw):
        cols = slice(h * hw, (h + 1) * hw)
        qq = _stacked_queries(q_ref[:, cols], head_dim)
        s_past = jnp.dot(qq, kc_ref[cols, :].astype(BF16), preferred_element_type=F32)
        s_new = _scores(qq, kn_ref[:, cols])
        m = jnp.maximum(jnp.max(s_past, axis=1, keepdims=True), jnp.max(s_new, axis=1, keepdims=True))
        p_past = jnp.exp2(s_past - m)
        p_new = jnp.exp2(s_new - m)
        l = jnp.sum(p_past, axis=1, keepdims=True) + jnp.sum(p_new, axis=1, keepdims=True)
        acc = (jnp.dot(p_past.astype(BF16), vc_ref[:, cols].astype(BF16), preferred_element_type=F32)
               + jnp.dot(p_new.astype(BF16), vn_ref[:, cols].astype(BF16), preferred_element_type=F32))
        o_ref[:, cols] = _finish_heads(acc, l, lam, hn_ref[...], lam_init, tq)


def _attn_sample(q, k_new, v_new, k_past, v_past, lams, head_norm, *, batch, seq_len, past_len, lam_init):
    n, d_attn = q.shape
    hw = d_attn // N_HEADS
    small = pl.BlockSpec((1, hw // 2), lambda b: (0, 0))
    new_spec = pl.BlockSpec((seq_len, d_attn), lambda b: (b, 0))
    past_spec = pl.BlockSpec((past_len, d_attn), lambda b: (b, 0))
    past_t_spec = pl.BlockSpec((d_attn, past_len), lambda b: (b, 0))
    kern = functools.partial(_attn_sample_kernel, lam_init=lam_init, head_dim=hw // 2)
    return pl.pallas_call(
        kern,
        grid=(batch,),
        in_specs=[small, small, small, small, pl.BlockSpec((1, hw), lambda b: (0, 0)),
                  new_spec, past_t_spec, past_spec, new_spec, new_spec],
        out_specs=new_spec,
        out_shape=jax.ShapeDtypeStruct((n, d_attn), BF16),
        compiler_params=_params(1),
        name="attn_sample",
    )(*lams, head_norm, q, k_past, v_past, k_new, v_new)


R_E1, R_E2, R_W1, R_W2, R_RANK1, R_RANK2 = 0, 1, 2, 3, 4, 5
OUT_PROJ_CHUNKS = 4


def _pack_bf16_pairs(lo_bf, hi_bf):
    lo = lax.bitcast_convert_type(lo_bf.astype(F32), U32)
    hi = lax.bitcast_convert_type(hi_bf.astype(F32), U32)
    return (hi & jnp.uint32(0xFFFF0000)) | (lo >> 16)


def _unpack_bf16_pairs(w):
    lo = lax.bitcast_convert_type(w << 16, F32).astype(BF16)
    hi = lax.bitcast_convert_type(w & jnp.uint32(0xFFFF0000), F32).astype(BF16)
    return lo, hi


def _out_proj_kernel(*refs, n_prompt_tiles):
    ap_ref, op_ref, xp_ref, as_ref, os_ref, xs_ref = refs[:6]
    n_w = 2 * OUT_PROJ_CHUNKS
    w_refs = (refs[6:6 + OUT_PROJ_CHUNKS], refs[6 + OUT_PROJ_CHUNKS:6 + n_w])
    g_ref, wr_ref, br_ref, h_ref, hnp_ref, rinfo_ref, rinfo_t_ref, cnt_ref = refs[6 + n_w:-5]
    hbufs, sss, carry_ref = refs[-5:-3], refs[-3:-1], refs[-1]
    i = pl.program_id(0)
    n_tiles = n_prompt_tiles + 1

    @pl.when(i == 0)
    def _():
        carry_ref[...] = jnp.zeros_like(carry_ref)

    def step(slot, inputs, route):
        streams = []
        if inputs is not None:
            streams.append(_out_proj_project(*inputs, w_refs, hbufs[slot], sss[slot]))
        if route:
            streams.append(_out_proj_route(hbufs[1 - slot], sss[1 - slot], g_ref, wr_ref, br_ref,
                                           h_ref, hnp_ref, rinfo_ref, rinfo_t_ref, cnt_ref, carry_ref))
        while streams:
            streams = [s for s in streams if next(s, None) is not None]

    prompt, sample = (ap_ref, op_ref, xp_ref), (as_ref, os_ref, xs_ref)
    pl.when(i == 0)(functools.partial(step, 0, prompt, False))
    for parity in (0, 1):
        pl.when((i >= 1) & (i < n_prompt_tiles) & (i % 2 == parity))(
            functools.partial(step, parity, prompt, True))
    pl.when(i == n_prompt_tiles)(functools.partial(step, n_prompt_tiles % 2, sample, True))
    pl.when(i == n_tiles)(functools.partial(step, n_tiles % 2, None, True))


def _out_proj_project(a_ref, o_ref, x_ref, w_refs, h_ref, ss_ref):
    tm, d = x_ref.shape
    cw = d // OUT_PROJ_CHUNKS
    a, o = a_ref[...], o_ref[...]
    ss = jnp.zeros((tm, 1), F32)
    for c in range(OUT_PROJ_CHUNKS):
        cols = slice(c * cw, (c + 1) * cw)
        h = x_ref[:, cols] + jnp.dot(a, w_refs[0][c][...], preferred_element_type=F32)
        yield True
        h = h + jnp.dot(o, w_refs[1][c][...], preferred_element_type=F32)
        h_ref[:, cols] = h
        ss = ss + jnp.sum(h * h, axis=1, keepdims=True)
        yield True
    ss_ref[...] = jnp.broadcast_to(ss, ss_ref.shape)


def _out_proj_route(hb_ref, ss_ref, g_ref, wr_ref, br_ref,
                    h_ref, hnp_ref, rinfo_ref, rinfo_t_ref, cnt_ref, carry_ref):
    tm, d = hb_ref.shape
    cw = d // OUT_PROJ_CHUNKS
    h_ref[...] = hb_ref[...]
    inv = lax.rsqrt(ss_ref[:, 0:1] / d + EPS)
    yield True

    logits = br_ref[...]
    for c in range(OUT_PROJ_CHUNKS // 2):
        halves = []
        for cc in (c, c + OUT_PROJ_CHUNKS // 2):
            cols = slice(cc * cw, (cc + 1) * cw)
            hb = (hb_ref[:, cols] * inv * g_ref[:, cols]).astype(BF16)
            logits = logits + jnp.dot(hb, wr_ref[cols, :], preferred_element_type=F32)
            halves.append(hb)
        hnp_ref[:, c * cw:(c + 1) * cw] = _pack_bf16_pairs(*halves)
        yield True

    lane = lax.broadcasted_iota(jnp.int32, logits.shape, 1)
    lane_f = lane.astype(F32)
    lg = jnp.where(lane < N_GROUPS, logits, NEG_INF)
    mg = jnp.max(lg, axis=1, keepdims=True)
    grp = jnp.min(jnp.where(lg == mg, lane_f, float(LANES)), axis=1, keepdims=True)
    gate_g = 1.0 / jnp.sum(jnp.exp(lg - mg), axis=1, keepdims=True)
    yield True
    ex = lane - N_GROUPS
    group_of = ex >> (EXPERTS_PER_GROUP.bit_length() - 1)
    in_grp = (ex >= 0) & (ex < N_EXPERTS) & (group_of.astype(F32) == grp)
    le = jnp.where(in_grp, logits, NEG_INF)
    v1 = jnp.max(le, axis=1, keepdims=True)
    i1 = jnp.min(jnp.where(le == v1, lane_f, float(LANES)), axis=1, keepdims=True)
    yield True
    le2 = jnp.where(lane_f == i1, NEG_INF, le)
    v2 = jnp.max(le2, axis=1, keepdims=True)
    i2 = jnp.min(jnp.where(le2 == v2, lane_f, float(LANES)), axis=1, keepdims=True)
    e21 = jnp.exp(v2 - v1)
    w1 = gate_g / (1.0 + e21)
    w2 = gate_g * e21 / (1.0 + e21)
    e1 = i1 - float(N_GROUPS)
    e2 = i2 - float(N_GROUPS)
    yield True

    onehot = jnp.where((lane_f == e1) | (lane_f == e2), 1.0, 0.0)
    r = lax.broadcasted_iota(jnp.int32, (tm, tm), 0)
    c = lax.broadcasted_iota(jnp.int32, (tm, tm), 1)
    earlier = jnp.where(c < r, 1.0, 0.0).astype(BF16)
    before = jnp.dot(earlier, onehot.astype(BF16), preferred_element_type=F32) + carry_ref[0:1, :]
    rank1 = _lane_pick(before, lane_f, e1)
    rank2 = _lane_pick(before, lane_f, e2)
    new_carry = carry_ref[0:1, :] + jnp.sum(onehot, axis=0, keepdims=True)
    carry_ref[...] = jnp.broadcast_to(new_carry, carry_ref.shape)
    cnt_ref[...] = jnp.broadcast_to(new_carry, cnt_ref.shape)
    yield True

    rec = jnp.zeros_like(logits)
    for idx, val in ((R_E1, e1), (R_E2, e2), (R_W1, w1), (R_W2, w2), (R_RANK1, rank1), (R_RANK2, rank2)):
        rec = jnp.where(lane == idx, val, rec)
    rinfo_ref[...] = rec
    rinfo_t_ref[...] = rec.T[:SUBLANES, :]


def _out_proj(a_p, o_p, x_p, a_s, o_s, x_s, w_out_bf, norm_g, w_router, b_router, *, tm):
    n_p, d = x_p.shape
    n_s = x_s.shape[0]
    dc = a_p.shape[1]
    assert n_p % tm == 0 and n_s == tm, "sample rows must fill exactly one tile"
    assert w_out_bf.shape == (2 * dc, d)
    npt = n_p // tm
    n_all = n_p + n_s
    cw = d // OUT_PROJ_CHUNKS
    p_idx = lambda i: (jnp.minimum(i, npt - 1), 0)
    zero = lambda i: (0, 0)

    def wspec(half, c):
        return pl.BlockSpec((dc, cw), lambda i: (half, c))

    kern = functools.partial(_out_proj_kernel, n_prompt_tiles=npt)
    done = lambda i: (jnp.maximum(i - 1, 0), 0)
    return pl.pallas_call(
        kern,
        grid=(npt + 2,),
        in_specs=[
            pl.BlockSpec((tm, dc), p_idx), pl.BlockSpec((tm, dc), p_idx), pl.BlockSpec((tm, d), p_idx),
            pl.BlockSpec((tm, dc), zero), pl.BlockSpec((tm, dc), zero), pl.BlockSpec((tm, d), zero),
            *[wspec(half, c) for half in range(2) for c in range(OUT_PROJ_CHUNKS)],
            pl.BlockSpec((1, d), zero),
            pl.BlockSpec((d, LANES), zero), pl.BlockSpec((1, LANES), zero),
        ],
        out_specs=[
            pl.BlockSpec((tm, d), done),
            pl.BlockSpec((tm, d // 2), done),
            pl.BlockSpec((tm, LANES), done),
            pl.BlockSpec((SUBLANES, tm), lambda i: (0, jnp.maximum(i - 1, 0))),
            pl.BlockSpec((SUBLANES, LANES), zero),
        ],
        out_shape=[
            jax.ShapeDtypeStruct((n_all, d), F32),
            jax.ShapeDtypeStruct((n_all, d // 2), U32),
            jax.ShapeDtypeStruct((n_all, LANES), F32),
            jax.ShapeDtypeStruct((SUBLANES, n_all), F32),
            jax.ShapeDtypeStruct((SUBLANES, LANES), F32),
        ],
        scratch_shapes=[pltpu.VMEM((tm, d), F32), pltpu.VMEM((tm, d), F32),
                        pltpu.VMEM((tm, LANES), F32), pltpu.VMEM((tm, LANES), F32),
                        pltpu.VMEM((SUBLANES, LANES), F32)],
        compiler_params=_params(1),
        name="out_proj",
    )(a_p, o_p, x_p, a_s, o_s, x_s, *([w_out_bf] * (2 * OUT_PROJ_CHUNKS)), norm_g, w_router, b_router)


EXPERT_DMA_GROUPS = 16
EXPERT_SLOTS = 3


def _experts_kernel(tile_expert_ref, n_used_ref, src_hbm, hnp_hbm, wi_f32_ref, wo_f32_ref, ys_ref,
                    src_ref, xs_ref, gu_ref, wi_ref, wo_ref, src_sem, gather_sem):
    w = pl.program_id(0)
    n_used = n_used_ref[0]
    tm = xs_ref.shape[1]
    n_tiles = pl.num_programs(0) - 1
    group = tm // EXPERT_DMA_GROUPS

    def src_copy(tile, slot):
        return pltpu.make_async_copy(src_hbm.at[pl.ds(tile, 1), :], src_ref.at[pl.ds(slot, 1), :],
                                     src_sem.at[slot])

    def gather_row(slot, r):
        return pltpu.make_async_copy(hnp_hbm.at[pl.ds(src_ref[slot, r], 1), :],
                                     xs_ref.at[slot, pl.ds(r, 1), :], gather_sem.at[slot])

    def wait_gathers(slot):
        pltpu.make_async_copy(xs_ref.at[slot], xs_ref.at[slot], gather_sem.at[slot]).wait()

    @pl.when(w == 0)
    def _():
        for t in range(EXPERT_SLOTS):
            src_copy(t, t).start()
        for t in range(EXPERT_SLOTS - 1):
            src_copy(t, t).wait()
            for r in range(tm):
                gather_row(t, r).start(priority=1)

    def compute_tile(cur):
        ahead = (cur + EXPERT_SLOTS - 1) % EXPERT_SLOTS
        wait_gathers(cur)
        src_copy(w + 2, ahead).wait()
        src_copy(w + 3, cur).start()

        def move_rows(g):
            for r in range(g * group, (g + 1) * group):
                gather_row(ahead, r).start(priority=1)

        @pl.when((w == 0) | (tile_expert_ref[w] != tile_expert_ref[jnp.maximum(w - 1, 0)]))
        def _():
            wi_ref[...] = wi_f32_ref[0].astype(BF16)
            wo_ref[...] = wo_f32_ref[0].astype(BF16)

        lo, hi = _unpack_bf16_pairs(xs_ref[cur])
        half = lo.shape[1]
        n_chunks = EXPERT_DMA_GROUPS // 4
        f2 = wi_ref.shape[1]
        cw = f2 // n_chunks
        fh = f2 // 4
        d = wo_ref.shape[1]
        ow = d // n_chunks
        groups = iter(range(EXPERT_DMA_GROUPS))

        def gate_up(c):
            cols = slice(c * cw, (c + 1) * cw)
            part = jnp.dot(lo, wi_ref[:half, cols], preferred_element_type=F32)
            move_rows(next(groups))
            gu_ref[:, cols] = part + jnp.dot(hi, wi_ref[half:, cols], preferred_element_type=F32)
            move_rows(next(groups))

        def hidden(part):
            gate = gu_ref[:, part * fh:(part + 1) * fh]
            up = gu_ref[:, f2 // 2 + part * fh:f2 // 2 + (part + 1) * fh]
            return (gate * jax.nn.sigmoid(gate) * up).astype(BF16)

        gate_up(0)
        gate_up(n_chunks // 2)
        act_lo = hidden(0)
        for c in range(1, n_chunks // 2):
            gate_up(c)
            gate_up(n_chunks // 2 + c)
        for c in range(n_chunks):
            cols = slice(c * ow, (c + 1) * ow)
            ys_ref[:, cols] = jnp.dot(act_lo, wo_ref[:fh, cols], preferred_element_type=F32)
            move_rows(next(groups))
        act_hi = hidden(1)
        for c in range(n_chunks):
            cols = slice(c * ow, (c + 1) * ow)
            ys_ref[:, cols] = ys_ref[:, cols] + jnp.dot(act_hi, wo_ref[fh:, cols], preferred_element_type=F32)
            move_rows(next(groups))

    def drain(cur):
        wait_gathers(cur)
        wait_gathers((cur + 1) % EXPERT_SLOTS)
        src_copy(w + 2, (cur + EXPERT_SLOTS - 1) % EXPERT_SLOTS).wait()

    for key in range(EXPERT_SLOTS):
        mine = (w % EXPERT_SLOTS) == key
        pl.when(mine & (w < n_used))(functools.partial(compute_tile, key))
        pl.when(mine & (w == n_used))(functools.partial(drain, key))

    @pl.when((w >= n_used) & (w < n_tiles))
    def _():
        ys_ref[...] = jnp.zeros_like(ys_ref)


def _experts(tile_expert, n_used, src_tab, hnp, w_ei, w_eo, *, tm):
    n_e, d, f2 = w_ei.shape
    dw = hnp.shape[1]
    n_tiles = src_tab.shape[0] - EXPERT_SLOTS
    n_steps = n_tiles + 1
    assert tm % EXPERT_DMA_GROUPS == 0 and src_tab.shape[1] == tm

    def expert_of(w, te, nu):
        return te[jnp.minimum(w, nu[0] - 1)]

    grid_spec = pltpu.PrefetchScalarGridSpec(
        num_scalar_prefetch=2,
        grid=(n_steps,),
        in_specs=[
            pl.BlockSpec(memory_space=pl.ANY),
            pl.BlockSpec(memory_space=pl.ANY),
            pl.BlockSpec((1, d, f2), lambda w, te, nu: (expert_of(w, te, nu), 0, 0)),
            pl.BlockSpec((1, f2 // 2, d), lambda w, te, nu: (expert_of(w, te, nu), 0, 0)),
        ],
        out_specs=pl.BlockSpec((tm, d), lambda w, te, nu: (jnp.minimum(w, n_tiles - 1), 0)),
        scratch_shapes=[pltpu.SMEM((EXPERT_SLOTS, tm), jnp.int32), pltpu.VMEM((EXPERT_SLOTS, tm, dw), U32),
                        pltpu.VMEM((tm, f2), F32),
                        pltpu.VMEM((d, f2), BF16), pltpu.VMEM((f2 // 2, d), BF16),
                        pltpu.SemaphoreType.DMA((EXPERT_SLOTS,)), pltpu.SemaphoreType.DMA((EXPERT_SLOTS,))],
    )
    return pl.pallas_call(
        _experts_kernel,
        grid_spec=grid_spec,
        out_shape=jax.ShapeDtypeStruct((n_tiles * tm, d), F32),
        compiler_params=_params(1),
        name="experts",
    )(tile_expert, n_used, src_tab, hnp, w_ei, w_eo)


SC_SUBCORES = 16
SC_INDEX_ROW = 128


def _token_table(pos, token, table_len):
    n = pos.shape[0]
    per_core = SC_SUBCORES * SC_INDEX_ROW
    rows = -(-n // per_core)
    spare = table_len
    chunk = -(-(table_len + 1) // (SC_SUBCORES * SC_SUBCORES)) * SC_SUBCORES
    out_len = chunk * SC_SUBCORES
    pad = rows * per_core - n
    pos_p = jnp.concatenate([pos, jnp.full((pad,), spare, jnp.int32)]).reshape(SC_SUBCORES, rows, SC_INDEX_ROW)
    tok_p = jnp.concatenate([token, jnp.zeros((pad,), jnp.int32)]).reshape(SC_SUBCORES, rows, SC_INDEX_ROW)
    mesh = plsc.VectorSubcoreMesh(core_axis_name="core", subcore_axis_name="subcore", num_cores=1)

    def body(pos_hbm, tok_hbm, zero_hbm, out_hbm, idx_ref, val_ref, zero_ref):
        s = lax.axis_index("subcore")
        pltpu.sync_copy(zero_hbm, zero_ref)
        pltpu.sync_copy(zero_ref, out_hbm.at[pl.ds(s * chunk, chunk)])
        pltpu.sync_copy(pos_hbm.at[s], idx_ref)
        pltpu.sync_copy(tok_hbm.at[s], val_ref)
        plsc.subcore_barrier()
        for j in range(rows):
            pltpu.sync_copy(val_ref.at[j], out_hbm.at[idx_ref.at[j]])

    table = pl.kernel(
        body,
        out_type=jax.ShapeDtypeStruct((out_len,), jnp.int32),
        mesh=mesh,
        scratch_types=[pltpu.VMEM((rows, SC_INDEX_ROW), jnp.int32), pltpu.VMEM((rows, SC_INDEX_ROW), jnp.int32),
                       pltpu.VMEM((chunk,), jnp.int32)],
        name="token_table",
    )(pos_p, tok_p, jnp.zeros((chunk,), jnp.int32))
    return table[:table_len]


def _expert_tiles(counts, e12, r12, n_all, tm):
    max_tiles = (2 * n_all) // tm + N_EXPERTS
    n_rows = max_tiles + EXPERT_SLOTS
    tiles_per_expert = (counts + tm - 1) // tm
    tile_end = jnp.cumsum(tiles_per_expert)
    n_used = tile_end[-1]
    row_start = (tile_end - tiles_per_expert) * tm
    t = jnp.minimum(jnp.arange(n_rows, dtype=jnp.int32), n_used - 1)
    tile_expert = jnp.sum(tile_end[None, :] <= t[:, None], axis=1).astype(jnp.int32)

    experts = jnp.arange(N_EXPERTS, dtype=jnp.int32)[:, None, None]
    pos = jnp.sum(jnp.where(e12[None] == experts, row_start[:, None, None], 0), axis=0) + r12
    token = jnp.broadcast_to(jnp.arange(n_all, dtype=jnp.int32), (2, n_all))
    src = _token_table(pos.reshape(-1), token.reshape(-1), n_rows * tm).reshape(n_rows, tm)
    pos_tiles = pos.reshape(2, n_all // tm, tm).transpose(1, 0, 2).reshape(n_all // tm, 2 * tm)
    return tile_expert, n_used.reshape(1).astype(jnp.int32), src, pos_tiles.astype(jnp.int32)


def _combine_kernel(pos_hbm, h_ref, rinfo_ref, g_ref, ys_hbm, yp_ref, ysm_ref,
                    idx_ref, buf_ref, idx_sem, row_sem, *, n_prompt_tiles):
    i = pl.program_id(0)
    nt = pl.num_programs(0)
    tm = h_ref.shape[0]

    def idx_copy(tile, slot):
        return pltpu.make_async_copy(pos_hbm.at[pl.ds(tile, 1), :], idx_ref.at[pl.ds(slot, 1), :],
                                     idx_sem.at[slot])

    def start_rows(slot):
        def body(r, _):
            for k in range(2):
                pltpu.make_async_copy(ys_hbm.at[pl.ds(idx_ref[slot, k * tm + r], 1), :],
                                      buf_ref.at[slot, k, pl.ds(r, 1), :], row_sem.at[slot]).start(priority=1)
            return 0
        lax.fori_loop(0, tm, body, 0, unroll=8)

    def wait_rows(slot):
        pltpu.make_async_copy(buf_ref.at[slot], buf_ref.at[slot], row_sem.at[slot]).wait()

    @pl.when(i == 0)
    def _():
        idx_copy(0, 0).start()
        idx_copy(0, 0).wait()
        start_rows(0)

        @pl.when(nt > 1)
        def _():
            idx_copy(1, 1).start()

    def tile(cur):
        nxt = 1 - cur

        @pl.when(i + 1 < nt)
        def _():
            idx_copy(i + 1, nxt).wait()
            start_rows(nxt)

        @pl.when(i + 2 < nt)
        def _():
            idx_copy(i + 2, cur).start()

        wait_rows(cur)
        rinfo = rinfo_ref[...]
        lane = lax.broadcasted_iota(jnp.int32, rinfo.shape, 1)
        w1 = _lane_pick(rinfo, lane, R_W1)
        w2 = _lane_pick(rinfo, lane, R_W2)
        hp = h_ref[...] + (w1 * buf_ref[cur, 0] + w2 * buf_ref[cur, 1])
        y = _rms(hp, g_ref[...])

        @pl.when(i < n_prompt_tiles)
        def _():
            yp_ref[...] = y

        @pl.when(i >= n_prompt_tiles)
        def _():
            ysm_ref[...] = y

    for parity in (0, 1):
        pl.when((i % 2) == parity)(functools.partial(tile, parity))


def _combine(pos_tiles, h, rinfo, norm_g, ys, *, n_prompt, tm):
    n_all, d = h.shape
    nt = n_all // tm
    npt = n_prompt // tm
    assert n_all - n_prompt == tm and pos_tiles.shape == (nt, 2 * tm)
    kern = functools.partial(_combine_kernel, n_prompt_tiles=npt)
    return pl.pallas_call(
        kern,
        grid=(nt,),
        in_specs=[
            pl.BlockSpec(memory_space=pl.ANY),
            pl.BlockSpec((tm, d), lambda i: (i, 0)),
            pl.BlockSpec((tm, LANES), lambda i: (i, 0)),
            pl.BlockSpec((1, d), lambda i: (0, 0)),
            pl.BlockSpec(memory_space=pl.ANY),
        ],
        out_specs=[
            pl.BlockSpec((tm, d), lambda i: (jnp.minimum(i, npt - 1), 0)),
            pl.BlockSpec((tm, d), lambda i: (0, 0)),
        ],
        out_shape=[jax.ShapeDtypeStruct((n_prompt, d), F32), jax.ShapeDtypeStruct((tm, d), F32)],
        scratch_shapes=[pltpu.SMEM((2, 2 * tm), jnp.int32), pltpu.VMEM((2, 2, tm, d), F32),
                        pltpu.SemaphoreType.DMA((2,)), pltpu.SemaphoreType.DMA((2,))],
        compiler_params=_params(1),
        name="combine",
    )(pos_tiles, h, rinfo, norm_g, ys)


ROW_TILE = 256
PROJ_ROWS = 1024
PROJ_COLS = 256
ATTN_Q_ROWS = 256
ATTN_HEADS_PER_STEP = 2


def kernel(x_prompt, x_sample, cache_k, cache_v, state_conv, norm_mix, w_in, conv_w, lambda_q1,
           lambda_k1, lambda_q2, lambda_k2, head_norm, w_out, norm_ffn, w_router_group,
           b_router_group, w_router_expert, b_router_expert, w_expert_in, w_expert_out, norm_final):
    depth = w_in.shape[0]
    b, t, d = x_prompt.shape
    db, dt, _ = x_sample.shape
    past = cache_k.shape[2]
    d_conv = conv_w.shape[2]
    d_attn = d - d_conv
    n_p, n_s = b * t, db * dt
    n_all = n_p + n_s
    tm = ROW_TILE

    hp = x_prompt.reshape(n_p, d)
    hs = x_sample.reshape(n_s, d)
    outs = {name: [] for name in ("kp", "vp", "cp", "ks", "vs", "cs")}
    y_p = y_s = None
    for l in range(depth):
        lam_init = 0.8 - 0.6 * math.exp(-0.3 * l)
        w_in_bf = w_in[l].astype(BF16)
        w_out_bf = w_out[l].astype(BF16)
        g_mix = norm_mix[l].reshape(1, d)
        lams = tuple(p[l].reshape(1, -1) for p in (lambda_q1, lambda_k1, lambda_q2, lambda_k2))
        hnorm = head_norm[l].reshape(1, -1)
        pad = LANES - N_GROUPS - N_EXPERTS
        w_router = jnp.concatenate([w_router_group[l], w_router_expert[l], jnp.zeros((d, pad), F32)],
                                   axis=1).astype(BF16)
        b_router = jnp.concatenate([b_router_group[l], b_router_expert[l], jnp.zeros((pad,), F32)]).reshape(1, LANES)

        zero_prev = jnp.zeros((b, CONV_WIDTH - 1, d_conv), F32)
        a_p, q_p, kb_p, kt_p, v_p, tail_p = _in_proj(hp, g_mix, w_in_bf, conv_w[l], zero_prev,
                                                     seq_len=t, tm=min(PROJ_ROWS, t), tn=PROJ_COLS)
        o_p = _attn_prompt(q_p, kb_p, v_p, lams, hnorm, batch=b, seq_len=t, lam_init=lam_init,
                           tq=min(ATTN_Q_ROWS, t))
        prev_rows = jnp.stack([jnp.repeat(state_conv[l][:, 0], dt, axis=0),
                               jnp.repeat(state_conv[l][:, 1], dt, axis=0)])
        a_s, q_s, kb_s, k_s, v_s, u_s = _in_proj(hs, g_mix, w_in_bf, conv_w[l], prev_rows,
                                                 seq_len=dt, tm=n_s, tn=2 * PROJ_COLS)
        kt_past = cache_k[l].transpose(0, 2, 3, 4, 1).reshape(db * d_attn, past)
        o_s = _attn_sample(q_s, kb_s, v_s, kt_past, cache_v[l].reshape(db * past, d_attn), lams, hnorm,
                           batch=db, seq_len=dt, past_len=past, lam_init=lam_init)

        h, hnp, rinfo, rinfo_t, cnt = _out_proj(a_p, o_p, hp, a_s, o_s, hs, w_out_bf,
                                       norm_ffn[l].reshape(1, d), w_router, b_router, tm=tm)
        counts = cnt[0, :N_EXPERTS].astype(jnp.int32)
        e12 = rinfo_t[R_E1:R_E2 + 1].astype(jnp.int32)
        r12 = rinfo_t[R_RANK1:R_RANK2 + 1].astype(jnp.int32)
        tile_expert, n_used, src_tab, pos_tiles = _expert_tiles(counts, e12, r12, n_all, tm)
        ys = _experts(tile_expert, n_used, src_tab, hnp, w_expert_in[l], w_expert_out[l], tm=tm)
        last = l == depth - 1
        g_out = norm_final.reshape(1, d) if last else jnp.ones((1, d), F32)
        y_p, y_s = _combine(pos_tiles, h, rinfo, g_out, ys, n_prompt=n_p, tm=tm)
        assert last, "multi-layer stacks need an un-normalised combine output"

        outs["kp"].append(kt_p.reshape(b, N_HEADS, 2, d_attn // (2 * N_HEADS), t).transpose(0, 4, 1, 2, 3))
        outs["vp"].append(v_p.reshape(b, t, N_HEADS, d_attn // N_HEADS))
        tails = tail_p.reshape(b, t // min(PROJ_ROWS, t), SUBLANES, d_conv)
        outs["cp"].append(tails[:, -1, SUBLANES - (CONV_WIDTH - 1):, :])
        outs["ks"].append(k_s.reshape(db, dt, N_HEADS, 2, d_attn // (2 * N_HEADS)))
        outs["vs"].append(v_s.reshape(db, dt, N_HEADS, d_attn // N_HEADS))
        outs["cs"].append(u_s.reshape(db, dt, d_conv)[:, dt - (CONV_WIDTH - 1):, :])

    return (y_p.reshape(b, t, d), y_s.reshape(db, dt, d),
            jnp.stack(outs["kp"]), jnp.stack(outs["vp"]), jnp.stack(outs["cp"]),
            jnp.stack(outs["ks"]), jnp.stack(outs["vs"]), jnp.stack(outs["cs"]))
```

```python
import functools
import math

import jax
import jax.numpy as jnp
from jax import lax
from jax.experimental import pallas as pl
from jax.experimental.pallas import tpu as pltpu
from jax.experimental.pallas import tpu_sc as plsc

EPS = 1e-6
N_HEADS = 8
CHUNK = 64
CONV_WIDTH = 3
N_GROUPS = 4
EXPERTS_PER_GROUP = 8
N_EXPERTS = N_GROUPS * EXPERTS_PER_GROUP
assert EXPERTS_PER_GROUP & (EXPERTS_PER_GROUP - 1) == 0, "expert -> group uses a shift"
LANES = 128
SUBLANES = 8
BF16_SUBLANES = 16
VMEM_LIMIT = 56 * 1024 * 1024

F32 = jnp.float32
BF16 = jnp.bfloat16
U32 = jnp.uint32
NEG_INF = float("-inf")


def _params(n_axes):
    return pltpu.CompilerParams(dimension_semantics=("arbitrary",) * n_axes,
                                vmem_limit_bytes=VMEM_LIMIT)


def _rms(x, g):
    return x * lax.rsqrt(jnp.mean(x * x, axis=-1, keepdims=True) + EPS) * g


def _lane_pick(x, lane, idx):
    return jnp.sum(jnp.where(lane == idx, x, 0.0), axis=1, keepdims=True)


def _in_proj_kernel(x_ref, g_ref, wgb_ref, wgc_ref, wx_ref, wq_ref, wk_ref, wv_ref, cw_ref, prev_ref,
                    a_ref, q_ref, kb_ref, ko_ref, v_ref, u_ref, xn_ref, carry_ref,
                    *, seq_len, tiles_per_seq, rows_span_sequences, q_scale):
    i = pl.program_id(0)
    j = pl.program_id(1)
    tm = x_ref.shape[0]

    @pl.when(j == 0)
    def _():
        xn_ref[...] = _rms(x_ref[...], g_ref[...]).astype(BF16)

    xn = xn_ref[...]

    gate_b = jnp.dot(xn, wgb_ref[...], preferred_element_type=F32)
    u = (jnp.dot(xn, wgc_ref[...], preferred_element_type=F32)
         * jnp.dot(xn, wx_ref[...], preferred_element_type=F32))
    row = lax.broadcasted_iota(jnp.int32, u.shape, 0)
    if rows_span_sequences:
        rmod = row % seq_len
        pm2, pm1 = prev_ref[0], prev_ref[1]
        first, second = rmod == 0, rmod == 1
        u_ref[...] = u
    else:
        seq_start = (i % tiles_per_seq) == 0
        tail = carry_ref[j]
        pm2 = jnp.where(seq_start, prev_ref[0, 0:1, :], tail[SUBLANES - 2:SUBLANES - 1, :])
        pm1 = jnp.where(seq_start, prev_ref[0, 1:2, :], tail[SUBLANES - 1:SUBLANES, :])
        first, second = row == 0, row == 1
        new_tail = u[tm - SUBLANES:, :]
        carry_ref[j] = new_tail
        u_ref[0] = new_tail
    u1 = jnp.where(first, pm1, pltpu.roll(u, 1, 0))
    u2 = jnp.where(first, pm2, jnp.where(second, pm1, pltpu.roll(u, 2, 0)))
    cw = cw_ref[...]
    y = cw[0:1, :] * u2 + cw[1:2, :] * u1 + cw[2:3, :] * u
    a_ref[...] = (gate_b * y).astype(BF16)

    q_ref[...] = (jnp.dot(xn, wq_ref[...], preferred_element_type=F32) * q_scale).astype(BF16)
    k = jnp.dot(xn, wk_ref[...], preferred_element_type=F32)
    kb_ref[...] = k.astype(BF16)
    if rows_span_sequences:
        ko_ref[...] = k
    else:
        ko_ref[0] = k.T
    v_ref[...] = jnp.dot(xn, wv_ref[...], preferred_element_type=F32)


def _in_proj(x2d, norm_g, w_in_bf, conv_w, prev, *, seq_len, tm, tn):
    n, d = x2d.shape
    d_conv = conv_w.shape[1]
    assert w_in_bf.shape == (d, 6 * d_conv), "conv and attention widths must match"
    assert n % tm == 0 and d_conv % tn == 0
    nj = d_conv // tn
    n_i = n // tm
    rows_span_sequences = tm > seq_len
    if rows_span_sequences:
        assert tm % seq_len == 0 and prev.shape == (2, n, d_conv)
        tiles_per_seq = 1
        prev_spec = pl.BlockSpec((2, tm, tn), lambda i, j: (0, i, j))
        u_shape = jax.ShapeDtypeStruct((n, d_conv), F32)
        u_spec = pl.BlockSpec((tm, tn), lambda i, j: (i, j))
        ko_shape = jax.ShapeDtypeStruct((n, d_conv), F32)
        ko_spec = pl.BlockSpec((tm, tn), lambda i, j: (i, j))
    else:
        assert seq_len % tm == 0 and prev.shape == (n // seq_len, CONV_WIDTH - 1, d_conv)
        tiles_per_seq = seq_len // tm
        prev_spec = pl.BlockSpec((1, CONV_WIDTH - 1, tn), lambda i, j: (i // tiles_per_seq, 0, j))
        u_shape = jax.ShapeDtypeStruct((n_i, SUBLANES, d_conv), F32)
        u_spec = pl.BlockSpec((1, SUBLANES, tn), lambda i, j: (i, 0, j))
        ko_shape = jax.ShapeDtypeStruct((n // seq_len, d_conv, seq_len), F32)
        ko_spec = pl.BlockSpec((1, tn, tm), lambda i, j: (i // tiles_per_seq, j, i % tiles_per_seq))

    def wspec(k):
        return pl.BlockSpec((d, tn), lambda i, j: (0, k * nj + j))

    tile = pl.BlockSpec((tm, tn), lambda i, j: (i, j))
    head_dim = d_conv // (2 * N_HEADS)
    kern = functools.partial(_in_proj_kernel, seq_len=seq_len, tiles_per_seq=tiles_per_seq,
                             rows_span_sequences=rows_span_sequences,
                             q_scale=head_dim ** -0.5 * math.log2(math.e))
    return pl.pallas_call(
        kern,
        grid=(n_i, nj),
        in_specs=[
            pl.BlockSpec((tm, d), lambda i, j: (i, 0)),
            pl.BlockSpec((1, d), lambda i, j: (0, 0)),
            wspec(0), wspec(1), wspec(2), wspec(3), wspec(4), wspec(5),
            pl.BlockSpec((CONV_WIDTH, tn), lambda i, j: (0, j)),
            prev_spec,
        ],
        out_specs=[tile, tile, tile, ko_spec, tile, u_spec],
        out_shape=[
            jax.ShapeDtypeStruct((n, d_conv), BF16),
            jax.ShapeDtypeStruct((n, d_conv), BF16),
            jax.ShapeDtypeStruct((n, d_conv), BF16),
            ko_shape,
            jax.ShapeDtypeStruct((n, d_conv), F32),
            u_shape,
        ],
        scratch_shapes=[pltpu.VMEM((tm, d), BF16), pltpu.VMEM((nj, SUBLANES, tn), F32)],
        compiler_params=_params(2),
        name="in_proj",
    )(x2d, norm_g, *([w_in_bf] * 6), conv_w, prev)


def _diff_lambda(lq1_ref, lk1_ref, lq2_ref, lk2_ref, lam_init):
    s1 = jnp.sum(lq1_ref[...] * lk1_ref[...], axis=1, keepdims=True)
    s2 = jnp.sum(lq2_ref[...] * lk2_ref[...], axis=1, keepdims=True)
    return jnp.exp(s1) - jnp.exp(s2) + lam_init


def _stacked_queries(q, head_dim):
    lane = lax.broadcasted_iota(jnp.int32, q.shape, 1)
    zero = jnp.zeros_like(q)
    return jnp.concatenate([jnp.where(lane < head_dim, q, zero),
                            jnp.where(lane >= head_dim, q, zero)], axis=0)


def _scores(qq, kblk):
    return lax.dot_general(qq, kblk, (((1,), (1,)), ((), ())), preferred_element_type=F32)


def _finish_heads(acc, l, lam, hn, lam_init, tq):
    o = acc[:tq] / l[:tq] - lam * (acc[tq:] / l[tq:])
    return (_rms(o, hn) * (1.0 - lam_init)).astype(BF16)


def _attn_prompt_kernel(lq1_ref, lk1_ref, lq2_ref, lk2_ref, hn_ref, q_ref, k_ref, v_ref, o_ref,
                        qt1_ref, qt2_ref, vt_ref, s0_ref, s1_ref, p0_ref, p1_ref,
                        *, lam_init, head_dim, tq):
    seq_len = q_ref.shape[0]
    hw = 2 * head_dim
    lam = _diff_lambda(lq1_ref, lk1_ref, lq2_ref, lk2_ref, lam_init)

    def prepare(h):
        cols = slice(h * hw, (h + 1) * hw)
        qt = q_ref[:, cols].astype(F32).T
        dim = lax.broadcasted_iota(jnp.int32, qt.shape, 0)
        qt1_ref[h] = jnp.where(dim < head_dim, qt, 0.0).astype(BF16)
        qt2_ref[h] = jnp.where(dim >= head_dim, qt, 0.0).astype(BF16)
        vt_ref[h, :hw, :] = v_ref[:, cols].T.astype(BF16)
        vt_ref[h, hw:, :] = jnp.ones((vt_ref.shape[1] - hw, seq_len), BF16)

    key_chunk = lax.broadcasted_iota(jnp.int32, (tq, 2 * tq), 0) // CHUNK
    qry_chunk = (lax.broadcasted_iota(jnp.int32, (tq, 2 * tq), 1) % tq) // CHUNK
    visible = key_chunk <= qry_chunk

    bufs = ((s0_ref, p0_ref), (s1_ref, p1_ref))
    n_blocks = seq_len // tq
    blocks = [(h, qi) for h in range(q_ref.shape[1] // hw) for qi in range(n_blocks)]

    col_max = {}
    piece = 2 * tq

    def scores(idx):
        h, qi = blocks[idx]
        cols = slice(h * hw, (h + 1) * hw)
        q0, n_keys, s_buf = qi * tq, (qi + 1) * tq, bufs[idx % 2][0]
        if qi == 0:
            prepare(h)
        qqt = jnp.concatenate([qt1_ref[h, :, q0:n_keys], qt2_ref[h, :, q0:n_keys]], axis=1)
        s = jnp.dot(k_ref[q0:n_keys, cols], qqt, preferred_element_type=F32)
        s = jnp.where(visible, s, NEG_INF)
        s_buf[q0:n_keys, :] = s
        m = jnp.max(s, axis=0, keepdims=True)
        yield True
        for r0 in range(0, q0, piece):
            r1 = min(r0 + piece, q0)
            s = jnp.dot(k_ref[r0:r1, cols], qqt, preferred_element_type=F32)
            s_buf[r0:r1, :] = s
            m = jnp.maximum(m, jnp.max(s, axis=0, keepdims=True))
            yield True
        col_max[idx] = m

    def finish(idx):
        h, qi = blocks[idx]
        q0, n_keys = qi * tq, (qi + 1) * tq
        s_buf, p_buf = bufs[idx % 2]
        m = col_max.pop(idx)
        for r0 in range(0, n_keys, piece):
            r1 = min(r0 + piece, n_keys)
            p_buf[r0:r1, :] = jnp.exp2(s_buf[r0:r1, :] - m).astype(BF16)
            yield True
        ot = jnp.dot(vt_ref[h, :, :n_keys], p_buf[:n_keys, :], preferred_element_type=F32)
        ot = ot[:hw] / ot[hw:hw + 1]
        o = (ot[:, :tq] - lam * ot[:, tq:]).T
        o_ref[q0:n_keys, h * hw:(h + 1) * hw] = (_rms(o, hn_ref[...]) * (1.0 - lam_init)).astype(BF16)

    for _ in scores(0):
        pass
    for idx in range(len(blocks)):
        streams = [finish(idx)]
        if idx + 1 < len(blocks):
            streams.insert(0, scores(idx + 1))
        while streams:
            streams = [s for s in streams if next(s, None) is not None]


def _attn_prompt(q, k, v, lams, head_norm, *, batch, seq_len, lam_init, tq):
    n, d_attn = q.shape
    hw = d_attn // N_HEADS
    nh = ATTN_HEADS_PER_STEP
    assert seq_len % tq == 0 and tq % CHUNK == 0 and N_HEADS % nh == 0
    small = pl.BlockSpec((1, hw // 2), lambda b, h: (0, 0))
    head = pl.BlockSpec((seq_len, nh * hw), lambda b, h: (b, h))
    kern = functools.partial(_attn_prompt_kernel, lam_init=lam_init, head_dim=hw // 2, tq=tq)
    return pl.pallas_call(
        kern,
        grid=(batch, N_HEADS // nh),
        in_specs=[small, small, small, small, pl.BlockSpec((1, hw), lambda b, h: (0, 0)),
                  head, head, head],
        out_specs=head,
        out_shape=jax.ShapeDtypeStruct((n, d_attn), BF16),
        scratch_shapes=[pltpu.VMEM((nh, hw, seq_len), BF16), pltpu.VMEM((nh, hw, seq_len), BF16),
                        pltpu.VMEM((nh, hw + BF16_SUBLANES, seq_len), BF16),
                        pltpu.VMEM((seq_len, 2 * tq), F32), pltpu.VMEM((seq_len, 2 * tq), F32),
                        pltpu.VMEM((seq_len, 2 * tq), BF16), pltpu.VMEM((seq_len, 2 * tq), BF16)],
        compiler_params=_params(2),
        name="attn_prompt",
    )(*lams, head_norm, q, k, v)


def _attn_sample_kernel(lq1_ref, lk1_ref, lq2_ref, lk2_ref, hn_ref, q_ref, kc_ref, vc_ref,
                        kn_ref, vn_ref, o_ref, *, lam_init, head_dim):
    tq = q_ref.shape[0]
    hw = 2 * head_dim
    lam = _diff_lambda(lq1_ref, lk1_ref, lq2_ref, lk2_ref, lam_init)
    for h in range(q_ref.shape[1] // hw):
        cols = slice(h * hw, (h + 1) * hw)
        qq = _stacked_queries(q_ref[:, cols], head_dim)
        s_past = jnp.dot(qq, kc_ref[cols, :].astype(BF16), preferred_element_type=F32)
        s_new = _scores(qq, kn_ref[:, cols])
        m = jnp.maximum(jnp.max(s_past, axis=1, keepdims=True), jnp.max(s_new, axis=1, keepdims=True))
        p_past = jnp.exp2(s_past - m)
        p_new = jnp.exp2(s_new - m)
        l = jnp.sum(p_past, axis=1, keepdims=True) + jnp.sum(p_new, axis=1, keepdims=True)
        acc = (jnp.dot(p_past.astype(BF16), vc_ref[:, cols].astype(BF16), preferred_element_type=F32)
               + jnp.dot(p_new.astype(BF16), vn_ref[:, cols].astype(BF16), preferred_element_type=F32))
        o_ref[:, cols] = _finish_heads(acc, l, lam, hn_ref[...], lam_init, tq)


def _attn_sample(q, k_new, v_new, k_past, v_past, lams, head_norm, *, batch, seq_len, past_len, lam_init):
    n, d_attn = q.shape
    hw = d_attn // N_HEADS
    small = pl.BlockSpec((1, hw // 2), lambda b: (0, 0))
    new_spec = pl.BlockSpec((seq_len, d_attn), lambda b: (b, 0))
    past_spec = pl.BlockSpec((past_len, d_attn), lambda b: (b, 0))
    past_t_spec = pl.BlockSpec((d_attn, past_len), lambda b: (b, 0))
    kern = functools.partial(_attn_sample_kernel, lam_init=lam_init, head_dim=hw // 2)
    return pl.pallas_call(
        kern,
        grid=(batch,),
        in_specs=[small, small, small, small, pl.BlockSpec((1, hw), lambda b: (0, 0)),
                  new_spec, past_t_spec, past_spec, new_spec, new_spec],
        out_specs=new_spec,
        out_shape=jax.ShapeDtypeStruct((n, d_attn), BF16),
        compiler_params=_params(1),
        name="attn_sample",
    )(*lams, head_norm, q, k_past, v_past, k_new, v_new)


R_E1, R_E2, R_W1, R_W2, R_RANK1, R_RANK2 = 0, 1, 2, 3, 4, 5
OUT_PROJ_CHUNKS = 4


def _pack_bf16_pairs(lo_bf, hi_bf):
    lo = lax.bitcast_convert_type(lo_bf.astype(F32), U32)
    hi = lax.bitcast_convert_type(hi_bf.astype(F32), U32)
    return (hi & jnp.uint32(0xFFFF0000)) | (lo >> 16)


def _unpack_bf16_pairs(w):
    lo = lax.bitcast_convert_type(w << 16, F32).astype(BF16)
    hi = lax.bitcast_convert_type(w & jnp.uint32(0xFFFF0000), F32).astype(BF16)
    return lo, hi


def _out_proj_kernel(*refs, n_prompt_tiles):
    ap_ref, op_ref, xp_ref, as_ref, os_ref, xs_ref = refs[:6]
    n_w = 2 * OUT_PROJ_CHUNKS
    w_refs = (refs[6:6 + OUT_PROJ_CHUNKS], refs[6 + OUT_PROJ_CHUNKS:6 + n_w])
    g_ref, wr_ref, br_ref, h_ref, hnp_ref, rinfo_ref, rinfo_t_ref, cnt_ref = refs[6 + n_w:-5]
    hbufs, sss, carry_ref = refs[-5:-3], refs[-3:-1], refs[-1]
    i = pl.program_id(0)
    n_tiles = n_prompt_tiles + 1

    @pl.when(i == 0)
    def _():
        carry_ref[...] = jnp.zeros_like(carry_ref)

    def step(slot, inputs, route):
        streams = []
        if inputs is not None:
            streams.append(_out_proj_project(*inputs, w_refs, hbufs[slot], sss[slot]))
        if route:
            streams.append(_out_proj_route(hbufs[1 - slot], sss[1 - slot], g_ref, wr_ref, br_ref,
                                           h_ref, hnp_ref, rinfo_ref, rinfo_t_ref, cnt_ref, carry_ref))
        while streams:
            streams = [s for s in streams if next(s, None) is not None]

    prompt, sample = (ap_ref, op_ref, xp_ref), (as_ref, os_ref, xs_ref)
    pl.when(i == 0)(functools.partial(step, 0, prompt, False))
    for parity in (0, 1):
        pl.when((i >= 1) & (i < n_prompt_tiles) & (i % 2 == parity))(
            functools.partial(step, parity, prompt, True))
    pl.when(i == n_prompt_tiles)(functools.partial(step, n_prompt_tiles % 2, sample, True))
    pl.when(i == n_tiles)(functools.partial(step, n_tiles % 2, None, True))


def _out_proj_project(a_ref, o_ref, x_ref, w_refs, h_ref, ss_ref):
    tm, d = x_ref.shape
    cw = d // OUT_PROJ_CHUNKS
    a, o = a_ref[...], o_ref[...]
    ss = jnp.zeros((tm, 1), F32)
    for c in range(OUT_PROJ_CHUNKS):
        cols = slice(c * cw, (c + 1) * cw)
        h = x_ref[:, cols] + jnp.dot(a, w_refs[0][c][...], preferred_element_type=F32)
        yield True
        h = h + jnp.dot(o, w_refs[1][c][...], preferred_element_type=F32)
        h_ref[:, cols] = h
        ss = ss + jnp.sum(h * h, axis=1, keepdims=True)
        yield True
    ss_ref[...] = jnp.broadcast_to(ss, ss_ref.shape)


def _out_proj_route(hb_ref, ss_ref, g_ref, wr_ref, br_ref,
                    h_ref, hnp_ref, rinfo_ref, rinfo_t_ref, cnt_ref, carry_ref):
    tm, d = hb_ref.shape
    cw = d // OUT_PROJ_CHUNKS
    h_ref[...] = hb_ref[...]
    inv = lax.rsqrt(ss_ref[:, 0:1] / d + EPS)
    yield True

    logits = br_ref[...]
    for c in range(OUT_PROJ_CHUNKS // 2):
        halves = []
        for cc in (c, c + OUT_PROJ_CHUNKS // 2):
            cols = slice(cc * cw, (cc + 1) * cw)
            hb = (hb_ref[:, cols] * inv * g_ref[:, cols]).astype(BF16)
            logits = logits + jnp.dot(hb, wr_ref[cols, :], preferred_element_type=F32)
            halves.append(hb)
        hnp_ref[:, c * cw:(c + 1) * cw] = _pack_bf16_pairs(*halves)
        yield True

    lane = lax.broadcasted_iota(jnp.int32, logits.shape, 1)
    lane_f = lane.astype(F32)
    lg = jnp.where(lane < N_GROUPS, logits, NEG_INF)
    mg = jnp.max(lg, axis=1, keepdims=True)
    grp = jnp.min(jnp.where(lg == mg, lane_f, float(LANES)), axis=1, keepdims=True)
    gate_g = 1.0 / jnp.sum(jnp.exp(lg - mg), axis=1, keepdims=True)
    yield True
    ex = lane - N_GROUPS
    group_of = ex >> (EXPERTS_PER_GROUP.bit_length() - 1)
    in_grp = (ex >= 0) & (ex < N_EXPERTS) & (group_of.astype(F32) == grp)
    le = jnp.where(in_grp, logits, NEG_INF)
    v1 = jnp.max(le, axis=1, keepdims=True)
    i1 = jnp.min(jnp.where(le == v1, lane_f, float(LANES)), axis=1, keepdims=True)
    yield True
    le2 = jnp.where(lane_f == i1, NEG_INF, le)
    v2 = jnp.max(le2, axis=1, keepdims=True)
    i2 = jnp.min(jnp.where(le2 == v2, lane_f, float(LANES)), axis=1, keepdims=True)
    e21 = jnp.exp(v2 - v1)
    w1 = gate_g / (1.0 + e21)
    w2 = gate_g * e21 / (1.0 + e21)
    e1 = i1 - float(N_GROUPS)
    e2 = i2 - float(N_GROUPS)
    yield True

    onehot = jnp.where((lane_f == e1) | (lane_f == e2), 1.0, 0.0)
    r = lax.broadcasted_iota(jnp.int32, (tm, tm), 0)
    c = lax.broadcasted_iota(jnp.int32, (tm, tm), 1)
    earlier = jnp.where(c < r, 1.0, 0.0).astype(BF16)
    before = jnp.dot(earlier, onehot.astype(BF16), preferred_element_type=F32) + carry_ref[0:1, :]
    rank1 = _lane_pick(before, lane_f, e1)
    rank2 = _lane_pick(before, lane_f, e2)
    new_carry = carry_ref[0:1, :] + jnp.sum(onehot, axis=0, keepdims=True)
    carry_ref[...] = jnp.broadcast_to(new_carry, carry_ref.shape)
    cnt_ref[...] = jnp.broadcast_to(new_carry, cnt_ref.shape)
    yield True

    rec = jnp.zeros_like(logits)
    for idx, val in ((R_E1, e1), (R_E2, e2), (R_W1, w1), (R_W2, w2), (R_RANK1, rank1), (R_RANK2, rank2)):
        rec = jnp.where(lane == idx, val, rec)
    rinfo_ref[...] = rec
    rinfo_t_ref[...] = rec.T[:SUBLANES, :]


def _out_proj(a_p, o_p, x_p, a_s, o_s, x_s, w_out_bf, norm_g, w_router, b_router, *, tm):
    n_p, d = x_p.shape
    n_s = x_s.shape[0]
    dc = a_p.shape[1]
    assert n_p % tm == 0 and n_s == tm, "sample rows must fill exactly one tile"
    assert w_out_bf.shape == (2 * dc, d)
    npt = n_p // tm
    n_all = n_p + n_s
    cw = d // OUT_PROJ_CHUNKS
    p_idx = lambda i: (jnp.minimum(i, npt - 1), 0)
    zero = lambda i: (0, 0)

    def wspec(half, c):
        return pl.BlockSpec((dc, cw), lambda i: (half, c))

    kern = functools.partial(_out_proj_kernel, n_prompt_tiles=npt)
    done = lambda i: (jnp.maximum(i - 1, 0), 0)
    return pl.pallas_call(
        kern,
        grid=(npt + 2,),
        in_specs=[
            pl.BlockSpec((tm, dc), p_idx), pl.BlockSpec((tm, dc), p_idx), pl.BlockSpec((tm, d), p_idx),
            pl.BlockSpec((tm, dc), zero), pl.BlockSpec((tm, dc), zero), pl.BlockSpec((tm, d), zero),
            *[wspec(half, c) for half in range(2) for c in range(OUT_PROJ_CHUNKS)],
            pl.BlockSpec((1, d), zero),
            pl.BlockSpec((d, LANES), zero), pl.BlockSpec((1, LANES), zero),
        ],
        out_specs=[
            pl.BlockSpec((tm, d), done),
            pl.BlockSpec((tm, d // 2), done),
            pl.BlockSpec((tm, LANES), done),
            pl.BlockSpec((SUBLANES, tm), lambda i: (0, jnp.maximum(i - 1, 0))),
            pl.BlockSpec((SUBLANES, LANES), zero),
        ],
        out_shape=[
            jax.ShapeDtypeStruct((n_all, d), F32),
            jax.ShapeDtypeStruct((n_all, d // 2), U32),
            jax.ShapeDtypeStruct((n_all, LANES), F32),
            jax.ShapeDtypeStruct((SUBLANES, n_all), F32),
            jax.ShapeDtypeStruct((SUBLANES, LANES), F32),
        ],
        scratch_shapes=[pltpu.VMEM((tm, d), F32), pltpu.VMEM((tm, d), F32),
                        pltpu.VMEM((tm, LANES), F32), pltpu.VMEM((tm, LANES), F32),
                        pltpu.VMEM((SUBLANES, LANES), F32)],
        compiler_params=_params(1),
        name="out_proj",
    )(a_p, o_p, x_p, a_s, o_s, x_s, *([w_out_bf] * (2 * OUT_PROJ_CHUNKS)), norm_g, w_router, b_router)


EXPERT_DMA_GROUPS = 16
EXPERT_SLOTS = 3


def _experts_kernel(tile_expert_ref, n_used_ref, src_hbm, hnp_hbm, wi_f32_ref, wo_f32_ref, ys_ref,
                    src_ref, xs_ref, gu_ref, wi_ref, wo_ref, src_sem, gather_sem):
    w = pl.program_id(0)
    n_used = n_used_ref[0]
    tm = xs_ref.shape[1]
    n_tiles = pl.num_programs(0) - 1
    group = tm // EXPERT_DMA_GROUPS

    def src_copy(tile, slot):
        return pltpu.make_async_copy(src_hbm.at[pl.ds(tile, 1), :], src_ref.at[pl.ds(slot, 1), :],
                                     src_sem.at[slot])

    def gather_row(slot, r):
        return pltpu.make_async_copy(hnp_hbm.at[pl.ds(src_ref[slot, r], 1), :],
                                     xs_ref.at[slot, pl.ds(r, 1), :], gather_sem.at[slot])

    def wait_gathers(slot):
        pltpu.make_async_copy(xs_ref.at[slot], xs_ref.at[slot], gather_sem.at[slot]).wait()

    @pl.when(w == 0)
    def _():
        for t in range(EXPERT_SLOTS):
            src_copy(t, t).start()
        for t in range(EXPERT_SLOTS - 1):
            src_copy(t, t).wait()
            for r in range(tm):
                gather_row(t, r).start(priority=1)

    def compute_tile(cur):
        ahead = (cur + EXPERT_SLOTS - 1) % EXPERT_SLOTS
        wait_gathers(cur)
        src_copy(w + 2, ahead).wait()
        src_copy(w + 3, cur).start()

        def move_rows(g):
            for r in range(g * group, (g + 1) * group):
                gather_row(ahead, r).start(priority=1)

        @pl.when((w == 0) | (tile_expert_ref[w] != tile_expert_ref[jnp.maximum(w - 1, 0)]))
        def _():
            wi_ref[...] = wi_f32_ref[0].astype(BF16)
            wo_ref[...] = wo_f32_ref[0].astype(BF16)

        lo, hi = _unpack_bf16_pairs(xs_ref[cur])
        half = lo.shape[1]
        n_chunks = EXPERT_DMA_GROUPS // 4
        f2 = wi_ref.shape[1]
        cw = f2 // n_chunks
        fh = f2 // 4
        d = wo_ref.shape[1]
        ow = d // n_chunks
        groups = iter(range(EXPERT_DMA_GROUPS))

        def gate_up(c):
            cols = slice(c * cw, (c + 1) * cw)
            part = jnp.dot(lo, wi_ref[:half, cols], preferred_element_type=F32)
            move_rows(next(groups))
            gu_ref[:, cols] = part + jnp.dot(hi, wi_ref[half:, cols], preferred_element_type=F32)
            move_rows(next(groups))

        def hidden(part):
            gate = gu_ref[:, part * fh:(part + 1) * fh]
            up = gu_ref[:, f2 // 2 + part * fh:f2 // 2 + (part + 1) * fh]
            return (gate * jax.nn.sigmoid(gate) * up).astype(BF16)

        gate_up(0)
        gate_up(n_chunks // 2)
        act_lo = hidden(0)
        for c in range(1, n_chunks // 2):
            gate_up(c)
            gate_up(n_chunks // 2 + c)
        for c in range(n_chunks):
            cols = slice(c * ow, (c + 1) * ow)
            ys_ref[:, cols] = jnp.dot(act_lo, wo_ref[:fh, cols], preferred_element_type=F32)
            move_rows(next(groups))
        act_hi = hidden(1)
        for c in range(n_chunks):
            cols = slice(c * ow, (c + 1) * ow)
            ys_ref[:, cols] = ys_ref[:, cols] + jnp.dot(act_hi, wo_ref[fh:, cols], preferred_element_type=F32)
            move_rows(next(groups))

    def drain(cur):
        wait_gathers(cur)
        wait_gathers((cur + 1) % EXPERT_SLOTS)
        src_copy(w + 2, (cur + EXPERT_SLOTS - 1) % EXPERT_SLOTS).wait()

    for key in range(EXPERT_SLOTS):
        mine = (w % EXPERT_SLOTS) == key
        pl.when(mine & (w < n_used))(functools.partial(compute_tile, key))
        pl.when(mine & (w == n_used))(functools.partial(drain, key))

    @pl.when((w >= n_used) & (w < n_tiles))
    def _():
        ys_ref[...] = jnp.zeros_like(ys_ref)


def _experts(tile_expert, n_used, src_tab, hnp, w_ei, w_eo, *, tm):
    n_e, d, f2 = w_ei.shape
    dw = hnp.shape[1]
    n_tiles = src_tab.shape[0] - EXPERT_SLOTS
    n_steps = n_tiles + 1
    assert tm % EXPERT_DMA_GROUPS == 0 and src_tab.shape[1] == tm

    def expert_of(w, te, nu):
        return te[jnp.minimum(w, nu[0] - 1)]

    grid_spec = pltpu.PrefetchScalarGridSpec(
        num_scalar_prefetch=2,
        grid=(n_steps,),
        in_specs=[
            pl.BlockSpec(memory_space=pl.ANY),
            pl.BlockSpec(memory_space=pl.ANY),
            pl.BlockSpec((1, d, f2), lambda w, te, nu: (expert_of(w, te, nu), 0, 0)),
            pl.BlockSpec((1, f2 // 2, d), lambda w, te, nu: (expert_of(w, te, nu), 0, 0)),
        ],
        out_specs=pl.BlockSpec((tm, d), lambda w, te, nu: (jnp.minimum(w, n_tiles - 1), 0)),
        scratch_shapes=[pltpu.SMEM((EXPERT_SLOTS, tm), jnp.int32), pltpu.VMEM((EXPERT_SLOTS, tm, dw), U32),
                        pltpu.VMEM((tm, f2), F32),
                        pltpu.VMEM((d, f2), BF16), pltpu.VMEM((f2 // 2, d), BF16),
                        pltpu.SemaphoreType.DMA((EXPERT_SLOTS,)), pltpu.SemaphoreType.DMA((EXPERT_SLOTS,))],
    )
    return pl.pallas_call(
        _experts_kernel,
        grid_spec=grid_spec,
        out_shape=jax.ShapeDtypeStruct((n_tiles * tm, d), F32),
        compiler_params=_params(1),
        name="experts",
    )(tile_expert, n_used, src_tab, hnp, w_ei, w_eo)


SC_SUBCORES = 16
SC_INDEX_ROW = 128


def _token_table(pos, token, table_len):
    n = pos.shape[0]
    per_core = SC_SUBCORES * SC_INDEX_ROW
    rows = -(-n // per_core)
    spare = table_len
    chunk = -(-(table_len + 1) // (SC_SUBCORES * SC_SUBCORES)) * SC_SUBCORES
    out_len = chunk * SC_SUBCORES
    pad = rows * per_core - n
    pos_p = jnp.concatenate([pos, jnp.full((pad,), spare, jnp.int32)]).reshape(SC_SUBCORES, rows, SC_INDEX_ROW)
    tok_p = jnp.concatenate([token, jnp.zeros((pad,), jnp.int32)]).reshape(SC_SUBCORES, rows, SC_INDEX_ROW)
    mesh = plsc.VectorSubcoreMesh(core_axis_name="core", subcore_axis_name="subcore", num_cores=1)

    def body(pos_hbm, tok_hbm, zero_hbm, out_hbm, idx_ref, val_ref, zero_ref, sem):
        s = lax.axis_index("subcore")
        loads = [pltpu.async_copy(zero_hbm, zero_ref, sem), pltpu.async_copy(pos_hbm.at[s], idx_ref, sem),
                 pltpu.async_copy(tok_hbm.at[s], val_ref, sem)]
        for copy in loads:
            copy.wait()
        pltpu.sync_copy(zero_ref, out_hbm.at[pl.ds(s * chunk, chunk)])
        plsc.subcore_barrier()
        scatters = [pltpu.async_copy(val_ref.at[j], out_hbm.at[idx_ref.at[j]], sem) for j in range(rows)]
        for copy in scatters:
            copy.wait()

    table = pl.kernel(
        body,
        out_type=jax.ShapeDtypeStruct((out_len,), jnp.int32),
        mesh=mesh,
        scratch_types=[pltpu.VMEM((rows, SC_INDEX_ROW), jnp.int32), pltpu.VMEM((rows, SC_INDEX_ROW), jnp.int32),
                       pltpu.VMEM((chunk,), jnp.int32), pltpu.SemaphoreType.DMA],
        name="token_table",
    )(pos_p, tok_p, jnp.zeros((chunk,), jnp.int32))
    return table[:table_len]


def _expert_tiles(counts, e12, r12, n_all, tm):
    max_tiles = (2 * n_all) // tm + N_EXPERTS
    n_rows = max_tiles + EXPERT_SLOTS
    tiles_per_expert = (counts + tm - 1) // tm
    tile_end = jnp.cumsum(tiles_per_expert)
    n_used = tile_end[-1]
    row_start = (tile_end - tiles_per_expert) * tm
    t = jnp.minimum(jnp.arange(n_rows, dtype=jnp.int32), n_used - 1)
    tile_expert = jnp.sum(tile_end[None, :] <= t[:, None], axis=1).astype(jnp.int32)

    experts = jnp.arange(N_EXPERTS, dtype=jnp.int32)[:, None, None]
    pos = jnp.sum(jnp.where(e12[None] == experts, row_start[:, None, None], 0), axis=0) + r12
    token = jnp.broadcast_to(jnp.arange(n_all, dtype=jnp.int32), (2, n_all))
    src = _token_table(pos.reshape(-1), token.reshape(-1), n_rows * tm).reshape(n_rows, tm)
    pos_tiles = pos.reshape(2, n_all // tm, tm).transpose(1, 0, 2).reshape(n_all // tm, 2 * tm)
    return tile_expert, n_used.reshape(1).astype(jnp.int32), src, pos_tiles.astype(jnp.int32)


def _combine_kernel(pos_hbm, h_ref, rinfo_ref, g_ref, ys_hbm, yp_ref, ysm_ref,
                    idx_ref, buf_ref, idx_sem, row_sem, *, n_prompt_tiles):
    i = pl.program_id(0)
    nt = pl.num_programs(0)
    tm = h_ref.shape[0]

    def idx_copy(tile, slot):
        return pltpu.make_async_copy(pos_hbm.at[pl.ds(tile, 1), :], idx_ref.at[pl.ds(slot, 1), :],
                                     idx_sem.at[slot])

    def start_rows(slot):
        def body(r, _):
            for k in range(2):
                pltpu.make_async_copy(ys_hbm.at[pl.ds(idx_ref[slot, k * tm + r], 1), :],
                                      buf_ref.at[slot, k, pl.ds(r, 1), :], row_sem.at[slot]).start(priority=1)
            return 0
        lax.fori_loop(0, tm, body, 0, unroll=8)

    def wait_rows(slot):
        pltpu.make_async_copy(buf_ref.at[slot], buf_ref.at[slot], row_sem.at[slot]).wait()

    @pl.when(i == 0)
    def _():
        idx_copy(0, 0).start()
        idx_copy(0, 0).wait()
        start_rows(0)

        @pl.when(nt > 1)
        def _():
            idx_copy(1, 1).start()

    def tile(cur):
        nxt = 1 - cur

        @pl.when(i + 1 < nt)
        def _():
            idx_copy(i + 1, nxt).wait()
            start_rows(nxt)

        @pl.when(i + 2 < nt)
        def _():
            idx_copy(i + 2, cur).start()

        wait_rows(cur)
        rinfo = rinfo_ref[...]
        lane = lax.broadcasted_iota(jnp.int32, rinfo.shape, 1)
        w1 = _lane_pick(rinfo, lane, R_W1)
        w2 = _lane_pick(rinfo, lane, R_W2)
        hp = h_ref[...] + (w1 * buf_ref[cur, 0] + w2 * buf_ref[cur, 1])
        y = _rms(hp, g_ref[...])

        @pl.when(i < n_prompt_tiles)
        def _():
            yp_ref[...] = y

        @pl.when(i >= n_prompt_tiles)
        def _():
            ysm_ref[...] = y

    for parity in (0, 1):
        pl.when((i % 2) == parity)(functools.partial(tile, parity))


def _combine(pos_tiles, h, rinfo, norm_g, ys, *, n_prompt, tm):
    n_all, d = h.shape
    nt = n_all // tm
    npt = n_prompt // tm
    assert n_all - n_prompt == tm and pos_tiles.shape == (nt, 2 * tm)
    kern = functools.partial(_combine_kernel, n_prompt_tiles=npt)
    return pl.pallas_call(
        kern,
        grid=(nt,),
        in_specs=[
            pl.BlockSpec(memory_space=pl.ANY),
            pl.BlockSpec((tm, d), lambda i: (i, 0)),
            pl.BlockSpec((tm, LANES), lambda i: (i, 0)),
            pl.BlockSpec((1, d), lambda i: (0, 0)),
            pl.BlockSpec(memory_space=pl.ANY),
        ],
        out_specs=[
            pl.BlockSpec((tm, d), lambda i: (jnp.minimum(i, npt - 1), 0)),
            pl.BlockSpec((tm, d), lambda i: (0, 0)),
        ],
        out_shape=[jax.ShapeDtypeStruct((n_prompt, d), F32), jax.ShapeDtypeStruct((tm, d), F32)],
        scratch_shapes=[pltpu.SMEM((2, 2 * tm), jnp.int32), pltpu.VMEM((2, 2, tm, d), F32),
                        pltpu.SemaphoreType.DMA((2,)), pltpu.SemaphoreType.DMA((2,))],
        compiler_params=_params(1),
        name="combine",
    )(pos_tiles, h, rinfo, norm_g, ys)


ROW_TILE = 256
PROJ_ROWS = 1024
PROJ_COLS = 256
ATTN_Q_ROWS = 256
ATTN_HEADS_PER_STEP = 2


def kernel(x_prompt, x_sample, cache_k, cache_v, state_conv, norm_mix, w_in, conv_w, lambda_q1,
           lambda_k1, lambda_q2, lambda_k2, head_norm, w_out, norm_ffn, w_router_group,
           b_router_group, w_router_expert, b_router_expert, w_expert_in, w_expert_out, norm_final):
    depth = w_in.shape[0]
    b, t, d = x_prompt.shape
    db, dt, _ = x_sample.shape
    past = cache_k.shape[2]
    d_conv = conv_w.shape[2]
    d_attn = d - d_conv
    n_p, n_s = b * t, db * dt
    n_all = n_p + n_s
    tm = ROW_TILE

    hp = x_prompt.reshape(n_p, d)
    hs = x_sample.reshape(n_s, d)
    outs = {name: [] for name in ("kp", "vp", "cp", "ks", "vs", "cs")}
    y_p = y_s = None
    for l in range(depth):
        lam_init = 0.8 - 0.6 * math.exp(-0.3 * l)
        w_in_bf = w_in[l].astype(BF16)
        w_out_bf = w_out[l].astype(BF16)
        g_mix = norm_mix[l].reshape(1, d)
        lams = tuple(p[l].reshape(1, -1) for p in (lambda_q1, lambda_k1, lambda_q2, lambda_k2))
        hnorm = head_norm[l].reshape(1, -1)
        pad = LANES - N_GROUPS - N_EXPERTS
        w_router = jnp.concatenate([w_router_group[l], w_router_expert[l], jnp.zeros((d, pad), F32)],
                                   axis=1).astype(BF16)
        b_router = jnp.concatenate([b_router_group[l], b_router_expert[l], jnp.zeros((pad,), F32)]).reshape(1, LANES)

        zero_prev = jnp.zeros((b, CONV_WIDTH - 1, d_conv), F32)
        a_p, q_p, kb_p, kt_p, v_p, tail_p = _in_proj(hp, g_mix, w_in_bf, conv_w[l], zero_prev,
                                                     seq_len=t, tm=min(PROJ_ROWS, t), tn=PROJ_COLS)
        o_p = _attn_prompt(q_p, kb_p, v_p, lams, hnorm, batch=b, seq_len=t, lam_init=lam_init,
                           tq=min(ATTN_Q_ROWS, t))
        prev_rows = jnp.stack([jnp.repeat(state_conv[l][:, 0], dt, axis=0),
                               jnp.repeat(state_conv[l][:, 1], dt, axis=0)])
        a_s, q_s, kb_s, k_s, v_s, u_s = _in_proj(hs, g_mix, w_in_bf, conv_w[l], prev_rows,
                                                 seq_len=dt, tm=n_s, tn=2 * PROJ_COLS)
        kt_past = cache_k[l].transpose(0, 2, 3, 4, 1).reshape(db * d_attn, past)
        o_s = _attn_sample(q_s, kb_s, v_s, kt_past, cache_v[l].reshape(db * past, d_attn), lams, hnorm,
                           batch=db, seq_len=dt, past_len=past, lam_init=lam_init)

        h, hnp, rinfo, rinfo_t, cnt = _out_proj(a_p, o_p, hp, a_s, o_s, hs, w_out_bf,
                                       norm_ffn[l].reshape(1, d), w_router, b_router, tm=tm)
        counts = cnt[0, :N_EXPERTS].astype(jnp.int32)
        e12 = rinfo_t[R_E1:R_E2 + 1].astype(jnp.int32)
        r12 = rinfo_t[R_RANK1:R_RANK2 + 1].astype(jnp.int32)
        tile_expert, n_used, src_tab, pos_tiles = _expert_tiles(counts, e12, r12, n_all, tm)
        ys = _experts(tile_expert, n_used, src_tab, hnp, w_expert_in[l], w_expert_out[l], tm=tm)
        last = l == depth - 1
        g_out = norm_final.reshape(1, d) if last else jnp.ones((1, d), F32)
        y_p, y_s = _combine(pos_tiles, h, rinfo, g_out, ys, n_prompt=n_p, tm=tm)
        assert last, "multi-layer stacks need an un-normalised combine output"

        outs["kp"].append(kt_p.reshape(b, N_HEADS, 2, d_attn // (2 * N_HEADS), t).transpose(0, 4, 1, 2, 3))
        outs["vp"].append(v_p.reshape(b, t, N_HEADS, d_attn // N_HEADS))
        tails = tail_p.reshape(b, t // min(PROJ_ROWS, t), SUBLANES, d_conv)
        outs["cp"].append(tails[:, -1, SUBLANES - (CONV_WIDTH - 1):, :])
        outs["ks"].append(k_s.reshape(db, dt, N_HEADS, 2, d_attn // (2 * N_HEADS)))
        outs["vs"].append(v_s.reshape(db, dt, N_HEADS, d_attn // N_HEADS))
        outs["cs"].append(u_s.reshape(db, dt, d_conv)[:, dt - (CONV_WIDTH - 1):, :])

    return (y_p.reshape(b, t, d), y_s.reshape(db, dt, d),
            jnp.stack(outs["kp"]), jnp.stack(outs["vp"]), jnp.stack(outs["cp"]),
            jnp.stack(outs["ks"]), jnp.stack(outs["vs"]), jnp.stack(outs["cs"]))
```
